```python
import jax
import jax.numpy as jnp
from jax import lax
import numpy as np


D_MODEL = 1024
BATCH = 8
SEQ = 2048
DEPTH = 2
DEC_BATCH = 128
DEC_SEQ = 1
PAST_LEN = 16384
PAGE_SIZE = 128

D_POOL = D_MODEL // 2
N_POOL_GROUPS = 4
POOL_GROUP = D_POOL // N_POOL_GROUPS
POOL_WINDOWS = (2, 4, 8, 16)
POOL_BUF = max(POOL_WINDOWS) - 1
D_CONV = D_MODEL // 2
CONV_WIDTH = 31
CONV_BUF = CONV_WIDTH - 1
D_LRU = D_MODEL // 2
N_LRU_BLOCKS = 8
LRU_BLOCK = D_LRU // N_LRU_BLOCKS
LRU_CONV_WIDTH = 4
LRU_CONV_BUF = LRU_CONV_WIDTH - 1
LRU_C = 8.0
N_BRANCH = 3
IN_COLS = D_POOL + 2 * D_CONV + 2 * D_LRU + N_BRANCH * D_MODEL
D_FF = 2816
N_EXPERTS = 8
TOP_K = 2
D_EXPERT = 3584
N_DENSE = (DEPTH + 1) // 2
N_MOE = DEPTH // 2
ALPHA = (2.0 * DEPTH) ** 0.25
BETA = (8.0 * DEPTH) ** -0.25
LN_EPS = 1e-5

kernel_name = "hybrid_pool_conformer_rglru_decoder_step"


def layer_norm(x, g, b):
    xf = x.astype(jnp.float32)
    mu = jnp.mean(xf, axis=-1, keepdims=True)
    var = jnp.mean(jnp.square(xf - mu), axis=-1, keepdims=True)
    y = (xf - mu) * lax.rsqrt(var + LN_EPS) * g.astype(jnp.float32) + b.astype(jnp.float32)
    return y.astype(x.dtype)


def causal_depthwise_conv(u, prev, w, b):
    width = w.shape[0]
    u_ext = jnp.concatenate([prev.astype(u.dtype), u], axis=1)
    y = lax.conv_general_dilated(
        u_ext, w[:, None, :].astype(u.dtype), window_strides=(1,), padding='VALID',
        dimension_numbers=('NWC', 'WIO', 'NWC'), feature_group_count=u.shape[-1])
    return y + b, u_ext[:, u_ext.shape[1] - (width - 1):]


def pool_mixer(u, prev, start_pos, w_pool, scale):
    B, T, _ = u.shape
    u_ext = jnp.concatenate([prev.astype(u.dtype), u], axis=1)
    cs = jnp.cumsum(u_ext.astype(jnp.float32), axis=1)
    cs = jnp.concatenate([jnp.zeros((B, 1, D_POOL), jnp.float32), cs], axis=1)
    end = cs[:, POOL_BUF + 1:]
    pos = (start_pos + jnp.arange(T))[:, None]
    groups = []
    for g, w in enumerate(POOL_WINDOWS):
        sl = slice(g * POOL_GROUP, (g + 1) * POOL_GROUP)
        start = cs[:, POOL_BUF + 1 - w:POOL_BUF + 1 - w + T, sl]
        cnt = jnp.minimum(w, pos + 1).astype(jnp.float32)
        groups.append((end[..., sl] - start) / cnt)
    pooled = jnp.concatenate(groups, axis=-1) - u.astype(jnp.float32)
    pooled = pooled.astype(u.dtype).reshape(B, T, N_POOL_GROUPS, POOL_GROUP)
    mixed = jnp.einsum('btgc,gcd->btgd', pooled, w_pool).reshape(B, T, D_POOL) * scale
    return mixed, u_ext[:, T:]


def rg_lru(xc, h0, start_pos, w_a, b_a, w_x, b_x, lam):
    B, T, _ = xc.shape
    xb = xc.reshape(B, T, N_LRU_BLOCKS, LRU_BLOCK)
    r = jax.nn.sigmoid((jnp.einsum('btnc,ncd->btnd', xb, w_a).reshape(B, T, D_LRU) + b_a).astype(jnp.float32))
    i = jax.nn.sigmoid((jnp.einsum('btnc,ncd->btnd', xb, w_x).reshape(B, T, D_LRU) + b_x).astype(jnp.float32))
    log_a = -LRU_C * r * jax.nn.softplus(-lam.astype(jnp.float32))
    reset = ((start_pos + jnp.arange(T)) == 0)[None, :, None]
    mult = jnp.where(reset, 1.0, jnp.sqrt(-jnp.expm1(2.0 * log_a)))
    a = jnp.where(reset, 0.0, jnp.exp(log_a))
    bterm = mult * i * xc.astype(jnp.float32)
    bterm = bterm.at[:, 0].add(a[:, 0] * h0.astype(jnp.float32))

    def combine(left, right):
        a1, b1 = left
        a2, b2 = right
        return a1 * a2, a2 * b1 + b2

    _, h = lax.associative_scan(combine, (a, bterm), axis=1)
    return h, h[:, -1]


def swiglu(x, w_gate, w_up, w_down):
    return (jax.nn.silu(x @ w_gate) * (x @ w_up)) @ w_down


def moe_swiglu(x, router, w_gate, w_up, w_down):
    B, T, D = x.shape
    xt = x.reshape(B * T, D)
    logits = (xt @ router).astype(jnp.float32)
    top_vals, top_idx = lax.top_k(logits, TOP_K)
    gates = jax.nn.softmax(top_vals, axis=-1)
    combine = jnp.sum(jax.nn.one_hot(top_idx, N_EXPERTS, dtype=jnp.float32) * gates[..., None], axis=1)
    y = jnp.zeros((B * T, D), jnp.float32)
    for e in range(N_EXPERTS):
        y_e = swiglu(xt, w_gate[e], w_up[e], w_down[e])
        y = y + combine[:, e:e + 1] * y_e.astype(jnp.float32)
    return y.astype(x.dtype).reshape(B, T, D)


def setup_inputs(seed: int = 0) -> dict:
    key = jax.random.key(seed)
    ks = list(jax.random.split(key, 40))
    kit = iter(ks)

    def nrm(shape, scale):
        return jax.random.normal(next(kit), shape, jnp.float32) * scale

    def gain(shape):
        return 1.0 + nrm(shape, 0.05)

    x_prompt = nrm((BATCH, SEQ, D_MODEL), 1.0)
    x_sample = nrm((DEC_BATCH, DEC_SEQ, D_MODEL), 1.0)
    state_pool = nrm((DEPTH, DEC_BATCH, POOL_BUF, D_POOL), 1.0)
    state_conv = nrm((DEPTH, DEC_BATCH, CONV_BUF, D_CONV), 0.5)
    state_lru_conv = nrm((DEPTH, DEC_BATCH, LRU_CONV_BUF, D_LRU), 1.0)
    state_lru_h = nrm((DEPTH, DEC_BATCH, D_LRU), 0.5)

    w_in = nrm((DEPTH, D_MODEL, IN_COLS), D_MODEL ** -0.5)
    b_gate = nrm((DEPTH, N_BRANCH * D_MODEL), 0.02)
    w_pool = nrm((DEPTH, N_POOL_GROUPS, POOL_GROUP, POOL_GROUP), POOL_GROUP ** -0.5)
    pool_scale = gain((DEPTH, D_POOL))
    conv_w = nrm((DEPTH, CONV_WIDTH, D_CONV), CONV_WIDTH ** -0.5)
    conv_b = nrm((DEPTH, D_CONV), 0.02)
    conv_ln_g = gain((DEPTH, D_CONV))
    conv_ln_b = nrm((DEPTH, D_CONV), 0.02)
    lru_conv_w = nrm((DEPTH, LRU_CONV_WIDTH, D_LRU), LRU_CONV_WIDTH ** -0.5)
    lru_conv_b = nrm((DEPTH, D_LRU), 0.02)
    lru_wa = nrm((DEPTH, N_LRU_BLOCKS, LRU_BLOCK, LRU_BLOCK), LRU_BLOCK ** -0.5)
    lru_ba = nrm((DEPTH, D_LRU), 0.02)
    lru_wx = nrm((DEPTH, N_LRU_BLOCKS, LRU_BLOCK, LRU_BLOCK), LRU_BLOCK ** -0.5)
    lru_bx = nrm((DEPTH, D_LRU), 0.02)
    a0 = jax.random.uniform(next(kit), (DEPTH, D_LRU), jnp.float32, 0.9, 0.999)
    lru_lambda = jnp.log(a0) - jnp.log1p(-a0)
    w_br_pool = nrm((DEPTH, D_POOL, D_MODEL), D_POOL ** -0.5)
    w_br_conv = nrm((DEPTH, D_CONV, D_MODEL), D_CONV ** -0.5)
    w_br_lru = nrm((DEPTH, D_LRU, D_MODEL), D_LRU ** -0.5)
    w_out = nrm((DEPTH, D_MODEL, D_MODEL), BETA * D_MODEL ** -0.5)
    ln1_g = gain((DEPTH, D_MODEL))
    ln1_b = nrm((DEPTH, D_MODEL), 0.02)
    ln2_g = gain((DEPTH, D_MODEL))
    ln2_b = nrm((DEPTH, D_MODEL), 0.02)
    ffn_w_gate = nrm((N_DENSE, D_MODEL, D_FF), D_MODEL ** -0.5)
    ffn_w_up = nrm((N_DENSE, D_MODEL, D_FF), D_MODEL ** -0.5)
    ffn_w_down = nrm((N_DENSE, D_FF, D_MODEL), BETA * D_FF ** -0.5)
    moe_router = nrm((N_MOE, D_MODEL, N_EXPERTS), D_MODEL ** -0.5)
    moe_w_gate = nrm((N_MOE, N_EXPERTS, D_MODEL, D_EXPERT), D_MODEL ** -0.5)
    moe_w_up = nrm((N_MOE, N_EXPERTS, D_MODEL, D_EXPERT), D_MODEL ** -0.5)
    moe_w_down = nrm((N_MOE, N_EXPERTS, D_EXPERT, D_MODEL), BETA * D_EXPERT ** -0.5)
    return {
        "x_prompt": x_prompt, "x_sample": x_sample,
        "state_pool": state_pool, "state_conv": state_conv,
        "state_lru_conv": state_lru_conv, "state_lru_h": state_lru_h,
        "w_in": w_in, "b_gate": b_gate, "w_pool": w_pool, "pool_scale": pool_scale,
        "conv_w": conv_w, "conv_b": conv_b, "conv_ln_g": conv_ln_g, "conv_ln_b": conv_ln_b,
        "lru_conv_w": lru_conv_w, "lru_conv_b": lru_conv_b, "lru_wa": lru_wa, "lru_ba": lru_ba,
        "lru_wx": lru_wx, "lru_bx": lru_bx, "lru_lambda": lru_lambda,
        "w_br_pool": w_br_pool, "w_br_conv": w_br_conv, "w_br_lru": w_br_lru, "w_out": w_out,
        "ln1_g": ln1_g, "ln1_b": ln1_b, "ln2_g": ln2_g, "ln2_b": ln2_b,
        "ffn_w_gate": ffn_w_gate, "ffn_w_up": ffn_w_up, "ffn_w_down": ffn_w_down,
        "moe_router": moe_router, "moe_w_gate": moe_w_gate, "moe_w_up": moe_w_up,
        "moe_w_down": moe_w_down,
    }


def reference(x_prompt, x_sample, state_pool, state_conv, state_lru_conv, state_lru_h,
              w_in, b_gate, w_pool, pool_scale, conv_w, conv_b, conv_ln_g, conv_ln_b,
              lru_conv_w, lru_conv_b, lru_wa, lru_ba, lru_wx, lru_bx, lru_lambda,
              w_br_pool, w_br_conv, w_br_lru, w_out, ln1_g, ln1_b, ln2_g, ln2_b,
              ffn_w_gate, ffn_w_up, ffn_w_down, moe_router, moe_w_gate, moe_w_up, moe_w_down):

    def mixer(l, x, prev_pool, prev_conv, prev_lru_conv, h0, start_pos):
        B, T, _ = x.shape
        proj = jnp.einsum('btd,dc->btc', x, w_in[l])
        o1 = D_POOL
        o2 = o1 + D_CONV
        o3 = o2 + D_CONV
        o4 = o3 + D_LRU
        o5 = o4 + D_LRU
        u_pool = proj[..., :o1]
        u_val = proj[..., o1:o2]
        u_glu = proj[..., o2:o3]
        u_lru = proj[..., o3:o4]
        u_gelu = proj[..., o4:o5]
        u_gate = proj[..., o5:]
        y_a, new_pool = pool_mixer(u_pool, prev_pool, start_pos, w_pool[l], pool_scale[l])
        y_a = y_a @ w_br_pool[l]
        glu = u_val * jax.nn.sigmoid(u_glu)
        c, new_conv = causal_depthwise_conv(glu, prev_conv, conv_w[l], conv_b[l])
        y_b = jax.nn.silu(layer_norm(c, conv_ln_g[l], conv_ln_b[l])) @ w_br_conv[l]
        xc, new_lru_conv = causal_depthwise_conv(u_lru, prev_lru_conv, lru_conv_w[l], lru_conv_b[l])
        h, h_last = rg_lru(xc, h0, start_pos, lru_wa[l], lru_ba[l], lru_wx[l], lru_bx[l], lru_lambda[l])
        y_c = (h.astype(x.dtype) * jax.nn.gelu(u_gelu)) @ w_br_lru[l]
        gates = jax.nn.sigmoid((u_gate + b_gate[l]).astype(jnp.float32)).astype(x.dtype)
        gates = gates.reshape(B, T, N_BRANCH, D_MODEL)
        merged = gates[:, :, 0] * y_a + gates[:, :, 1] * y_b + gates[:, :, 2] * y_c
        return merged @ w_out[l], new_pool, new_conv, new_lru_conv, h_last.astype(h0.dtype)

    def trunk(x, prev_pool, prev_conv, prev_lru_conv, h0, start_pos):
        pools, convs, lconvs, hs = [], [], [], []
        for l in range(DEPTH):
            m, sp, sc, slc, sh = mixer(l, x, prev_pool[l], prev_conv[l], prev_lru_conv[l], h0[l], start_pos)
            pools.append(sp)
            convs.append(sc)
            lconvs.append(slc)
            hs.append(sh)
            x = layer_norm(ALPHA * x + m, ln1_g[l], ln1_b[l])
            if l % 2 == 0:
                f = swiglu(x, ffn_w_gate[l // 2], ffn_w_up[l // 2], ffn_w_down[l // 2])
            else:
                f = moe_swiglu(x, moe_router[l // 2], moe_w_gate[l // 2], moe_w_up[l // 2], moe_w_down[l // 2])
            x = layer_norm(ALPHA * x + f, ln2_g[l], ln2_b[l])
        return x, jnp.stack(pools), jnp.stack(convs), jnp.stack(lconvs), jnp.stack(hs)

    nb = x_prompt.shape[0]
    dt = x_prompt.dtype
    y_prompt, pool_p, conv_p, lconv_p, h_p = trunk(
        x_prompt,
        jnp.zeros((DEPTH, nb, POOL_BUF, D_POOL), dt),
        jnp.zeros((DEPTH, nb, CONV_BUF, D_CONV), dt),
        jnp.zeros((DEPTH, nb, LRU_CONV_BUF, D_LRU), dt),
        jnp.zeros((DEPTH, nb, D_LRU), state_lru_h.dtype),
        0)
    y_sample, pool_s, conv_s, lconv_s, h_s = trunk(
        x_sample, state_pool, state_conv, state_lru_conv, state_lru_h, PAST_LEN)
    return (y_prompt, y_sample, pool_p, conv_p, lconv_p, h_p, pool_s, conv_s, lconv_s, h_s)
```

```python
import functools

import jax
import jax.numpy as jnp
from jax import lax
from jax.experimental import pallas as pl
from jax.experimental.pallas import tpu as pltpu

D_MODEL = 1024
DEPTH = 2
PAST_LEN = 16384
D_POOL = 512
N_POOL_GROUPS = 4
POOL_GROUP = 128
POOL_WINDOWS = (2, 4, 8, 16)
POOL_BUF = 15
D_CONV = 512
CONV_WIDTH = 31
CONV_BUF = 30
D_LRU = 512
N_LRU_BLOCKS = 8
LRU_BLOCK = 64
LRU_CONV_WIDTH = 4
LRU_CONV_BUF = 3
LRU_C = 8.0
N_EXPERTS = 8
TOP_K = 2
ALPHA = (2.0 * DEPTH) ** 0.25
LN_EPS = 1e-5

O_POOL = 0
O_VAL = D_POOL
O_GLU = O_VAL + D_CONV
O_LRU = O_GLU + D_CONV
O_GELU = O_LRU + D_LRU
O_GATE = O_GELU + D_LRU

SUBLANES = 8
LANES = 128
VMEM_LIMIT_BYTES = 56 * 1024 * 1024

SEQ_TILE = 512
CONV_ROWS = 64
POOL_OFF = 16
CONV_OFF = 32
LCONV_OFF = 8
FFN_CHUNK = 256
EXPERT_F_TILE = 512
MOE_TILE_SEQ = 512
MOE_TILE_STEP = 128

BF16 = jnp.bfloat16
F32 = jnp.float32


def _sigmoid(x):
    return 0.5 * jnp.tanh(0.5 * x) + 0.5


def _silu(x):
    return x * _sigmoid(x)


def _gelu_tanh(x):
    return x * (0.5 * (1.0 + jnp.tanh(0.7978845608028654 * (x + 0.044715 * (x * x * x)))))


def _softplus(z):
    return jnp.maximum(z, 0.0) + jnp.log1p(jnp.exp(-jnp.abs(z)))


def _layer_norm(x, g, b):
    mu = jnp.mean(x, axis=-1, keepdims=True)
    xc = x - mu
    var = jnp.mean(xc * xc, axis=-1, keepdims=True)
    return xc * lax.rsqrt(var + LN_EPS) * g + b


def _dot(a, b):
    return jnp.dot(a, b, preferred_element_type=F32)


def _lru_coeffs(xc, r, ig, lam, reset):
    log_a = (-LRU_C * r) * _softplus(-lam)
    a = jnp.exp(log_a)
    mult = jnp.sqrt(-jnp.tanh(log_a) * (a * a + 1.0))
    if reset is not None:
        a = jnp.where(reset, 0.0, a)
        mult = jnp.where(reset, 1.0, mult)
    return a, mult * ig * xc


def _mixer_seq_kernel(x_ref, w_in_ref, b_gate_ref, w_pool_ref, pool_scale_ref,
                      conv_w_ref, conv_b_ref, cln_g_ref, cln_b_ref,
                      lconv_w_ref, lconv_b_ref, wa_ref, ba_ref, wx_ref, bx_ref, lam_ref,
                      wbp_ref, wbc_ref, wbl_ref, w_out_ref, ln_g_ref, ln_b_ref,
                      y_ref, sp_ref, sc_ref, slc_ref, sh_ref,
                      pool_ext, conv_ext, lconv_ext, act_buf, a_buf, b_buf, h_carry, *, tm):
    i = pl.program_id(1)

    @pl.when(i == 0)
    def _():
        pool_ext[0:POOL_OFF, :] = jnp.zeros((POOL_OFF, D_POOL), F32)
        conv_ext[0:CONV_OFF, :] = jnp.zeros((CONV_OFF, D_CONV), F32)
        lconv_ext[0:LCONV_OFF, :] = jnp.zeros((LCONV_OFF, D_LRU), F32)
        h_carry[...] = jnp.zeros((1, D_LRU), F32)

    x = x_ref[...]
    xb = x.astype(BF16)

    def proj(lo, width):
        return _dot(xb, w_in_ref[:, lo:lo + width])

    def gate(n):
        lo = n * D_MODEL
        return _sigmoid(proj(O_GATE + lo, D_MODEL) + b_gate_ref[:, lo:lo + D_MODEL])

    pos = i * tm + lax.broadcasted_iota(jnp.int32, (tm, 1), 0)

    u_pool = proj(O_POOL, D_POOL)
    pool_ext[POOL_OFF:POOL_OFF + tm, :] = u_pool
    parts = []
    for g, w in enumerate(POOL_WINDOWS):
        lo = g * POOL_GROUP
        acc = u_pool[:, lo:lo + POOL_GROUP]
        for k in range(1, w):
            acc = acc + pool_ext[POOL_OFF - k:POOL_OFF - k + tm, lo:lo + POOL_GROUP]
        inv_cnt = 1.0 / jnp.minimum(w, pos + 1).astype(F32)
        pooled = acc * inv_cnt - u_pool[:, lo:lo + POOL_GROUP]
        parts.append(_dot(pooled.astype(BF16), w_pool_ref[g]))
    mixed = jnp.concatenate(parts, axis=1) * pool_scale_ref[...]
    merged = gate(0) * _dot(mixed.astype(BF16), wbp_ref[...])
    sp_ref[0] = pool_ext[tm + POOL_OFF - POOL_BUF:tm + POOL_OFF, :]
    pool_ext[0:POOL_OFF, :] = pool_ext[tm:tm + POOL_OFF, :]

    glu = proj(O_VAL, D_CONV) * _sigmoid(proj(O_GLU, D_CONV))
    conv_ext[CONV_OFF:CONV_OFF + tm, :] = glu
    base = CONV_OFF - CONV_BUF
    for c0 in range(0, tm, CONV_ROWS):
        acc = jnp.zeros((CONV_ROWS, D_CONV), F32) + conv_b_ref[...]
        for k in range(CONV_WIDTH):
            acc = acc + conv_ext[c0 + base + k:c0 + base + k + CONV_ROWS, :] * conv_w_ref[k:k + 1, :]
        act_buf[c0:c0 + CONV_ROWS, :] = _silu(_layer_norm(acc, cln_g_ref[...], cln_b_ref[...]))
    merged = merged + gate(1) * _dot(act_buf[...].astype(BF16), wbc_ref[...])
    sc_ref[0] = conv_ext[tm + CONV_OFF - CONV_BUF:tm + CONV_OFF, :]
    conv_ext[0:CONV_OFF, :] = conv_ext[tm:tm + CONV_OFF, :]

    lconv_ext[LCONV_OFF:LCONV_OFF + tm, :] = proj(O_LRU, D_LRU)
    base = LCONV_OFF - LRU_CONV_BUF
    xc = jnp.zeros((tm, D_LRU), F32) + lconv_b_ref[...]
    for k in range(LRU_CONV_WIDTH):
        xc = xc + lconv_ext[base + k:base + k + tm, :] * lconv_w_ref[k:k + 1, :]
    slc_ref[0] = lconv_ext[tm + LCONV_OFF - LRU_CONV_BUF:tm + LCONV_OFF, :]
    lconv_ext[0:LCONV_OFF, :] = lconv_ext[tm:tm + LCONV_OFF, :]
    xcb = xc.astype(BF16)
    r = _sigmoid(_dot(xcb, wa_ref[...]) + ba_ref[...])
    ig = _sigmoid(_dot(xcb, wx_ref[...]) + bx_ref[...])
    a, b = _lru_coeffs(xc, r, ig, lam_ref[...], pos == 0)
    a_buf[...] = a
    b_buf[...] = b

    row8 = lax.broadcasted_iota(jnp.int32, (SUBLANES, D_LRU), 0)

    def scan_block(j, h):
        r0 = pl.multiple_of(j * SUBLANES, SUBLANES)
        a8 = a_buf[pl.ds(r0, SUBLANES), :]
        b8 = b_buf[pl.ds(r0, SUBLANES), :]
        for k in (1, 2, 4):
            a_s = pltpu.roll(a8, k, 0)
            b_s = pltpu.roll(b8, k, 0)
            m = row8 >= k
            b8 = jnp.where(m, a8 * b_s + b8, b8)
            a8 = jnp.where(m, a8 * a_s, a8)
        h8 = a8 * h + b8
        a_buf[pl.ds(r0, SUBLANES), :] = h8
        return h8[SUBLANES - 1:SUBLANES, :]

    h_last = lax.fori_loop(0, tm // SUBLANES, scan_block, h_carry[...], unroll=2)
    h_carry[...] = h_last
    sh_ref[0] = h_last
    hg = a_buf[...] * _gelu_tanh(proj(O_GELU, D_LRU))
    merged = merged + gate(2) * _dot(hg.astype(BF16), wbl_ref[...])

    m_out = _dot(merged.astype(BF16), w_out_ref[...])
    y_ref[...] = _layer_norm(ALPHA * x + m_out, ln_g_ref[...], ln_b_ref[...])


def _mixer_step_kernel(x_ref, st_pool_ref, st_conv_ref, st_lconv_ref, st_h_ref,
                       w_in_ref, b_gate_ref, w_pool_ref, pool_scale_ref,
                       conv_w_ref, conv_b_ref, cln_g_ref, cln_b_ref,
                       lconv_w_ref, lconv_b_ref, wa_ref, ba_ref, wx_ref, bx_ref, lam_ref,
                       wbp_ref, wbc_ref, wbl_ref, w_out_ref, ln_g_ref, ln_b_ref,
                       y_ref, sp_ref, sc_ref, slc_ref, sh_ref, *, start_pos):
    x = x_ref[...]
    xb = x.astype(BF16)

    def proj(lo, width):
        return _dot(xb, w_in_ref[:, lo:lo + width])

    def gate(n):
        lo = n * D_MODEL
        return _sigmoid(proj(O_GATE + lo, D_MODEL) + b_gate_ref[:, lo:lo + D_MODEL])

    u_pool = proj(O_POOL, D_POOL)
    parts = []
    for g, w in enumerate(POOL_WINDOWS):
        lo = g * POOL_GROUP
        acc = u_pool[:, lo:lo + POOL_GROUP]
        for k in range(1, w):
            acc = acc + st_pool_ref[POOL_BUF - k, :, lo:lo + POOL_GROUP]
        pooled = acc * (1.0 / min(w, start_pos + 1)) - u_pool[:, lo:lo + POOL_GROUP]
        parts.append(_dot(pooled.astype(BF16), w_pool_ref[g]))
    mixed = jnp.concatenate(parts, axis=1) * pool_scale_ref[...]
    merged = gate(0) * _dot(mixed.astype(BF16), wbp_ref[...])
    for k in range(POOL_BUF - 1):
        sp_ref[k] = st_pool_ref[k + 1]
    sp_ref[POOL_BUF - 1] = u_pool

    glu = proj(O_VAL, D_CONV) * _sigmoid(proj(O_GLU, D_CONV))
    acc = glu * conv_w_ref[CONV_BUF:CONV_BUF + 1, :] + conv_b_ref[...]
    for k in range(CONV_BUF):
        acc = acc + st_conv_ref[k] * conv_w_ref[k:k + 1, :]
    act = _silu(_layer_norm(acc, cln_g_ref[...], cln_b_ref[...]))
    merged = merged + gate(1) * _dot(act.astype(BF16), wbc_ref[...])
    for k in range(CONV_BUF - 1):
        sc_ref[k] = st_conv_ref[k + 1]
    sc_ref[CONV_BUF - 1] = glu

    u_lru = proj(O_LRU, D_LRU)
    xc = u_lru * lconv_w_ref[LRU_CONV_BUF:LRU_CONV_BUF + 1, :] + lconv_b_ref[...]
    for k in range(LRU_CONV_BUF):
        xc = xc + st_lconv_ref[k] * lconv_w_ref[k:k + 1, :]
    for k in range(LRU_CONV_BUF - 1):
        slc_ref[k] = st_lconv_ref[k + 1]
    slc_ref[LRU_CONV_BUF - 1] = u_lru
    xcb = xc.astype(BF16)
    r = _sigmoid(_dot(xcb, wa_ref[...]) + ba_ref[...])
    ig = _sigmoid(_dot(xcb, wx_ref[...]) + bx_ref[...])
    reset = jnp.full(xc.shape, True) if start_pos == 0 else None
    a, b = _lru_coeffs(xc, r, ig, lam_ref[...], reset)
    h = a * st_h_ref[...] + b
    sh_ref[...] = h
    hg = h * _gelu_tanh(proj(O_GELU, D_LRU))
    merged = merged + gate(2) * _dot(hg.astype(BF16), wbl_ref[...])

    m_out = _dot(merged.astype(BF16), w_out_ref[...])
    y_ref[...] = _layer_norm(ALPHA * x + m_out, ln_g_ref[...], ln_b_ref[...])


def _const_spec(shape):
    nd = len(shape)
    return pl.BlockSpec(shape, lambda *_: (0,) * nd, pipeline_mode=pl.Buffered(1))


def _mixer_weight_list(p):
    return [p["w_in"], p["b_gate"], p["w_pool"], p["pool_scale"], p["conv_w"], p["conv_b"],
            p["conv_ln_g"], p["conv_ln_b"], p["lru_conv_w"], p["lru_conv_b"], p["lru_wa"],
            p["lru_ba"], p["lru_wx"], p["lru_bx"], p["lru_lambda"], p["w_br_pool"],
            p["w_br_conv"], p["w_br_lru"], p["w_out"], p["ln1_g"], p["ln1_b"]]


def _mixer_seq(x, p, batch, seq):
    tm = min(SEQ_TILE, seq)
    nt = seq // tm
    weights = _mixer_weight_list(p)
    in_specs = [pl.BlockSpec((tm, D_MODEL), lambda b, i: (b * nt + i, 0))]
    in_specs += [_const_spec(w.shape) for w in weights]
    out_shape = (
        jax.ShapeDtypeStruct((batch * seq, D_MODEL), F32),
        jax.ShapeDtypeStruct((batch, POOL_BUF, D_POOL), F32),
        jax.ShapeDtypeStruct((batch, CONV_BUF, D_CONV), F32),
        jax.ShapeDtypeStruct((batch, LRU_CONV_BUF, D_LRU), F32),
        jax.ShapeDtypeStruct((batch, 1, D_LRU), F32),
    )
    out_specs = (
        pl.BlockSpec((tm, D_MODEL), lambda b, i: (b * nt + i, 0)),
        pl.BlockSpec((1, POOL_BUF, D_POOL), lambda b, i: (b, 0, 0)),
        pl.BlockSpec((1, CONV_BUF, D_CONV), lambda b, i: (b, 0, 0)),
        pl.BlockSpec((1, LRU_CONV_BUF, D_LRU), lambda b, i: (b, 0, 0)),
        pl.BlockSpec((1, 1, D_LRU), lambda b, i: (b, 0, 0)),
    )
    scratch = [
        pltpu.VMEM((tm + POOL_OFF, D_POOL), F32),
        pltpu.VMEM((tm + CONV_OFF, D_CONV), F32),
        pltpu.VMEM((tm + LCONV_OFF, D_LRU), F32),
        pltpu.VMEM((tm, D_CONV), F32),
        pltpu.VMEM((tm, D_LRU), F32),
        pltpu.VMEM((tm, D_LRU), F32),
        pltpu.VMEM((1, D_LRU), F32),
    ]
    y, sp, sc, slc, sh = pl.pallas_call(
        functools.partial(_mixer_seq_kernel, tm=tm),
        grid=(batch, nt),
        in_specs=in_specs,
        out_specs=out_specs,
        out_shape=out_shape,
        scratch_shapes=scratch,
        compiler_params=pltpu.CompilerParams(
            dimension_semantics=("arbitrary", "arbitrary"),
            vmem_limit_bytes=VMEM_LIMIT_BYTES),
        name="mixer_seq",
    )(x, *weights)
    return y, sp, sc, slc, sh.reshape(batch, D_LRU)


def _mixer_step(x, st_pool, st_conv, st_lconv, st_h, p, start_pos):
    batch = x.shape[0]
    weights = _mixer_weight_list(p)
    states = [jnp.transpose(st_pool, (1, 0, 2)), jnp.transpose(st_conv, (1, 0, 2)),
              jnp.transpose(st_lconv, (1, 0, 2)), st_h]
    args = [x] + states + weights
    out_shape = (
        jax.ShapeDtypeStruct((batch, D_MODEL), F32),
        jax.ShapeDtypeStruct((POOL_BUF, batch, D_POOL), F32),
        jax.ShapeDtypeStruct((CONV_BUF, batch, D_CONV), F32),
        jax.ShapeDtypeStruct((LRU_CONV_BUF, batch, D_LRU), F32),
        jax.ShapeDtypeStruct((batch, D_LRU), F32),
    )
    y, sp, sc, slc, sh = pl.pallas_call(
        functools.partial(_mixer_step_kernel, start_pos=start_pos),
        out_shape=out_shape,
        compiler_params=pltpu.CompilerParams(vmem_limit_bytes=VMEM_LIMIT_BYTES),
        name="mixer_step",
    )(*args)
    return (y, jnp.transpose(sp, (1, 0, 2)), jnp.transpose(sc, (1, 0, 2)),
            jnp.transpose(slc, (1, 0, 2)), sh)


def _ffn_kernel(x_ref, wg_ref, wu_ref, wd_ref, g_ref, b_ref, y_ref, *, d_ff):
    x = x_ref[...]
    xb = x.astype(BF16)
    acc = jnp.zeros(x.shape, F32)
    for c0 in range(0, d_ff, FFN_CHUNK):
        h = _silu(_dot(xb, wg_ref[:, c0:c0 + FFN_CHUNK])) * _dot(xb, wu_ref[:, c0:c0 + FFN_CHUNK])
        acc = acc + _dot(h.astype(BF16), wd_ref[c0:c0 + FFN_CHUNK, :])
    y_ref[...] = _layer_norm(ALPHA * x + acc, g_ref[...], b_ref[...])


def _ffn(x, wg, wu, wd, g, b):
    n = x.shape[0]
    tm = min(SEQ_TILE, n)
    d_ff = wg.shape[1]
    weights = [wg, wu, wd, g, b]
    return pl.pallas_call(
        functools.partial(_ffn_kernel, d_ff=d_ff),
        grid=(n // tm,),
        in_specs=[pl.BlockSpec((tm, D_MODEL), lambda i: (i, 0))] + [_const_spec(w.shape) for w in weights],
        out_specs=pl.BlockSpec((tm, D_MODEL), lambda i: (i, 0)),
        out_shape=jax.ShapeDtypeStruct((n, D_MODEL), F32),
        compiler_params=pltpu.CompilerParams(
            dimension_semantics=("arbitrary",), vmem_limit_bytes=VMEM_LIMIT_BYTES),
        name="ffn_dense",
    )(x, *weights)


def _router_kernel(x_ref, wr_ref, idx_ref, gate_ref):
    logits = jnp.dot(x_ref[...], wr_ref[...], preferred_element_type=F32,
                     precision=lax.Precision.HIGHEST)
    lane = lax.broadcasted_iota(jnp.int32, logits.shape, 1)
    neg = jnp.float32(-jnp.inf)
    l1 = jnp.where(lane < N_EXPERTS, logits, neg)
    v1 = jnp.max(l1, axis=-1, keepdims=True)
    i1 = jnp.min(jnp.where(l1 == v1, lane, LANES), axis=-1, keepdims=True)
    l2 = jnp.where(lane == i1, neg, l1)
    v2 = jnp.max(l2, axis=-1, keepdims=True)
    i2 = jnp.min(jnp.where(l2 == v2, lane, LANES), axis=-1, keepdims=True)
    e2 = jnp.exp(v2 - v1)
    den = 1.0 + e2
    idx_ref[...] = jnp.where(lane == 0, i1, jnp.where(lane == 1, i2, 0))
    gate_ref[...] = jnp.where(lane == 0, 1.0 / den, jnp.where(lane == 1, e2 / den, 0.0))


def _router(x, wr_pad):
    n = x.shape[0]
    tm = min(SEQ_TILE, n)
    return pl.pallas_call(
        _router_kernel,
        grid=(n // tm,),
        in_specs=[pl.BlockSpec((tm, D_MODEL), lambda i: (i, 0)), _const_spec(wr_pad.shape)],
        out_specs=(pl.BlockSpec((tm, LANES), lambda i: (i, 0)),
                   pl.BlockSpec((tm, LANES), lambda i: (i, 0))),
        out_shape=(jax.ShapeDtypeStruct((n, LANES), jnp.int32),
                   jax.ShapeDtypeStruct((n, LANES), F32)),
        compiler_params=pltpu.CompilerParams(dimension_semantics=("arbitrary",)),
        name="moe_router",
    )(x, wr_pad)


def _row_gather(src_hbm, idx_ref, idx_base, dst_ref, sem, n_rows):
    def copy(r):
        return pltpu.make_async_copy(
            src_hbm.at[pl.ds(idx_ref[0, 0, idx_base + r], 1)], dst_ref.at[pl.ds(r, 1)], sem)

    def start(r, c):
        copy(r).start()
        return c

    def wait(r, c):
        copy(r).wait()
        return c

    lax.fori_loop(0, n_rows, start, 0, unroll=8)
    lax.fori_loop(0, n_rows, wait, 0, unroll=8)


def _expert_kernel(te_ref, nu_ref, tok_ref, x_hbm, wg_ref, wu_ref, wd_ref, ys_ref,
                   xs_f32, xs_bf, acc, sem, *, tm):
    del te_ref
    i = pl.program_id(0)
    j = pl.program_id(1)
    nf = pl.num_programs(1)
    used = i < nu_ref[0]

    @pl.when(jnp.logical_and(used, j == 0))
    def _():
        _row_gather(x_hbm, tok_ref, 0, xs_f32, sem, tm)
        xs_bf[...] = xs_f32[...].astype(BF16)
        acc[...] = jnp.zeros(acc.shape, F32)

    @pl.when(used)
    def _():
        xb = xs_bf[...]
        h = _silu(_dot(xb, wg_ref[...])) * _dot(xb, wu_ref[...])
        acc[...] += _dot(h.astype(BF16), wd_ref[...])

    @pl.when(jnp.logical_and(used, j == nf - 1))
    def _():
        ys_ref[...] = acc[...]

    @pl.when(jnp.logical_and(jnp.logical_not(used), j == nf - 1))
    def _():
        ys_ref[...] = jnp.zeros(ys_ref.shape, F32)


def _experts(x, tile_expert, n_used, tok_of_slot, wg, wu, wd, tm):
    n_tiles = tile_expert.shape[0]
    d_e = wg.shape[2]
    tf = EXPERT_F_TILE
    nf = d_e // tf
    tok3 = tok_of_slot.reshape(n_tiles, 1, tm)

    def jeff(i, j, nu):
        return jnp.where(i < nu[0], j, nf - 1)

    grid_spec = pltpu.PrefetchScalarGridSpec(
        num_scalar_prefetch=2,
        grid=(n_tiles, nf),
        in_specs=[
            pl.BlockSpec((1, 1, tm), lambda i, j, te, nu: (i, 0, 0), memory_space=pltpu.SMEM),
            pl.BlockSpec(memory_space=pl.ANY),
            pl.BlockSpec((None, D_MODEL, tf), lambda i, j, te, nu: (te[i], 0, jeff(i, j, nu))),
            pl.BlockSpec((None, D_MODEL, tf), lambda i, j, te, nu: (te[i], 0, jeff(i, j, nu))),
            pl.BlockSpec((None, tf, D_MODEL), lambda i, j, te, nu: (te[i], jeff(i, j, nu), 0)),
        ],
        out_specs=pl.BlockSpec((tm, D_MODEL), lambda i, j, te, nu: (i, 0)),
        scratch_shapes=[
            pltpu.VMEM((tm, D_MODEL), F32),
            pltpu.VMEM((tm, D_MODEL), BF16),
            pltpu.VMEM((tm, D_MODEL), F32),
            pltpu.SemaphoreType.DMA(()),
        ],
    )
    return pl.pallas_call(
        functools.partial(_expert_kernel, tm=tm),
        grid_spec=grid_spec,
        out_shape=jax.ShapeDtypeStruct((n_tiles * tm, D_MODEL), F32),
        compiler_params=pltpu.CompilerParams(
            dimension_semantics=("arbitrary", "arbitrary"), vmem_limit_bytes=VMEM_LIMIT_BYTES),
        name="moe_experts",
    )(tile_expert, n_used, tok3, x, wg, wu, wd)


def _combine_kernel(slot_ref, x_ref, gate_ref, ys_hbm, g_ref, b_ref, y_ref, y0, y1, sem, *, tm):
    _row_gather(ys_hbm, slot_ref, 0, y0, sem, tm)
    _row_gather(ys_hbm, slot_ref, tm, y1, sem, tm)
    x = x_ref[...]
    gates = gate_ref[...]
    f = gates[:, 0:1] * y0[...] + gates[:, 1:2] * y1[...]
    y_ref[...] = _layer_norm(ALPHA * x + f, g_ref[...], b_ref[...])


def _combine(x, gates, slots, ys, g, b):
    n = x.shape[0]
    tm = min(SEQ_TILE, n)
    nt = n // tm
    slot3 = jnp.transpose(slots.reshape(nt, tm, TOP_K), (0, 2, 1)).reshape(nt, 1, TOP_K * tm)
    return pl.pallas_call(
        functools.partial(_combine_kernel, tm=tm),
        grid=(nt,),
        in_specs=[
            pl.BlockSpec((1, 1, TOP_K * tm), lambda i: (i, 0, 0), memory_space=pltpu.SMEM),
            pl.BlockSpec((tm, D_MODEL), lambda i: (i, 0)),
            pl.BlockSpec((tm, LANES), lambda i: (i, 0)),
            pl.BlockSpec(memory_space=pl.ANY),
            _const_spec(g.shape),
            _const_spec(b.shape),
        ],
        out_specs=pl.BlockSpec((tm, D_MODEL), lambda i: (i, 0)),
        out_shape=jax.ShapeDtypeStruct((n, D_MODEL), F32),
        scratch_shapes=[
            pltpu.VMEM((tm, D_MODEL), F32),
            pltpu.VMEM((tm, D_MODEL), F32),
            pltpu.SemaphoreType.DMA(()),
        ],
        compiler_params=pltpu.CompilerParams(dimension_semantics=("arbitrary",)),
        name="moe_combine",
    )(slot3, x, gates, ys, g, b)


def _moe(x, wr_pad, wg, wu, wd, g, b, tm):
    n = x.shape[0]
    idx, gates = _router(x, wr_pad)
    flat_e = idx[:, :TOP_K].reshape(-1)
    onehot = (flat_e[:, None] == jnp.arange(N_EXPERTS, dtype=jnp.int32)[None, :]).astype(jnp.int32)
    csum = jnp.cumsum(onehot, axis=0)
    rank = jnp.sum((csum - onehot) * onehot, axis=1)
    counts = csum[-1]
    padded = ((counts + tm - 1) // tm) * tm
    ends = jnp.cumsum(padded)
    slot = (ends - padded)[flat_e] + rank
    n_tiles = (TOP_K * n) // tm + N_EXPERTS
    tok_of_slot = jnp.zeros((n_tiles * tm,), jnp.int32).at[slot].set(
        jnp.arange(TOP_K * n, dtype=jnp.int32) // TOP_K)
    n_used = (ends[-1] // tm).astype(jnp.int32).reshape(1)
    tile_start = jnp.arange(n_tiles, dtype=jnp.int32) * tm
    tile_expert = jnp.sum((tile_start[:, None] >= ends[None, :]).astype(jnp.int32), axis=1)
    last_e = jnp.sum((ends[-1] - 1 >= ends).astype(jnp.int32))
    tile_expert = jnp.minimum(tile_expert, last_e).astype(jnp.int32)
    ys = _experts(x, tile_expert, n_used, tok_of_slot, wg, wu, wd, tm)
    return _combine(x, gates, slot.reshape(n, TOP_K), ys, g, b)


def _block_diag(w):
    n, c, _ = w.shape
    eye = jnp.eye(n, dtype=w.dtype)
    return (eye[:, None, :, None] * w[:, :, None, :]).reshape(n * c, n * c)


def _row(v):
    return v.reshape(1, -1)


def kernel(x_prompt, x_sample, state_pool, state_conv, state_lru_conv, state_lru_h, w_in, b_gate, w_pool, pool_scale, conv_w, conv_b, conv_ln_g, conv_ln_b, lru_conv_w, lru_conv_b, lru_wa, lru_ba, lru_wx, lru_bx, lru_lambda, w_br_pool, w_br_conv, w_br_lru, w_out, ln1_g, ln1_b, ln2_g, ln2_b, ffn_w_gate, ffn_w_up, ffn_w_down, moe_router, moe_w_gate, moe_w_up, moe_w_down):
    batch, seq, _ = x_prompt.shape
    dec_batch = x_sample.shape[0]

    layers = []
    for l in range(DEPTH):
        layers.append({
            "w_in": w_in[l].astype(BF16), "b_gate": _row(b_gate[l]),
            "w_pool": w_pool[l].astype(BF16), "pool_scale": _row(pool_scale[l]),
            "conv_w": conv_w[l], "conv_b": _row(conv_b[l]),
            "conv_ln_g": _row(conv_ln_g[l]), "conv_ln_b": _row(conv_ln_b[l]),
            "lru_conv_w": lru_conv_w[l], "lru_conv_b": _row(lru_conv_b[l]),
            "lru_wa": _block_diag(lru_wa[l]).astype(BF16), "lru_ba": _row(lru_ba[l]),
            "lru_wx": _block_diag(lru_wx[l]).astype(BF16), "lru_bx": _row(lru_bx[l]),
            "lru_lambda": _row(lru_lambda[l]),
            "w_br_pool": w_br_pool[l].astype(BF16), "w_br_conv": w_br_conv[l].astype(BF16),
            "w_br_lru": w_br_lru[l].astype(BF16), "w_out": w_out[l].astype(BF16),
            "ln1_g": _row(ln1_g[l]), "ln1_b": _row(ln1_b[l]),
        })
    ffn_w = [(ffn_w_gate[m].astype(BF16), ffn_w_up[m].astype(BF16), ffn_w_down[m].astype(BF16))
             for m in range(ffn_w_gate.shape[0])]
    moe_w = [(jnp.pad(moe_router[m], ((0, 0), (0, LANES - N_EXPERTS))),
              moe_w_gate[m].astype(BF16), moe_w_up[m].astype(BF16), moe_w_down[m].astype(BF16))
             for m in range(moe_router.shape[0])]

    def channel_mixer(l, x, moe_tile):
        g, b = _row(ln2_g[l]), _row(ln2_b[l])
        if l % 2 == 0:
            return _ffn(x, *ffn_w[l // 2], g, b)
        return _moe(x, *moe_w[l // 2], g, b, moe_tile)

    x = x_prompt.reshape(batch * seq, D_MODEL)
    p_states = []
    for l in range(DEPTH):
        x, sp, sc, slc, sh = _mixer_seq(x, layers[l], batch, seq)
        p_states.append((sp, sc, slc, sh))
        x = channel_mixer(l, x, MOE_TILE_SEQ)
    y_prompt = x.reshape(batch, seq, D_MODEL)

    x = x_sample.reshape(dec_batch, D_MODEL)
    s_states = []
    for l in range(DEPTH):
        x, sp, sc, slc, sh = _mixer_step(x, state_pool[l], state_conv[l], state_lru_conv[l],
                                         state_lru_h[l], layers[l], PAST_LEN)
        s_states.append((sp, sc, slc, sh))
        x = channel_mixer(l, x, MOE_TILE_STEP)
    y_sample = x.reshape(dec_batch, 1, D_MODEL)

    def stack(states, k):
        return jnp.stack([s[k] for s in states])

    return (y_prompt, y_sample,
            stack(p_states, 0), stack(p_states, 1), stack(p_states, 2), stack(p_states, 3),
            stack(s_states, 0), stack(s_states, 1), stack(s_states, 2), stack(s_states, 3))
```

```python
import functools

import jax
import jax.numpy as jnp
from jax import lax
from jax.experimental import pallas as pl
from jax.experimental.pallas import tpu as pltpu

D_MODEL = 1024
DEPTH = 2
PAST_LEN = 16384
D_POOL = 512
N_POOL_GROUPS = 4
POOL_GROUP = 128
POOL_WINDOWS = (2, 4, 8, 16)
POOL_BUF = 15
D_CONV = 512
CONV_WIDTH = 31
CONV_BUF = 30
D_LRU = 512
N_LRU_BLOCKS = 8
LRU_BLOCK = 64
LRU_CONV_WIDTH = 4
LRU_CONV_BUF = 3
LRU_C = 8.0
N_EXPERTS = 8
TOP_K = 2
ALPHA = (2.0 * DEPTH) ** 0.25
LN_EPS = 1e-5

O_POOL = 0
O_VAL = D_POOL
O_GLU = O_VAL + D_CONV
O_LRU = O_GLU + D_CONV
O_GELU = O_LRU + D_LRU
O_GATE = O_GELU + D_LRU

SUBLANES = 8
LANES = 128
VMEM_LIMIT_BYTES = 56 * 1024 * 1024

SEQ_TILE = 512
CONV_ROWS = 64
POOL_OFF = 16
CONV_OFF = 32
LCONV_OFF = 8
FFN_CHUNK = 256
EXPERT_F_TILE = 1792
MOE_TILE_SEQ = 512
MOE_TILE_STEP = 128

BF16 = jnp.bfloat16
F32 = jnp.float32


def _sigmoid(x):
    return 0.5 * jnp.tanh(0.5 * x) + 0.5


def _silu(x):
    return x * _sigmoid(x)


def _gelu_tanh(x):
    return x * (0.5 * (1.0 + jnp.tanh(0.7978845608028654 * (x + 0.044715 * (x * x * x)))))


def _softplus(z):
    return jnp.maximum(z, 0.0) + jnp.log1p(jnp.exp(-jnp.abs(z)))


def _layer_norm(x, g, b):
    mu = jnp.mean(x, axis=-1, keepdims=True)
    xc = x - mu
    var = jnp.mean(xc * xc, axis=-1, keepdims=True)
    return xc * lax.rsqrt(var + LN_EPS) * g + b


def _dot(a, b):
    return jnp.dot(a, b, preferred_element_type=F32)


def _lru_coeffs(xc, r, ig, lam, reset):
    log_a = (-LRU_C * r) * _softplus(-lam)
    a = jnp.exp(log_a)
    mult = jnp.sqrt(-jnp.tanh(log_a) * (a * a + 1.0))
    if reset is not None:
        a = jnp.where(reset, 0.0, a)
        mult = jnp.where(reset, 1.0, mult)
    return a, mult * ig * xc


def _mixer_seq_kernel(x_ref, w_in_ref, b_gate_ref, w_pool_ref, pool_scale_ref,
                      conv_w_ref, conv_b_ref, cln_g_ref, cln_b_ref,
                      lconv_w_ref, lconv_b_ref, wa_ref, ba_ref, wx_ref, bx_ref, lam_ref,
                      wbp_ref, wbc_ref, wbl_ref, w_out_ref, ln_g_ref, ln_b_ref,
                      y_ref, sp_ref, sc_ref, slc_ref, sh_ref,
                      pool_ext, conv_ext, lconv_ext, act_buf, a_buf, b_buf, h_carry, *, tm):
    i = pl.program_id(1)

    @pl.when(i == 0)
    def _():
        pool_ext[0:POOL_OFF, :] = jnp.zeros((POOL_OFF, D_POOL), F32)
        conv_ext[0:CONV_OFF, :] = jnp.zeros((CONV_OFF, D_CONV), F32)
        conv_ext[tm + CONV_OFF:tm + CONV_OFF + SUBLANES, :] = jnp.zeros((SUBLANES, D_CONV), F32)
        lconv_ext[0:LCONV_OFF, :] = jnp.zeros((LCONV_OFF, D_LRU), F32)
        h_carry[...] = jnp.zeros((1, D_LRU), F32)

    x = x_ref[...]
    xb = x.astype(BF16)

    def proj(lo, width):
        return _dot(xb, w_in_ref[:, lo:lo + width])

    def gate(n):
        lo = n * D_MODEL
        return _sigmoid(proj(O_GATE + lo, D_MODEL) + b_gate_ref[:, lo:lo + D_MODEL])

    pos = i * tm + lax.broadcasted_iota(jnp.int32, (tm, 1), 0)

    u_pool = proj(O_POOL, D_POOL)
    pool_ext[POOL_OFF:POOL_OFF + tm, :] = u_pool
    parts = []
    for g, w in enumerate(POOL_WINDOWS):
        lo = g * POOL_GROUP
        acc = u_pool[:, lo:lo + POOL_GROUP]
        for k in range(1, w):
            acc = acc + pool_ext[POOL_OFF - k:POOL_OFF - k + tm, lo:lo + POOL_GROUP]
        inv_cnt = 1.0 / jnp.minimum(w, pos + 1).astype(F32)
        pooled = acc * inv_cnt - u_pool[:, lo:lo + POOL_GROUP]
        parts.append(_dot(pooled.astype(BF16), w_pool_ref[g]))
    mixed = jnp.concatenate(parts, axis=1) * pool_scale_ref[...]
    merged = gate(0) * _dot(mixed.astype(BF16), wbp_ref[...])
    sp_ref[0] = pool_ext[tm + POOL_OFF - POOL_BUF:tm + POOL_OFF, :]
    pool_ext[0:POOL_OFF, :] = pool_ext[tm:tm + POOL_OFF, :]

    glu = proj(O_VAL, D_CONV) * _sigmoid(proj(O_GLU, D_CONV))
    conv_ext[CONV_OFF:CONV_OFF + tm, :] = glu
    base = CONV_OFF - CONV_BUF
    rows = CONV_ROWS + SUBLANES
    for c0 in range(0, tm, CONV_ROWS):
        acc = jnp.zeros((CONV_ROWS, D_CONV), F32) + conv_b_ref[...]
        for r in range(SUBLANES):
            z = None
            for k in range(CONV_WIDTH):
                if (base + k) % SUBLANES != r:
                    continue
                q8 = base + k - r
                t = conv_ext[c0 + q8:c0 + q8 + rows, :] * conv_w_ref[k:k + 1, :]
                z = t if z is None else z + t
            acc = acc + z[r:r + CONV_ROWS, :]
        act_buf[c0:c0 + CONV_ROWS, :] = _silu(_layer_norm(acc, cln_g_ref[...], cln_b_ref[...]))
    merged = merged + gate(1) * _dot(act_buf[...].astype(BF16), wbc_ref[...])
    sc_ref[0] = conv_ext[tm + CONV_OFF - CONV_BUF:tm + CONV_OFF, :]
    conv_ext[0:CONV_OFF, :] = conv_ext[tm:tm + CONV_OFF, :]

    lconv_ext[LCONV_OFF:LCONV_OFF + tm, :] = proj(O_LRU, D_LRU)
    base = LCONV_OFF - LRU_CONV_BUF
    xc = jnp.zeros((tm, D_LRU), F32) + lconv_b_ref[...]
    for k in range(LRU_CONV_WIDTH):
        xc = xc + lconv_ext[base + k:base + k + tm, :] * lconv_w_ref[k:k + 1, :]
    slc_ref[0] = lconv_ext[tm + LCONV_OFF - LRU_CONV_BUF:tm + LCONV_OFF, :]
    lconv_ext[0:LCONV_OFF, :] = lconv_ext[tm:tm + LCONV_OFF, :]
    xcb = xc.astype(BF16)
    r = _sigmoid(_dot(xcb, wa_ref[...]) + ba_ref[...])
    ig = _sigmoid(_dot(xcb, wx_ref[...]) + bx_ref[...])
    a, b = _lru_coeffs(xc, r, ig, lam_ref[...], pos == 0)
    a_buf[...] = a
    b_buf[...] = b

    row8 = lax.broadcasted_iota(jnp.int32, (SUBLANES, D_LRU), 0)

    def scan_block(j, h):
        r0 = pl.multiple_of(j * SUBLANES, SUBLANES)
        a8 = a_buf[pl.ds(r0, SUBLANES), :]
        b8 = b_buf[pl.ds(r0, SUBLANES), :]
        for k in (1, 2, 4):
            a_s = pltpu.roll(a8, k, 0)
            b_s = pltpu.roll(b8, k, 0)
            m = row8 >= k
            b8 = jnp.where(m, a8 * b_s + b8, b8)
            a8 = jnp.where(m, a8 * a_s, a8)
        h8 = a8 * h + b8
        a_buf[pl.ds(r0, SUBLANES), :] = h8
        return h8[SUBLANES - 1:SUBLANES, :]

    h_last = lax.fori_loop(0, tm // SUBLANES, scan_block, h_carry[...], unroll=2)
    h_carry[...] = h_last
    sh_ref[0] = h_last
    hg = a_buf[...] * _gelu_tanh(proj(O_GELU, D_LRU))
    merged = merged + gate(2) * _dot(hg.astype(BF16), wbl_ref[...])

    m_out = _dot(merged.astype(BF16), w_out_ref[...])
    y_ref[...] = _layer_norm(ALPHA * x + m_out, ln_g_ref[...], ln_b_ref[...])


def _mixer_step_kernel(x_ref, st_pool_ref, st_conv_ref, st_lconv_ref, st_h_ref,
                       w_in_ref, b_gate_ref, w_pool_ref, pool_scale_ref,
                       conv_w_ref, conv_b_ref, cln_g_ref, cln_b_ref,
                       lconv_w_ref, lconv_b_ref, wa_ref, ba_ref, wx_ref, bx_ref, lam_ref,
                       wbp_ref, wbc_ref, wbl_ref, w_out_ref, ln_g_ref, ln_b_ref,
                       y_ref, sp_ref, sc_ref, slc_ref, sh_ref, *, start_pos):
    x = x_ref[...]
    xb = x.astype(BF16)

    def proj(lo, width):
        return _dot(xb, w_in_ref[:, lo:lo + width])

    def gate(n):
        lo = n * D_MODEL
        return _sigmoid(proj(O_GATE + lo, D_MODEL) + b_gate_ref[:, lo:lo + D_MODEL])

    u_pool = proj(O_POOL, D_POOL)
    parts = []
    for g, w in enumerate(POOL_WINDOWS):
        lo = g * POOL_GROUP
        acc = u_pool[:, lo:lo + POOL_GROUP]
        for k in range(1, w):
            acc = acc + st_pool_ref[POOL_BUF - k, :, lo:lo + POOL_GROUP]
        pooled = acc * (1.0 / min(w, start_pos + 1)) - u_pool[:, lo:lo + POOL_GROUP]
        parts.append(_dot(pooled.astype(BF16), w_pool_ref[g]))
    mixed = jnp.concatenate(parts, axis=1) * pool_scale_ref[...]
    merged = gate(0) * _dot(mixed.astype(BF16), wbp_ref[...])
    for k in range(POOL_BUF - 1):
        sp_ref[k] = st_pool_ref[k + 1]
    sp_ref[POOL_BUF - 1] = u_pool

    glu = proj(O_VAL, D_CONV) * _sigmoid(proj(O_GLU, D_CONV))
    acc = glu * conv_w_ref[CONV_BUF:CONV_BUF + 1, :] + conv_b_ref[...]
    for k in range(CONV_BUF):
        acc = acc + st_conv_ref[k] * conv_w_ref[k:k + 1, :]
    act = _silu(_layer_norm(acc, cln_g_ref[...], cln_b_ref[...]))
    merged = merged + gate(1) * _dot(act.astype(BF16), wbc_ref[...])
    for k in range(CONV_BUF - 1):
        sc_ref[k] = st_conv_ref[k + 1]
    sc_ref[CONV_BUF - 1] = glu

    u_lru = proj(O_LRU, D_LRU)
    xc = u_lru * lconv_w_ref[LRU_CONV_BUF:LRU_CONV_BUF + 1, :] + lconv_b_ref[...]
    for k in range(LRU_CONV_BUF):
        xc = xc + st_lconv_ref[k] * lconv_w_ref[k:k + 1, :]
    for k in range(LRU_CONV_BUF - 1):
        slc_ref[k] = st_lconv_ref[k + 1]
    slc_ref[LRU_CONV_BUF - 1] = u_lru
    xcb = xc.astype(BF16)
    r = _sigmoid(_dot(xcb, wa_ref[...]) + ba_ref[...])
    ig = _sigmoid(_dot(xcb, wx_ref[...]) + bx_ref[...])
    reset = jnp.full(xc.shape, True) if start_pos == 0 else None
    a, b = _lru_coeffs(xc, r, ig, lam_ref[...], reset)
    h = a * st_h_ref[...] + b
    sh_ref[...] = h
    hg = h * _gelu_tanh(proj(O_GELU, D_LRU))
    merged = merged + gate(2) * _dot(hg.astype(BF16), wbl_ref[...])

    m_out = _dot(merged.astype(BF16), w_out_ref[...])
    y_ref[...] = _layer_norm(ALPHA * x + m_out, ln_g_ref[...], ln_b_ref[...])


def _const_spec(shape):
    nd = len(shape)
    return pl.BlockSpec(shape, lambda *_: (0,) * nd, pipeline_mode=pl.Buffered(1))


def _mixer_weight_list(p):
    return [p["w_in"], p["b_gate"], p["w_pool"], p["pool_scale"], p["conv_w"], p["conv_b"],
            p["conv_ln_g"], p["conv_ln_b"], p["lru_conv_w"], p["lru_conv_b"], p["lru_wa"],
            p["lru_ba"], p["lru_wx"], p["lru_bx"], p["lru_lambda"], p["w_br_pool"],
            p["w_br_conv"], p["w_br_lru"], p["w_out"], p["ln1_g"], p["ln1_b"]]


def _mixer_seq(x, p, batch, seq):
    tm = min(SEQ_TILE, seq)
    nt = seq // tm
    weights = _mixer_weight_list(p)
    in_specs = [pl.BlockSpec((tm, D_MODEL), lambda b, i: (b * nt + i, 0))]
    in_specs += [_const_spec(w.shape) for w in weights]
    out_shape = (
        jax.ShapeDtypeStruct((batch * seq, D_MODEL), F32),
        jax.ShapeDtypeStruct((batch, POOL_BUF, D_POOL), F32),
        jax.ShapeDtypeStruct((batch, CONV_BUF, D_CONV), F32),
        jax.ShapeDtypeStruct((batch, LRU_CONV_BUF, D_LRU), F32),
        jax.ShapeDtypeStruct((batch, 1, D_LRU), F32),
    )
    out_specs = (
        pl.BlockSpec((tm, D_MODEL), lambda b, i: (b * nt + i, 0)),
        pl.BlockSpec((1, POOL_BUF, D_POOL), lambda b, i: (b, 0, 0)),
        pl.BlockSpec((1, CONV_BUF, D_CONV), lambda b, i: (b, 0, 0)),
        pl.BlockSpec((1, LRU_CONV_BUF, D_LRU), lambda b, i: (b, 0, 0)),
        pl.BlockSpec((1, 1, D_LRU), lambda b, i: (b, 0, 0)),
    )
    scratch = [
        pltpu.VMEM((tm + POOL_OFF, D_POOL), F32),
        pltpu.VMEM((tm + CONV_OFF + SUBLANES, D_CONV), F32),
        pltpu.VMEM((tm + LCONV_OFF, D_LRU), F32),
        pltpu.VMEM((tm, D_CONV), F32),
        pltpu.VMEM((tm, D_LRU), F32),
        pltpu.VMEM((tm, D_LRU), F32),
        pltpu.VMEM((1, D_LRU), F32),
    ]
    y, sp, sc, slc, sh = pl.pallas_call(
        functools.partial(_mixer_seq_kernel, tm=tm),
        grid=(batch, nt),
        in_specs=in_specs,
        out_specs=out_specs,
        out_shape=out_shape,
        scratch_shapes=scratch,
        compiler_params=pltpu.CompilerParams(
            dimension_semantics=("arbitrary", "arbitrary"),
            vmem_limit_bytes=VMEM_LIMIT_BYTES),
        name="mixer_seq",
    )(x, *weights)
    return y, sp, sc, slc, sh.reshape(batch, D_LRU)


def _mixer_step(x, st_pool, st_conv, st_lconv, st_h, p, start_pos):
    batch = x.shape[0]
    weights = _mixer_weight_list(p)
    states = [jnp.transpose(st_pool, (1, 0, 2)), jnp.transpose(st_conv, (1, 0, 2)),
              jnp.transpose(st_lconv, (1, 0, 2)), st_h]
    args = [x] + states + weights
    out_shape = (
        jax.ShapeDtypeStruct((batch, D_MODEL), F32),
        jax.ShapeDtypeStruct((POOL_BUF, batch, D_POOL), F32),
        jax.ShapeDtypeStruct((CONV_BUF, batch, D_CONV), F32),
        jax.ShapeDtypeStruct((LRU_CONV_BUF, batch, D_LRU), F32),
        jax.ShapeDtypeStruct((batch, D_LRU), F32),
    )
    y, sp, sc, slc, sh = pl.pallas_call(
        functools.partial(_mixer_step_kernel, start_pos=start_pos),
        out_shape=out_shape,
        compiler_params=pltpu.CompilerParams(vmem_limit_bytes=VMEM_LIMIT_BYTES),
        name="mixer_step",
    )(*args)
    return (y, jnp.transpose(sp, (1, 0, 2)), jnp.transpose(sc, (1, 0, 2)),
            jnp.transpose(slc, (1, 0, 2)), sh)


def _ffn_kernel(x_ref, wg_ref, wu_ref, wd_ref, g_ref, b_ref, y_ref, *, d_ff):
    x = x_ref[...]
    xb = x.astype(BF16)
    acc = jnp.zeros(x.shape, F32)
    for c0 in range(0, d_ff, FFN_CHUNK):
        h = _silu(_dot(xb, wg_ref[:, c0:c0 + FFN_CHUNK])) * _dot(xb, wu_ref[:, c0:c0 + FFN_CHUNK])
        acc = acc + _dot(h.astype(BF16), wd_ref[c0:c0 + FFN_CHUNK, :])
    y_ref[...] = _layer_norm(ALPHA * x + acc, g_ref[...], b_ref[...])


def _ffn(x, wg, wu, wd, g, b):
    n = x.shape[0]
    tm = min(SEQ_TILE, n)
    d_ff = wg.shape[1]
    weights = [wg, wu, wd, g, b]
    return pl.pallas_call(
        functools.partial(_ffn_kernel, d_ff=d_ff),
        grid=(n // tm,),
        in_specs=[pl.BlockSpec((tm, D_MODEL), lambda i: (i, 0))] + [_const_spec(w.shape) for w in weights],
        out_specs=pl.BlockSpec((tm, D_MODEL), lambda i: (i, 0)),
        out_shape=jax.ShapeDtypeStruct((n, D_MODEL), F32),
        compiler_params=pltpu.CompilerParams(
            dimension_semantics=("arbitrary",), vmem_limit_bytes=VMEM_LIMIT_BYTES),
        name="ffn_dense",
    )(x, *weights)


def _router_kernel(x_ref, wr_ref, idx_ref, gate_ref):
    logits = jnp.dot(x_ref[...], wr_ref[...], preferred_element_type=F32,
                     precision=lax.Precision.HIGHEST)
    lane = lax.broadcasted_iota(jnp.int32, logits.shape, 1)
    neg = jnp.float32(-jnp.inf)
    l1 = jnp.where(lane < N_EXPERTS, logits, neg)
    v1 = jnp.max(l1, axis=-1, keepdims=True)
    i1 = jnp.min(jnp.where(l1 == v1, lane, LANES), axis=-1, keepdims=True)
    l2 = jnp.where(lane == i1, neg, l1)
    v2 = jnp.max(l2, axis=-1, keepdims=True)
    i2 = jnp.min(jnp.where(l2 == v2, lane, LANES), axis=-1, keepdims=True)
    e2 = jnp.exp(v2 - v1)
    den = 1.0 + e2
    idx_ref[...] = jnp.where(lane == 0, i1, jnp.where(lane == 1, i2, 0))
    gate_ref[...] = jnp.where(lane == 0, 1.0 / den, jnp.where(lane == 1, e2 / den, 0.0))


def _router(x, wr_pad):
    n = x.shape[0]
    tm = min(SEQ_TILE, n)
    return pl.pallas_call(
        _router_kernel,
        grid=(n // tm,),
        in_specs=[pl.BlockSpec((tm, D_MODEL), lambda i: (i, 0)), _const_spec(wr_pad.shape)],
        out_specs=(pl.BlockSpec((tm, LANES), lambda i: (i, 0)),
                   pl.BlockSpec((tm, LANES), lambda i: (i, 0))),
        out_shape=(jax.ShapeDtypeStruct((n, LANES), jnp.int32),
                   jax.ShapeDtypeStruct((n, LANES), F32)),
        compiler_params=pltpu.CompilerParams(dimension_semantics=("arbitrary",)),
        name="moe_router",
    )(x, wr_pad)


def _row_copy(src_hbm, idx_ref, dst_ref, sem, r):
    return pltpu.make_async_copy(
        src_hbm.at[pl.ds(idx_ref[0, 0, r], 1)], dst_ref.at[pl.ds(r, 1)], sem)


def _row_gather_start(src_hbm, idx_ref, dst_ref, sem, n_rows):
    def start(r, c):
        _row_copy(src_hbm, idx_ref, dst_ref, sem, r).start()
        return c

    lax.fori_loop(0, n_rows, start, 0, unroll=8)


def _row_gather_wait(src_hbm, idx_ref, dst_ref, sem, n_rows):
    def wait(r, c):
        _row_copy(src_hbm, idx_ref, dst_ref, sem, r).wait()
        return c

    lax.fori_loop(0, n_rows, wait, 0, unroll=8)


def _expert_kernel(te_ref, nu_ref, tok_ref, tok_next_ref, x_hbm, wg_ref, wu_ref, wd_ref, ys_ref,
                   xs_f32, xs_bf, acc, sem, *, tm, tf):
    del te_ref
    i = pl.program_id(0)
    j = pl.program_id(1)
    nf = pl.num_programs(1)
    n_used = nu_ref[0]
    used = i < n_used
    buf = i % 2

    @pl.when(jnp.logical_and(used, jnp.logical_and(i == 0, j == 0)))
    def _():
        _row_gather_start(x_hbm, tok_ref, xs_f32.at[0], sem.at[0], tm)

    @pl.when(jnp.logical_and(used, j == 0))
    def _():
        _row_gather_wait(x_hbm, tok_ref, xs_f32.at[buf], sem.at[buf], tm)
        xs_bf[...] = xs_f32[buf].astype(BF16)

        @pl.when(i + 1 < n_used)
        def _():
            _row_gather_start(x_hbm, tok_next_ref, xs_f32.at[1 - buf], sem.at[1 - buf], tm)

    @pl.when(used)
    def _():
        xb = xs_bf[...]
        part = None
        for c0 in range(0, tf, FFN_CHUNK):
            h = _silu(_dot(xb, wg_ref[:, c0:c0 + FFN_CHUNK])) * _dot(xb, wu_ref[:, c0:c0 + FFN_CHUNK])
            d = _dot(h.astype(BF16), wd_ref[c0:c0 + FFN_CHUNK, :])
            part = d if part is None else part + d

        @pl.when(j == 0)
        def _():
            acc[...] = part

        @pl.when(jnp.logical_and(j > 0, j < nf - 1))
        def _():
            acc[...] += part

        @pl.when(j == nf - 1)
        def _():
            ys_ref[...] = acc[...] + part

    @pl.when(jnp.logical_and(jnp.logical_not(used), j == nf - 1))
    def _():
        ys_ref[...] = jnp.zeros(ys_ref.shape, F32)


def _experts(x, tile_expert, n_used, tok_of_slot, wg, wu, wd, tm):
    n_tiles = tile_expert.shape[0]
    d_e = wg.shape[2]
    tf = EXPERT_F_TILE
    nf = d_e // tf
    assert nf >= 2 and nf * tf == d_e and tf % FFN_CHUNK == 0
    tok3 = tok_of_slot.reshape(n_tiles, 1, tm)

    def jeff(i, j, nu):
        return jnp.where(i < nu[0], j, nf - 1)

    grid_spec = pltpu.PrefetchScalarGridSpec(
        num_scalar_prefetch=2,
        grid=(n_tiles, nf),
        in_specs=[
            pl.BlockSpec((1, 1, tm), lambda i, j, te, nu: (i, 0, 0), memory_space=pltpu.SMEM),
            pl.BlockSpec((1, 1, tm), lambda i, j, te, nu: (jnp.minimum(i + 1, n_tiles - 1), 0, 0),
                         memory_space=pltpu.SMEM),
            pl.BlockSpec(memory_space=pl.ANY),
            pl.BlockSpec((None, D_MODEL, tf), lambda i, j, te, nu: (te[i], 0, jeff(i, j, nu))),
            pl.BlockSpec((None, D_MODEL, tf), lambda i, j, te, nu: (te[i], 0, jeff(i, j, nu))),
            pl.BlockSpec((None, tf, D_MODEL), lambda i, j, te, nu: (te[i], jeff(i, j, nu), 0)),
        ],
        out_specs=pl.BlockSpec((tm, D_MODEL), lambda i, j, te, nu: (i, 0)),
        scratch_shapes=[
            pltpu.VMEM((2, tm, D_MODEL), F32),
            pltpu.VMEM((tm, D_MODEL), BF16),
            pltpu.VMEM((tm, D_MODEL), F32),
            pltpu.SemaphoreType.DMA((2,)),
        ],
    )
    return pl.pallas_call(
        functools.partial(_expert_kernel, tm=tm, tf=tf),
        grid_spec=grid_spec,
        out_shape=jax.ShapeDtypeStruct((n_tiles * tm, D_MODEL), F32),
        compiler_params=pltpu.CompilerParams(
            dimension_semantics=("arbitrary", "arbitrary"), vmem_limit_bytes=VMEM_LIMIT_BYTES),
        name="moe_experts",
    )(tile_expert, n_used, tok3, tok3, x, wg, wu, wd)


def _combine_kernel(s0_ref, s1_ref, s0n_ref, s1n_ref, x_ref, gate_ref, ys_hbm, g_ref, b_ref, y_ref,
                    y0, y1, sem, *, tm):
    i = pl.program_id(0)
    nt = pl.num_programs(0)
    buf = i % 2

    @pl.when(i == 0)
    def _():
        _row_gather_start(ys_hbm, s0_ref, y0.at[0], sem.at[0, 0], tm)
        _row_gather_start(ys_hbm, s1_ref, y1.at[0], sem.at[1, 0], tm)

    @pl.when(i + 1 < nt)
    def _():
        _row_gather_start(ys_hbm, s0n_ref, y0.at[1 - buf], sem.at[0, 1 - buf], tm)
        _row_gather_start(ys_hbm, s1n_ref, y1.at[1 - buf], sem.at[1, 1 - buf], tm)

    _row_gather_wait(ys_hbm, s0_ref, y0.at[buf], sem.at[0, buf], tm)
    _row_gather_wait(ys_hbm, s1_ref, y1.at[buf], sem.at[1, buf], tm)
    x = x_ref[...]
    gates = gate_ref[...]
    f = gates[:, 0:1] * y0[buf] + gates[:, 1:2] * y1[buf]
    y_ref[...] = _layer_norm(ALPHA * x + f, g_ref[...], b_ref[...])


def _combine(x, gates, slots, ys, g, b):
    n = x.shape[0]
    tm = min(SEQ_TILE, n)
    nt = n // tm
    s0 = slots[:, 0].reshape(nt, 1, tm)
    s1 = slots[:, 1].reshape(nt, 1, tm)

    def cur(i):
        return (i, 0, 0)

    def nxt(i):
        return (jnp.minimum(i + 1, nt - 1), 0, 0)

    smem = functools.partial(pl.BlockSpec, (1, 1, tm), memory_space=pltpu.SMEM)
    return pl.pallas_call(
        functools.partial(_combine_kernel, tm=tm),
        grid=(nt,),
        in_specs=[
            smem(cur), smem(cur), smem(nxt), smem(nxt),
            pl.BlockSpec((tm, D_MODEL), lambda i: (i, 0)),
            pl.BlockSpec((tm, LANES), lambda i: (i, 0)),
            pl.BlockSpec(memory_space=pl.ANY),
            _const_spec(g.shape),
            _const_spec(b.shape),
        ],
        out_specs=pl.BlockSpec((tm, D_MODEL), lambda i: (i, 0)),
        out_shape=jax.ShapeDtypeStruct((n, D_MODEL), F32),
        scratch_shapes=[
            pltpu.VMEM((2, tm, D_MODEL), F32),
            pltpu.VMEM((2, tm, D_MODEL), F32),
            pltpu.SemaphoreType.DMA((2, 2)),
        ],
        compiler_params=pltpu.CompilerParams(
            dimension_semantics=("arbitrary",), vmem_limit_bytes=VMEM_LIMIT_BYTES),
        name="moe_combine",
    )(s0, s1, s0, s1, x, gates, ys, g, b)


def _moe(x, wr_pad, wg, wu, wd, g, b, tm):
    n = x.shape[0]
    idx, gates = _router(x, wr_pad)
    flat_e = idx[:, :TOP_K].reshape(-1)
    onehot = (flat_e[:, None] == jnp.arange(N_EXPERTS, dtype=jnp.int32)[None, :]).astype(jnp.int32)
    csum = jnp.cumsum(onehot, axis=0)
    rank = jnp.sum((csum - onehot) * onehot, axis=1)
    counts = csum[-1]
    padded = ((counts + tm - 1) // tm) * tm
    ends = jnp.cumsum(padded)
    slot = (ends - padded)[flat_e] + rank
    n_pairs = TOP_K * n
    n_tiles = n_pairs // tm + N_EXPERTS
    n_used = (ends[-1] // tm).astype(jnp.int32).reshape(1)
    tile_start = jnp.arange(n_tiles, dtype=jnp.int32) * tm
    tile_expert = jnp.sum((tile_start[:, None] >= ends[None, :]).astype(jnp.int32), axis=1)
    last_e = jnp.sum((ends[-1] - 1 >= ends).astype(jnp.int32))
    tile_expert = jnp.minimum(tile_expert, last_e).astype(jnp.int32)
    pair_sorted = jnp.sort(flat_e * n_pairs + jnp.arange(n_pairs, dtype=jnp.int32)) % n_pairs
    slot_ids = jnp.arange(n_tiles * tm, dtype=jnp.int32)
    slot_e = jnp.repeat(tile_expert, tm)
    local = slot_ids - (ends - padded)[slot_e]
    q = (jnp.cumsum(counts) - counts)[slot_e] + local
    valid = jnp.logical_and(local >= 0, local < counts[slot_e])
    tok_of_slot = jnp.where(valid, pair_sorted[jnp.clip(q, 0, n_pairs - 1)] // TOP_K, 0)
    ys = _experts(x, tile_expert, n_used, tok_of_slot, wg, wu, wd, tm)
    return _combine(x, gates, slot.reshape(n, TOP_K), ys, g, b)


def _block_diag(w):
    n, c, _ = w.shape
    eye = jnp.eye(n, dtype=w.dtype)
    return (eye[:, None, :, None] * w[:, :, None, :]).reshape(n * c, n * c)


def _row(v):
    return v.reshape(1, -1)


def kernel(x_prompt, x_sample, state_pool, state_conv, state_lru_conv, state_lru_h, w_in, b_gate, w_pool, pool_scale, conv_w, conv_b, conv_ln_g, conv_ln_b, lru_conv_w, lru_conv_b, lru_wa, lru_ba, lru_wx, lru_bx, lru_lambda, w_br_pool, w_br_conv, w_br_lru, w_out, ln1_g, ln1_b, ln2_g, ln2_b, ffn_w_gate, ffn_w_up, ffn_w_down, moe_router, moe_w_gate, moe_w_up, moe_w_down):
    batch, seq, _ = x_prompt.shape
    dec_batch = x_sample.shape[0]

    layers = []
    for l in range(DEPTH):
        layers.append({
            "w_in": w_in[l].astype(BF16), "b_gate": _row(b_gate[l]),
            "w_pool": w_pool[l].astype(BF16), "pool_scale": _row(pool_scale[l]),
            "conv_w": conv_w[l], "conv_b": _row(conv_b[l]),
            "conv_ln_g": _row(conv_ln_g[l]), "conv_ln_b": _row(conv_ln_b[l]),
            "lru_conv_w": lru_conv_w[l], "lru_conv_b": _row(lru_conv_b[l]),
            "lru_wa": _block_diag(lru_wa[l]).astype(BF16), "lru_ba": _row(lru_ba[l]),
            "lru_wx": _block_diag(lru_wx[l]).astype(BF16), "lru_bx": _row(lru_bx[l]),
            "lru_lambda": _row(lru_lambda[l]),
            "w_br_pool": w_br_pool[l].astype(BF16), "w_br_conv": w_br_conv[l].astype(BF16),
            "w_br_lru": w_br_lru[l].astype(BF16), "w_out": w_out[l].astype(BF16),
            "ln1_g": _row(ln1_g[l]), "ln1_b": _row(ln1_b[l]),
        })
    ffn_w = [(ffn_w_gate[m].astype(BF16), ffn_w_up[m].astype(BF16), ffn_w_down[m].astype(BF16))
             for m in range(ffn_w_gate.shape[0])]
    moe_w = [(jnp.pad(moe_router[m], ((0, 0), (0, LANES - N_EXPERTS))),
              moe_w_gate[m].astype(BF16), moe_w_up[m].astype(BF16), moe_w_down[m].astype(BF16))
             for m in range(moe_router.shape[0])]

    def channel_mixer(l, x, moe_tile):
        g, b = _row(ln2_g[l]), _row(ln2_b[l])
        if l % 2 == 0:
            return _ffn(x, *ffn_w[l // 2], g, b)
        return _moe(x, *moe_w[l // 2], g, b, moe_tile)

    x = x_prompt.reshape(batch * seq, D_MODEL)
    p_states = []
    for l in range(DEPTH):
        x, sp, sc, slc, sh = _mixer_seq(x, layers[l], batch, seq)
        p_states.append((sp, sc, slc, sh))
        x = channel_mixer(l, x, MOE_TILE_SEQ)
    y_prompt = x.reshape(batch, seq, D_MODEL)

    x = x_sample.reshape(dec_batch, D_MODEL)
    s_states = []
    for l in range(DEPTH):
        x, sp, sc, slc, sh = _mixer_step(x, state_pool[l], state_conv[l], state_lru_conv[l],
                                         state_lru_h[l], layers[l], PAST_LEN)
        s_states.append((sp, sc, slc, sh))
        x = channel_mixer(l, x, MOE_TILE_STEP)
    y_sample = x.reshape(dec_batch, 1, D_MODEL)

    def stack(states, k):
        return jnp.stack([s[k] for s in states])

    return (y_prompt, y_sample,
            stack(p_states, 0), stack(p_states, 1), stack(p_states, 2), stack(p_states, 3),
            stack(s_states, 0), stack(s_states, 1), stack(s_states, 2), stack(s_states, 3))
```

```python
import functools

import jax
import jax.numpy as jnp
from jax import lax
from jax.experimental import pallas as pl
from jax.experimental.pallas import tpu as pltpu
from jax.experimental.pallas import tpu_sc as plsc

D_MODEL = 1024
DEPTH = 2
PAST_LEN = 16384
D_POOL = 512
N_POOL_GROUPS = 4
POOL_GROUP = 128
POOL_WINDOWS = (2, 4, 8, 16)
POOL_BUF = 15
D_CONV = 512
CONV_WIDTH = 31
CONV_BUF = 30
D_LRU = 512
N_LRU_BLOCKS = 8
LRU_BLOCK = 64
LRU_CONV_WIDTH = 4
LRU_CONV_BUF = 3
LRU_C = 8.0
N_EXPERTS = 8
TOP_K = 2
ALPHA = (2.0 * DEPTH) ** 0.25
LN_EPS = 1e-5

O_POOL = 0
O_VAL = D_POOL
O_GLU = O_VAL + D_CONV
O_LRU = O_GLU + D_CONV
O_GELU = O_LRU + D_LRU
O_GATE = O_GELU + D_LRU

SUBLANES = 8
LANES = 128
VMEM_LIMIT_BYTES = 56 * 1024 * 1024
SC_CORES = 2
SC_SUBCORES = 16
SC_WINDOW = 128
SC_SUB_ROWS = 32

SEQ_TILE = 512
CONV_ROWS = 64
POOL_OFF = 16
CONV_OFF = 32
LCONV_OFF = 8
FFN_CHUNK = 256
PIECE = 256
EXPERT_F_TILE = 1792
MOE_TILE_SEQ = 512
MOE_TILE_STEP = 128

BF16 = jnp.bfloat16
F32 = jnp.float32


def _sigmoid(x):
    return 0.5 * jnp.tanh(0.5 * x) + 0.5


def _silu(x):
    return x * _sigmoid(x)


def _gelu_tanh(x):
    return x * (0.5 * (1.0 + jnp.tanh(0.7978845608028654 * (x + 0.044715 * (x * x * x)))))


def _softplus(z):
    return jnp.maximum(z, 0.0) + jnp.log1p(jnp.exp(-jnp.abs(z)))


def _layer_norm(x, g, b):
    mu = jnp.mean(x, axis=-1, keepdims=True)
    xc = x - mu
    var = jnp.mean(xc * xc, axis=-1, keepdims=True)
    return xc * lax.rsqrt(var + LN_EPS) * g + b


def _dot(a, b):
    return jnp.dot(a, b, preferred_element_type=F32)


def _lru_coeffs(xc, r, ig, lam, reset):
    log_a = (-LRU_C * r) * _softplus(-lam)
    a = jnp.exp(log_a)
    mult = jnp.sqrt(-jnp.tanh(log_a) * (a * a + 1.0))
    if reset is not None:
        a = jnp.where(reset, 0.0, a)
        mult = jnp.where(reset, 1.0, mult)
    return a, mult * ig * xc


def _mixer_seq_kernel(x_ref, w_in_ref, b_gate_ref, w_pool_ref, pool_scale_ref,
                      conv_w_ref, conv_b_ref, cln_g_ref, cln_b_ref,
                      lconv_w_ref, lconv_b_ref, wa_ref, ba_ref, wx_ref, bx_ref, lam_ref,
                      wbp_ref, wbc_ref, wbl_ref, w_out_ref, ln_g_ref, ln_b_ref,
                      y_ref, sp_ref, sc_ref, slc_ref, sh_ref,
                      pool_ext, conv_ext, lconv_ext, act_buf, a_buf, b_buf, h_carry, gate_buf, gelu_buf,
                      merged_buf, xb_buf, *, tm):
    i = pl.program_id(1)

    @pl.when(i == 0)
    def _():
        pool_ext[0:POOL_OFF, :] = jnp.zeros((POOL_OFF, D_POOL), F32)
        conv_ext[0:CONV_OFF, :] = jnp.zeros((CONV_OFF, D_CONV), F32)
        conv_ext[tm + CONV_OFF:tm + CONV_OFF + SUBLANES, :] = jnp.zeros((SUBLANES, D_CONV), F32)
        lconv_ext[0:LCONV_OFF, :] = jnp.zeros((LCONV_OFF, D_LRU), F32)
        h_carry[...] = jnp.zeros((1, D_LRU), F32)

    xb_buf[...] = x_ref[...].astype(BF16)

    def proj(lo, width):
        return _dot(xb_buf[...], w_in_ref[:, lo:lo + width])

    pos = i * tm + lax.broadcasted_iota(jnp.int32, (tm, 1), 0)

    glu = proj(O_VAL, D_CONV) * _sigmoid(proj(O_GLU, D_CONV))
    conv_ext[CONV_OFF:CONV_OFF + tm, :] = glu

    def pool_piece():
        u_pool = proj(O_POOL, D_POOL)
        pool_ext[POOL_OFF:POOL_OFF + tm, :] = u_pool
        parts = []
        for g, w in enumerate(POOL_WINDOWS):
            lo = g * POOL_GROUP
            acc = u_pool[:, lo:lo + POOL_GROUP]
            for k in range(1, w):
                acc = acc + pool_ext[POOL_OFF - k:POOL_OFF - k + tm, lo:lo + POOL_GROUP]
            inv_cnt = 1.0 / jnp.minimum(w, pos + 1).astype(F32)
            pooled = acc * inv_cnt - u_pool[:, lo:lo + POOL_GROUP]
            parts.append(_dot(pooled.astype(BF16), w_pool_ref[g]))
        mixed = jnp.concatenate(parts, axis=1) * pool_scale_ref[...]
        merged_buf[...] = gate_buf[:, 0:D_MODEL] * _dot(mixed.astype(BF16), wbp_ref[...])
        sp_ref[0] = pool_ext[tm + POOL_OFF - POOL_BUF:tm + POOL_OFF, :]
        pool_ext[0:POOL_OFF, :] = pool_ext[tm:tm + POOL_OFF, :]

    def lru_piece(lo):
        lconv_ext[LCONV_OFF:LCONV_OFF + tm, lo:lo + PIECE] = proj(O_LRU + lo, PIECE)

    def gelu_piece(lo):
        gelu_buf[:, lo:lo + PIECE] = _gelu_tanh(proj(O_GELU + lo, PIECE))

    def gate_piece(lo):
        gate_buf[:, lo:lo + PIECE] = _sigmoid(
            proj(O_GATE + lo, PIECE) + b_gate_ref[:, lo:lo + PIECE])

    pieces = ([functools.partial(gate_piece, lo) for lo in range(0, D_MODEL, PIECE)]
              + [pool_piece]
              + [functools.partial(lru_piece, lo) for lo in range(0, D_LRU, PIECE)]
              + [functools.partial(gate_piece, lo) for lo in range(D_MODEL, 2 * D_MODEL, PIECE)]
              + [functools.partial(gelu_piece, lo) for lo in range(0, D_LRU, PIECE)]
              + [functools.partial(gate_piece, lo) for lo in range(2 * D_MODEL, 3 * D_MODEL, PIECE)])
    n_conv_chunks = tm // CONV_ROWS
    base = CONV_OFF - CONV_BUF
    rows = CONV_ROWS + SUBLANES
    n_units = n_conv_chunks * (D_CONV // LANES)
    unit = 0
    for c in range(n_conv_chunks):
        c0 = c * CONV_ROWS
        for l0 in range(0, D_CONV, LANES):
            acc = jnp.zeros((CONV_ROWS, LANES), F32) + conv_b_ref[:, l0:l0 + LANES]
            for r in range(SUBLANES):
                z = None
                for k in range(CONV_WIDTH):
                    if (base + k) % SUBLANES != r:
                        continue
                    q8 = base + k - r
                    t = (conv_ext[c0 + q8:c0 + q8 + rows, l0:l0 + LANES]
                         * conv_w_ref[k:k + 1, l0:l0 + LANES])
                    z = t if z is None else z + t
                acc = acc + z[r:r + CONV_ROWS, :]
            act_buf[c0:c0 + CONV_ROWS, l0:l0 + LANES] = acc
            for piece in pieces[unit * len(pieces) // n_units:(unit + 1) * len(pieces) // n_units]:
                piece()
            unit += 1
        act_buf[c0:c0 + CONV_ROWS, :] = _silu(_layer_norm(
            act_buf[c0:c0 + CONV_ROWS, :], cln_g_ref[...], cln_b_ref[...]))
    merged_buf[...] += gate_buf[:, D_MODEL:2 * D_MODEL] * _dot(
        act_buf[...].astype(BF16), wbc_ref[...])
    sc_ref[0] = conv_ext[tm + CONV_OFF - CONV_BUF:tm + CONV_OFF, :]
    conv_ext[0:CONV_OFF, :] = conv_ext[tm:tm + CONV_OFF, :]

    base = LCONV_OFF - LRU_CONV_BUF
    xc = jnp.zeros((tm, D_LRU), F32) + lconv_b_ref[...]
    for k in range(LRU_CONV_WIDTH):
        xc = xc + lconv_ext[base + k:base + k + tm, :] * lconv_w_ref[k:k + 1, :]
    slc_ref[0] = lconv_ext[tm + LCONV_OFF - LRU_CONV_BUF:tm + LCONV_OFF, :]
    lconv_ext[0:LCONV_OFF, :] = lconv_ext[tm:tm + LCONV_OFF, :]
    xcb = xc.astype(BF16)
    r = _sigmoid(_dot(xcb, wa_ref[...]) + ba_ref[...])
    ig = _sigmoid(_dot(xcb, wx_ref[...]) + bx_ref[...])
    a, b = _lru_coeffs(xc, r, ig, lam_ref[...], pos == 0)
    a_buf[...] = a
    b_buf[...] = b

    row8 = lax.broadcasted_iota(jnp.int32, (SUBLANES, D_LRU), 0)

    def scan_block(j, h):
        r0 = pl.multiple_of(j * SUBLANES, SUBLANES)
        a8 = a_buf[pl.ds(r0, SUBLANES), :]
        b8 = b_buf[pl.ds(r0, SUBLANES), :]
        for k in (1, 2, 4):
            a_s = pltpu.roll(a8, k, 0)
            b_s = pltpu.roll(b8, k, 0)
            m = row8 >= k
            b8 = jnp.where(m, a8 * b_s + b8, b8)
            a8 = jnp.where(m, a8 * a_s, a8)
        h8 = a8 * h + b8
        a_buf[pl.ds(r0, SUBLANES), :] = h8
        return h8[SUBLANES - 1:SUBLANES, :]

    h_last = lax.fori_loop(0, tm // SUBLANES, scan_block, h_carry[...], unroll=2)
    h_carry[...] = h_last
    sh_ref[0] = h_last
    hg = a_buf[...] * gelu_buf[...]
    merged = merged_buf[...] + gate_buf[:, 2 * D_MODEL:3 * D_MODEL] * _dot(
        hg.astype(BF16), wbl_ref[...])

    m_out = _dot(merged.astype(BF16), w_out_ref[...])
    y_ref[...] = _layer_norm(ALPHA * x_ref[...] + m_out, ln_g_ref[...], ln_b_ref[...])


def _mixer_step_kernel(x_ref, st_pool_ref, st_conv_ref, st_lconv_ref, st_h_ref,
                       w_in_ref, b_gate_ref, w_pool_ref, pool_scale_ref,
                       conv_w_ref, conv_b_ref, cln_g_ref, cln_b_ref,
                       lconv_w_ref, lconv_b_ref, wa_ref, ba_ref, wx_ref, bx_ref, lam_ref,
                       wbp_ref, wbc_ref, wbl_ref, w_out_ref, ln_g_ref, ln_b_ref,
                       y_ref, sp_ref, sc_ref, slc_ref, sh_ref, *, start_pos):
    x = x_ref[...]
    xb = x.astype(BF16)

    def proj(lo, width):
        return _dot(xb, w_in_ref[:, lo:lo + width])

    def gate(n):
        lo = n * D_MODEL
        return _sigmoid(proj(O_GATE + lo, D_MODEL) + b_gate_ref[:, lo:lo + D_MODEL])

    u_pool = proj(O_POOL, D_POOL)
    parts = []
    for g, w in enumerate(POOL_WINDOWS):
        lo = g * POOL_GROUP
        acc = u_pool[:, lo:lo + POOL_GROUP]
        for k in range(1, w):
            acc = acc + st_pool_ref[POOL_BUF - k, :, lo:lo + POOL_GROUP]
        pooled = acc * (1.0 / min(w, start_pos + 1)) - u_pool[:, lo:lo + POOL_GROUP]
        parts.append(_dot(pooled.astype(BF16), w_pool_ref[g]))
    mixed = jnp.concatenate(parts, axis=1) * pool_scale_ref[...]
    merged = gate(0) * _dot(mixed.astype(BF16), wbp_ref[...])
    for k in range(POOL_BUF - 1):
        sp_ref[k] = st_pool_ref[k + 1]
    sp_ref[POOL_BUF - 1] = u_pool

    glu = proj(O_VAL, D_CONV) * _sigmoid(proj(O_GLU, D_CONV))
    acc = glu * conv_w_ref[CONV_BUF:CONV_BUF + 1, :] + conv_b_ref[...]
    for k in range(CONV_BUF):
        acc = acc + st_conv_ref[k] * conv_w_ref[k:k + 1, :]
    act = _silu(_layer_norm(acc, cln_g_ref[...], cln_b_ref[...]))
    merged = merged + gate(1) * _dot(act.astype(BF16), wbc_ref[...])
    for k in range(CONV_BUF - 1):
        sc_ref[k] = st_conv_ref[k + 1]
    sc_ref[CONV_BUF - 1] = glu

    u_lru = proj(O_LRU, D_LRU)
    xc = u_lru * lconv_w_ref[LRU_CONV_BUF:LRU_CONV_BUF + 1, :] + lconv_b_ref[...]
    for k in range(LRU_CONV_BUF):
        xc = xc + st_lconv_ref[k] * lconv_w_ref[k:k + 1, :]
    for k in range(LRU_CONV_BUF - 1):
        slc_ref[k] = st_lconv_ref[k + 1]
    slc_ref[LRU_CONV_BUF - 1] = u_lru
    xcb = xc.astype(BF16)
    r = _sigmoid(_dot(xcb, wa_ref[...]) + ba_ref[...])
    ig = _sigmoid(_dot(xcb, wx_ref[...]) + bx_ref[...])
    reset = jnp.full(xc.shape, True) if start_pos == 0 else None
    a, b = _lru_coeffs(xc, r, ig, lam_ref[...], reset)
    h = a * st_h_ref[...] + b
    sh_ref[...] = h
    hg = h * _gelu_tanh(proj(O_GELU, D_LRU))
    merged = merged + gate(2) * _dot(hg.astype(BF16), wbl_ref[...])

    m_out = _dot(merged.astype(BF16), w_out_ref[...])
    y_ref[...] = _layer_norm(ALPHA * x + m_out, ln_g_ref[...], ln_b_ref[...])


def _const_spec(shape):
    nd = len(shape)
    return pl.BlockSpec(shape, lambda *_: (0,) * nd, pipeline_mode=pl.Buffered(1))


def _mixer_weight_list(p):
    return [p["w_in"], p["b_gate"], p["w_pool"], p["pool_scale"], p["conv_w"], p["conv_b"],
            p["conv_ln_g"], p["conv_ln_b"], p["lru_conv_w"], p["lru_conv_b"], p["lru_wa"],
            p["lru_ba"], p["lru_wx"], p["lru_bx"], p["lru_lambda"], p["w_br_pool"],
            p["w_br_conv"], p["w_br_lru"], p["w_out"], p["ln1_g"], p["ln1_b"]]


def _mixer_seq(x, p, batch, seq):
    tm = min(SEQ_TILE, seq)
    nt = seq // tm
    weights = _mixer_weight_list(p)
    in_specs = [pl.BlockSpec((tm, D_MODEL), lambda b, i: (b * nt + i, 0))]
    in_specs += [_const_spec(w.shape) for w in weights]
    out_shape = (
        jax.ShapeDtypeStruct((batch * seq, D_MODEL), F32),
        jax.ShapeDtypeStruct((batch, POOL_BUF, D_POOL), F32),
        jax.ShapeDtypeStruct((batch, CONV_BUF, D_CONV), F32),
        jax.ShapeDtypeStruct((batch, LRU_CONV_BUF, D_LRU), F32),
        jax.ShapeDtypeStruct((batch, 1, D_LRU), F32),
    )
    out_specs = (
        pl.BlockSpec((tm, D_MODEL), lambda b, i: (b * nt + i, 0)),
        pl.BlockSpec((1, POOL_BUF, D_POOL), lambda b, i: (b, 0, 0)),
        pl.BlockSpec((1, CONV_BUF, D_CONV), lambda b, i: (b, 0, 0)),
        pl.BlockSpec((1, LRU_CONV_BUF, D_LRU), lambda b, i: (b, 0, 0)),
        pl.BlockSpec((1, 1, D_LRU), lambda b, i: (b, 0, 0)),
    )
    scratch = [
        pltpu.VMEM((tm + POOL_OFF, D_POOL), F32),
        pltpu.VMEM((tm + CONV_OFF + SUBLANES, D_CONV), F32),
        pltpu.VMEM((tm + LCONV_OFF, D_LRU), F32),
        pltpu.VMEM((tm, D_CONV), F32),
        pltpu.VMEM((tm, D_LRU), F32),
        pltpu.VMEM((tm, D_LRU), F32),
        pltpu.VMEM((1, D_LRU), F32),
        pltpu.VMEM((tm, 3 * D_MODEL), F32),
        pltpu.VMEM((tm, D_LRU), F32),
        pltpu.VMEM((tm, D_MODEL), F32),
        pltpu.VMEM((tm, D_MODEL), BF16),
    ]
    y, sp, sc, slc, sh = pl.pallas_call(
        functools.partial(_mixer_seq_kernel, tm=tm),
        grid=(batch, nt),
        in_specs=in_specs,
        out_specs=out_specs,
        out_shape=out_shape,
        scratch_shapes=scratch,
        compiler_params=pltpu.CompilerParams(
            dimension_semantics=("arbitrary", "arbitrary"),
            vmem_limit_bytes=VMEM_LIMIT_BYTES),
        name="mixer_seq",
    )(x, *weights)
    return y, sp, sc, slc, sh.reshape(batch, D_LRU)


def _mixer_step(x, st_pool, st_conv, st_lconv, st_h, p, start_pos):
    batch = x.shape[0]
    weights = _mixer_weight_list(p)
    states = [jnp.transpose(st_pool, (1, 0, 2)), jnp.transpose(st_conv, (1, 0, 2)),
              jnp.transpose(st_lconv, (1, 0, 2)), st_h]
    args = [x] + states + weights
    out_shape = (
        jax.ShapeDtypeStruct((batch, D_MODEL), F32),
        jax.ShapeDtypeStruct((POOL_BUF, batch, D_POOL), F32),
        jax.ShapeDtypeStruct((CONV_BUF, batch, D_CONV), F32),
        jax.ShapeDtypeStruct((LRU_CONV_BUF, batch, D_LRU), F32),
        jax.ShapeDtypeStruct((batch, D_LRU), F32),
    )
    y, sp, sc, slc, sh = pl.pallas_call(
        functools.partial(_mixer_step_kernel, start_pos=start_pos),
        out_shape=out_shape,
        compiler_params=pltpu.CompilerParams(vmem_limit_bytes=VMEM_LIMIT_BYTES),
        name="mixer_step",
    )(*args)
    return (y, jnp.transpose(sp, (1, 0, 2)), jnp.transpose(sc, (1, 0, 2)),
            jnp.transpose(slc, (1, 0, 2)), sh)


def _ffn_kernel(x_ref, wg_ref, wu_ref, wd_ref, g_ref, b_ref, y_ref, *, d_ff):
    x = x_ref[...]
    xb = x.astype(BF16)
    acc = jnp.zeros(x.shape, F32)
    for c0 in range(0, d_ff, FFN_CHUNK):
        h = _silu(_dot(xb, wg_ref[:, c0:c0 + FFN_CHUNK])) * _dot(xb, wu_ref[:, c0:c0 + FFN_CHUNK])
        acc = acc + _dot(h.astype(BF16), wd_ref[c0:c0 + FFN_CHUNK, :])
    y_ref[...] = _layer_norm(ALPHA * x + acc, g_ref[...], b_ref[...])


def _ffn(x, wg, wu, wd, g, b):
    n = x.shape[0]
    tm = min(SEQ_TILE, n)
    d_ff = wg.shape[1]
    weights = [wg, wu, wd, g, b]
    return pl.pallas_call(
        functools.partial(_ffn_kernel, d_ff=d_ff),
        grid=(n // tm,),
        in_specs=[pl.BlockSpec((tm, D_MODEL), lambda i: (i, 0))] + [_const_spec(w.shape) for w in weights],
        out_specs=pl.BlockSpec((tm, D_MODEL), lambda i: (i, 0)),
        out_shape=jax.ShapeDtypeStruct((n, D_MODEL), F32),
        compiler_params=pltpu.CompilerParams(
            dimension_semantics=("arbitrary",), vmem_limit_bytes=VMEM_LIMIT_BYTES),
        name="ffn_dense",
    )(x, *weights)


def _router_kernel(x_ref, wr_ref, idx_ref, gate_ref):
    logits = jnp.dot(x_ref[...], wr_ref[...], preferred_element_type=F32,
                     precision=lax.Precision.HIGHEST)
    lane = lax.broadcasted_iota(jnp.int32, logits.shape, 1)
    neg = jnp.float32(-jnp.inf)
    l1 = jnp.where(lane < N_EXPERTS, logits, neg)
    v1 = jnp.max(l1, axis=-1, keepdims=True)
    i1 = jnp.min(jnp.where(l1 == v1, lane, LANES), axis=-1, keepdims=True)
    l2 = jnp.where(lane == i1, neg, l1)
    v2 = jnp.max(l2, axis=-1, keepdims=True)
    i2 = jnp.min(jnp.where(l2 == v2, lane, LANES), axis=-1, keepdims=True)
    e2 = jnp.exp(v2 - v1)
    den = 1.0 + e2
    idx_ref[...] = jnp.where(lane == 0, i1, jnp.where(lane == 1, i2, 0))
    gate_ref[...] = jnp.where(lane == 0, 1.0 / den, jnp.where(lane == 1, e2 / den, 0.0))


def _router(x, wr_pad):
    n = x.shape[0]
    tm = min(SEQ_TILE, n)
    return pl.pallas_call(
        _router_kernel,
        grid=(n // tm,),
        in_specs=[pl.BlockSpec((tm, D_MODEL), lambda i: (i, 0)), _const_spec(wr_pad.shape)],
        out_specs=(pl.BlockSpec((tm, LANES), lambda i: (i, 0)),
                   pl.BlockSpec((tm, LANES), lambda i: (i, 0))),
        out_shape=(jax.ShapeDtypeStruct((n, LANES), jnp.int32),
                   jax.ShapeDtypeStruct((n, LANES), F32)),
        compiler_params=pltpu.CompilerParams(dimension_semantics=("arbitrary",)),
        name="moe_router",
    )(x, wr_pad)


def _sc_row_gather(src, idx):
    n_idx = idx.shape[0]
    assert n_idx % SC_WINDOW == 0
    n_win = n_idx // SC_WINDOW
    n_workers = SC_CORES * SC_SUBCORES
    n_sub = SC_WINDOW // SC_SUB_ROWS
    mesh = plsc.VectorSubcoreMesh(core_axis_name="c", subcore_axis_name="s",
                                  num_cores=SC_CORES, num_subcores=SC_SUBCORES)

    def body(src_hbm, idx_hbm, out_hbm, idx_v, buf, g0, g1, w0, w1):
        gsem = (g0, g1)
        wsem = (w0, w1)
        worker = lax.axis_index("c") * SC_SUBCORES + lax.axis_index("s")

        @pl.loop(0, pl.cdiv(n_win, n_workers))
        def _(t):
            win = worker + t * n_workers

            @pl.when(win < n_win)
            def _():
                base = win * SC_WINDOW
                pltpu.sync_copy(idx_hbm.at[:, pl.ds(base, SC_WINDOW)], idx_v)

                def fetch(q):
                    return pltpu.async_copy(
                        src_hbm.at[idx_v.at[0, pl.ds(q * SC_SUB_ROWS, SC_SUB_ROWS)]],
                        buf.at[q % 2], gsem[q % 2])

                def put(q):
                    return pltpu.async_copy(
                        buf.at[q % 2], out_hbm.at[pl.ds(base + q * SC_SUB_ROWS, SC_SUB_ROWS)],
                        wsem[q % 2])

                fetches = {0: fetch(0)}
                puts = {}
                for q in range(n_sub):
                    if q + 1 < n_sub:
                        if q >= 1:
                            puts[q - 1].wait()
                        fetches[q + 1] = fetch(q + 1)
                    fetches[q].wait()
                    puts[q] = put(q)
                for q in range(max(0, n_sub - 2), n_sub):
                    puts[q].wait()

    return pl.kernel(
        body,
        out_type=jax.ShapeDtypeStruct((n_idx, D_MODEL), src.dtype),
        mesh=mesh,
        scratch_types=[pltpu.VMEM((1, SC_WINDOW), jnp.int32),
                       pltpu.VMEM((2, SC_SUB_ROWS, D_MODEL), src.dtype),
                       pltpu.SemaphoreType.DMA, pltpu.SemaphoreType.DMA,
                       pltpu.SemaphoreType.DMA, pltpu.SemaphoreType.DMA],
    )(src, idx.reshape(1, n_idx))


def _expert_kernel(te_ref, nu_ref, xs_ref, wg_ref, wu_ref, wd_ref, ys_ref, xs_bf, acc, *, tf, nf):
    del te_ref
    i = pl.program_id(0)
    j = pl.program_id(1)
    used = i < nu_ref[0]

    @pl.when(jnp.logical_and(used, j == 0))
    def _():
        xs_bf[...] = xs_ref[...].astype(BF16)

    @pl.when(used)
    def _():
        xb = xs_bf[...]
        part = None
        for c0 in range(0, tf, FFN_CHUNK):
            h = _silu(_dot(xb, wg_ref[:, c0:c0 + FFN_CHUNK])) * _dot(xb, wu_ref[:, c0:c0 + FFN_CHUNK])
            d = _dot(h.astype(BF16), wd_ref[c0:c0 + FFN_CHUNK, :])
            part = d if part is None else part + d

        @pl.when(j == 0)
        def _():
            acc[...] = part

        @pl.when(jnp.logical_and(j > 0, j < nf - 1))
        def _():
            acc[...] += part

        @pl.when(j == nf - 1)
        def _():
            ys_ref[...] = acc[...] + part

    @pl.when(jnp.logical_and(jnp.logical_not(used), j == nf - 1))
    def _():
        ys_ref[...] = jnp.zeros(ys_ref.shape, F32)


def _experts(xs, tile_expert, n_used, wg, wu, wd, tm):
    n_tiles = tile_expert.shape[0]
    d_e = wg.shape[2]
    tf = EXPERT_F_TILE
    nf = d_e // tf
    assert nf >= 2 and nf * tf == d_e and tf % FFN_CHUNK == 0

    def jeff(i, j, nu):
        return jnp.where(i < nu[0], j, nf - 1)

    grid_spec = pltpu.PrefetchScalarGridSpec(
        num_scalar_prefetch=2,
        grid=(n_tiles, nf),
        in_specs=[
            pl.BlockSpec((tm, D_MODEL), lambda i, j, te, nu: (jnp.minimum(i, nu[0] - 1), 0)),
            pl.BlockSpec((None, D_MODEL, tf), lambda i, j, te, nu: (te[i], 0, jeff(i, j, nu))),
            pl.BlockSpec((None, D_MODEL, tf), lambda i, j, te, nu: (te[i], 0, jeff(i, j, nu))),
            pl.BlockSpec((None, tf, D_MODEL), lambda i, j, te, nu: (te[i], jeff(i, j, nu), 0)),
        ],
        out_specs=pl.BlockSpec((tm, D_MODEL), lambda i, j, te, nu: (i, 0)),
        scratch_shapes=[
            pltpu.VMEM((tm, D_MODEL), BF16),
            pltpu.VMEM((tm, D_MODEL), F32),
        ],
    )
    return pl.pallas_call(
        functools.partial(_expert_kernel, tf=tf, nf=nf),
        grid_spec=grid_spec,
        out_shape=jax.ShapeDtypeStruct((n_tiles * tm, D_MODEL), F32),
        compiler_params=pltpu.CompilerParams(
            dimension_semantics=("arbitrary", "arbitrary"), vmem_limit_bytes=VMEM_LIMIT_BYTES),
        name="moe_experts",
    )(tile_expert, n_used, xs, wg, wu, wd)


def _combine_kernel(x_ref, gate_ref, y0_ref, y1_ref, g_ref, b_ref, y_ref):
    gates = gate_ref[...]
    f = gates[:, 0:1] * y0_ref[...] + gates[:, 1:2] * y1_ref[...]
    y_ref[...] = _layer_norm(ALPHA * x_ref[...] + f, g_ref[...], b_ref[...])


def _combine(x, gates, y01, g, b):
    n = x.shape[0]
    tm = min(SEQ_TILE, n)
    nt = n // tm
    return pl.pallas_call(
        _combine_kernel,
        grid=(nt,),
        in_specs=[
            pl.BlockSpec((tm, D_MODEL), lambda i: (i, 0)),
            pl.BlockSpec((tm, LANES), lambda i: (i, 0)),
            pl.BlockSpec((tm, D_MODEL), lambda i: (i, 0)),
            pl.BlockSpec((tm, D_MODEL), lambda i: (i + nt, 0)),
            _const_spec(g.shape),
            _const_spec(b.shape),
        ],
        out_specs=pl.BlockSpec((tm, D_MODEL), lambda i: (i, 0)),
        out_shape=jax.ShapeDtypeStruct((n, D_MODEL), F32),
        compiler_params=pltpu.CompilerParams(
            dimension_semantics=("arbitrary",), vmem_limit_bytes=VMEM_LIMIT_BYTES),
        name="moe_combine",
    )(x, gates, y01, y01, g, b)


def _moe(x, wr_pad, wg, wu, wd, g, b, tm):
    n = x.shape[0]
    idx, gates = _router(x, wr_pad)
    flat_e = idx[:, :TOP_K].reshape(-1)
    onehot = (flat_e[:, None] == jnp.arange(N_EXPERTS, dtype=jnp.int32)[None, :]).astype(jnp.int32)
    csum = jnp.cumsum(onehot, axis=0)
    rank = jnp.sum((csum - onehot) * onehot, axis=1)
    counts = csum[-1]
    padded = ((counts + tm - 1) // tm) * tm
    ends = jnp.cumsum(padded)
    slot = (ends - padded)[flat_e] + rank
    n_pairs = TOP_K * n
    n_tiles = n_pairs // tm + N_EXPERTS
    n_used = (ends[-1] // tm).astype(jnp.int32).reshape(1)
    tile_start = jnp.arange(n_tiles, dtype=jnp.int32) * tm
    tile_expert = jnp.sum((tile_start[:, None] >= ends[None, :]).astype(jnp.int32), axis=1)
    last_e = jnp.sum((ends[-1] - 1 >= ends).astype(jnp.int32))
    tile_expert = jnp.minimum(tile_expert, last_e).astype(jnp.int32)
    pair_sorted = jnp.sort(flat_e * n_pairs + jnp.arange(n_pairs, dtype=jnp.int32)) % n_pairs
    slot_ids = jnp.arange(n_tiles * tm, dtype=jnp.int32)
    slot_e = jnp.repeat(tile_expert, tm)
    local = slot_ids - (ends - padded)[slot_e]
    q = (jnp.cumsum(counts) - counts)[slot_e] + local
    valid = jnp.logical_and(local >= 0, local < counts[slot_e])
    tok_of_slot = jnp.where(valid, pair_sorted[jnp.clip(q, 0, n_pairs - 1)] // TOP_K, 0)
    xs = _sc_row_gather(x, tok_of_slot)
    ys = _experts(xs, tile_expert, n_used, wg, wu, wd, tm)
    y01 = _sc_row_gather(ys, jnp.transpose(slot.reshape(n, TOP_K)).reshape(-1))
    return _combine(x, gates, y01, g, b)


def _block_diag(w):
    n, c, _ = w.shape
    eye = jnp.eye(n, dtype=w.dtype)
    return (eye[:, None, :, None] * w[:, :, None, :]).reshape(n * c, n * c)


def _row(v):
    return v.reshape(1, -1)


def kernel(x_prompt, x_sample, state_pool, state_conv, state_lru_conv, state_lru_h, w_in, b_gate, w_pool, pool_scale, conv_w, conv_b, conv_ln_g, conv_ln_b, lru_conv_w, lru_conv_b, lru_wa, lru_ba, lru_wx, lru_bx, lru_lambda, w_br_pool, w_br_conv, w_br_lru, w_out, ln1_g, ln1_b, ln2_g, ln2_b, ffn_w_gate, ffn_w_up, ffn_w_down, moe_router, moe_w_gate, moe_w_up, moe_w_down):
    batch, seq, _ = x_prompt.shape
    dec_batch = x_sample.shape[0]

    layers = []
    for l in range(DEPTH):
        layers.append({
            "w_in": w_in[l].astype(BF16), "b_gate": _row(b_gate[l]),
            "w_pool": w_pool[l].astype(BF16), "pool_scale": _row(pool_scale[l]),
            "conv_w": conv_w[l], "conv_b": _row(conv_b[l]),
            "conv_ln_g": _row(conv_ln_g[l]), "conv_ln_b": _row(conv_ln_b[l]),
            "lru_conv_w": lru_conv_w[l], "lru_conv_b": _row(lru_conv_b[l]),
            "lru_wa": _block_diag(lru_wa[l]).astype(BF16), "lru_ba": _row(lru_ba[l]),
            "lru_wx": _block_diag(lru_wx[l]).astype(BF16), "lru_bx": _row(lru_bx[l]),
            "lru_lambda": _row(lru_lambda[l]),
            "w_br_pool": w_br_pool[l].astype(BF16), "w_br_conv": w_br_conv[l].astype(BF16),
            "w_br_lru": w_br_lru[l].astype(BF16), "w_out": w_out[l].astype(BF16),
            "ln1_g": _row(ln1_g[l]), "ln1_b": _row(ln1_b[l]),
        })
    ffn_w = [(ffn_w_gate[m].astype(BF16), ffn_w_up[m].astype(BF16), ffn_w_down[m].astype(BF16))
             for m in range(ffn_w_gate.shape[0])]
    moe_w = [(jnp.pad(moe_router[m], ((0, 0), (0, LANES - N_EXPERTS))),
              moe_w_gate[m].astype(BF16), moe_w_up[m].astype(BF16), moe_w_down[m].astype(BF16))
             for m in range(moe_router.shape[0])]

    def channel_mixer(l, x, moe_tile):
        g, b = _row(ln2_g[l]), _row(ln2_b[l])
        if l % 2 == 0:
            return _ffn(x, *ffn_w[l // 2], g, b)
        return _moe(x, *moe_w[l // 2], g, b, moe_tile)

    x = x_prompt.reshape(batch * seq, D_MODEL)
    p_states = []
    for l in range(DEPTH):
        x, sp, sc, slc, sh = _mixer_seq(x, layers[l], batch, seq)
        p_states.append((sp, sc, slc, sh))
        x = channel_mixer(l, x, MOE_TILE_SEQ)
    y_prompt = x.reshape(batch, seq, D_MODEL)

    x = x_sample.reshape(dec_batch, D_MODEL)
    s_states = []
    for l in range(DEPTH):
        x, sp, sc, slc, sh = _mixer_step(x, state_pool[l], state_conv[l], state_lru_conv[l],
                                         state_lru_h[l], layers[l], PAST_LEN)
        s_states.append((sp, sc, slc, sh))
        x = channel_mixer(l, x, MOE_TILE_STEP)
    y_sample = x.reshape(dec_batch, 1, D_MODEL)

    def stack(states, k):
        return jnp.stack([s[k] for s in states])

    return (y_prompt, y_sample,
            stack(p_states, 0), stack(p_states, 1), stack(p_states, 2), stack(p_states, 3),
            stack(s_states, 0), stack(s_states, 1), stack(s_states, 2), stack(s_states, 3))
```

```python
import functools

import jax
import jax.numpy as jnp
from jax import lax
from jax.experimental import pallas as pl
from jax.experimental.pallas import tpu as pltpu
from jax.experimental.pallas import tpu_sc as plsc

D_MODEL = 1024
DEPTH = 2
PAST_LEN = 16384
D_POOL = 512
N_POOL_GROUPS = 4
POOL_GROUP = 128
POOL_WINDOWS = (2, 4, 8, 16)
POOL_BUF = 15
D_CONV = 512
CONV_WIDTH = 31
CONV_BUF = 30
D_LRU = 512
N_LRU_BLOCKS = 8
LRU_BLOCK = 64
LRU_CONV_WIDTH = 4
LRU_CONV_BUF = 3
LRU_C = 8.0
N_EXPERTS = 8
TOP_K = 2
ALPHA = (2.0 * DEPTH) ** 0.25
LN_EPS = 1e-5

O_POOL = 0
O_VAL = D_POOL
O_GLU = O_VAL + D_CONV
O_LRU = O_GLU + D_CONV
O_GELU = O_LRU + D_LRU
O_GATE = O_GELU + D_LRU

SUBLANES = 8
LANES = 128
VMEM_LIMIT_BYTES = 56 * 1024 * 1024
SC_CORES = 2
SC_SUBCORES = 16
SC_WINDOW = 128
SC_SUB_ROWS = 32

SEQ_TILE = 512
CONV_ROWS = 64
POOL_OFF = 16
CONV_OFF = 32
LCONV_OFF = 8
FFN_CHUNK = 256
PIECE = 256
STEP_COLS = 512
EXPERT_F_TILE = 1792
MOE_TILE_SEQ = 512
MOE_TILE_STEP = 128

BF16 = jnp.bfloat16
F32 = jnp.float32


def _sigmoid(x):
    return 0.5 * jnp.tanh(0.5 * x) + 0.5


def _silu(x):
    return x * _sigmoid(x)


def _gelu_tanh(x):
    return x * (0.5 * (1.0 + jnp.tanh(0.7978845608028654 * (x + 0.044715 * (x * x * x)))))


def _softplus(z):
    return jnp.maximum(z, 0.0) + jnp.log1p(jnp.exp(-jnp.abs(z)))


def _layer_norm(x, g, b):
    mu = jnp.mean(x, axis=-1, keepdims=True)
    xc = x - mu
    var = jnp.mean(xc * xc, axis=-1, keepdims=True)
    return xc * lax.rsqrt(var + LN_EPS) * g + b


def _dot(a, b):
    return jnp.dot(a, b, preferred_element_type=F32)


def _mm(a, w):
    return _dot(a.astype(BF16), w)


def _mm_hi(a, w):
    return jnp.dot(a, w, preferred_element_type=F32, precision=lax.Precision.HIGHEST)


def _lru_coeffs(xc, r, ig, lam, reset):
    log_a = (-LRU_C * r) * _softplus(-lam)
    a = jnp.exp(log_a)
    mult = jnp.sqrt(-jnp.tanh(log_a) * (a * a + 1.0))
    if reset is not None:
        a = jnp.where(reset, 0.0, a)
        mult = jnp.where(reset, 1.0, mult)
    return a, mult * ig * xc


def _mixer_seq_kernel(x_ref, w_in_ref, b_gate_ref, w_pool_ref, pool_scale_ref,
                      conv_w_ref, conv_b_ref, cln_g_ref, cln_b_ref,
                      lconv_w_ref, lconv_b_ref, wa_ref, ba_ref, wx_ref, bx_ref, lam_ref,
                      wbp_ref, wbc_ref, wbl_ref, w_out_ref, ln_g_ref, ln_b_ref, *rest, tm, with_router):
    if with_router:
        wr_ref, rest = rest[0], rest[1:]
        idx_ref, gate_ref, rest = rest[5], rest[6], rest[:5] + rest[7:]
    (y_ref, sp_ref, sc_ref, slc_ref, sh_ref,
     pool_ext, conv_ext, lconv_ext, act_buf, a_buf, b_buf, h_carry, gate_buf, gelu_buf,
     merged_buf, xb_buf) = rest
    i = pl.program_id(1)

    @pl.when(i == 0)
    def _():
        pool_ext[0:POOL_OFF, :] = jnp.zeros((POOL_OFF, D_POOL), F32)
        conv_ext[0:CONV_OFF, :] = jnp.zeros((CONV_OFF, D_CONV), F32)
        conv_ext[tm + CONV_OFF:tm + CONV_OFF + SUBLANES, :] = jnp.zeros((SUBLANES, D_CONV), F32)
        lconv_ext[0:LCONV_OFF, :] = jnp.zeros((LCONV_OFF, D_LRU), F32)
        h_carry[...] = jnp.zeros((1, D_LRU), F32)

    xb_buf[...] = x_ref[...].astype(BF16)

    def proj(lo, width):
        return _dot(xb_buf[...], w_in_ref[:, lo:lo + width])

    pos = i * tm + lax.broadcasted_iota(jnp.int32, (tm, 1), 0)

    glu = proj(O_VAL, D_CONV) * _sigmoid(proj(O_GLU, D_CONV))
    conv_ext[CONV_OFF:CONV_OFF + tm, :] = glu

    def pool_piece():
        u_pool = proj(O_POOL, D_POOL)
        pool_ext[POOL_OFF:POOL_OFF + tm, :] = u_pool
        parts = []
        for g, w in enumerate(POOL_WINDOWS):
            lo = g * POOL_GROUP
            acc = u_pool[:, lo:lo + POOL_GROUP]
            for k in range(1, w):
                acc = acc + pool_ext[POOL_OFF - k:POOL_OFF - k + tm, lo:lo + POOL_GROUP]
            inv_cnt = 1.0 / jnp.minimum(w, pos + 1).astype(F32)
            pooled = acc * inv_cnt - u_pool[:, lo:lo + POOL_GROUP]
            parts.append(_dot(pooled.astype(BF16), w_pool_ref[g]))
        mixed = jnp.concatenate(parts, axis=1) * pool_scale_ref[...]
        merged_buf[...] = gate_buf[:, 0:D_MODEL] * _dot(mixed.astype(BF16), wbp_ref[...])
        sp_ref[0] = pool_ext[tm + POOL_OFF - POOL_BUF:tm + POOL_OFF, :]
        pool_ext[0:POOL_OFF, :] = pool_ext[tm:tm + POOL_OFF, :]

    def lru_piece(lo):
        lconv_ext[LCONV_OFF:LCONV_OFF + tm, lo:lo + PIECE] = proj(O_LRU + lo, PIECE)

    def gelu_piece(lo):
        gelu_buf[:, lo:lo + PIECE] = _gelu_tanh(proj(O_GELU + lo, PIECE))

    def gate_piece(lo):
        gate_buf[:, lo:lo + PIECE] = _sigmoid(
            proj(O_GATE + lo, PIECE) + b_gate_ref[:, lo:lo + PIECE])

    pieces = ([functools.partial(gate_piece, lo) for lo in range(0, D_MODEL, PIECE)]
              + [pool_piece]
              + [functools.partial(lru_piece, lo) for lo in range(0, D_LRU, PIECE)]
              + [functools.partial(gate_piece, lo) for lo in range(D_MODEL, 2 * D_MODEL, PIECE)]
              + [functools.partial(gelu_piece, lo) for lo in range(0, D_LRU, PIECE)]
              + [functools.partial(gate_piece, lo) for lo in range(2 * D_MODEL, 3 * D_MODEL, PIECE)])
    n_conv_chunks = tm // CONV_ROWS
    base = CONV_OFF - CONV_BUF
    rows = CONV_ROWS + SUBLANES
    n_units = n_conv_chunks * (D_CONV // LANES)
    unit = 0
    for c in range(n_conv_chunks):
        c0 = c * CONV_ROWS
        for l0 in range(0, D_CONV, LANES):
            acc = jnp.zeros((CONV_ROWS, LANES), F32) + conv_b_ref[:, l0:l0 + LANES]
            for r in range(SUBLANES):
                z = None
                for k in range(CONV_WIDTH):
                    if (base + k) % SUBLANES != r:
                        continue
                    q8 = base + k - r
                    t = (conv_ext[c0 + q8:c0 + q8 + rows, l0:l0 + LANES]
                         * conv_w_ref[k:k + 1, l0:l0 + LANES])
                    z = t if z is None else z + t
                acc = acc + z[r:r + CONV_ROWS, :]
            act_buf[c0:c0 + CONV_ROWS, l0:l0 + LANES] = acc
            for piece in pieces[unit * len(pieces) // n_units:(unit + 1) * len(pieces) // n_units]:
                piece()
            unit += 1
        act_buf[c0:c0 + CONV_ROWS, :] = _silu(_layer_norm(
            act_buf[c0:c0 + CONV_ROWS, :], cln_g_ref[...], cln_b_ref[...]))
    merged_buf[...] += gate_buf[:, D_MODEL:2 * D_MODEL] * _dot(
        act_buf[...].astype(BF16), wbc_ref[...])
    sc_ref[0] = conv_ext[tm + CONV_OFF - CONV_BUF:tm + CONV_OFF, :]
    conv_ext[0:CONV_OFF, :] = conv_ext[tm:tm + CONV_OFF, :]

    base = LCONV_OFF - LRU_CONV_BUF
    xc = jnp.zeros((tm, D_LRU), F32) + lconv_b_ref[...]
    for k in range(LRU_CONV_WIDTH):
        xc = xc + lconv_ext[base + k:base + k + tm, :] * lconv_w_ref[k:k + 1, :]
    slc_ref[0] = lconv_ext[tm + LCONV_OFF - LRU_CONV_BUF:tm + LCONV_OFF, :]
    lconv_ext[0:LCONV_OFF, :] = lconv_ext[tm:tm + LCONV_OFF, :]
    xcb = xc.astype(BF16)
    r = _sigmoid(_dot(xcb, wa_ref[...]) + ba_ref[...])
    ig = _sigmoid(_dot(xcb, wx_ref[...]) + bx_ref[...])
    a, b = _lru_coeffs(xc, r, ig, lam_ref[...], pos == 0)
    a_buf[...] = a
    b_buf[...] = b

    row8 = lax.broadcasted_iota(jnp.int32, (SUBLANES, D_LRU), 0)

    def scan_block(j, h):
        r0 = pl.multiple_of(j * SUBLANES, SUBLANES)
        a8 = a_buf[pl.ds(r0, SUBLANES), :]
        b8 = b_buf[pl.ds(r0, SUBLANES), :]
        for k in (1, 2, 4):
            a_s = pltpu.roll(a8, k, 0)
            b_s = pltpu.roll(b8, k, 0)
            m = row8 >= k
            b8 = jnp.where(m, a8 * b_s + b8, b8)
            a8 = jnp.where(m, a8 * a_s, a8)
        h8 = a8 * h + b8
        a_buf[pl.ds(r0, SUBLANES), :] = h8
        return h8[SUBLANES - 1:SUBLANES, :]

    h_last = lax.fori_loop(0, tm // SUBLANES, scan_block, h_carry[...], unroll=2)
    h_carry[...] = h_last
    sh_ref[0] = h_last
    hg = a_buf[...] * gelu_buf[...]
    merged = merged_buf[...] + gate_buf[:, 2 * D_MODEL:3 * D_MODEL] * _dot(
        hg.astype(BF16), wbl_ref[...])

    m_out = _dot(merged.astype(BF16), w_out_ref[...])
    y = _layer_norm(ALPHA * x_ref[...] + m_out, ln_g_ref[...], ln_b_ref[...])
    y_ref[...] = y
    if with_router:
        idx_ref[...], gate_ref[...] = _route(y, wr_ref[...], precise=False)


def _mixer_step_kernel(x_ref, st_pool_ref, st_conv_ref, st_lconv_ref, st_h_ref,
                       w_in_ref, b_gate_ref, w_pool_ref, pool_scale_ref,
                       conv_w_ref, conv_b_ref, cln_g_ref, cln_b_ref,
                       lconv_w_ref, lconv_b_ref, wa_ref, ba_ref, wx_ref, bx_ref, lam_ref,
                       wbp_ref, wbc_ref, wbl_ref, w_out_ref, ln_g_ref, ln_b_ref,
                       y_ref, up_ref, glu_ref, ul_ref, sh_ref, u_buf, *, start_pos, n_blocks):
    k = pl.program_id(0)
    u_buf[k] = _mm_hi(x_ref[...], w_in_ref[...])

    @pl.when(k == n_blocks - 1)
    def _():
        x = x_ref[...]

        def proj(lo):
            return u_buf[lo // STEP_COLS]

        def gate(n):
            lo = n * D_MODEL
            u = jnp.concatenate([proj(O_GATE + lo), proj(O_GATE + lo + STEP_COLS)], axis=1)
            return _sigmoid(u + b_gate_ref[:, lo:lo + D_MODEL])

        u_pool = proj(O_POOL)
        parts = []
        for g, w in enumerate(POOL_WINDOWS):
            lo = g * POOL_GROUP
            acc = u_pool[:, lo:lo + POOL_GROUP]
            for j in range(1, w):
                acc = acc + st_pool_ref[POOL_BUF - j, :, lo:lo + POOL_GROUP]
            pooled = acc * (1.0 / min(w, start_pos + 1)) - u_pool[:, lo:lo + POOL_GROUP]
            parts.append(_mm_hi(pooled, w_pool_ref[g]))
        mixed = jnp.concatenate(parts, axis=1) * pool_scale_ref[...]
        merged = gate(0) * _mm_hi(mixed, wbp_ref[...])
        up_ref[...] = u_pool

        glu = proj(O_VAL) * _sigmoid(proj(O_GLU))
        acc = glu * conv_w_ref[CONV_BUF:CONV_BUF + 1, :] + conv_b_ref[...]
        for j in range(CONV_BUF):
            acc = acc + st_conv_ref[j] * conv_w_ref[j:j + 1, :]
        act = _silu(_layer_norm(acc, cln_g_ref[...], cln_b_ref[...]))
        merged = merged + gate(1) * _mm_hi(act, wbc_ref[...])
        glu_ref[...] = glu

        u_lru = proj(O_LRU)
        xc = u_lru * lconv_w_ref[LRU_CONV_BUF:LRU_CONV_BUF + 1, :] + lconv_b_ref[...]
        for j in range(LRU_CONV_BUF):
            xc = xc + st_lconv_ref[j] * lconv_w_ref[j:j + 1, :]
        ul_ref[...] = u_lru
        r = _sigmoid(_mm_hi(xc, wa_ref[...]) + ba_ref[...])
        ig = _sigmoid(_mm_hi(xc, wx_ref[...]) + bx_ref[...])
        reset = jnp.full(xc.shape, True) if start_pos == 0 else None
        a, b = _lru_coeffs(xc, r, ig, lam_ref[...], reset)
        h = a * st_h_ref[...] + b
        sh_ref[...] = h
        hg = h * _gelu_tanh(proj(O_GELU))
        merged = merged + gate(2) * _mm_hi(hg, wbl_ref[...])

        m_out = _mm_hi(merged, w_out_ref[...])
        y_ref[...] = _layer_norm(ALPHA * x + m_out, ln_g_ref[...], ln_b_ref[...])


def _const_spec(shape):
    nd = len(shape)
    return pl.BlockSpec(shape, lambda *_: (0,) * nd, pipeline_mode=pl.Buffered(1))


def _mixer_weight_list(p):
    return [p["w_in"], p["b_gate"], p["w_pool"], p["pool_scale"], p["conv_w"], p["conv_b"],
            p["conv_ln_g"], p["conv_ln_b"], p["lru_conv_w"], p["lru_conv_b"], p["lru_wa"],
            p["lru_ba"], p["lru_wx"], p["lru_bx"], p["lru_lambda"], p["w_br_pool"],
            p["w_br_conv"], p["w_br_lru"], p["w_out"], p["ln1_g"], p["ln1_b"]]


def _mixer_seq(x, p, batch, seq, wr_pad=None):
    tm = min(SEQ_TILE, seq)
    nt = seq // tm
    with_router = wr_pad is not None
    weights = _mixer_weight_list(p) + ([wr_pad] if with_router else [])
    in_specs = [pl.BlockSpec((tm, D_MODEL), lambda b, i: (b * nt + i, 0))]
    in_specs += [_const_spec(w.shape) for w in weights]
    out_shape = [
        jax.ShapeDtypeStruct((batch * seq, D_MODEL), F32),
        jax.ShapeDtypeStruct((batch, POOL_BUF, D_POOL), F32),
        jax.ShapeDtypeStruct((batch, CONV_BUF, D_CONV), F32),
        jax.ShapeDtypeStruct((batch, LRU_CONV_BUF, D_LRU), F32),
        jax.ShapeDtypeStruct((batch, 1, D_LRU), F32),
    ]
    out_specs = [
        pl.BlockSpec((tm, D_MODEL), lambda b, i: (b * nt + i, 0)),
        pl.BlockSpec((1, POOL_BUF, D_POOL), lambda b, i: (b, 0, 0)),
        pl.BlockSpec((1, CONV_BUF, D_CONV), lambda b, i: (b, 0, 0)),
        pl.BlockSpec((1, LRU_CONV_BUF, D_LRU), lambda b, i: (b, 0, 0)),
        pl.BlockSpec((1, 1, D_LRU), lambda b, i: (b, 0, 0)),
    ]
    if with_router:
        out_shape += [jax.ShapeDtypeStruct((batch * seq, LANES), jnp.int32),
                      jax.ShapeDtypeStruct((batch * seq, LANES), F32)]
        out_specs += [pl.BlockSpec((tm, LANES), lambda b, i: (b * nt + i, 0)),
                      pl.BlockSpec((tm, LANES), lambda b, i: (b * nt + i, 0))]
    scratch = [
        pltpu.VMEM((tm + POOL_OFF, D_POOL), F32),
        pltpu.VMEM((tm + CONV_OFF + SUBLANES, D_CONV), F32),
        pltpu.VMEM((tm + LCONV_OFF, D_LRU), F32),
        pltpu.VMEM((tm, D_CONV), F32),
        pltpu.VMEM((tm, D_LRU), F32),
        pltpu.VMEM((tm, D_LRU), F32),
        pltpu.VMEM((1, D_LRU), F32),
        pltpu.VMEM((tm, 3 * D_MODEL), F32),
        pltpu.VMEM((tm, D_LRU), F32),
        pltpu.VMEM((tm, D_MODEL), F32),
        pltpu.VMEM((tm, D_MODEL), BF16),
    ]
    y, sp, sc, slc, sh, *routing = pl.pallas_call(
        functools.partial(_mixer_seq_kernel, tm=tm, with_router=with_router),
        grid=(batch, nt),
        in_specs=in_specs,
        out_specs=out_specs,
        out_shape=out_shape,
        scratch_shapes=scratch,
        compiler_params=pltpu.CompilerParams(
            dimension_semantics=("arbitrary", "arbitrary"),
            vmem_limit_bytes=VMEM_LIMIT_BYTES),
        name="mixer_seq",
    )(x, *weights)
    return (y, sp, sc, slc, sh.reshape(batch, D_LRU), *routing)


def _mixer_step(x, st_pool, st_conv, st_lconv, st_h, p, start_pos):
    batch = x.shape[0]
    weights = _mixer_weight_list(p)
    states = [jnp.transpose(st_pool, (1, 0, 2)), jnp.transpose(st_conv, (1, 0, 2)),
              jnp.transpose(st_lconv, (1, 0, 2)), st_h]
    args = [x] + states + weights
    in_cols = weights[0].shape[1]
    n_blocks = in_cols // STEP_COLS
    assert n_blocks * STEP_COLS == in_cols
    in_specs = [_const_spec(a.shape) for a in args]
    in_specs[1 + len(states)] = pl.BlockSpec((D_MODEL, STEP_COLS), lambda k: (0, k))
    out_shape = (
        jax.ShapeDtypeStruct((batch, D_MODEL), F32),
        jax.ShapeDtypeStruct((batch, D_POOL), F32),
        jax.ShapeDtypeStruct((batch, D_CONV), F32),
        jax.ShapeDtypeStruct((batch, D_LRU), F32),
        jax.ShapeDtypeStruct((batch, D_LRU), F32),
    )
    y, u_pool, glu, u_lru, sh = pl.pallas_call(
        functools.partial(_mixer_step_kernel, start_pos=start_pos, n_blocks=n_blocks),
        grid=(n_blocks,),
        in_specs=in_specs,
        out_specs=tuple(pl.BlockSpec(s.shape, lambda k: (0, 0)) for s in out_shape),
        out_shape=out_shape,
        scratch_shapes=[pltpu.VMEM((n_blocks, batch, STEP_COLS), F32)],
        compiler_params=pltpu.CompilerParams(
            dimension_semantics=("arbitrary",), vmem_limit_bytes=VMEM_LIMIT_BYTES),
        name="mixer_step",
    )(*args)

    def push(state, row):
        return jnp.concatenate([state[:, 1:], row[:, None]], axis=1)

    return y, push(st_pool, u_pool), push(st_conv, glu), push(st_lconv, u_lru), sh


def _ffn_kernel(x_ref, wg_ref, wu_ref, wd_ref, g_ref, b_ref, y_ref, *, d_ff, precise):
    x = x_ref[...]
    mm = _mm_hi if precise else _dot
    xb = x if precise else x.astype(BF16)
    acc = jnp.zeros(x.shape, F32)
    for c0 in range(0, d_ff, FFN_CHUNK):
        h = _silu(mm(xb, wg_ref[:, c0:c0 + FFN_CHUNK])) * mm(xb, wu_ref[:, c0:c0 + FFN_CHUNK])
        acc = acc + mm(h if precise else h.astype(BF16), wd_ref[c0:c0 + FFN_CHUNK, :])
    y_ref[...] = _layer_norm(ALPHA * x + acc, g_ref[...], b_ref[...])


def _ffn(x, wg, wu, wd, g, b, precise):
    n = x.shape[0]
    tm = min(SEQ_TILE, n)
    d_ff = wg.shape[1]
    weights = [wg, wu, wd, g, b]
    return pl.pallas_call(
        functools.partial(_ffn_kernel, d_ff=d_ff, precise=precise),
        grid=(n // tm,),
        in_specs=[pl.BlockSpec((tm, D_MODEL), lambda i: (i, 0))] + [_const_spec(w.shape) for w in weights],
        out_specs=pl.BlockSpec((tm, D_MODEL), lambda i: (i, 0)),
        out_shape=jax.ShapeDtypeStruct((n, D_MODEL), F32),
        compiler_params=pltpu.CompilerParams(
            dimension_semantics=("arbitrary",), vmem_limit_bytes=VMEM_LIMIT_BYTES),
        name="ffn_dense",
    )(x, *weights)


def _route(x, wr, precise):
    logits = _mm_hi(x, wr) if precise else _mm(x, wr)
    lane = lax.broadcasted_iota(jnp.int32, logits.shape, 1)
    neg = jnp.float32(-jnp.inf)
    l1 = jnp.where(lane < N_EXPERTS, logits, neg)
    v1 = jnp.max(l1, axis=-1, keepdims=True)
    i1 = jnp.min(jnp.where(l1 == v1, lane, LANES), axis=-1, keepdims=True)
    l2 = jnp.where(lane == i1, neg, l1)
    v2 = jnp.max(l2, axis=-1, keepdims=True)
    i2 = jnp.min(jnp.where(l2 == v2, lane, LANES), axis=-1, keepdims=True)
    e2 = jnp.exp(v2 - v1)
    den = 1.0 + e2
    idx = jnp.where(lane == 0, i1, jnp.where(lane == 1, i2, 0))
    gates = jnp.where(lane == 0, 1.0 / den, jnp.where(lane == 1, e2 / den, 0.0))
    return idx, gates


def _router_kernel(x_ref, wr_ref, idx_ref, gate_ref):
    idx_ref[...], gate_ref[...] = _route(x_ref[...], wr_ref[...], precise=True)


def _router(x, wr_pad):
    n = x.shape[0]
    tm = min(SEQ_TILE, n)
    return pl.pallas_call(
        _router_kernel,
        grid=(n // tm,),
        in_specs=[pl.BlockSpec((tm, D_MODEL), lambda i: (i, 0)), _const_spec(wr_pad.shape)],
        out_specs=(pl.BlockSpec((tm, LANES), lambda i: (i, 0)),
                   pl.BlockSpec((tm, LANES), lambda i: (i, 0))),
        out_shape=(jax.ShapeDtypeStruct((n, LANES), jnp.int32),
                   jax.ShapeDtypeStruct((n, LANES), F32)),
        compiler_params=pltpu.CompilerParams(dimension_semantics=("arbitrary",)),
        name="moe_router",
    )(x, wr_pad)


def _sc_row_gather(src, idx):
    n_idx = idx.shape[0]
    assert n_idx % SC_WINDOW == 0
    n_win = n_idx // SC_WINDOW
    n_workers = SC_CORES * SC_SUBCORES
    n_sub = SC_WINDOW // SC_SUB_ROWS
    mesh = plsc.VectorSubcoreMesh(core_axis_name="c", subcore_axis_name="s",
                                  num_cores=SC_CORES, num_subcores=SC_SUBCORES)

    def body(src_hbm, idx_hbm, out_hbm, idx_v, buf, g0, g1, w0, w1):
        gsem = (g0, g1)
        wsem = (w0, w1)
        worker = lax.axis_index("c") * SC_SUBCORES + lax.axis_index("s")

        @pl.loop(0, pl.cdiv(n_win, n_workers))
        def _(t):
            win = worker + t * n_workers

            @pl.when(win < n_win)
            def _():
                base = win * SC_WINDOW
                pltpu.sync_copy(idx_hbm.at[:, pl.ds(base, SC_WINDOW)], idx_v)

                def fetch(q):
                    return pltpu.async_copy(
                        src_hbm.at[idx_v.at[0, pl.ds(q * SC_SUB_ROWS, SC_SUB_ROWS)]],
                        buf.at[q % 2], gsem[q % 2])

                def put(q):
                    return pltpu.async_copy(
                        buf.at[q % 2], out_hbm.at[pl.ds(base + q * SC_SUB_ROWS, SC_SUB_ROWS)],
                        wsem[q % 2])

                fetches = {0: fetch(0)}
                puts = {}
                for q in range(n_sub):
                    if q + 1 < n_sub:
                        if q >= 1:
                            puts[q - 1].wait()
                        fetches[q + 1] = fetch(q + 1)
                    fetches[q].wait()
                    puts[q] = put(q)
                for q in range(max(0, n_sub - 2), n_sub):
                    puts[q].wait()

    return pl.kernel(
        body,
        out_type=jax.ShapeDtypeStruct((n_idx, D_MODEL), src.dtype),
        mesh=mesh,
        scratch_types=[pltpu.VMEM((1, SC_WINDOW), jnp.int32),
                       pltpu.VMEM((2, SC_SUB_ROWS, D_MODEL), src.dtype),
                       pltpu.SemaphoreType.DMA, pltpu.SemaphoreType.DMA,
                       pltpu.SemaphoreType.DMA, pltpu.SemaphoreType.DMA],
    )(src, idx.reshape(1, n_idx))


def _expert_kernel(te_ref, nu_ref, xs_ref, wg_ref, wu_ref, wd_ref, ys_ref, xs_bf, acc, *, tf, nf):
    del te_ref
    i = pl.program_id(0)
    j = pl.program_id(1)
    used = i < nu_ref[0]

    @pl.when(jnp.logical_and(used, j == 0))
    def _():
        xs_bf[...] = xs_ref[...].astype(BF16)

    @pl.when(used)
    def _():
        xb = xs_bf[...]
        part = None
        for c0 in range(0, tf, FFN_CHUNK):
            h = _silu(_dot(xb, wg_ref[:, c0:c0 + FFN_CHUNK])) * _dot(xb, wu_ref[:, c0:c0 + FFN_CHUNK])
            d = _dot(h.astype(BF16), wd_ref[c0:c0 + FFN_CHUNK, :])
            part = d if part is None else part + d

        @pl.when(j == 0)
        def _():
            acc[...] = part

        @pl.when(jnp.logical_and(j > 0, j < nf - 1))
        def _():
            acc[...] += part

        @pl.when(j == nf - 1)
        def _():
            ys_ref[...] = acc[...] + part

    @pl.when(jnp.logical_and(jnp.logical_not(used), j == nf - 1))
    def _():
        ys_ref[...] = jnp.zeros(ys_ref.shape, F32)


def _experts(xs, tile_expert, n_used, wg, wu, wd, tm):
    n_tiles = tile_expert.shape[0]
    d_e = wg.shape[2]
    tf = EXPERT_F_TILE
    nf = d_e // tf
    assert nf >= 2 and nf * tf == d_e and tf % FFN_CHUNK == 0

    def jeff(i, j, nu):
        return jnp.where(i < nu[0], j, nf - 1)

    grid_spec = pltpu.PrefetchScalarGridSpec(
        num_scalar_prefetch=2,
        grid=(n_tiles, nf),
        in_specs=[
            pl.BlockSpec((tm, D_MODEL), lambda i, j, te, nu: (jnp.minimum(i, nu[0] - 1), 0)),
            pl.BlockSpec((None, D_MODEL, tf), lambda i, j, te, nu: (te[i], 0, jeff(i, j, nu))),
            pl.BlockSpec((None, D_MODEL, tf), lambda i, j, te, nu: (te[i], 0, jeff(i, j, nu))),
            pl.BlockSpec((None, tf, D_MODEL), lambda i, j, te, nu: (te[i], jeff(i, j, nu), 0)),
        ],
        out_specs=pl.BlockSpec((tm, D_MODEL), lambda i, j, te, nu: (i, 0)),
        scratch_shapes=[
            pltpu.VMEM((tm, D_MODEL), BF16),
            pltpu.VMEM((tm, D_MODEL), F32),
        ],
    )
    return pl.pallas_call(
        functools.partial(_expert_kernel, tf=tf, nf=nf),
        grid_spec=grid_spec,
        out_shape=jax.ShapeDtypeStruct((n_tiles * tm, D_MODEL), F32),
        compiler_params=pltpu.CompilerParams(
            dimension_semantics=("arbitrary", "arbitrary"), vmem_limit_bytes=VMEM_LIMIT_BYTES),
        name="moe_experts",
    )(tile_expert, n_used, xs, wg, wu, wd)


def _combine_kernel(x_ref, gate_ref, y0_ref, y1_ref, g_ref, b_ref, y_ref):
    gates = gate_ref[...]
    f = gates[:, 0:1] * y0_ref[...] + gates[:, 1:2] * y1_ref[...]
    y_ref[...] = _layer_norm(ALPHA * x_ref[...] + f, g_ref[...], b_ref[...])


def _combine(x, gates, y01, g, b):
    n = x.shape[0]
    tm = min(SEQ_TILE, n)
    nt = n // tm
    return pl.pallas_call(
        _combine_kernel,
        grid=(nt,),
        in_specs=[
            pl.BlockSpec((tm, D_MODEL), lambda i: (i, 0)),
            pl.BlockSpec((tm, LANES), lambda i: (i, 0)),
            pl.BlockSpec((tm, D_MODEL), lambda i: (i, 0)),
            pl.BlockSpec((tm, D_MODEL), lambda i: (i + nt, 0)),
            _const_spec(g.shape),
            _const_spec(b.shape),
        ],
        out_specs=pl.BlockSpec((tm, D_MODEL), lambda i: (i, 0)),
        out_shape=jax.ShapeDtypeStruct((n, D_MODEL), F32),
        compiler_params=pltpu.CompilerParams(
            dimension_semantics=("arbitrary",), vmem_limit_bytes=VMEM_LIMIT_BYTES),
        name="moe_combine",
    )(x, gates, y01, y01, g, b)


def _moe(x, wr_pad, wg, wu, wd, g, b, tm, routing=None):
    n = x.shape[0]
    idx, gates = routing if routing is not None else _router(x, wr_pad)
    flat_e = idx[:, :TOP_K].reshape(-1)
    onehot = (flat_e[:, None] == jnp.arange(N_EXPERTS, dtype=jnp.int32)[None, :]).astype(jnp.int32)
    csum = jnp.cumsum(onehot, axis=0)
    rank = jnp.sum((csum - onehot) * onehot, axis=1)
    counts = csum[-1]
    padded = ((counts + tm - 1) // tm) * tm
    ends = jnp.cumsum(padded)
    slot = (ends - padded)[flat_e] + rank
    n_pairs = TOP_K * n
    n_tiles = n_pairs // tm + N_EXPERTS
    n_used = (ends[-1] // tm).astype(jnp.int32).reshape(1)
    tile_start = jnp.arange(n_tiles, dtype=jnp.int32) * tm
    tile_expert = jnp.sum((tile_start[:, None] >= ends[None, :]).astype(jnp.int32), axis=1)
    last_e = jnp.sum((ends[-1] - 1 >= ends).astype(jnp.int32))
    tile_expert = jnp.minimum(tile_expert, last_e).astype(jnp.int32)
    pair_sorted = jnp.sort(flat_e * n_pairs + jnp.arange(n_pairs, dtype=jnp.int32)) % n_pairs
    slot_ids = jnp.arange(n_tiles * tm, dtype=jnp.int32)
    slot_e = jnp.repeat(tile_expert, tm)
    local = slot_ids - (ends - padded)[slot_e]
    q = (jnp.cumsum(counts) - counts)[slot_e] + local
    valid = jnp.logical_and(local >= 0, local < counts[slot_e])
    tok_of_slot = jnp.where(valid, pair_sorted[jnp.clip(q, 0, n_pairs - 1)] // TOP_K, 0)
    xs = _sc_row_gather(x, tok_of_slot)
    ys = _experts(xs, tile_expert, n_used, wg, wu, wd, tm)
    y01 = _sc_row_gather(ys, jnp.transpose(slot.reshape(n, TOP_K)).reshape(-1))
    return _combine(x, gates, y01, g, b)


def _block_diag(w):
    n, c, _ = w.shape
    eye = jnp.eye(n, dtype=w.dtype)
    return (eye[:, None, :, None] * w[:, :, None, :]).reshape(n * c, n * c)


def _row(v):
    return v.reshape(1, -1)


def kernel(x_prompt, x_sample, state_pool, state_conv, state_lru_conv, state_lru_h, w_in, b_gate, w_pool, pool_scale, conv_w, conv_b, conv_ln_g, conv_ln_b, lru_conv_w, lru_conv_b, lru_wa, lru_ba, lru_wx, lru_bx, lru_lambda, w_br_pool, w_br_conv, w_br_lru, w_out, ln1_g, ln1_b, ln2_g, ln2_b, ffn_w_gate, ffn_w_up, ffn_w_down, moe_router, moe_w_gate, moe_w_up, moe_w_down):
    batch, seq, _ = x_prompt.shape
    dec_batch = x_sample.shape[0]

    def layer_params(l, mat):
        return {
            "w_in": w_in[l].astype(mat), "b_gate": _row(b_gate[l]),
            "w_pool": w_pool[l].astype(mat), "pool_scale": _row(pool_scale[l]),
            "conv_w": conv_w[l], "conv_b": _row(conv_b[l]),
            "conv_ln_g": _row(conv_ln_g[l]), "conv_ln_b": _row(conv_ln_b[l]),
            "lru_conv_w": lru_conv_w[l], "lru_conv_b": _row(lru_conv_b[l]),
            "lru_wa": _block_diag(lru_wa[l]).astype(mat), "lru_ba": _row(lru_ba[l]),
            "lru_wx": _block_diag(lru_wx[l]).astype(mat), "lru_bx": _row(lru_bx[l]),
            "lru_lambda": _row(lru_lambda[l]),
            "w_br_pool": w_br_pool[l].astype(mat), "w_br_conv": w_br_conv[l].astype(mat),
            "w_br_lru": w_br_lru[l].astype(mat), "w_out": w_out[l].astype(mat),
            "ln1_g": _row(ln1_g[l]), "ln1_b": _row(ln1_b[l]),
        }

    layers = [layer_params(l, BF16) for l in range(DEPTH)]
    layers_f32 = [layer_params(l, F32) for l in range(DEPTH)]
    ffn_w = [(ffn_w_gate[m].astype(BF16), ffn_w_up[m].astype(BF16), ffn_w_down[m].astype(BF16))
             for m in range(ffn_w_gate.shape[0])]
    ffn_w_f32 = [(ffn_w_gate[m], ffn_w_up[m], ffn_w_down[m]) for m in range(ffn_w_gate.shape[0])]
    moe_w = [(jnp.pad(moe_router[m], ((0, 0), (0, LANES - N_EXPERTS))),
              moe_w_gate[m].astype(BF16), moe_w_up[m].astype(BF16), moe_w_down[m].astype(BF16))
             for m in range(moe_router.shape[0])]

    x_prompt, moe_w = lax.optimization_barrier((x_prompt, moe_w))

    def channel_mixer(l, x, moe_tile, routing=None, precise=False):
        g, b = _row(ln2_g[l]), _row(ln2_b[l])
        if l % 2 == 0:
            return _ffn(x, *(ffn_w_f32 if precise else ffn_w)[l // 2], g, b, precise)
        return _moe(x, *moe_w[l // 2], g, b, moe_tile, routing)

    x = x_prompt.reshape(batch * seq, D_MODEL)
    p_states = []
    for l in range(DEPTH):
        wr_pad = moe_w[l // 2][0].astype(BF16) if l % 2 == 1 else None
        x, sp, sc, slc, sh, *routing = _mixer_seq(x, layers[l], batch, seq, wr_pad)
        p_states.append((sp, sc, slc, sh))
        x = channel_mixer(l, x, MOE_TILE_SEQ, tuple(routing) or None)
    y_prompt = x.reshape(batch, seq, D_MODEL)

    x = x_sample.reshape(dec_batch, D_MODEL)
    s_states = []
    for l in range(DEPTH):
        x, sp, sc, slc, sh = _mixer_step(x, state_pool[l], state_conv[l], state_lru_conv[l],
                                         state_lru_h[l], layers_f32[l], PAST_LEN)
        s_states.append((sp, sc, slc, sh))
        x = channel_mixer(l, x, MOE_TILE_STEP, precise=True)
    y_sample = x.reshape(dec_batch, 1, D_MODEL)

    def stack(states, k):
        return jnp.stack([s[k] for s in states])

    return (y_prompt, y_sample,
            stack(p_states, 0), stack(p_states, 1), stack(p_states, 2), stack(p_states, 3),
            stack(s_states, 0), stack(s_states, 1), stack(s_states, 2), stack(s_states, 3))
```

```python
import functools

import jax
import jax.numpy as jnp
from jax import lax
from jax.experimental import pallas as pl
from jax.experimental.pallas import tpu as pltpu
from jax.experimental.pallas import tpu_sc as plsc

D_MODEL = 1024
DEPTH = 2
PAST_LEN = 16384
D_POOL = 512
N_POOL_GROUPS = 4
POOL_GROUP = 128
POOL_WINDOWS = (2, 4, 8, 16)
POOL_BUF = 15
D_CONV = 512
CONV_WIDTH = 31
CONV_BUF = 30
D_LRU = 512
N_LRU_BLOCKS = 8
LRU_BLOCK = 64
LRU_CONV_WIDTH = 4
LRU_CONV_BUF = 3
LRU_C = 8.0
N_EXPERTS = 8
TOP_K = 2
ALPHA = (2.0 * DEPTH) ** 0.25
LN_EPS = 1e-5

O_POOL = 0
O_VAL = D_POOL
O_GLU = O_VAL + D_CONV
O_LRU = O_GLU + D_CONV
O_GELU = O_LRU + D_LRU
O_GATE = O_GELU + D_LRU

SUBLANES = 8
LANES = 128
VMEM_LIMIT_BYTES = 56 * 1024 * 1024
SC_CORES = 2
SC_SUBCORES = 16
SC_WINDOW = 128
SC_SUB_ROWS = 32

SEQ_TILE = 512
CONV_ROWS = 64
POOL_OFF = 16
CONV_OFF = 32
LCONV_OFF = 8
FFN_CHUNK = 256
PIECE = 256
STEP_COLS = 512
EXPERT_F_TILE = 1792
MOE_TILE_SEQ = 512
MOE_TILE_STEP = 128

BF16 = jnp.bfloat16
F32 = jnp.float32


def _sigmoid(x):
    return 0.5 * jnp.tanh(0.5 * x) + 0.5


def _silu(x):
    return x * _sigmoid(x)


def _gelu_tanh(x):
    return x * (0.5 * (1.0 + jnp.tanh(0.7978845608028654 * (x + 0.044715 * (x * x * x)))))


def _softplus(z):
    return jnp.maximum(z, 0.0) + jnp.log1p(jnp.exp(-jnp.abs(z)))


def _layer_norm(x, g, b):
    mu = jnp.mean(x, axis=-1, keepdims=True)
    xc = x - mu
    var = jnp.mean(xc * xc, axis=-1, keepdims=True)
    return xc * lax.rsqrt(var + LN_EPS) * g + b


def _dot(a, b):
    return jnp.dot(a, b, preferred_element_type=F32)


def _mm(a, w):
    return _dot(a.astype(BF16), w)


def _mm_hi(a, w):
    return jnp.dot(a, w, preferred_element_type=F32, precision=lax.Precision.HIGHEST)


def _lru_coeffs(xc, r, ig, lam, reset):
    log_a = (-LRU_C * r) * _softplus(-lam)
    a = jnp.exp(log_a)
    mult = jnp.sqrt(-jnp.tanh(log_a) * (a * a + 1.0))
    if reset is not None:
        a = jnp.where(reset, 0.0, a)
        mult = jnp.where(reset, 1.0, mult)
    return a, mult * ig * xc


def _mixer_seq_kernel(x_ref, w_in_ref, b_gate_ref, w_pool_ref, pool_scale_ref,
                      conv_w_ref, conv_b_ref, cln_g_ref, cln_b_ref,
                      lconv_w_ref, lconv_b_ref, wa_ref, ba_ref, wx_ref, bx_ref, lam_ref,
                      wbp_ref, wbc_ref, wbl_ref, w_out_ref, ln_g_ref, ln_b_ref, *rest, tm, with_router):
    if with_router:
        wr_ref, rest = rest[0], rest[1:]
        idx_ref, gate_ref, rest = rest[5], rest[6], rest[:5] + rest[7:]
    (y_ref, sp_ref, sc_ref, slc_ref, sh_ref,
     pool_ext, conv_ext, lconv_ext, act_buf, a_buf, b_buf, h_carry, gate_buf, gelu_buf,
     merged_buf, xb_buf) = rest
    i = pl.program_id(1)

    @pl.when(i == 0)
    def _():
        pool_ext[0:POOL_OFF, :] = jnp.zeros((POOL_OFF, D_POOL), F32)
        conv_ext[0:CONV_OFF, :] = jnp.zeros((CONV_OFF, D_CONV), F32)
        conv_ext[tm + CONV_OFF:tm + CONV_OFF + SUBLANES, :] = jnp.zeros((SUBLANES, D_CONV), F32)
        lconv_ext[0:LCONV_OFF, :] = jnp.zeros((LCONV_OFF, D_LRU), F32)
        h_carry[...] = jnp.zeros((1, D_LRU), F32)

    xb_buf[...] = x_ref[...].astype(BF16)

    def proj(lo, width):
        return _dot(xb_buf[...], w_in_ref[:, lo:lo + width])

    pos = i * tm + lax.broadcasted_iota(jnp.int32, (tm, 1), 0)

    glu = proj(O_VAL, D_CONV) * _sigmoid(proj(O_GLU, D_CONV))
    conv_ext[CONV_OFF:CONV_OFF + tm, :] = glu

    def pool_piece():
        u_pool = proj(O_POOL, D_POOL)
        pool_ext[POOL_OFF:POOL_OFF + tm, :] = u_pool
        parts = []
        for g, w in enumerate(POOL_WINDOWS):
            lo = g * POOL_GROUP
            acc = u_pool[:, lo:lo + POOL_GROUP]
            for k in range(1, w):
                acc = acc + pool_ext[POOL_OFF - k:POOL_OFF - k + tm, lo:lo + POOL_GROUP]
            inv_cnt = 1.0 / jnp.minimum(w, pos + 1).astype(F32)
            pooled = acc * inv_cnt - u_pool[:, lo:lo + POOL_GROUP]
            parts.append(_dot(pooled.astype(BF16), w_pool_ref[g]))
        mixed = jnp.concatenate(parts, axis=1) * pool_scale_ref[...]
        merged_buf[...] = gate_buf[:, 0:D_MODEL] * _dot(mixed.astype(BF16), wbp_ref[...])
        sp_ref[0] = pool_ext[tm + POOL_OFF - POOL_BUF:tm + POOL_OFF, :]
        pool_ext[0:POOL_OFF, :] = pool_ext[tm:tm + POOL_OFF, :]

    def lru_piece(lo):
        lconv_ext[LCONV_OFF:LCONV_OFF + tm, lo:lo + PIECE] = proj(O_LRU + lo, PIECE)

    def gelu_piece(lo):
        gelu_buf[:, lo:lo + PIECE] = _gelu_tanh(proj(O_GELU + lo, PIECE))

    def gate_piece(lo):
        gate_buf[:, lo:lo + PIECE] = _sigmoid(
            proj(O_GATE + lo, PIECE) + b_gate_ref[:, lo:lo + PIECE])

    pieces = ([functools.partial(gate_piece, lo) for lo in range(0, D_MODEL, PIECE)]
              + [pool_piece]
              + [functools.partial(lru_piece, lo) for lo in range(0, D_LRU, PIECE)]
              + [functools.partial(gate_piece, lo) for lo in range(D_MODEL, 2 * D_MODEL, PIECE)]
              + [functools.partial(gelu_piece, lo) for lo in range(0, D_LRU, PIECE)]
              + [functools.partial(gate_piece, lo) for lo in range(2 * D_MODEL, 3 * D_MODEL, PIECE)])
    n_conv_chunks = tm // CONV_ROWS
    base = CONV_OFF - CONV_BUF
    rows = CONV_ROWS + SUBLANES
    n_units = n_conv_chunks * (D_CONV // LANES)
    unit = 0
    for c in range(n_conv_chunks):
        c0 = c * CONV_ROWS
        for l0 in range(0, D_CONV, LANES):
            acc = jnp.zeros((CONV_ROWS, LANES), F32) + conv_b_ref[:, l0:l0 + LANES]
            for r in range(SUBLANES):
                z = None
                for k in range(CONV_WIDTH):
                    if (base + k) % SUBLANES != r:
                        continue
                    q8 = base + k - r
                    t = (conv_ext[c0 + q8:c0 + q8 + rows, l0:l0 + LANES]
                         * conv_w_ref[k:k + 1, l0:l0 + LANES])
                    z = t if z is None else z + t
                acc = acc + z[r:r + CONV_ROWS, :]
            act_buf[c0:c0 + CONV_ROWS, l0:l0 + LANES] = acc
            for piece in pieces[unit * len(pieces) // n_units:(unit + 1) * len(pieces) // n_units]:
                piece()
            unit += 1
        act_buf[c0:c0 + CONV_ROWS, :] = _silu(_layer_norm(
            act_buf[c0:c0 + CONV_ROWS, :], cln_g_ref[...], cln_b_ref[...]))
    merged_buf[...] += gate_buf[:, D_MODEL:2 * D_MODEL] * _dot(
        act_buf[...].astype(BF16), wbc_ref[...])
    sc_ref[0] = conv_ext[tm + CONV_OFF - CONV_BUF:tm + CONV_OFF, :]
    conv_ext[0:CONV_OFF, :] = conv_ext[tm:tm + CONV_OFF, :]

    base = LCONV_OFF - LRU_CONV_BUF
    xc = jnp.zeros((tm, D_LRU), F32) + lconv_b_ref[...]
    for k in range(LRU_CONV_WIDTH):
        xc = xc + lconv_ext[base + k:base + k + tm, :] * lconv_w_ref[k:k + 1, :]
    slc_ref[0] = lconv_ext[tm + LCONV_OFF - LRU_CONV_BUF:tm + LCONV_OFF, :]
    lconv_ext[0:LCONV_OFF, :] = lconv_ext[tm:tm + LCONV_OFF, :]
    xcb = xc.astype(BF16)
    r = _sigmoid(_dot(xcb, wa_ref[...]) + ba_ref[...])
    ig = _sigmoid(_dot(xcb, wx_ref[...]) + bx_ref[...])
    a, b = _lru_coeffs(xc, r, ig, lam_ref[...], pos == 0)
    a_buf[...] = a
    b_buf[...] = b

    row8 = lax.broadcasted_iota(jnp.int32, (SUBLANES, D_LRU), 0)

    def scan_block(j, h):
        r0 = pl.multiple_of(j * SUBLANES, SUBLANES)
        a8 = a_buf[pl.ds(r0, SUBLANES), :]
        b8 = b_buf[pl.ds(r0, SUBLANES), :]
        for k in (1, 2, 4):
            a_s = pltpu.roll(a8, k, 0)
            b_s = pltpu.roll(b8, k, 0)
            m = row8 >= k
            b8 = jnp.where(m, a8 * b_s + b8, b8)
            a8 = jnp.where(m, a8 * a_s, a8)
        h8 = a8 * h + b8
        a_buf[pl.ds(r0, SUBLANES), :] = h8
        return h8[SUBLANES - 1:SUBLANES, :]

    h_last = lax.fori_loop(0, tm // SUBLANES, scan_block, h_carry[...], unroll=2)
    h_carry[...] = h_last
    sh_ref[0] = h_last
    hg = a_buf[...] * gelu_buf[...]
    merged = merged_buf[...] + gate_buf[:, 2 * D_MODEL:3 * D_MODEL] * _dot(
        hg.astype(BF16), wbl_ref[...])

    m_out = _dot(merged.astype(BF16), w_out_ref[...])
    y = _layer_norm(ALPHA * x_ref[...] + m_out, ln_g_ref[...], ln_b_ref[...])
    y_ref[...] = y
    if with_router:
        idx_ref[...], gate_ref[...] = _route(y, wr_ref[...], precise=False)


def _mixer_step_kernel(x_ref, st_pool_ref, st_conv_ref, st_lconv_ref, st_h_ref,
                       w_in_ref, b_gate_ref, w_pool_ref, pool_scale_ref,
                       conv_w_ref, conv_b_ref, cln_g_ref, cln_b_ref,
                       lconv_w_ref, lconv_b_ref, wa_ref, ba_ref, wx_ref, bx_ref, lam_ref,
                       wbp_ref, wbc_ref, wbl_ref, w_out_ref, ln_g_ref, ln_b_ref,
                       y_ref, up_ref, glu_ref, ul_ref, sh_ref, u_buf, *, start_pos, n_blocks):
    k = pl.program_id(0)
    u_buf[k] = _mm_hi(x_ref[...], w_in_ref[...])

    @pl.when(k == n_blocks - 1)
    def _():
        x = x_ref[...]

        def proj(lo):
            return u_buf[lo // STEP_COLS]

        def gate(n):
            lo = n * D_MODEL
            u = jnp.concatenate([proj(O_GATE + lo), proj(O_GATE + lo + STEP_COLS)], axis=1)
            return _sigmoid(u + b_gate_ref[:, lo:lo + D_MODEL])

        u_pool = proj(O_POOL)
        parts = []
        for g, w in enumerate(POOL_WINDOWS):
            lo = g * POOL_GROUP
            acc = u_pool[:, lo:lo + POOL_GROUP]
            for j in range(1, w):
                acc = acc + st_pool_ref[POOL_BUF - j, :, lo:lo + POOL_GROUP]
            pooled = acc * (1.0 / min(w, start_pos + 1)) - u_pool[:, lo:lo + POOL_GROUP]
            parts.append(_mm_hi(pooled, w_pool_ref[g]))
        mixed = jnp.concatenate(parts, axis=1) * pool_scale_ref[...]
        merged = gate(0) * _mm_hi(mixed, wbp_ref[...])
        up_ref[...] = u_pool

        glu = proj(O_VAL) * _sigmoid(proj(O_GLU))
        acc = glu * conv_w_ref[CONV_BUF:CONV_BUF + 1, :] + conv_b_ref[...]
        for j in range(CONV_BUF):
            acc = acc + st_conv_ref[j] * conv_w_ref[j:j + 1, :]
        act = _silu(_layer_norm(acc, cln_g_ref[...], cln_b_ref[...]))
        merged = merged + gate(1) * _mm_hi(act, wbc_ref[...])
        glu_ref[...] = glu

        u_lru = proj(O_LRU)
        xc = u_lru * lconv_w_ref[LRU_CONV_BUF:LRU_CONV_BUF + 1, :] + lconv_b_ref[...]
        for j in range(LRU_CONV_BUF):
            xc = xc + st_lconv_ref[j] * lconv_w_ref[j:j + 1, :]
        ul_ref[...] = u_lru
        r = _sigmoid(_mm_hi(xc, wa_ref[...]) + ba_ref[...])
        ig = _sigmoid(_mm_hi(xc, wx_ref[...]) + bx_ref[...])
        reset = jnp.full(xc.shape, True) if start_pos == 0 else None
        a, b = _lru_coeffs(xc, r, ig, lam_ref[...], reset)
        h = a * st_h_ref[...] + b
        sh_ref[...] = h
        hg = h * _gelu_tanh(proj(O_GELU))
        merged = merged + gate(2) * _mm_hi(hg, wbl_ref[...])

        m_out = _mm_hi(merged, w_out_ref[...])
        y_ref[...] = _layer_norm(ALPHA * x + m_out, ln_g_ref[...], ln_b_ref[...])


def _const_spec(shape):
    nd = len(shape)
    return pl.BlockSpec(shape, lambda *_: (0,) * nd, pipeline_mode=pl.Buffered(1))


def _mixer_weight_list(p):
    return [p["w_in"], p["b_gate"], p["w_pool"], p["pool_scale"], p["conv_w"], p["conv_b"],
            p["conv_ln_g"], p["conv_ln_b"], p["lru_conv_w"], p["lru_conv_b"], p["lru_wa"],
            p["lru_ba"], p["lru_wx"], p["lru_bx"], p["lru_lambda"], p["w_br_pool"],
            p["w_br_conv"], p["w_br_lru"], p["w_out"], p["ln1_g"], p["ln1_b"]]


def _mixer_seq(x, p, batch, seq, wr_pad=None):
    tm = min(SEQ_TILE, seq)
    nt = seq // tm
    with_router = wr_pad is not None
    weights = _mixer_weight_list(p) + ([wr_pad] if with_router else [])
    in_specs = [pl.BlockSpec((tm, D_MODEL), lambda b, i: (b * nt + i, 0))]
    in_specs += [_const_spec(w.shape) for w in weights]
    out_shape = [
        jax.ShapeDtypeStruct((batch * seq, D_MODEL), F32),
        jax.ShapeDtypeStruct((batch, POOL_BUF, D_POOL), F32),
        jax.ShapeDtypeStruct((batch, CONV_BUF, D_CONV), F32),
        jax.ShapeDtypeStruct((batch, LRU_CONV_BUF, D_LRU), F32),
        jax.ShapeDtypeStruct((batch, 1, D_LRU), F32),
    ]
    out_specs = [
        pl.BlockSpec((tm, D_MODEL), lambda b, i: (b * nt + i, 0)),
        pl.BlockSpec((1, POOL_BUF, D_POOL), lambda b, i: (b, 0, 0)),
        pl.BlockSpec((1, CONV_BUF, D_CONV), lambda b, i: (b, 0, 0)),
        pl.BlockSpec((1, LRU_CONV_BUF, D_LRU), lambda b, i: (b, 0, 0)),
        pl.BlockSpec((1, 1, D_LRU), lambda b, i: (b, 0, 0)),
    ]
    if with_router:
        out_shape += [jax.ShapeDtypeStruct((batch * seq, LANES), jnp.int32),
                      jax.ShapeDtypeStruct((batch * seq, LANES), F32)]
        out_specs += [pl.BlockSpec((tm, LANES), lambda b, i: (b * nt + i, 0)),
                      pl.BlockSpec((tm, LANES), lambda b, i: (b * nt + i, 0))]
    scratch = [
        pltpu.VMEM((tm + POOL_OFF, D_POOL), F32),
        pltpu.VMEM((tm + CONV_OFF + SUBLANES, D_CONV), F32),
        pltpu.VMEM((tm + LCONV_OFF, D_LRU), F32),
        pltpu.VMEM((tm, D_CONV), F32),
        pltpu.VMEM((tm, D_LRU), F32),
        pltpu.VMEM((tm, D_LRU), F32),
        pltpu.VMEM((1, D_LRU), F32),
        pltpu.VMEM((tm, 3 * D_MODEL), F32),
        pltpu.VMEM((tm, D_LRU), F32),
        pltpu.VMEM((tm, D_MODEL), F32),
        pltpu.VMEM((tm, D_MODEL), BF16),
    ]
    y, sp, sc, slc, sh, *routing = pl.pallas_call(
        functools.partial(_mixer_seq_kernel, tm=tm, with_router=with_router),
        grid=(batch, nt),
        in_specs=in_specs,
        out_specs=out_specs,
        out_shape=out_shape,
        scratch_shapes=scratch,
        compiler_params=pltpu.CompilerParams(
            dimension_semantics=("arbitrary", "arbitrary"),
            vmem_limit_bytes=VMEM_LIMIT_BYTES),
        name="mixer_seq",
    )(x, *weights)
    return (y, sp, sc, slc, sh.reshape(batch, D_LRU), *routing)


def _mixer_step(x, st_pool, st_conv, st_lconv, st_h, p, start_pos):
    batch = x.shape[0]
    weights = _mixer_weight_list(p)
    states = [jnp.transpose(st_pool, (1, 0, 2)), jnp.transpose(st_conv, (1, 0, 2)),
              jnp.transpose(st_lconv, (1, 0, 2)), st_h]

    def spec(a):
        if not isinstance(a, tuple):
            return _const_spec(a.shape)
        arr, layer = a
        zeros = (0,) * (arr.ndim - 1)
        return pl.BlockSpec((None,) + arr.shape[1:], lambda k: (layer,) + zeros,
                            pipeline_mode=pl.Buffered(1))

    w_in_all, layer = weights[0]
    in_cols = w_in_all.shape[2]
    n_blocks = in_cols // STEP_COLS
    assert n_blocks * STEP_COLS == in_cols
    in_specs = [spec(a) for a in [x] + states + weights]
    in_specs[1 + len(states)] = pl.BlockSpec((None, D_MODEL, STEP_COLS), lambda k: (layer, 0, k))
    args = [x] + states + [w[0] if isinstance(w, tuple) else w for w in weights]
    out_shape = (
        jax.ShapeDtypeStruct((batch, D_MODEL), F32),
        jax.ShapeDtypeStruct((batch, D_POOL), F32),
        jax.ShapeDtypeStruct((batch, D_CONV), F32),
        jax.ShapeDtypeStruct((batch, D_LRU), F32),
        jax.ShapeDtypeStruct((batch, D_LRU), F32),
    )
    y, u_pool, glu, u_lru, sh = pl.pallas_call(
        functools.partial(_mixer_step_kernel, start_pos=start_pos, n_blocks=n_blocks),
        grid=(n_blocks,),
        in_specs=in_specs,
        out_specs=tuple(pl.BlockSpec(s.shape, lambda k: (0, 0)) for s in out_shape),
        out_shape=out_shape,
        scratch_shapes=[pltpu.VMEM((n_blocks, batch, STEP_COLS), F32)],
        compiler_params=pltpu.CompilerParams(
            dimension_semantics=("arbitrary",), vmem_limit_bytes=VMEM_LIMIT_BYTES),
        name="mixer_step",
    )(*args)

    def push(state, row):
        return jnp.concatenate([state[:, 1:], row[:, None]], axis=1)

    return y, push(st_pool, u_pool), push(st_conv, glu), push(st_lconv, u_lru), sh


def _ffn_kernel(x_ref, wg_ref, wu_ref, wd_ref, g_ref, b_ref, y_ref, *, d_ff, precise):
    x = x_ref[...]
    mm = _mm_hi if precise else _dot
    xb = x if precise else x.astype(BF16)
    acc = jnp.zeros(x.shape, F32)
    for c0 in range(0, d_ff, FFN_CHUNK):
        h = _silu(mm(xb, wg_ref[:, c0:c0 + FFN_CHUNK])) * mm(xb, wu_ref[:, c0:c0 + FFN_CHUNK])
        acc = acc + mm(h if precise else h.astype(BF16), wd_ref[c0:c0 + FFN_CHUNK, :])
    y_ref[...] = _layer_norm(ALPHA * x + acc, g_ref[...], b_ref[...])


def _ffn(x, wg, wu, wd, g, b, precise):
    n = x.shape[0]
    tm = min(SEQ_TILE, n)
    d_ff = wg.shape[1]
    weights = [wg, wu, wd, g, b]
    return pl.pallas_call(
        functools.partial(_ffn_kernel, d_ff=d_ff, precise=precise),
        grid=(n // tm,),
        in_specs=[pl.BlockSpec((tm, D_MODEL), lambda i: (i, 0))] + [_const_spec(w.shape) for w in weights],
        out_specs=pl.BlockSpec((tm, D_MODEL), lambda i: (i, 0)),
        out_shape=jax.ShapeDtypeStruct((n, D_MODEL), F32),
        compiler_params=pltpu.CompilerParams(
            dimension_semantics=("arbitrary",), vmem_limit_bytes=VMEM_LIMIT_BYTES),
        name="ffn_dense",
    )(x, *weights)


def _route(x, wr, precise):
    logits = _mm_hi(x, wr) if precise else _mm(x, wr)
    lane = lax.broadcasted_iota(jnp.int32, logits.shape, 1)
    neg = jnp.float32(-jnp.inf)
    l1 = jnp.where(lane < N_EXPERTS, logits, neg)
    v1 = jnp.max(l1, axis=-1, keepdims=True)
    i1 = jnp.min(jnp.where(l1 == v1, lane, LANES), axis=-1, keepdims=True)
    l2 = jnp.where(lane == i1, neg, l1)
    v2 = jnp.max(l2, axis=-1, keepdims=True)
    i2 = jnp.min(jnp.where(l2 == v2, lane, LANES), axis=-1, keepdims=True)
    e2 = jnp.exp(v2 - v1)
    den = 1.0 + e2
    idx = jnp.where(lane == 0, i1, jnp.where(lane == 1, i2, 0))
    gates = jnp.where(lane == 0, 1.0 / den, jnp.where(lane == 1, e2 / den, 0.0))
    return idx, gates


def _router_kernel(x_ref, wr_ref, idx_ref, gate_ref):
    idx_ref[...], gate_ref[...] = _route(x_ref[...], wr_ref[...], precise=True)


def _router(x, wr_pad):
    n = x.shape[0]
    tm = min(SEQ_TILE, n)
    return pl.pallas_call(
        _router_kernel,
        grid=(n // tm,),
        in_specs=[pl.BlockSpec((tm, D_MODEL), lambda i: (i, 0)), _const_spec(wr_pad.shape)],
        out_specs=(pl.BlockSpec((tm, LANES), lambda i: (i, 0)),
                   pl.BlockSpec((tm, LANES), lambda i: (i, 0))),
        out_shape=(jax.ShapeDtypeStruct((n, LANES), jnp.int32),
                   jax.ShapeDtypeStruct((n, LANES), F32)),
        compiler_params=pltpu.CompilerParams(dimension_semantics=("arbitrary",)),
        name="moe_router",
    )(x, wr_pad)


def _sc_row_gather(src, idx):
    n_idx = idx.shape[0]
    assert n_idx % SC_WINDOW == 0
    n_win = n_idx // SC_WINDOW
    n_workers = SC_CORES * SC_SUBCORES
    n_sub = SC_WINDOW // SC_SUB_ROWS
    mesh = plsc.VectorSubcoreMesh(core_axis_name="c", subcore_axis_name="s",
                                  num_cores=SC_CORES, num_subcores=SC_SUBCORES)

    def body(src_hbm, idx_hbm, out_hbm, idx_v, buf, g0, g1, w0, w1):
        gsem = (g0, g1)
        wsem = (w0, w1)
        worker = lax.axis_index("c") * SC_SUBCORES + lax.axis_index("s")

        @pl.loop(0, pl.cdiv(n_win, n_workers))
        def _(t):
            win = worker + t * n_workers

            @pl.when(win < n_win)
            def _():
                base = win * SC_WINDOW
                pltpu.sync_copy(idx_hbm.at[:, pl.ds(base, SC_WINDOW)], idx_v)

                def fetch(q):
                    return pltpu.async_copy(
                        src_hbm.at[idx_v.at[0, pl.ds(q * SC_SUB_ROWS, SC_SUB_ROWS)]],
                        buf.at[q % 2], gsem[q % 2])

                def put(q):
                    return pltpu.async_copy(
                        buf.at[q % 2], out_hbm.at[pl.ds(base + q * SC_SUB_ROWS, SC_SUB_ROWS)],
                        wsem[q % 2])

                fetches = {0: fetch(0)}
                puts = {}
                for q in range(n_sub):
                    if q + 1 < n_sub:
                        if q >= 1:
                            puts[q - 1].wait()
                        fetches[q + 1] = fetch(q + 1)
                    fetches[q].wait()
                    puts[q] = put(q)
                for q in range(max(0, n_sub - 2), n_sub):
                    puts[q].wait()

    return pl.kernel(
        body,
        out_type=jax.ShapeDtypeStruct((n_idx, D_MODEL), src.dtype),
        mesh=mesh,
        scratch_types=[pltpu.VMEM((1, SC_WINDOW), jnp.int32),
                       pltpu.VMEM((2, SC_SUB_ROWS, D_MODEL), src.dtype),
                       pltpu.SemaphoreType.DMA, pltpu.SemaphoreType.DMA,
                       pltpu.SemaphoreType.DMA, pltpu.SemaphoreType.DMA],
    )(src, idx.reshape(1, n_idx))


def _expert_kernel(te_ref, nu_ref, xs_ref, wg_ref, wu_ref, wd_ref, ys_ref, xs_bf, acc, *, tf, nf):
    del te_ref
    i = pl.program_id(0)
    j = pl.program_id(1)
    used = i < nu_ref[0]

    @pl.when(jnp.logical_and(used, j == 0))
    def _():
        xs_bf[...] = xs_ref[...].astype(BF16)

    @pl.when(used)
    def _():
        xb = xs_bf[...]
        part = None
        for c0 in range(0, tf, FFN_CHUNK):
            h = _silu(_dot(xb, wg_ref[:, c0:c0 + FFN_CHUNK])) * _dot(xb, wu_ref[:, c0:c0 + FFN_CHUNK])
            d = _dot(h.astype(BF16), wd_ref[c0:c0 + FFN_CHUNK, :])
            part = d if part is None else part + d

        @pl.when(j == 0)
        def _():
            acc[...] = part

        @pl.when(jnp.logical_and(j > 0, j < nf - 1))
        def _():
            acc[...] += part

        @pl.when(j == nf - 1)
        def _():
            ys_ref[...] = acc[...] + part

    @pl.when(jnp.logical_and(jnp.logical_not(used), j == nf - 1))
    def _():
        ys_ref[...] = jnp.zeros(ys_ref.shape, F32)


def _experts(xs, tile_expert, n_used, wg, wu, wd, tm):
    n_tiles = tile_expert.shape[0]
    d_e = wg.shape[2]
    tf = EXPERT_F_TILE
    nf = d_e // tf
    assert nf >= 2 and nf * tf == d_e and tf % FFN_CHUNK == 0

    def jeff(i, j, nu):
        return jnp.where(i < nu[0], j, nf - 1)

    grid_spec = pltpu.PrefetchScalarGridSpec(
        num_scalar_prefetch=2,
        grid=(n_tiles, nf),
        in_specs=[
            pl.BlockSpec((tm, D_MODEL), lambda i, j, te, nu: (jnp.minimum(i, nu[0] - 1), 0)),
            pl.BlockSpec((None, D_MODEL, tf), lambda i, j, te, nu: (te[i], 0, jeff(i, j, nu))),
            pl.BlockSpec((None, D_MODEL, tf), lambda i, j, te, nu: (te[i], 0, jeff(i, j, nu))),
            pl.BlockSpec((None, tf, D_MODEL), lambda i, j, te, nu: (te[i], jeff(i, j, nu), 0)),
        ],
        out_specs=pl.BlockSpec((tm, D_MODEL), lambda i, j, te, nu: (i, 0)),
        scratch_shapes=[
            pltpu.VMEM((tm, D_MODEL), BF16),
            pltpu.VMEM((tm, D_MODEL), F32),
        ],
    )
    return pl.pallas_call(
        functools.partial(_expert_kernel, tf=tf, nf=nf),
        grid_spec=grid_spec,
        out_shape=jax.ShapeDtypeStruct((n_tiles * tm, D_MODEL), F32),
        compiler_params=pltpu.CompilerParams(
            dimension_semantics=("arbitrary", "arbitrary"), vmem_limit_bytes=VMEM_LIMIT_BYTES),
        name="moe_experts",
    )(tile_expert, n_used, xs, wg, wu, wd)


def _combine_kernel(x_ref, gate_ref, y0_ref, y1_ref, g_ref, b_ref, y_ref):
    gates = gate_ref[...]
    f = gates[:, 0:1] * y0_ref[...] + gates[:, 1:2] * y1_ref[...]
    y_ref[...] = _layer_norm(ALPHA * x_ref[...] + f, g_ref[...], b_ref[...])


def _combine(x, gates, y01, g, b):
    n = x.shape[0]
    tm = min(SEQ_TILE, n)
    nt = n // tm
    return pl.pallas_call(
        _combine_kernel,
        grid=(nt,),
        in_specs=[
            pl.BlockSpec((tm, D_MODEL), lambda i: (i, 0)),
            pl.BlockSpec((tm, LANES), lambda i: (i, 0)),
            pl.BlockSpec((tm, D_MODEL), lambda i: (i, 0)),
            pl.BlockSpec((tm, D_MODEL), lambda i: (i + nt, 0)),
            _const_spec(g.shape),
            _const_spec(b.shape),
        ],
        out_specs=pl.BlockSpec((tm, D_MODEL), lambda i: (i, 0)),
        out_shape=jax.ShapeDtypeStruct((n, D_MODEL), F32),
        compiler_params=pltpu.CompilerParams(
            dimension_semantics=("arbitrary",), vmem_limit_bytes=VMEM_LIMIT_BYTES),
        name="moe_combine",
    )(x, gates, y01, y01, g, b)


def _moe(x, wr_pad, wg, wu, wd, g, b, tm, routing=None):
    n = x.shape[0]
    idx, gates = routing if routing is not None else _router(x, wr_pad)
    flat_e = idx[:, :TOP_K].reshape(-1)
    onehot = (flat_e[:, None] == jnp.arange(N_EXPERTS, dtype=jnp.int32)[None, :]).astype(jnp.int32)
    csum = jnp.cumsum(onehot, axis=0)
    rank = jnp.sum((csum - onehot) * onehot, axis=1)
    counts = csum[-1]
    padded = ((counts + tm - 1) // tm) * tm
    ends = jnp.cumsum(padded)
    slot = (ends - padded)[flat_e] + rank
    n_pairs = TOP_K * n
    n_tiles = n_pairs // tm + N_EXPERTS
    n_used = (ends[-1] // tm).astype(jnp.int32).reshape(1)
    tile_start = jnp.arange(n_tiles, dtype=jnp.int32) * tm
    tile_expert = jnp.sum((tile_start[:, None] >= ends[None, :]).astype(jnp.int32), axis=1)
    last_e = jnp.sum((ends[-1] - 1 >= ends).astype(jnp.int32))
    tile_expert = jnp.minimum(tile_expert, last_e).astype(jnp.int32)
    pair_sorted = jnp.sort(flat_e * n_pairs + jnp.arange(n_pairs, dtype=jnp.int32)) % n_pairs
    slot_ids = jnp.arange(n_tiles * tm, dtype=jnp.int32)
    slot_e = jnp.repeat(tile_expert, tm)
    local = slot_ids - (ends - padded)[slot_e]
    q = (jnp.cumsum(counts) - counts)[slot_e] + local
    valid = jnp.logical_and(local >= 0, local < counts[slot_e])
    tok_of_slot = jnp.where(valid, pair_sorted[jnp.clip(q, 0, n_pairs - 1)] // TOP_K, slot_ids % n)
    xs = _sc_row_gather(x, tok_of_slot)
    ys = _experts(xs, tile_expert, n_used, wg, wu, wd, tm)
    y01 = _sc_row_gather(ys, jnp.transpose(slot.reshape(n, TOP_K)).reshape(-1))
    return _combine(x, gates, y01, g, b)


def _block_diag(w):
    n, c, _ = w.shape
    eye = jnp.eye(n, dtype=w.dtype)
    return (eye[:, None, :, None] * w[:, :, None, :]).reshape(n * c, n * c)


def _row(v):
    return v.reshape(1, -1)


def kernel(x_prompt, x_sample, state_pool, state_conv, state_lru_conv, state_lru_h, w_in, b_gate, w_pool, pool_scale, conv_w, conv_b, conv_ln_g, conv_ln_b, lru_conv_w, lru_conv_b, lru_wa, lru_ba, lru_wx, lru_bx, lru_lambda, w_br_pool, w_br_conv, w_br_lru, w_out, ln1_g, ln1_b, ln2_g, ln2_b, ffn_w_gate, ffn_w_up, ffn_w_down, moe_router, moe_w_gate, moe_w_up, moe_w_down):
    batch, seq, _ = x_prompt.shape
    dec_batch = x_sample.shape[0]

    def layer_params(l, mat):
        return {
            "w_in": w_in[l].astype(mat), "b_gate": _row(b_gate[l]),
            "w_pool": w_pool[l].astype(mat), "pool_scale": _row(pool_scale[l]),
            "conv_w": conv_w[l], "conv_b": _row(conv_b[l]),
            "conv_ln_g": _row(conv_ln_g[l]), "conv_ln_b": _row(conv_ln_b[l]),
            "lru_conv_w": lru_conv_w[l], "lru_conv_b": _row(lru_conv_b[l]),
            "lru_wa": _block_diag(lru_wa[l]).astype(mat), "lru_ba": _row(lru_ba[l]),
            "lru_wx": _block_diag(lru_wx[l]).astype(mat), "lru_bx": _row(lru_bx[l]),
            "lru_lambda": _row(lru_lambda[l]),
            "w_br_pool": w_br_pool[l].astype(mat), "w_br_conv": w_br_conv[l].astype(mat),
            "w_br_lru": w_br_lru[l].astype(mat), "w_out": w_out[l].astype(mat),
            "ln1_g": _row(ln1_g[l]), "ln1_b": _row(ln1_b[l]),
        }

    layers = [layer_params(l, BF16) for l in range(DEPTH)]
    layers_f32 = [layer_params(l, F32) for l in range(DEPTH)]
    for l in range(DEPTH):
        layers_f32[l].update({"w_in": (w_in, l), "w_br_pool": (w_br_pool, l), "w_br_conv": (w_br_conv, l),
                              "w_br_lru": (w_br_lru, l), "w_out": (w_out, l)})
    ffn_w = [(ffn_w_gate[m].astype(BF16), ffn_w_up[m].astype(BF16), ffn_w_down[m].astype(BF16))
             for m in range(ffn_w_gate.shape[0])]
    ffn_w_f32 = [(ffn_w_gate[m], ffn_w_up[m], ffn_w_down[m]) for m in range(ffn_w_gate.shape[0])]
    moe_w = [(jnp.pad(moe_router[m], ((0, 0), (0, LANES - N_EXPERTS))),
              moe_w_gate[m].astype(BF16), moe_w_up[m].astype(BF16), moe_w_down[m].astype(BF16))
             for m in range(moe_router.shape[0])]

    x_prompt, moe_w = lax.optimization_barrier((x_prompt, moe_w))

    def channel_mixer(l, x, moe_tile, routing=None, precise=False):
        g, b = _row(ln2_g[l]), _row(ln2_b[l])
        if l % 2 == 0:
            return _ffn(x, *(ffn_w_f32 if precise else ffn_w)[l // 2], g, b, precise)
        return _moe(x, *moe_w[l // 2], g, b, moe_tile, routing)

    x = x_prompt.reshape(batch * seq, D_MODEL)
    p_states = []
    for l in range(DEPTH):
        wr_pad = moe_w[l // 2][0].astype(BF16) if l % 2 == 1 else None
        x, sp, sc, slc, sh, *routing = _mixer_seq(x, layers[l], batch, seq, wr_pad)
        p_states.append((sp, sc, slc, sh))
        x = channel_mixer(l, x, MOE_TILE_SEQ, tuple(routing) or None)
    y_prompt = x.reshape(batch, seq, D_MODEL)

    x = x_sample.reshape(dec_batch, D_MODEL)
    s_states = []
    for l in range(DEPTH):
        x, sp, sc, slc, sh = _mixer_step(x, state_pool[l], state_conv[l], state_lru_conv[l],
                                         state_lru_h[l], layers_f32[l], PAST_LEN)
        s_states.append((sp, sc, slc, sh))
        x = channel_mixer(l, x, MOE_TILE_STEP, precise=True)
    y_sample = x.reshape(dec_batch, 1, D_MODEL)

    def stack(states, k):
        return jnp.stack([s[k] for s in states])

    return (y_prompt, y_sample,
            stack(p_states, 0), stack(p_states, 1), stack(p_states, 2), stack(p_states, 3),
            stack(s_states, 0), stack(s_states, 1), stack(s_states, 2), stack(s_states, 3))
```

```python
import functools

import jax
import jax.numpy as jnp
from jax import lax
from jax.experimental import pallas as pl
from jax.experimental.pallas import tpu as pltpu
from jax.experimental.pallas import tpu_sc as plsc

D_MODEL = 1024
DEPTH = 2
PAST_LEN = 16384
D_POOL = 512
N_POOL_GROUPS = 4
POOL_GROUP = 128
POOL_WINDOWS = (2, 4, 8, 16)
POOL_BUF = 15
D_CONV = 512
CONV_WIDTH = 31
CONV_BUF = 30
D_LRU = 512
N_LRU_BLOCKS = 8
LRU_BLOCK = 64
LRU_CONV_WIDTH = 4
LRU_CONV_BUF = 3
LRU_C = 8.0
N_EXPERTS = 8
TOP_K = 2
ALPHA = (2.0 * DEPTH) ** 0.25
LN_EPS = 1e-5

O_POOL = 0
O_VAL = D_POOL
O_GLU = O_VAL + D_CONV
O_LRU = O_GLU + D_CONV
O_GELU = O_LRU + D_LRU
O_GATE = O_GELU + D_LRU

SUBLANES = 8
LANES = 128
VMEM_LIMIT_BYTES = 56 * 1024 * 1024
SC_CORES = 2
SC_SUBCORES = 16
SC_WINDOW = 128
SC_SUB_ROWS = 32

SEQ_TILE = 512
CONV_ROWS = 128
POOL_OFF = 16
CONV_OFF = 32
LCONV_OFF = 8
FFN_CHUNK = 256
PIECE = 256
STEP_COLS = 512
EXPERT_F_TILE = 1792
MOE_TILE_SEQ = 512
MOE_TILE_STEP = 128

BF16 = jnp.bfloat16
F32 = jnp.float32


def _sigmoid(x):
    return 0.5 * jnp.tanh(0.5 * x) + 0.5


def _silu(x):
    return x * _sigmoid(x)


def _gelu_tanh(x):
    return x * (0.5 * (1.0 + jnp.tanh(0.7978845608028654 * (x + 0.044715 * (x * x * x)))))


def _softplus(z):
    return jnp.maximum(z, 0.0) + jnp.log1p(jnp.exp(-jnp.abs(z)))


def _layer_norm(x, g, b):
    mu = jnp.mean(x, axis=-1, keepdims=True)
    xc = x - mu
    var = jnp.mean(xc * xc, axis=-1, keepdims=True)
    return xc * lax.rsqrt(var + LN_EPS) * g + b


def _dot(a, b):
    return jnp.dot(a, b, preferred_element_type=F32)


def _mm(a, w):
    return _dot(a.astype(BF16), w)


def _mm_hi(a, w):
    return jnp.dot(a, w, preferred_element_type=F32, precision=lax.Precision.HIGHEST)


def _lru_coeffs(xc, r, ig, lam, reset):
    log_a = (-LRU_C * r) * _softplus(-lam)
    a = jnp.exp(log_a)
    mult = jnp.sqrt(-jnp.tanh(log_a) * (a * a + 1.0))
    if reset is not None:
        a = jnp.where(reset, 0.0, a)
        mult = jnp.where(reset, 1.0, mult)
    return a, mult * ig * xc


def _mixer_seq_kernel(x_ref, w_in_ref, b_gate_ref, w_pool_ref, pool_scale_ref,
                      conv_w_ref, conv_b_ref, cln_g_ref, cln_b_ref,
                      lconv_w_ref, lconv_b_ref, wa_ref, ba_ref, wx_ref, bx_ref, lam_ref,
                      wbp_ref, wbc_ref, wbl_ref, w_out_ref, ln_g_ref, ln_b_ref, *rest, tm, with_router):
    if with_router:
        wr_ref, rest = rest[0], rest[1:]
        idx_ref, gate_ref, rest = rest[5], rest[6], rest[:5] + rest[7:]
    (y_ref, sp_ref, sc_ref, slc_ref, sh_ref,
     pool_ext, conv_ext, lconv_ext, act_buf, a_buf, b_buf, h_carry, gate_buf, gelu_buf,
     merged_buf, xb_buf) = rest
    i = pl.program_id(1)

    @pl.when(i == 0)
    def _():
        pool_ext[0:POOL_OFF, :] = jnp.zeros((POOL_OFF, D_POOL), F32)
        conv_ext[0:CONV_OFF, :] = jnp.zeros((CONV_OFF, D_CONV), F32)
        conv_ext[tm + CONV_OFF:tm + CONV_OFF + SUBLANES, :] = jnp.zeros((SUBLANES, D_CONV), F32)
        lconv_ext[0:LCONV_OFF, :] = jnp.zeros((LCONV_OFF, D_LRU), F32)
        h_carry[...] = jnp.zeros((1, D_LRU), F32)

    xb_buf[...] = x_ref[...].astype(BF16)

    def proj(lo, width):
        return _dot(xb_buf[...], w_in_ref[:, lo:lo + width])

    pos = i * tm + lax.broadcasted_iota(jnp.int32, (tm, 1), 0)

    glu = proj(O_VAL, D_CONV) * _sigmoid(proj(O_GLU, D_CONV))
    conv_ext[CONV_OFF:CONV_OFF + tm, :] = glu

    def pool_piece():
        u_pool = proj(O_POOL, D_POOL)
        pool_ext[POOL_OFF:POOL_OFF + tm, :] = u_pool
        parts = []
        for g, w in enumerate(POOL_WINDOWS):
            lo = g * POOL_GROUP
            acc = u_pool[:, lo:lo + POOL_GROUP]
            for k in range(1, w):
                acc = acc + pool_ext[POOL_OFF - k:POOL_OFF - k + tm, lo:lo + POOL_GROUP]
            inv_cnt = 1.0 / jnp.minimum(w, pos + 1).astype(F32)
            pooled = acc * inv_cnt - u_pool[:, lo:lo + POOL_GROUP]
            parts.append(_dot(pooled.astype(BF16), w_pool_ref[g]))
        mixed = jnp.concatenate(parts, axis=1) * pool_scale_ref[...]
        merged_buf[...] = gate_buf[:, 0:D_MODEL] * _dot(mixed.astype(BF16), wbp_ref[...])
        sp_ref[0] = pool_ext[tm + POOL_OFF - POOL_BUF:tm + POOL_OFF, :]
        pool_ext[0:POOL_OFF, :] = pool_ext[tm:tm + POOL_OFF, :]

    def lru_piece(lo):
        lconv_ext[LCONV_OFF:LCONV_OFF + tm, lo:lo + PIECE] = proj(O_LRU + lo, PIECE)

    def gelu_piece(lo):
        gelu_buf[:, lo:lo + PIECE] = _gelu_tanh(proj(O_GELU + lo, PIECE))

    def gate_piece(lo):
        gate_buf[:, lo:lo + PIECE] = _sigmoid(
            proj(O_GATE + lo, PIECE) + b_gate_ref[:, lo:lo + PIECE])

    pieces = ([functools.partial(gate_piece, lo) for lo in range(0, D_MODEL, PIECE)]
              + [pool_piece]
              + [functools.partial(lru_piece, lo) for lo in range(0, D_LRU, PIECE)]
              + [functools.partial(gate_piece, lo) for lo in range(D_MODEL, 2 * D_MODEL, PIECE)]
              + [functools.partial(gelu_piece, lo) for lo in range(0, D_LRU, PIECE)]
              + [functools.partial(gate_piece, lo) for lo in range(2 * D_MODEL, 3 * D_MODEL, PIECE)])
    n_conv_chunks = tm // CONV_ROWS
    base = CONV_OFF - CONV_BUF
    rows = CONV_ROWS + SUBLANES
    n_units = n_conv_chunks * (D_CONV // LANES)
    unit = 0
    for c in range(n_conv_chunks):
        c0 = c * CONV_ROWS
        for l0 in range(0, D_CONV, LANES):
            acc = jnp.zeros((CONV_ROWS, LANES), F32) + conv_b_ref[:, l0:l0 + LANES]
            for r in range(SUBLANES):
                z = None
                for k in range(CONV_WIDTH):
                    if (base + k) % SUBLANES != r:
                        continue
                    q8 = base + k - r
                    t = (conv_ext[c0 + q8:c0 + q8 + rows, l0:l0 + LANES]
                         * conv_w_ref[k:k + 1, l0:l0 + LANES])
                    z = t if z is None else z + t
                acc = acc + z[r:r + CONV_ROWS, :]
            act_buf[c0:c0 + CONV_ROWS, l0:l0 + LANES] = acc
            for piece in pieces[unit * len(pieces) // n_units:(unit + 1) * len(pieces) // n_units]:
                piece()
            unit += 1
        act_buf[c0:c0 + CONV_ROWS, :] = _silu(_layer_norm(
            act_buf[c0:c0 + CONV_ROWS, :], cln_g_ref[...], cln_b_ref[...]))
    merged_buf[...] += gate_buf[:, D_MODEL:2 * D_MODEL] * _dot(
        act_buf[...].astype(BF16), wbc_ref[...])
    sc_ref[0] = conv_ext[tm + CONV_OFF - CONV_BUF:tm + CONV_OFF, :]
    conv_ext[0:CONV_OFF, :] = conv_ext[tm:tm + CONV_OFF, :]

    base = LCONV_OFF - LRU_CONV_BUF
    xc = jnp.zeros((tm, D_LRU), F32) + lconv_b_ref[...]
    for k in range(LRU_CONV_WIDTH):
        xc = xc + lconv_ext[base + k:base + k + tm, :] * lconv_w_ref[k:k + 1, :]
    slc_ref[0] = lconv_ext[tm + LCONV_OFF - LRU_CONV_BUF:tm + LCONV_OFF, :]
    lconv_ext[0:LCONV_OFF, :] = lconv_ext[tm:tm + LCONV_OFF, :]
    xcb = xc.astype(BF16)
    r = _sigmoid(_dot(xcb, wa_ref[...]) + ba_ref[...])
    ig = _sigmoid(_dot(xcb, wx_ref[...]) + bx_ref[...])
    a, b = _lru_coeffs(xc, r, ig, lam_ref[...], pos == 0)
    a_buf[...] = a
    b_buf[...] = b

    row8 = lax.broadcasted_iota(jnp.int32, (SUBLANES, D_LRU), 0)

    def scan_block(j, h):
        r0 = pl.multiple_of(j * SUBLANES, SUBLANES)
        a8 = a_buf[pl.ds(r0, SUBLANES), :]
        b8 = b_buf[pl.ds(r0, SUBLANES), :]
        for k in (1, 2, 4):
            a_s = pltpu.roll(a8, k, 0)
            b_s = pltpu.roll(b8, k, 0)
            m = row8 >= k
            b8 = jnp.where(m, a8 * b_s + b8, b8)
            a8 = jnp.where(m, a8 * a_s, a8)
        h8 = a8 * h + b8
        a_buf[pl.ds(r0, SUBLANES), :] = h8
        return h8[SUBLANES - 1:SUBLANES, :]

    h_last = lax.fori_loop(0, tm // SUBLANES, scan_block, h_carry[...], unroll=2)
    h_carry[...] = h_last
    sh_ref[0] = h_last
    hg = a_buf[...] * gelu_buf[...]
    merged = merged_buf[...] + gate_buf[:, 2 * D_MODEL:3 * D_MODEL] * _dot(
        hg.astype(BF16), wbl_ref[...])

    m_out = _dot(merged.astype(BF16), w_out_ref[...])
    y = _layer_norm(ALPHA * x_ref[...] + m_out, ln_g_ref[...], ln_b_ref[...])
    y_ref[...] = y
    if with_router:
        idx_ref[...], gate_ref[...] = _route(y, wr_ref[...], precise=False)


def _mixer_step_kernel(x_ref, st_pool_ref, st_conv_ref, st_lconv_ref, st_h_ref,
                       w_in_ref, b_gate_ref, w_pool_ref, pool_scale_ref,
                       conv_w_ref, conv_b_ref, cln_g_ref, cln_b_ref,
                       lconv_w_ref, lconv_b_ref, wa_ref, ba_ref, wx_ref, bx_ref, lam_ref,
                       wbp_ref, wbc_ref, wbl_ref, w_out_ref, ln_g_ref, ln_b_ref,
                       y_ref, up_ref, glu_ref, ul_ref, sh_ref, u_buf, *, start_pos, n_blocks):
    k = pl.program_id(0)
    u_buf[k] = _mm_hi(x_ref[...], w_in_ref[...])

    @pl.when(k == n_blocks - 1)
    def _():
        x = x_ref[...]

        def proj(lo):
            return u_buf[lo // STEP_COLS]

        def gate(n):
            lo = n * D_MODEL
            u = jnp.concatenate([proj(O_GATE + lo), proj(O_GATE + lo + STEP_COLS)], axis=1)
            return _sigmoid(u + b_gate_ref[:, lo:lo + D_MODEL])

        u_pool = proj(O_POOL)
        parts = []
        for g, w in enumerate(POOL_WINDOWS):
            lo = g * POOL_GROUP
            acc = u_pool[:, lo:lo + POOL_GROUP]
            for j in range(1, w):
                acc = acc + st_pool_ref[POOL_BUF - j, :, lo:lo + POOL_GROUP]
            pooled = acc * (1.0 / min(w, start_pos + 1)) - u_pool[:, lo:lo + POOL_GROUP]
            parts.append(_mm_hi(pooled, w_pool_ref[g]))
        mixed = jnp.concatenate(parts, axis=1) * pool_scale_ref[...]
        merged = gate(0) * _mm_hi(mixed, wbp_ref[...])
        up_ref[...] = u_pool

        glu = proj(O_VAL) * _sigmoid(proj(O_GLU))
        acc = glu * conv_w_ref[CONV_BUF:CONV_BUF + 1, :] + conv_b_ref[...]
        for j in range(CONV_BUF):
            acc = acc + st_conv_ref[j] * conv_w_ref[j:j + 1, :]
        act = _silu(_layer_norm(acc, cln_g_ref[...], cln_b_ref[...]))
        merged = merged + gate(1) * _mm_hi(act, wbc_ref[...])
        glu_ref[...] = glu

        u_lru = proj(O_LRU)
        xc = u_lru * lconv_w_ref[LRU_CONV_BUF:LRU_CONV_BUF + 1, :] + lconv_b_ref[...]
        for j in range(LRU_CONV_BUF):
            xc = xc + st_lconv_ref[j] * lconv_w_ref[j:j + 1, :]
        ul_ref[...] = u_lru
        r = _sigmoid(_mm_hi(xc, wa_ref[...]) + ba_ref[...])
        ig = _sigmoid(_mm_hi(xc, wx_ref[...]) + bx_ref[...])
        reset = jnp.full(xc.shape, True) if start_pos == 0 else None
        a, b = _lru_coeffs(xc, r, ig, lam_ref[...], reset)
        h = a * st_h_ref[...] + b
        sh_ref[...] = h
        hg = h * _gelu_tanh(proj(O_GELU))
        merged = merged + gate(2) * _mm_hi(hg, wbl_ref[...])

        m_out = _mm_hi(merged, w_out_ref[...])
        y_ref[...] = _layer_norm(ALPHA * x + m_out, ln_g_ref[...], ln_b_ref[...])


def _const_spec(shape):
    nd = len(shape)
    return pl.BlockSpec(shape, lambda *_: (0,) * nd, pipeline_mode=pl.Buffered(1))


def _mixer_weight_list(p):
    return [p["w_in"], p["b_gate"], p["w_pool"], p["pool_scale"], p["conv_w"], p["conv_b"],
            p["conv_ln_g"], p["conv_ln_b"], p["lru_conv_w"], p["lru_conv_b"], p["lru_wa"],
            p["lru_ba"], p["lru_wx"], p["lru_bx"], p["lru_lambda"], p["w_br_pool"],
            p["w_br_conv"], p["w_br_lru"], p["w_out"], p["ln1_g"], p["ln1_b"]]


def _mixer_seq(x, p, batch, seq, wr_pad=None):
    tm = min(SEQ_TILE, seq)
    nt = seq // tm
    with_router = wr_pad is not None
    weights = _mixer_weight_list(p) + ([wr_pad] if with_router else [])
    in_specs = [pl.BlockSpec((tm, D_MODEL), lambda b, i: (b * nt + i, 0))]
    in_specs += [_const_spec(w.shape) for w in weights]
    out_shape = [
        jax.ShapeDtypeStruct((batch * seq, D_MODEL), F32),
        jax.ShapeDtypeStruct((batch, POOL_BUF, D_POOL), F32),
        jax.ShapeDtypeStruct((batch, CONV_BUF, D_CONV), F32),
        jax.ShapeDtypeStruct((batch, LRU_CONV_BUF, D_LRU), F32),
        jax.ShapeDtypeStruct((batch, 1, D_LRU), F32),
    ]
    out_specs = [
        pl.BlockSpec((tm, D_MODEL), lambda b, i: (b * nt + i, 0)),
        pl.BlockSpec((1, POOL_BUF, D_POOL), lambda b, i: (b, 0, 0)),
        pl.BlockSpec((1, CONV_BUF, D_CONV), lambda b, i: (b, 0, 0)),
        pl.BlockSpec((1, LRU_CONV_BUF, D_LRU), lambda b, i: (b, 0, 0)),
        pl.BlockSpec((1, 1, D_LRU), lambda b, i: (b, 0, 0)),
    ]
    if with_router:
        out_shape += [jax.ShapeDtypeStruct((batch * seq, LANES), jnp.int32),
                      jax.ShapeDtypeStruct((batch * seq, LANES), F32)]
        out_specs += [pl.BlockSpec((tm, LANES), lambda b, i: (b * nt + i, 0)),
                      pl.BlockSpec((tm, LANES), lambda b, i: (b * nt + i, 0))]
    scratch = [
        pltpu.VMEM((tm + POOL_OFF, D_POOL), F32),
        pltpu.VMEM((tm + CONV_OFF + SUBLANES, D_CONV), F32),
        pltpu.VMEM((tm + LCONV_OFF, D_LRU), F32),
        pltpu.VMEM((tm, D_CONV), F32),
        pltpu.VMEM((tm, D_LRU), F32),
        pltpu.VMEM((tm, D_LRU), F32),
        pltpu.VMEM((1, D_LRU), F32),
        pltpu.VMEM((tm, 3 * D_MODEL), F32),
        pltpu.VMEM((tm, D_LRU), F32),
        pltpu.VMEM((tm, D_MODEL), F32),
        pltpu.VMEM((tm, D_MODEL), BF16),
    ]
    y, sp, sc, slc, sh, *routing = pl.pallas_call(
        functools.partial(_mixer_seq_kernel, tm=tm, with_router=with_router),
        grid=(batch, nt),
        in_specs=in_specs,
        out_specs=out_specs,
        out_shape=out_shape,
        scratch_shapes=scratch,
        compiler_params=pltpu.CompilerParams(
            dimension_semantics=("arbitrary", "arbitrary"),
            vmem_limit_bytes=VMEM_LIMIT_BYTES),
        name="mixer_seq",
    )(x, *weights)
    return (y, sp, sc, slc, sh.reshape(batch, D_LRU), *routing)


def _mixer_step(x, st_pool, st_conv, st_lconv, st_h, p, start_pos):
    batch = x.shape[0]
    weights = _mixer_weight_list(p)
    states = [jnp.transpose(st_pool, (1, 0, 2)), jnp.transpose(st_conv, (1, 0, 2)),
              jnp.transpose(st_lconv, (1, 0, 2)), st_h]

    def spec(a):
        if not isinstance(a, tuple):
            return _const_spec(a.shape)
        arr, layer = a
        zeros = (0,) * (arr.ndim - 1)
        return pl.BlockSpec((None,) + arr.shape[1:], lambda k: (layer,) + zeros,
                            pipeline_mode=pl.Buffered(1))

    w_in_all, layer = weights[0]
    in_cols = w_in_all.shape[2]
    n_blocks = in_cols // STEP_COLS
    assert n_blocks * STEP_COLS == in_cols
    in_specs = [spec(a) for a in [x] + states + weights]
    in_specs[1 + len(states)] = pl.BlockSpec((None, D_MODEL, STEP_COLS), lambda k: (layer, 0, k))
    args = [x] + states + [w[0] if isinstance(w, tuple) else w for w in weights]
    out_shape = (
        jax.ShapeDtypeStruct((batch, D_MODEL), F32),
        jax.ShapeDtypeStruct((batch, D_POOL), F32),
        jax.ShapeDtypeStruct((batch, D_CONV), F32),
        jax.ShapeDtypeStruct((batch, D_LRU), F32),
        jax.ShapeDtypeStruct((batch, D_LRU), F32),
    )
    y, u_pool, glu, u_lru, sh = pl.pallas_call(
        functools.partial(_mixer_step_kernel, start_pos=start_pos, n_blocks=n_blocks),
        grid=(n_blocks,),
        in_specs=in_specs,
        out_specs=tuple(pl.BlockSpec(s.shape, lambda k: (0, 0)) for s in out_shape),
        out_shape=out_shape,
        scratch_shapes=[pltpu.VMEM((n_blocks, batch, STEP_COLS), F32)],
        compiler_params=pltpu.CompilerParams(
            dimension_semantics=("arbitrary",), vmem_limit_bytes=VMEM_LIMIT_BYTES),
        name="mixer_step",
    )(*args)

    def push(state, row):
        return jnp.concatenate([state[:, 1:], row[:, None]], axis=1)

    return y, push(st_pool, u_pool), push(st_conv, glu), push(st_lconv, u_lru), sh


def _ffn_kernel(x_ref, wg_ref, wu_ref, wd_ref, g_ref, b_ref, y_ref, *, d_ff, precise):
    x = x_ref[...]
    mm = _mm_hi if precise else _dot
    xb = x if precise else x.astype(BF16)
    acc = jnp.zeros(x.shape, F32)
    for c0 in range(0, d_ff, FFN_CHUNK):
        h = _silu(mm(xb, wg_ref[:, c0:c0 + FFN_CHUNK])) * mm(xb, wu_ref[:, c0:c0 + FFN_CHUNK])
        acc = acc + mm(h if precise else h.astype(BF16), wd_ref[c0:c0 + FFN_CHUNK, :])
    y_ref[...] = _layer_norm(ALPHA * x + acc, g_ref[...], b_ref[...])


def _ffn(x, wg, wu, wd, g, b, precise):
    n = x.shape[0]
    tm = min(SEQ_TILE, n)
    d_ff = wg.shape[1]
    weights = [wg, wu, wd, g, b]
    return pl.pallas_call(
        functools.partial(_ffn_kernel, d_ff=d_ff, precise=precise),
        grid=(n // tm,),
        in_specs=[pl.BlockSpec((tm, D_MODEL), lambda i: (i, 0))] + [_const_spec(w.shape) for w in weights],
        out_specs=pl.BlockSpec((tm, D_MODEL), lambda i: (i, 0)),
        out_shape=jax.ShapeDtypeStruct((n, D_MODEL), F32),
        compiler_params=pltpu.CompilerParams(
            dimension_semantics=("arbitrary",), vmem_limit_bytes=VMEM_LIMIT_BYTES),
        name="ffn_dense",
    )(x, *weights)


def _route(x, wr, precise):
    logits = _mm_hi(x, wr) if precise else _mm(x, wr)
    lane = lax.broadcasted_iota(jnp.int32, logits.shape, 1)
    neg = jnp.float32(-jnp.inf)
    l1 = jnp.where(lane < N_EXPERTS, logits, neg)
    v1 = jnp.max(l1, axis=-1, keepdims=True)
    i1 = jnp.min(jnp.where(l1 == v1, lane, LANES), axis=-1, keepdims=True)
    l2 = jnp.where(lane == i1, neg, l1)
    v2 = jnp.max(l2, axis=-1, keepdims=True)
    i2 = jnp.min(jnp.where(l2 == v2, lane, LANES), axis=-1, keepdims=True)
    e2 = jnp.exp(v2 - v1)
    den = 1.0 + e2
    idx = jnp.where(lane == 0, i1, jnp.where(lane == 1, i2, 0))
    gates = jnp.where(lane == 0, 1.0 / den, jnp.where(lane == 1, e2 / den, 0.0))
    return idx, gates


def _router_kernel(x_ref, wr_ref, idx_ref, gate_ref):
    idx_ref[...], gate_ref[...] = _route(x_ref[...], wr_ref[...], precise=True)


def _router(x, wr_pad):
    n = x.shape[0]
    tm = min(SEQ_TILE, n)
    return pl.pallas_call(
        _router_kernel,
        grid=(n // tm,),
        in_specs=[pl.BlockSpec((tm, D_MODEL), lambda i: (i, 0)), _const_spec(wr_pad.shape)],
        out_specs=(pl.BlockSpec((tm, LANES), lambda i: (i, 0)),
                   pl.BlockSpec((tm, LANES), lambda i: (i, 0))),
        out_shape=(jax.ShapeDtypeStruct((n, LANES), jnp.int32),
                   jax.ShapeDtypeStruct((n, LANES), F32)),
        compiler_params=pltpu.CompilerParams(dimension_semantics=("arbitrary",)),
        name="moe_router",
    )(x, wr_pad)


def _sc_row_gather(src, idx):
    n_idx = idx.shape[0]
    assert n_idx % SC_WINDOW == 0
    n_win = n_idx // SC_WINDOW
    n_workers = SC_CORES * SC_SUBCORES
    n_sub = SC_WINDOW // SC_SUB_ROWS
    mesh = plsc.VectorSubcoreMesh(core_axis_name="c", subcore_axis_name="s",
                                  num_cores=SC_CORES, num_subcores=SC_SUBCORES)

    def body(src_hbm, idx_hbm, out_hbm, idx_v, buf, g0, g1, w0, w1):
        gsem = (g0, g1)
        wsem = (w0, w1)
        worker = lax.axis_index("c") * SC_SUBCORES + lax.axis_index("s")

        @pl.loop(0, pl.cdiv(n_win, n_workers))
        def _(t):
            win = worker + t * n_workers

            @pl.when(win < n_win)
            def _():
                base = win * SC_WINDOW
                pltpu.sync_copy(idx_hbm.at[:, pl.ds(base, SC_WINDOW)], idx_v)

                def fetch(q):
                    return pltpu.async_copy(
                        src_hbm.at[idx_v.at[0, pl.ds(q * SC_SUB_ROWS, SC_SUB_ROWS)]],
                        buf.at[q % 2], gsem[q % 2])

                def put(q):
                    return pltpu.async_copy(
                        buf.at[q % 2], out_hbm.at[pl.ds(base + q * SC_SUB_ROWS, SC_SUB_ROWS)],
                        wsem[q % 2])

                fetches = {0: fetch(0)}
                puts = {}
                for q in range(n_sub):
                    if q + 1 < n_sub:
                        if q >= 1:
                            puts[q - 1].wait()
                        fetches[q + 1] = fetch(q + 1)
                    fetches[q].wait()
                    puts[q] = put(q)
                for q in range(max(0, n_sub - 2), n_sub):
                    puts[q].wait()

    return pl.kernel(
        body,
        out_type=jax.ShapeDtypeStruct((n_idx, D_MODEL), src.dtype),
        mesh=mesh,
        scratch_types=[pltpu.VMEM((1, SC_WINDOW), jnp.int32),
                       pltpu.VMEM((2, SC_SUB_ROWS, D_MODEL), src.dtype),
                       pltpu.SemaphoreType.DMA, pltpu.SemaphoreType.DMA,
                       pltpu.SemaphoreType.DMA, pltpu.SemaphoreType.DMA],
    )(src, idx.reshape(1, n_idx))


def _expert_kernel(te_ref, nu_ref, xs_ref, wg_ref, wu_ref, wd_ref, ys_ref, xs_bf, acc, *, tf, nf):
    del te_ref
    i = pl.program_id(0)
    j = pl.program_id(1)
    used = i < nu_ref[0]

    @pl.when(jnp.logical_and(used, j == 0))
    def _():
        xs_bf[...] = xs_ref[...].astype(BF16)

    @pl.when(used)
    def _():
        xb = xs_bf[...]
        part = None
        for c0 in range(0, tf, FFN_CHUNK):
            h = _silu(_dot(xb, wg_ref[:, c0:c0 + FFN_CHUNK])) * _dot(xb, wu_ref[:, c0:c0 + FFN_CHUNK])
            d = _dot(h.astype(BF16), wd_ref[c0:c0 + FFN_CHUNK, :])
            part = d if part is None else part + d

        @pl.when(j == 0)
        def _():
            acc[...] = part

        @pl.when(jnp.logical_and(j > 0, j < nf - 1))
        def _():
            acc[...] += part

        @pl.when(j == nf - 1)
        def _():
            ys_ref[...] = acc[...] + part

    @pl.when(jnp.logical_and(jnp.logical_not(used), j == nf - 1))
    def _():
        ys_ref[...] = jnp.zeros(ys_ref.shape, F32)


def _experts(xs, tile_expert, n_used, wg, wu, wd, tm):
    n_tiles = tile_expert.shape[0]
    d_e = wg.shape[2]
    tf = EXPERT_F_TILE
    nf = d_e // tf
    assert nf >= 2 and nf * tf == d_e and tf % FFN_CHUNK == 0

    def jeff(i, j, nu):
        return jnp.where(i < nu[0], j, nf - 1)

    grid_spec = pltpu.PrefetchScalarGridSpec(
        num_scalar_prefetch=2,
        grid=(n_tiles, nf),
        in_specs=[
            pl.BlockSpec((tm, D_MODEL), lambda i, j, te, nu: (jnp.minimum(i, nu[0] - 1), 0)),
            pl.BlockSpec((None, D_MODEL, tf), lambda i, j, te, nu: (te[i], 0, jeff(i, j, nu))),
            pl.BlockSpec((None, D_MODEL, tf), lambda i, j, te, nu: (te[i], 0, jeff(i, j, nu))),
            pl.BlockSpec((None, tf, D_MODEL), lambda i, j, te, nu: (te[i], jeff(i, j, nu), 0)),
        ],
        out_specs=pl.BlockSpec((tm, D_MODEL), lambda i, j, te, nu: (i, 0)),
        scratch_shapes=[
            pltpu.VMEM((tm, D_MODEL), BF16),
            pltpu.VMEM((tm, D_MODEL), F32),
        ],
    )
    return pl.pallas_call(
        functools.partial(_expert_kernel, tf=tf, nf=nf),
        grid_spec=grid_spec,
        out_shape=jax.ShapeDtypeStruct((n_tiles * tm, D_MODEL), F32),
        compiler_params=pltpu.CompilerParams(
            dimension_semantics=("arbitrary", "arbitrary"), vmem_limit_bytes=VMEM_LIMIT_BYTES),
        name="moe_experts",
    )(tile_expert, n_used, xs, wg, wu, wd)


def _combine_kernel(x_ref, gate_ref, y0_ref, y1_ref, g_ref, b_ref, y_ref):
    gates = gate_ref[...]
    f = gates[:, 0:1] * y0_ref[...] + gates[:, 1:2] * y1_ref[...]
    y_ref[...] = _layer_norm(ALPHA * x_ref[...] + f, g_ref[...], b_ref[...])


def _combine(x, gates, y01, g, b):
    n = x.shape[0]
    tm = min(SEQ_TILE, n)
    nt = n // tm
    return pl.pallas_call(
        _combine_kernel,
        grid=(nt,),
        in_specs=[
            pl.BlockSpec((tm, D_MODEL), lambda i: (i, 0)),
            pl.BlockSpec((tm, LANES), lambda i: (i, 0)),
            pl.BlockSpec((tm, D_MODEL), lambda i: (i, 0)),
            pl.BlockSpec((tm, D_MODEL), lambda i: (i + nt, 0)),
            _const_spec(g.shape),
            _const_spec(b.shape),
        ],
        out_specs=pl.BlockSpec((tm, D_MODEL), lambda i: (i, 0)),
        out_shape=jax.ShapeDtypeStruct((n, D_MODEL), F32),
        compiler_params=pltpu.CompilerParams(
            dimension_semantics=("arbitrary",), vmem_limit_bytes=VMEM_LIMIT_BYTES),
        name="moe_combine",
    )(x, gates, y01, y01, g, b)


def _moe(x, wr_pad, wg, wu, wd, g, b, tm, routing=None, after=None):
    n = x.shape[0]
    idx, gates = routing if routing is not None else _router(x, wr_pad)
    flat_e = idx[:, :TOP_K].reshape(-1)
    onehot = (flat_e[:, None] == jnp.arange(N_EXPERTS, dtype=jnp.int32)[None, :]).astype(jnp.int32)
    csum = jnp.cumsum(onehot, axis=0)
    rank = jnp.sum((csum - onehot) * onehot, axis=1)
    counts = csum[-1]
    padded = ((counts + tm - 1) // tm) * tm
    ends = jnp.cumsum(padded)
    slot = (ends - padded)[flat_e] + rank
    n_pairs = TOP_K * n
    n_tiles = n_pairs // tm + N_EXPERTS
    n_used = (ends[-1] // tm).astype(jnp.int32).reshape(1)
    tile_start = jnp.arange(n_tiles, dtype=jnp.int32) * tm
    tile_expert = jnp.sum((tile_start[:, None] >= ends[None, :]).astype(jnp.int32), axis=1)
    last_e = jnp.sum((ends[-1] - 1 >= ends).astype(jnp.int32))
    tile_expert = jnp.minimum(tile_expert, last_e).astype(jnp.int32)
    pair_sorted = jnp.sort(flat_e * n_pairs + jnp.arange(n_pairs, dtype=jnp.int32)) % n_pairs
    slot_ids = jnp.arange(n_tiles * tm, dtype=jnp.int32)
    slot_e = jnp.repeat(tile_expert, tm)
    local = slot_ids - (ends - padded)[slot_e]
    q = (jnp.cumsum(counts) - counts)[slot_e] + local
    valid = jnp.logical_and(local >= 0, local < counts[slot_e])
    tok_of_slot = jnp.where(valid, pair_sorted[jnp.clip(q, 0, n_pairs - 1)] // TOP_K, slot_ids % n)
    if after is not None:
        tok_of_slot, after = lax.optimization_barrier((tok_of_slot, after))
    xs = _sc_row_gather(x, tok_of_slot)
    ys = _experts(xs, tile_expert, n_used, wg, wu, wd, tm)
    y01 = _sc_row_gather(ys, jnp.transpose(slot.reshape(n, TOP_K)).reshape(-1))
    return _combine(x, gates, y01, g, b), after


def _block_diag(w):
    n, c, _ = w.shape
    eye = jnp.eye(n, dtype=w.dtype)
    return (eye[:, None, :, None] * w[:, :, None, :]).reshape(n * c, n * c)


def _row(v):
    return v.reshape(1, -1)


def kernel(x_prompt, x_sample, state_pool, state_conv, state_lru_conv, state_lru_h, w_in, b_gate, w_pool, pool_scale, conv_w, conv_b, conv_ln_g, conv_ln_b, lru_conv_w, lru_conv_b, lru_wa, lru_ba, lru_wx, lru_bx, lru_lambda, w_br_pool, w_br_conv, w_br_lru, w_out, ln1_g, ln1_b, ln2_g, ln2_b, ffn_w_gate, ffn_w_up, ffn_w_down, moe_router, moe_w_gate, moe_w_up, moe_w_down):
    batch, seq, _ = x_prompt.shape
    dec_batch = x_sample.shape[0]

    def layer_params(l, mat):
        return {
            "w_in": w_in[l].astype(mat), "b_gate": _row(b_gate[l]),
            "w_pool": w_pool[l].astype(mat), "pool_scale": _row(pool_scale[l]),
            "conv_w": conv_w[l], "conv_b": _row(conv_b[l]),
            "conv_ln_g": _row(conv_ln_g[l]), "conv_ln_b": _row(conv_ln_b[l]),
            "lru_conv_w": lru_conv_w[l], "lru_conv_b": _row(lru_conv_b[l]),
            "lru_wa": _block_diag(lru_wa[l]).astype(mat), "lru_ba": _row(lru_ba[l]),
            "lru_wx": _block_diag(lru_wx[l]).astype(mat), "lru_bx": _row(lru_bx[l]),
            "lru_lambda": _row(lru_lambda[l]),
            "w_br_pool": w_br_pool[l].astype(mat), "w_br_conv": w_br_conv[l].astype(mat),
            "w_br_lru": w_br_lru[l].astype(mat), "w_out": w_out[l].astype(mat),
            "ln1_g": _row(ln1_g[l]), "ln1_b": _row(ln1_b[l]),
        }

    layers = [layer_params(l, BF16) for l in range(DEPTH)]
    layers_f32 = [layer_params(l, F32) for l in range(DEPTH)]
    for l in range(DEPTH):
        layers_f32[l].update({"w_in": (w_in, l), "w_br_pool": (w_br_pool, l), "w_br_conv": (w_br_conv, l),
                              "w_br_lru": (w_br_lru, l), "w_out": (w_out, l)})
    ffn_w = [(ffn_w_gate[m].astype(BF16), ffn_w_up[m].astype(BF16), ffn_w_down[m].astype(BF16))
             for m in range(ffn_w_gate.shape[0])]
    ffn_w_f32 = [(ffn_w_gate[m], ffn_w_up[m], ffn_w_down[m]) for m in range(ffn_w_gate.shape[0])]
    moe_w = [(jnp.pad(moe_router[m], ((0, 0), (0, LANES - N_EXPERTS))),
              moe_w_gate[m].astype(BF16), moe_w_up[m].astype(BF16), moe_w_down[m].astype(BF16))
             for m in range(moe_router.shape[0])]

    x_prompt, moe_w = lax.optimization_barrier((x_prompt, moe_w))

    def channel_mixer(l, x, moe_tile, routing=None, precise=False, after=None):
        g, b = _row(ln2_g[l]), _row(ln2_b[l])
        if l % 2 == 0:
            return _ffn(x, *(ffn_w_f32 if precise else ffn_w)[l // 2], g, b, precise), after
        return _moe(x, *moe_w[l // 2], g, b, moe_tile, routing, after)

    x = x_prompt.reshape(batch * seq, D_MODEL)
    xs_in = x_sample.reshape(dec_batch, D_MODEL)
    p_states = []
    for l in range(DEPTH):
        wr_pad = moe_w[l // 2][0].astype(BF16) if l % 2 == 1 else None
        x, sp, sc, slc, sh, *routing = _mixer_seq(x, layers[l], batch, seq, wr_pad)
        p_states.append((sp, sc, slc, sh))
        x, xs_in = channel_mixer(l, x, MOE_TILE_SEQ, tuple(routing) or None, after=xs_in)
    y_prompt = x.reshape(batch, seq, D_MODEL)

    x = xs_in
    s_states = []
    for l in range(DEPTH):
        x, sp, sc, slc, sh = _mixer_step(x, state_pool[l], state_conv[l], state_lru_conv[l],
                                         state_lru_h[l], layers_f32[l], PAST_LEN)
        s_states.append((sp, sc, slc, sh))
        x, _ = channel_mixer(l, x, MOE_TILE_STEP, precise=True)
    y_sample = x.reshape(dec_batch, 1, D_MODEL)

    def stack(states, k):
        return jnp.stack([s[k] for s in states])

    return (y_prompt, y_sample,
            stack(p_states, 0), stack(p_states, 1), stack(p_states, 2), stack(p_states, 3),
            stack(s_states, 0), stack(s_states, 1), stack(s_states, 2), stack(s_states, 3))
```

```python
import functools

import jax
import jax.numpy as jnp
from jax import lax
from jax.experimental import pallas as pl
from jax.experimental.pallas import tpu as pltpu
from jax.experimental.pallas import tpu_sc as plsc

D_MODEL = 1024
DEPTH = 2
PAST_LEN = 16384
D_POOL = 512
N_POOL_GROUPS = 4
POOL_GROUP = 128
POOL_WINDOWS = (2, 4, 8, 16)
POOL_BUF = 15
D_CONV = 512
CONV_WIDTH = 31
CONV_BUF = 30
D_LRU = 512
N_LRU_BLOCKS = 8
LRU_BLOCK = 64
LRU_CONV_WIDTH = 4
LRU_CONV_BUF = 3
LRU_C = 8.0
N_EXPERTS = 8
TOP_K = 2
ALPHA = (2.0 * DEPTH) ** 0.25
LN_EPS = 1e-5

O_POOL = 0
O_VAL = D_POOL
O_GLU = O_VAL + D_CONV
O_LRU = O_GLU + D_CONV
O_GELU = O_LRU + D_LRU
O_GATE = O_GELU + D_LRU

SUBLANES = 8
LANES = 128
VMEM_LIMIT_BYTES = 56 * 1024 * 1024
SC_CORES = 2
SC_SUBCORES = 16
SC_WINDOW = 128
SC_SUB_ROWS = 32

SEQ_TILE = 512
CONV_ROWS = 128
POOL_OFF = 16
CONV_OFF = 32
LCONV_OFF = 8
FFN_CHUNK = 256
PIECE = 256
STEP_COLS = 512
EXPERT_F_TILE = 1792
MOE_TILE_SEQ = 512
MOE_TILE_STEP = 128

BF16 = jnp.bfloat16
F32 = jnp.float32


def _sigmoid(x):
    return 0.5 * jnp.tanh(0.5 * x) + 0.5


def _silu(x):
    return x * _sigmoid(x)


def _gelu_tanh(x):
    return x * (0.5 * (1.0 + jnp.tanh(0.7978845608028654 * (x + 0.044715 * (x * x * x)))))


def _softplus(z):
    return jnp.maximum(z, 0.0) + jnp.log1p(jnp.exp(-jnp.abs(z)))


def _layer_norm(x, g, b):
    mu = jnp.mean(x, axis=-1, keepdims=True)
    xc = x - mu
    var = jnp.mean(xc * xc, axis=-1, keepdims=True)
    return xc * lax.rsqrt(var + LN_EPS) * g + b


def _dot(a, b):
    return jnp.dot(a, b, preferred_element_type=F32)


def _mm(a, w):
    return _dot(a.astype(BF16), w)


def _mm_hi(a, w):
    return jnp.dot(a, w, preferred_element_type=F32, precision=lax.Precision.HIGHEST)


def _lru_coeffs(xc, r, ig, lam, reset):
    log_a = (-LRU_C * r) * _softplus(-lam)
    a = jnp.exp(log_a)
    mult = jnp.sqrt(-jnp.tanh(log_a) * (a * a + 1.0))
    if reset is not None:
        a = jnp.where(reset, 0.0, a)
        mult = jnp.where(reset, 1.0, mult)
    return a, mult * ig * xc


def _mixer_seq_kernel(x_ref, w_in_ref, b_gate_ref, w_pool_ref, pool_scale_ref,
                      conv_w_ref, conv_b_ref, cln_g_ref, cln_b_ref,
                      lconv_w_ref, lconv_b_ref, wa_ref, ba_ref, wx_ref, bx_ref, lam_ref,
                      wbp_ref, wbc_ref, wbl_ref, w_out_ref, ln_g_ref, ln_b_ref, *rest, tm, with_router):
    if with_router:
        wr_ref, rest = rest[0], rest[1:]
        idx_ref, gate_ref, rest = rest[5], rest[6], rest[:5] + rest[7:]
    (y_ref, sp_ref, sc_ref, slc_ref, sh_ref,
     pool_ext, conv_ext, lconv_ext, act_buf, a_buf, b_buf, h_carry, gate_buf, gelu_buf,
     merged_buf, xb_buf) = rest
    i = pl.program_id(1)

    @pl.when(i == 0)
    def _():
        pool_ext[0:POOL_OFF, :] = jnp.zeros((POOL_OFF, D_POOL), F32)
        conv_ext[0:CONV_OFF, :] = jnp.zeros((CONV_OFF, D_CONV), F32)
        conv_ext[tm + CONV_OFF:tm + CONV_OFF + SUBLANES, :] = jnp.zeros((SUBLANES, D_CONV), F32)
        lconv_ext[0:LCONV_OFF, :] = jnp.zeros((LCONV_OFF, D_LRU), F32)
        h_carry[...] = jnp.zeros((1, D_LRU), F32)

    xb_buf[...] = x_ref[...].astype(BF16)

    def proj(lo, width):
        return _dot(xb_buf[...], w_in_ref[:, lo:lo + width])

    pos = i * tm + lax.broadcasted_iota(jnp.int32, (tm, 1), 0)

    glu = proj(O_VAL, D_CONV) * _sigmoid(proj(O_GLU, D_CONV))
    conv_ext[CONV_OFF:CONV_OFF + tm, :] = glu

    def pool_piece():
        u_pool = proj(O_POOL, D_POOL)
        pool_ext[POOL_OFF:POOL_OFF + tm, :] = u_pool
        parts = []
        for g, w in enumerate(POOL_WINDOWS):
            lo = g * POOL_GROUP
            acc = u_pool[:, lo:lo + POOL_GROUP]
            for k in range(1, w):
                acc = acc + pool_ext[POOL_OFF - k:POOL_OFF - k + tm, lo:lo + POOL_GROUP]
            inv_cnt = 1.0 / jnp.minimum(w, pos + 1).astype(F32)
            pooled = acc * inv_cnt - u_pool[:, lo:lo + POOL_GROUP]
            parts.append(_dot(pooled.astype(BF16), w_pool_ref[g]))
        mixed = jnp.concatenate(parts, axis=1) * pool_scale_ref[...]
        merged_buf[...] = gate_buf[:, 0:D_MODEL] * _dot(mixed.astype(BF16), wbp_ref[...])
        sp_ref[0] = pool_ext[tm + POOL_OFF - POOL_BUF:tm + POOL_OFF, :]
        pool_ext[0:POOL_OFF, :] = pool_ext[tm:tm + POOL_OFF, :]

    def lru_piece(lo):
        lconv_ext[LCONV_OFF:LCONV_OFF + tm, lo:lo + PIECE] = proj(O_LRU + lo, PIECE)

    def gelu_piece(lo):
        gelu_buf[:, lo:lo + PIECE] = _gelu_tanh(proj(O_GELU + lo, PIECE))

    def gate_piece(lo):
        gate_buf[:, lo:lo + PIECE] = _sigmoid(
            proj(O_GATE + lo, PIECE) + b_gate_ref[:, lo:lo + PIECE])

    pieces = ([functools.partial(gate_piece, lo) for lo in range(0, D_MODEL, PIECE)]
              + [pool_piece]
              + [functools.partial(lru_piece, lo) for lo in range(0, D_LRU, PIECE)]
              + [functools.partial(gate_piece, lo) for lo in range(D_MODEL, 2 * D_MODEL, PIECE)]
              + [functools.partial(gelu_piece, lo) for lo in range(0, D_LRU, PIECE)]
              + [functools.partial(gate_piece, lo) for lo in range(2 * D_MODEL, 3 * D_MODEL, PIECE)])
    n_conv_chunks = tm // CONV_ROWS
    base = CONV_OFF - CONV_BUF
    rows = CONV_ROWS + SUBLANES
    n_units = n_conv_chunks * (D_CONV // LANES)
    unit = 0
    for c in range(n_conv_chunks):
        c0 = c * CONV_ROWS
        for l0 in range(0, D_CONV, LANES):
            acc = jnp.zeros((CONV_ROWS, LANES), F32) + conv_b_ref[:, l0:l0 + LANES]
            for r in range(SUBLANES):
                z = None
                for k in range(CONV_WIDTH):
                    if (base + k) % SUBLANES != r:
                        continue
                    q8 = base + k - r
                    t = (conv_ext[c0 + q8:c0 + q8 + rows, l0:l0 + LANES]
                         * conv_w_ref[k:k + 1, l0:l0 + LANES])
                    z = t if z is None else z + t
                acc = acc + z[r:r + CONV_ROWS, :]
            act_buf[c0:c0 + CONV_ROWS, l0:l0 + LANES] = acc
            for piece in pieces[unit * len(pieces) // n_units:(unit + 1) * len(pieces) // n_units]:
                piece()
            unit += 1
        act_buf[c0:c0 + CONV_ROWS, :] = _silu(_layer_norm(
            act_buf[c0:c0 + CONV_ROWS, :], cln_g_ref[...], cln_b_ref[...]))
    merged_buf[...] += gate_buf[:, D_MODEL:2 * D_MODEL] * _dot(
        act_buf[...].astype(BF16), wbc_ref[...])
    sc_ref[0] = conv_ext[tm + CONV_OFF - CONV_BUF:tm + CONV_OFF, :]
    conv_ext[0:CONV_OFF, :] = conv_ext[tm:tm + CONV_OFF, :]

    base = LCONV_OFF - LRU_CONV_BUF
    xc = jnp.zeros((tm, D_LRU), F32) + lconv_b_ref[...]
    for k in range(LRU_CONV_WIDTH):
        xc = xc + lconv_ext[base + k:base + k + tm, :] * lconv_w_ref[k:k + 1, :]
    slc_ref[0] = lconv_ext[tm + LCONV_OFF - LRU_CONV_BUF:tm + LCONV_OFF, :]
    lconv_ext[0:LCONV_OFF, :] = lconv_ext[tm:tm + LCONV_OFF, :]
    xcb = xc.astype(BF16)
    r = _sigmoid(_dot(xcb, wa_ref[...]) + ba_ref[...])
    ig = _sigmoid(_dot(xcb, wx_ref[...]) + bx_ref[...])
    a, b = _lru_coeffs(xc, r, ig, lam_ref[...], pos == 0)
    a_buf[...] = a
    b_buf[...] = b

    row8 = lax.broadcasted_iota(jnp.int32, (SUBLANES, D_LRU), 0)

    def scan_block(j, h):
        r0 = pl.multiple_of(j * SUBLANES, SUBLANES)
        a8 = a_buf[pl.ds(r0, SUBLANES), :]
        b8 = b_buf[pl.ds(r0, SUBLANES), :]
        for k in (1, 2, 4):
            a_s = pltpu.roll(a8, k, 0)
            b_s = pltpu.roll(b8, k, 0)
            m = row8 >= k
            b8 = jnp.where(m, a8 * b_s + b8, b8)
            a8 = jnp.where(m, a8 * a_s, a8)
        h8 = a8 * h + b8
        a_buf[pl.ds(r0, SUBLANES), :] = h8
        return h8[SUBLANES - 1:SUBLANES, :]

    h_last = lax.fori_loop(0, tm // SUBLANES, scan_block, h_carry[...], unroll=2)
    h_carry[...] = h_last
    sh_ref[0] = h_last
    hg = a_buf[...] * gelu_buf[...]
    merged = merged_buf[...] + gate_buf[:, 2 * D_MODEL:3 * D_MODEL] * _dot(
        hg.astype(BF16), wbl_ref[...])

    m_out = _dot(merged.astype(BF16), w_out_ref[...])
    y = _layer_norm(ALPHA * x_ref[...] + m_out, ln_g_ref[...], ln_b_ref[...])
    y_ref[...] = y
    if with_router:
        idx_ref[...], gate_ref[...] = _route(y, wr_ref[...], precise=False)


def _mixer_step_kernel(x_ref, st_pool_ref, st_conv_ref, st_lconv_ref, st_h_ref,
                       w_in_ref, b_gate_ref, w_pool_ref, pool_scale_ref,
                       conv_w_ref, conv_b_ref, cln_g_ref, cln_b_ref,
                       lconv_w_ref, lconv_b_ref, wa_ref, ba_ref, wx_ref, bx_ref, lam_ref,
                       wbp_ref, wbc_ref, wbl_ref, w_out_ref, ln_g_ref, ln_b_ref,
                       y_ref, up_ref, glu_ref, ul_ref, sh_ref, u_buf, *, start_pos, n_blocks):
    k = pl.program_id(0)
    u_buf[k] = _mm_hi(x_ref[...], w_in_ref[...])

    @pl.when(k == n_blocks - 1)
    def _():
        x = x_ref[...]

        def proj(lo):
            return u_buf[lo // STEP_COLS]

        def gate(n):
            lo = n * D_MODEL
            u = jnp.concatenate([proj(O_GATE + lo), proj(O_GATE + lo + STEP_COLS)], axis=1)
            return _sigmoid(u + b_gate_ref[:, lo:lo + D_MODEL])

        u_pool = proj(O_POOL)
        parts = []
        for g, w in enumerate(POOL_WINDOWS):
            lo = g * POOL_GROUP
            acc = u_pool[:, lo:lo + POOL_GROUP]
            for j in range(1, w):
                acc = acc + st_pool_ref[POOL_BUF - j, :, lo:lo + POOL_GROUP]
            pooled = acc * (1.0 / min(w, start_pos + 1)) - u_pool[:, lo:lo + POOL_GROUP]
            parts.append(_mm_hi(pooled, w_pool_ref[g]))
        mixed = jnp.concatenate(parts, axis=1) * pool_scale_ref[...]
        merged = gate(0) * _mm_hi(mixed, wbp_ref[...])
        up_ref[...] = u_pool

        glu = proj(O_VAL) * _sigmoid(proj(O_GLU))
        acc = glu * conv_w_ref[CONV_BUF:CONV_BUF + 1, :] + conv_b_ref[...]
        for j in range(CONV_BUF):
            acc = acc + st_conv_ref[j] * conv_w_ref[j:j + 1, :]
        act = _silu(_layer_norm(acc, cln_g_ref[...], cln_b_ref[...]))
        merged = merged + gate(1) * _mm_hi(act, wbc_ref[...])
        glu_ref[...] = glu

        u_lru = proj(O_LRU)
        xc = u_lru * lconv_w_ref[LRU_CONV_BUF:LRU_CONV_BUF + 1, :] + lconv_b_ref[...]
        for j in range(LRU_CONV_BUF):
            xc = xc + st_lconv_ref[j] * lconv_w_ref[j:j + 1, :]
        ul_ref[...] = u_lru
        r = _sigmoid(_mm_hi(xc, wa_ref[...]) + ba_ref[...])
        ig = _sigmoid(_mm_hi(xc, wx_ref[...]) + bx_ref[...])
        reset = jnp.full(xc.shape, True) if start_pos == 0 else None
        a, b = _lru_coeffs(xc, r, ig, lam_ref[...], reset)
        h = a * st_h_ref[...] + b
        sh_ref[...] = h
        hg = h * _gelu_tanh(proj(O_GELU))
        merged = merged + gate(2) * _mm_hi(hg, wbl_ref[...])

        m_out = _mm_hi(merged, w_out_ref[...])
        y_ref[...] = _layer_norm(ALPHA * x + m_out, ln_g_ref[...], ln_b_ref[...])


def _const_spec(shape):
    nd = len(shape)
    return pl.BlockSpec(shape, lambda *_: (0,) * nd, pipeline_mode=pl.Buffered(1))


def _mixer_weight_list(p):
    return [p["w_in"], p["b_gate"], p["w_pool"], p["pool_scale"], p["conv_w"], p["conv_b"],
            p["conv_ln_g"], p["conv_ln_b"], p["lru_conv_w"], p["lru_conv_b"], p["lru_wa"],
            p["lru_ba"], p["lru_wx"], p["lru_bx"], p["lru_lambda"], p["w_br_pool"],
            p["w_br_conv"], p["w_br_lru"], p["w_out"], p["ln1_g"], p["ln1_b"]]


def _mixer_seq(x, p, batch, seq, wr_pad=None):
    tm = min(SEQ_TILE, seq)
    nt = seq // tm
    with_router = wr_pad is not None
    weights = _mixer_weight_list(p) + ([wr_pad] if with_router else [])
    in_specs = [pl.BlockSpec((tm, D_MODEL), lambda b, i: (b * nt + i, 0))]
    in_specs += [_const_spec(w.shape) for w in weights]
    out_shape = [
        jax.ShapeDtypeStruct((batch * seq, D_MODEL), F32),
        jax.ShapeDtypeStruct((batch, POOL_BUF, D_POOL), F32),
        jax.ShapeDtypeStruct((batch, CONV_BUF, D_CONV), F32),
        jax.ShapeDtypeStruct((batch, LRU_CONV_BUF, D_LRU), F32),
        jax.ShapeDtypeStruct((batch, 1, D_LRU), F32),
    ]
    out_specs = [
        pl.BlockSpec((tm, D_MODEL), lambda b, i: (b * nt + i, 0)),
        pl.BlockSpec((1, POOL_BUF, D_POOL), lambda b, i: (b, 0, 0)),
        pl.BlockSpec((1, CONV_BUF, D_CONV), lambda b, i: (b, 0, 0)),
        pl.BlockSpec((1, LRU_CONV_BUF, D_LRU), lambda b, i: (b, 0, 0)),
        pl.BlockSpec((1, 1, D_LRU), lambda b, i: (b, 0, 0)),
    ]
    if with_router:
        out_shape += [jax.ShapeDtypeStruct((batch * seq, LANES), jnp.int32),
                      jax.ShapeDtypeStruct((batch * seq, LANES), F32)]
        out_specs += [pl.BlockSpec((tm, LANES), lambda b, i: (b * nt + i, 0)),
                      pl.BlockSpec((tm, LANES), lambda b, i: (b * nt + i, 0))]
    scratch = [
        pltpu.VMEM((tm + POOL_OFF, D_POOL), F32),
        pltpu.VMEM((tm + CONV_OFF + SUBLANES, D_CONV), F32),
        pltpu.VMEM((tm + LCONV_OFF, D_LRU), F32),
        pltpu.VMEM((tm, D_CONV), F32),
        pltpu.VMEM((tm, D_LRU), F32),
        pltpu.VMEM((tm, D_LRU), F32),
        pltpu.VMEM((1, D_LRU), F32),
        pltpu.VMEM((tm, 3 * D_MODEL), F32),
        pltpu.VMEM((tm, D_LRU), F32),
        pltpu.VMEM((tm, D_MODEL), F32),
        pltpu.VMEM((tm, D_MODEL), BF16),
    ]
    y, sp, sc, slc, sh, *routing = pl.pallas_call(
        functools.partial(_mixer_seq_kernel, tm=tm, with_router=with_router),
        grid=(batch, nt),
        in_specs=in_specs,
        out_specs=out_specs,
        out_shape=out_shape,
        scratch_shapes=scratch,
        compiler_params=pltpu.CompilerParams(
            dimension_semantics=("arbitrary", "arbitrary"),
            vmem_limit_bytes=VMEM_LIMIT_BYTES),
        name="mixer_seq",
    )(x, *weights)
    return (y, sp, sc, slc, sh.reshape(batch, D_LRU), *routing)


def _mixer_step(x, st_pool, st_conv, st_lconv, st_h, p, start_pos):
    batch = x.shape[0]
    weights = _mixer_weight_list(p)
    states = [jnp.transpose(st_pool, (1, 0, 2)), jnp.transpose(st_conv, (1, 0, 2)),
              jnp.transpose(st_lconv, (1, 0, 2)), st_h]

    def spec(a):
        if not isinstance(a, tuple):
            return _const_spec(a.shape)
        arr, layer = a
        zeros = (0,) * (arr.ndim - 1)
        return pl.BlockSpec((None,) + arr.shape[1:], lambda k: (layer,) + zeros,
                            pipeline_mode=pl.Buffered(1))

    w_in_all, layer = weights[0]
    in_cols = w_in_all.shape[2]
    n_blocks = in_cols // STEP_COLS
    assert n_blocks * STEP_COLS == in_cols
    in_specs = [spec(a) for a in [x] + states + weights]
    in_specs[1 + len(states)] = pl.BlockSpec((None, D_MODEL, STEP_COLS), lambda k: (layer, 0, k))
    args = [x] + states + [w[0] if isinstance(w, tuple) else w for w in weights]
    out_shape = (
        jax.ShapeDtypeStruct((batch, D_MODEL), F32),
        jax.ShapeDtypeStruct((batch, D_POOL), F32),
        jax.ShapeDtypeStruct((batch, D_CONV), F32),
        jax.ShapeDtypeStruct((batch, D_LRU), F32),
        jax.ShapeDtypeStruct((batch, D_LRU), F32),
    )
    y, u_pool, glu, u_lru, sh = pl.pallas_call(
        functools.partial(_mixer_step_kernel, start_pos=start_pos, n_blocks=n_blocks),
        grid=(n_blocks,),
        in_specs=in_specs,
        out_specs=tuple(pl.BlockSpec(s.shape, lambda k: (0, 0)) for s in out_shape),
        out_shape=out_shape,
        scratch_shapes=[pltpu.VMEM((n_blocks, batch, STEP_COLS), F32)],
        compiler_params=pltpu.CompilerParams(
            dimension_semantics=("arbitrary",), vmem_limit_bytes=VMEM_LIMIT_BYTES),
        name="mixer_step",
    )(*args)

    def push(state, row):
        return jnp.concatenate([state[:, 1:], row[:, None]], axis=1)

    return y, push(st_pool, u_pool), push(st_conv, glu), push(st_lconv, u_lru), sh


def _ffn_kernel(x_ref, wg_ref, wu_ref, wd_ref, g_ref, b_ref, y_ref, *, d_ff, precise):
    x = x_ref[...]
    mm = _mm_hi if precise else _dot
    xb = x if precise else x.astype(BF16)
    acc = jnp.zeros(x.shape, F32)
    for c0 in range(0, d_ff, FFN_CHUNK):
        h = _silu(mm(xb, wg_ref[:, c0:c0 + FFN_CHUNK])) * mm(xb, wu_ref[:, c0:c0 + FFN_CHUNK])
        acc = acc + mm(h if precise else h.astype(BF16), wd_ref[c0:c0 + FFN_CHUNK, :])
    y_ref[...] = _layer_norm(ALPHA * x + acc, g_ref[...], b_ref[...])


def _ffn(x, wg, wu, wd, g, b, precise):
    n = x.shape[0]
    tm = min(SEQ_TILE, n)
    d_ff = wg.shape[1]
    weights = [wg, wu, wd, g, b]
    return pl.pallas_call(
        functools.partial(_ffn_kernel, d_ff=d_ff, precise=precise),
        grid=(n // tm,),
        in_specs=[pl.BlockSpec((tm, D_MODEL), lambda i: (i, 0))] + [_const_spec(w.shape) for w in weights],
        out_specs=pl.BlockSpec((tm, D_MODEL), lambda i: (i, 0)),
        out_shape=jax.ShapeDtypeStruct((n, D_MODEL), F32),
        compiler_params=pltpu.CompilerParams(
            dimension_semantics=("arbitrary",), vmem_limit_bytes=VMEM_LIMIT_BYTES),
        name="ffn_dense",
    )(x, *weights)


def _route(x, wr, precise):
    logits = _mm_hi(x, wr) if precise else _mm(x, wr)
    lane = lax.broadcasted_iota(jnp.int32, logits.shape, 1)
    neg = jnp.float32(-jnp.inf)
    l1 = jnp.where(lane < N_EXPERTS, logits, neg)
    v1 = jnp.max(l1, axis=-1, keepdims=True)
    i1 = jnp.min(jnp.where(l1 == v1, lane, LANES), axis=-1, keepdims=True)
    l2 = jnp.where(lane == i1, neg, l1)
    v2 = jnp.max(l2, axis=-1, keepdims=True)
    i2 = jnp.min(jnp.where(l2 == v2, lane, LANES), axis=-1, keepdims=True)
    e2 = jnp.exp(v2 - v1)
    den = 1.0 + e2
    idx = jnp.where(lane == 0, i1, jnp.where(lane == 1, i2, 0))
    gates = jnp.where(lane == 0, 1.0 / den, jnp.where(lane == 1, e2 / den, 0.0))
    return idx, gates


def _router_kernel(x_ref, wr_ref, idx_ref, gate_ref):
    idx_ref[...], gate_ref[...] = _route(x_ref[...], wr_ref[...], precise=True)


def _router(x, wr_pad):
    n = x.shape[0]
    tm = min(SEQ_TILE, n)
    return pl.pallas_call(
        _router_kernel,
        grid=(n // tm,),
        in_specs=[pl.BlockSpec((tm, D_MODEL), lambda i: (i, 0)), _const_spec(wr_pad.shape)],
        out_specs=(pl.BlockSpec((tm, LANES), lambda i: (i, 0)),
                   pl.BlockSpec((tm, LANES), lambda i: (i, 0))),
        out_shape=(jax.ShapeDtypeStruct((n, LANES), jnp.int32),
                   jax.ShapeDtypeStruct((n, LANES), F32)),
        compiler_params=pltpu.CompilerParams(dimension_semantics=("arbitrary",)),
        name="moe_router",
    )(x, wr_pad)


def _sc_row_gather(src, idx):
    n_idx = idx.shape[0]
    assert n_idx % SC_WINDOW == 0
    n_win = n_idx // SC_WINDOW
    n_workers = SC_CORES * SC_SUBCORES
    n_sub = SC_WINDOW // SC_SUB_ROWS
    mesh = plsc.VectorSubcoreMesh(core_axis_name="c", subcore_axis_name="s",
                                  num_cores=SC_CORES, num_subcores=SC_SUBCORES)

    def body(src_hbm, idx_hbm, out_hbm, idx_v, buf, g0, g1, w0, w1):
        gsem = (g0, g1)
        wsem = (w0, w1)
        worker = lax.axis_index("c") * SC_SUBCORES + lax.axis_index("s")

        @pl.loop(0, pl.cdiv(n_win, n_workers))
        def _(t):
            win = worker + t * n_workers

            @pl.when(win < n_win)
            def _():
                base = win * SC_WINDOW
                pltpu.sync_copy(idx_hbm.at[:, pl.ds(base, SC_WINDOW)], idx_v)

                def fetch(q):
                    return pltpu.async_copy(
                        src_hbm.at[idx_v.at[0, pl.ds(q * SC_SUB_ROWS, SC_SUB_ROWS)]],
                        buf.at[q % 2], gsem[q % 2])

                def put(q):
                    return pltpu.async_copy(
                        buf.at[q % 2], out_hbm.at[pl.ds(base + q * SC_SUB_ROWS, SC_SUB_ROWS)],
                        wsem[q % 2])

                fetches = {0: fetch(0)}
                puts = {}
                for q in range(n_sub):
                    if q + 1 < n_sub:
                        if q >= 1:
                            puts[q - 1].wait()
                        fetches[q + 1] = fetch(q + 1)
                    fetches[q].wait()
                    puts[q] = put(q)
                for q in range(max(0, n_sub - 2), n_sub):
                    puts[q].wait()

    return pl.kernel(
        body,
        out_type=jax.ShapeDtypeStruct((n_idx, D_MODEL), src.dtype),
        mesh=mesh,
        scratch_types=[pltpu.VMEM((1, SC_WINDOW), jnp.int32),
                       pltpu.VMEM((2, SC_SUB_ROWS, D_MODEL), src.dtype),
                       pltpu.SemaphoreType.DMA, pltpu.SemaphoreType.DMA,
                       pltpu.SemaphoreType.DMA, pltpu.SemaphoreType.DMA],
        cost_estimate=pl.CostEstimate(
            flops=0, transcendentals=0,
            bytes_accessed=n_idx * (2 * D_MODEL * src.dtype.itemsize + idx.dtype.itemsize)),
        name="sc_row_gather",
    )(src, idx.reshape(1, n_idx))


def _expert_kernel(te_ref, nu_ref, xs_ref, wg_ref, wu_ref, wd_ref, ys_ref, xs_bf, acc, *, tf, nf):
    del te_ref
    i = pl.program_id(0)
    j = pl.program_id(1)
    used = i < nu_ref[0]

    @pl.when(jnp.logical_and(used, j == 0))
    def _():
        xs_bf[...] = xs_ref[...].astype(BF16)

    @pl.when(used)
    def _():
        xb = xs_bf[...]
        part = None
        for c0 in range(0, tf, FFN_CHUNK):
            h = _silu(_dot(xb, wg_ref[:, c0:c0 + FFN_CHUNK])) * _dot(xb, wu_ref[:, c0:c0 + FFN_CHUNK])
            d = _dot(h.astype(BF16), wd_ref[c0:c0 + FFN_CHUNK, :])
            part = d if part is None else part + d

        @pl.when(j == 0)
        def _():
            acc[...] = part

        @pl.when(jnp.logical_and(j > 0, j < nf - 1))
        def _():
            acc[...] += part

        @pl.when(j == nf - 1)
        def _():
            ys_ref[...] = acc[...] + part

    @pl.when(jnp.logical_and(jnp.logical_not(used), j == nf - 1))
    def _():
        ys_ref[...] = jnp.zeros(ys_ref.shape, F32)


def _experts(xs, tile_expert, n_used, wg, wu, wd, tm):
    n_tiles = tile_expert.shape[0]
    d_e = wg.shape[2]
    tf = EXPERT_F_TILE
    nf = d_e // tf
    assert nf >= 2 and nf * tf == d_e and tf % FFN_CHUNK == 0

    def jeff(i, j, nu):
        return jnp.where(i < nu[0], j, nf - 1)

    grid_spec = pltpu.PrefetchScalarGridSpec(
        num_scalar_prefetch=2,
        grid=(n_tiles, nf),
        in_specs=[
            pl.BlockSpec((tm, D_MODEL), lambda i, j, te, nu: (jnp.minimum(i, nu[0] - 1), 0)),
            pl.BlockSpec((None, D_MODEL, tf), lambda i, j, te, nu: (te[i], 0, jeff(i, j, nu))),
            pl.BlockSpec((None, D_MODEL, tf), lambda i, j, te, nu: (te[i], 0, jeff(i, j, nu))),
            pl.BlockSpec((None, tf, D_MODEL), lambda i, j, te, nu: (te[i], jeff(i, j, nu), 0)),
        ],
        out_specs=pl.BlockSpec((tm, D_MODEL), lambda i, j, te, nu: (i, 0)),
        scratch_shapes=[
            pltpu.VMEM((tm, D_MODEL), BF16),
            pltpu.VMEM((tm, D_MODEL), F32),
        ],
    )
    return pl.pallas_call(
        functools.partial(_expert_kernel, tf=tf, nf=nf),
        grid_spec=grid_spec,
        out_shape=jax.ShapeDtypeStruct((n_tiles * tm, D_MODEL), F32),
        compiler_params=pltpu.CompilerParams(
            dimension_semantics=("arbitrary", "arbitrary"), vmem_limit_bytes=VMEM_LIMIT_BYTES),
        name="moe_experts",
    )(tile_expert, n_used, xs, wg, wu, wd)


def _combine_kernel(x_ref, gate_ref, y0_ref, y1_ref, g_ref, b_ref, y_ref):
    gates = gate_ref[...]
    f = gates[:, 0:1] * y0_ref[...] + gates[:, 1:2] * y1_ref[...]
    y_ref[...] = _layer_norm(ALPHA * x_ref[...] + f, g_ref[...], b_ref[...])


def _combine(x, gates, y01, g, b):
    n = x.shape[0]
    tm = min(SEQ_TILE, n)
    nt = n // tm
    return pl.pallas_call(
        _combine_kernel,
        grid=(nt,),
        in_specs=[
            pl.BlockSpec((tm, D_MODEL), lambda i: (i, 0)),
            pl.BlockSpec((tm, LANES), lambda i: (i, 0)),
            pl.BlockSpec((tm, D_MODEL), lambda i: (i, 0)),
            pl.BlockSpec((tm, D_MODEL), lambda i: (i + nt, 0)),
            _const_spec(g.shape),
            _const_spec(b.shape),
        ],
        out_specs=pl.BlockSpec((tm, D_MODEL), lambda i: (i, 0)),
        out_shape=jax.ShapeDtypeStruct((n, D_MODEL), F32),
        compiler_params=pltpu.CompilerParams(
            dimension_semantics=("arbitrary",), vmem_limit_bytes=VMEM_LIMIT_BYTES),
        name="moe_combine",
    )(x, gates, y01, y01, g, b)


def _moe(x, wr_pad, wg, wu, wd, g, b, tm, routing=None, after=None, wait_for=None):
    n = x.shape[0]
    if wait_for is not None:
        x, _ = lax.optimization_barrier((x, wait_for))
    idx, gates = routing if routing is not None else _router(x, wr_pad)
    flat_e = idx[:, :TOP_K].reshape(-1)
    onehot = (flat_e[:, None] == jnp.arange(N_EXPERTS, dtype=jnp.int32)[None, :]).astype(jnp.int32)
    csum = jnp.cumsum(onehot, axis=0)
    rank = jnp.sum((csum - onehot) * onehot, axis=1)
    counts = csum[-1]
    padded = ((counts + tm - 1) // tm) * tm
    ends = jnp.cumsum(padded)
    slot = (ends - padded)[flat_e] + rank
    n_pairs = TOP_K * n
    n_tiles = n_pairs // tm + N_EXPERTS
    n_used = (ends[-1] // tm).astype(jnp.int32).reshape(1)
    tile_start = jnp.arange(n_tiles, dtype=jnp.int32) * tm
    tile_expert = jnp.sum((tile_start[:, None] >= ends[None, :]).astype(jnp.int32), axis=1)
    last_e = jnp.sum((ends[-1] - 1 >= ends).astype(jnp.int32))
    tile_expert = jnp.minimum(tile_expert, last_e).astype(jnp.int32)
    pair_sorted = jnp.sort(flat_e * n_pairs + jnp.arange(n_pairs, dtype=jnp.int32)) % n_pairs
    slot_ids = jnp.arange(n_tiles * tm, dtype=jnp.int32)
    slot_e = jnp.repeat(tile_expert, tm)
    local = slot_ids - (ends - padded)[slot_e]
    q = (jnp.cumsum(counts) - counts)[slot_e] + local
    valid = jnp.logical_and(local >= 0, local < counts[slot_e])
    tok_of_slot = jnp.where(valid, pair_sorted[jnp.clip(q, 0, n_pairs - 1)] // TOP_K, slot_ids % n)
    if after is not None:
        tok_of_slot, after = lax.optimization_barrier((tok_of_slot, after))
    xs = _sc_row_gather(x, tok_of_slot)
    ys = _experts(xs, tile_expert, n_used, wg, wu, wd, tm)
    y01 = _sc_row_gather(ys, jnp.transpose(slot.reshape(n, TOP_K)).reshape(-1))
    return _combine(x, gates, y01, g, b), after, xs


def _block_diag(w):
    n, c, _ = w.shape
    eye = jnp.eye(n, dtype=w.dtype)
    return (eye[:, None, :, None] * w[:, :, None, :]).reshape(n * c, n * c)


def _row(v):
    return v.reshape(1, -1)


def kernel(x_prompt, x_sample, state_pool, state_conv, state_lru_conv, state_lru_h, w_in, b_gate, w_pool, pool_scale, conv_w, conv_b, conv_ln_g, conv_ln_b, lru_conv_w, lru_conv_b, lru_wa, lru_ba, lru_wx, lru_bx, lru_lambda, w_br_pool, w_br_conv, w_br_lru, w_out, ln1_g, ln1_b, ln2_g, ln2_b, ffn_w_gate, ffn_w_up, ffn_w_down, moe_router, moe_w_gate, moe_w_up, moe_w_down):
    batch, seq, _ = x_prompt.shape
    dec_batch = x_sample.shape[0]

    def layer_params(l, mat):
        return {
            "w_in": w_in[l].astype(mat), "b_gate": _row(b_gate[l]),
            "w_pool": w_pool[l].astype(mat), "pool_scale": _row(pool_scale[l]),
            "conv_w": conv_w[l], "conv_b": _row(conv_b[l]),
            "conv_ln_g": _row(conv_ln_g[l]), "conv_ln_b": _row(conv_ln_b[l]),
            "lru_conv_w": lru_conv_w[l], "lru_conv_b": _row(lru_conv_b[l]),
            "lru_wa": _block_diag(lru_wa[l]).astype(mat), "lru_ba": _row(lru_ba[l]),
            "lru_wx": _block_diag(lru_wx[l]).astype(mat), "lru_bx": _row(lru_bx[l]),
            "lru_lambda": _row(lru_lambda[l]),
            "w_br_pool": w_br_pool[l].astype(mat), "w_br_conv": w_br_conv[l].astype(mat),
            "w_br_lru": w_br_lru[l].astype(mat), "w_out": w_out[l].astype(mat),
            "ln1_g": _row(ln1_g[l]), "ln1_b": _row(ln1_b[l]),
        }

    layers = [layer_params(l, BF16) for l in range(DEPTH)]
    layers_f32 = [layer_params(l, F32) for l in range(DEPTH)]
    for l in range(DEPTH):
        layers_f32[l].update({"w_in": (w_in, l), "w_br_pool": (w_br_pool, l), "w_br_conv": (w_br_conv, l),
                              "w_br_lru": (w_br_lru, l), "w_out": (w_out, l)})
    ffn_w = [(ffn_w_gate[m].astype(BF16), ffn_w_up[m].astype(BF16), ffn_w_down[m].astype(BF16))
             for m in range(ffn_w_gate.shape[0])]
    ffn_w_f32 = [(ffn_w_gate[m], ffn_w_up[m], ffn_w_down[m]) for m in range(ffn_w_gate.shape[0])]
    moe_w = [(jnp.pad(moe_router[m], ((0, 0), (0, LANES - N_EXPERTS))),
              moe_w_gate[m].astype(BF16), moe_w_up[m].astype(BF16), moe_w_down[m].astype(BF16))
             for m in range(moe_router.shape[0])]

    x_prompt, moe_w = lax.optimization_barrier((x_prompt, moe_w))

    def channel_mixer(l, x, moe_tile, routing=None, precise=False, after=None, wait_for=None):
        g, b = _row(ln2_g[l]), _row(ln2_b[l])
        if l % 2 == 0:
            return _ffn(x, *(ffn_w_f32 if precise else ffn_w)[l // 2], g, b, precise), after, wait_for
        return _moe(x, *moe_w[l // 2], g, b, moe_tile, routing, after, wait_for)

    x = x_prompt.reshape(batch * seq, D_MODEL)
    xs_in = x_sample.reshape(dec_batch, D_MODEL)
    gathered = None
    p_states = []
    for l in range(DEPTH):
        wr_pad = moe_w[l // 2][0].astype(BF16) if l % 2 == 1 else None
        x, sp, sc, slc, sh, *routing = _mixer_seq(x, layers[l], batch, seq, wr_pad)
        p_states.append((sp, sc, slc, sh))
        x, xs_in, gathered = channel_mixer(l, x, MOE_TILE_SEQ, tuple(routing) or None, after=xs_in,
                                           wait_for=gathered)
    y_prompt = x.reshape(batch, seq, D_MODEL)

    x = xs_in
    s_states = []
    for l in range(DEPTH):
        x, sp, sc, slc, sh = _mixer_step(x, state_pool[l], state_conv[l], state_lru_conv[l],
                                         state_lru_h[l], layers_f32[l], PAST_LEN)
        s_states.append((sp, sc, slc, sh))
        x, _, _ = channel_mixer(l, x, MOE_TILE_STEP, precise=True, wait_for=gathered)
    y_sample = x.reshape(dec_batch, 1, D_MODEL)

    def stack(states, k):
        return jnp.stack([s[k] for s in states])

    return (y_prompt, y_sample,
            stack(p_states, 0), stack(p_states, 1), stack(p_states, 2), stack(p_states, 3),
            stack(s_states, 0), stack(s_states, 1), stack(s_states, 2), stack(s_states, 3))
```

```python
import functools

import jax
import jax.numpy as jnp
from jax import lax
from jax.experimental import pallas as pl
from jax.experimental.pallas import tpu as pltpu
from jax.experimental.pallas import tpu_sc as plsc

D_MODEL = 1024
DEPTH = 2
PAST_LEN = 16384
D_POOL = 512
N_POOL_GROUPS = 4
POOL_GROUP = 128
POOL_WINDOWS = (2, 4, 8, 16)
POOL_BUF = 15
D_CONV = 512
CONV_WIDTH = 31
CONV_BUF = 30
D_LRU = 512
N_LRU_BLOCKS = 8
LRU_BLOCK = 64
LRU_CONV_WIDTH = 4
LRU_CONV_BUF = 3
LRU_C = 8.0
N_EXPERTS = 8
TOP_K = 2
ALPHA = (2.0 * DEPTH) ** 0.25
LN_EPS = 1e-5

O_POOL = 0
O_VAL = D_POOL
O_GLU = O_VAL + D_CONV
O_LRU = O_GLU + D_CONV
O_GELU = O_LRU + D_LRU
O_GATE = O_GELU + D_LRU

SUBLANES = 8
LANES = 128
VMEM_LIMIT_BYTES = 56 * 1024 * 1024
SC_CORES = 2
SC_SUBCORES = 16
SC_WINDOW = 128
SC_SUB_ROWS = 32

SEQ_TILE = 512
CONV_ROWS = 128
POOL_OFF = 16
CONV_OFF = 32
LCONV_OFF = 8
FFN_CHUNK = 256
PIECE = 256
STEP_COLS = 512
EXPERT_F_TILE = 3584
MOE_TILE_SEQ = 512
MOE_TILE_STEP = 128

BF16 = jnp.bfloat16
F32 = jnp.float32


def _sigmoid(x):
    return 0.5 * jnp.tanh(0.5 * x) + 0.5


def _silu(x):
    return x * _sigmoid(x)


def _gelu_tanh(x):
    return x * (0.5 * (1.0 + jnp.tanh(0.7978845608028654 * (x + 0.044715 * (x * x * x)))))


def _softplus(z):
    return jnp.maximum(z, 0.0) + jnp.log1p(jnp.exp(-jnp.abs(z)))


def _layer_norm(x, g, b):
    mu = jnp.mean(x, axis=-1, keepdims=True)
    xc = x - mu
    var = jnp.mean(xc * xc, axis=-1, keepdims=True)
    return xc * lax.rsqrt(var + LN_EPS) * g + b


def _dot(a, b):
    return jnp.dot(a, b, preferred_element_type=F32)


def _mm(a, w):
    return _dot(a.astype(BF16), w)


def _mm_hi(a, w):
    return jnp.dot(a, w, preferred_element_type=F32, precision=lax.Precision.HIGHEST)


def _lru_coeffs(xc, r, ig, lam, reset):
    log_a = (-LRU_C * r) * _softplus(-lam)
    a = jnp.exp(log_a)
    mult = jnp.sqrt(-jnp.tanh(log_a) * (a * a + 1.0))
    if reset is not None:
        a = jnp.where(reset, 0.0, a)
        mult = jnp.where(reset, 1.0, mult)
    return a, mult * ig * xc


def _mixer_seq_kernel(x_ref, w_in_ref, b_gate_ref, w_pool_ref, pool_scale_ref,
                      conv_w_ref, conv_b_ref, cln_g_ref, cln_b_ref,
                      lconv_w_ref, lconv_b_ref, wa_ref, ba_ref, wx_ref, bx_ref, lam_ref,
                      wbp_ref, wbc_ref, wbl_ref, w_out_ref, ln_g_ref, ln_b_ref, *rest, tm, with_router):
    if with_router:
        wr_ref, rest = rest[0], rest[1:]
        idx_ref, gate_ref, rest = rest[5], rest[6], rest[:5] + rest[7:]
    (y_ref, sp_ref, sc_ref, slc_ref, sh_ref,
     pool_ext, conv_ext, lconv_ext, act_buf, a_buf, b_buf, h_carry, gate_buf, gelu_buf,
     merged_buf, xb_buf) = rest
    i = pl.program_id(1)

    @pl.when(i == 0)
    def _():
        pool_ext[0:POOL_OFF, :] = jnp.zeros((POOL_OFF, D_POOL), F32)
        conv_ext[0:CONV_OFF, :] = jnp.zeros((CONV_OFF, D_CONV), F32)
        conv_ext[tm + CONV_OFF:tm + CONV_OFF + SUBLANES, :] = jnp.zeros((SUBLANES, D_CONV), F32)
        lconv_ext[0:LCONV_OFF, :] = jnp.zeros((LCONV_OFF, D_LRU), F32)
        h_carry[...] = jnp.zeros((1, D_LRU), F32)

    xb_buf[...] = x_ref[...].astype(BF16)

    def proj(lo, width):
        return _dot(xb_buf[...], w_in_ref[:, lo:lo + width])

    pos = i * tm + lax.broadcasted_iota(jnp.int32, (tm, 1), 0)

    glu = proj(O_VAL, D_CONV) * _sigmoid(proj(O_GLU, D_CONV))
    conv_ext[CONV_OFF:CONV_OFF + tm, :] = glu

    def pool_piece():
        u_pool = proj(O_POOL, D_POOL)
        pool_ext[POOL_OFF:POOL_OFF + tm, :] = u_pool
        parts = []
        for g, w in enumerate(POOL_WINDOWS):
            lo = g * POOL_GROUP
            acc = u_pool[:, lo:lo + POOL_GROUP]
            for k in range(1, w):
                acc = acc + pool_ext[POOL_OFF - k:POOL_OFF - k + tm, lo:lo + POOL_GROUP]
            inv_cnt = 1.0 / jnp.minimum(w, pos + 1).astype(F32)
            pooled = acc * inv_cnt - u_pool[:, lo:lo + POOL_GROUP]
            parts.append(_dot(pooled.astype(BF16), w_pool_ref[g]))
        mixed = jnp.concatenate(parts, axis=1) * pool_scale_ref[...]
        merged_buf[...] = gate_buf[:, 0:D_MODEL] * _dot(mixed.astype(BF16), wbp_ref[...])
        sp_ref[0] = pool_ext[tm + POOL_OFF - POOL_BUF:tm + POOL_OFF, :]
        pool_ext[0:POOL_OFF, :] = pool_ext[tm:tm + POOL_OFF, :]

    def lru_piece(lo):
        lconv_ext[LCONV_OFF:LCONV_OFF + tm, lo:lo + PIECE] = proj(O_LRU + lo, PIECE)

    def gelu_piece(lo):
        gelu_buf[:, lo:lo + PIECE] = _gelu_tanh(proj(O_GELU + lo, PIECE))

    def gate_piece(lo):
        gate_buf[:, lo:lo + PIECE] = _sigmoid(
            proj(O_GATE + lo, PIECE) + b_gate_ref[:, lo:lo + PIECE])

    pieces = ([functools.partial(gate_piece, lo) for lo in range(0, D_MODEL, PIECE)]
              + [pool_piece]
              + [functools.partial(lru_piece, lo) for lo in range(0, D_LRU, PIECE)]
              + [functools.partial(gate_piece, lo) for lo in range(D_MODEL, 2 * D_MODEL, PIECE)]
              + [functools.partial(gelu_piece, lo) for lo in range(0, D_LRU, PIECE)]
              + [functools.partial(gate_piece, lo) for lo in range(2 * D_MODEL, 3 * D_MODEL, PIECE)])
    n_conv_chunks = tm // CONV_ROWS
    base = CONV_OFF - CONV_BUF
    rows = CONV_ROWS + SUBLANES
    n_units = n_conv_chunks * (D_CONV // LANES)
    unit = 0
    for c in range(n_conv_chunks):
        c0 = c * CONV_ROWS
        for l0 in range(0, D_CONV, LANES):
            acc = jnp.zeros((CONV_ROWS, LANES), F32) + conv_b_ref[:, l0:l0 + LANES]
            for r in range(SUBLANES):
                z = None
                for k in range(CONV_WIDTH):
                    if (base + k) % SUBLANES != r:
                        continue
                    q8 = base + k - r
                    t = (conv_ext[c0 + q8:c0 + q8 + rows, l0:l0 + LANES]
                         * conv_w_ref[k:k + 1, l0:l0 + LANES])
                    z = t if z is None else z + t
                acc = acc + z[r:r + CONV_ROWS, :]
            act_buf[c0:c0 + CONV_ROWS, l0:l0 + LANES] = acc
            for piece in pieces[unit * len(pieces) // n_units:(unit + 1) * len(pieces) // n_units]:
                piece()
            unit += 1
        act_buf[c0:c0 + CONV_ROWS, :] = _silu(_layer_norm(
            act_buf[c0:c0 + CONV_ROWS, :], cln_g_ref[...], cln_b_ref[...]))
    merged_buf[...] += gate_buf[:, D_MODEL:2 * D_MODEL] * _dot(
        act_buf[...].astype(BF16), wbc_ref[...])
    sc_ref[0] = conv_ext[tm + CONV_OFF - CONV_BUF:tm + CONV_OFF, :]
    conv_ext[0:CONV_OFF, :] = conv_ext[tm:tm + CONV_OFF, :]

    base = LCONV_OFF - LRU_CONV_BUF
    xc = jnp.zeros((tm, D_LRU), F32) + lconv_b_ref[...]
    for k in range(LRU_CONV_WIDTH):
        xc = xc + lconv_ext[base + k:base + k + tm, :] * lconv_w_ref[k:k + 1, :]
    slc_ref[0] = lconv_ext[tm + LCONV_OFF - LRU_CONV_BUF:tm + LCONV_OFF, :]
    lconv_ext[0:LCONV_OFF, :] = lconv_ext[tm:tm + LCONV_OFF, :]
    xcb = xc.astype(BF16)
    r = _sigmoid(_dot(xcb, wa_ref[...]) + ba_ref[...])
    ig = _sigmoid(_dot(xcb, wx_ref[...]) + bx_ref[...])
    a, b = _lru_coeffs(xc, r, ig, lam_ref[...], pos == 0)
    a_buf[...] = a
    b_buf[...] = b

    row8 = lax.broadcasted_iota(jnp.int32, (SUBLANES, D_LRU), 0)

    def scan_block(j, h):
        r0 = pl.multiple_of(j * SUBLANES, SUBLANES)
        a8 = a_buf[pl.ds(r0, SUBLANES), :]
        b8 = b_buf[pl.ds(r0, SUBLANES), :]
        for k in (1, 2, 4):
            a_s = pltpu.roll(a8, k, 0)
            b_s = pltpu.roll(b8, k, 0)
            m = row8 >= k
            b8 = jnp.where(m, a8 * b_s + b8, b8)
            a8 = jnp.where(m, a8 * a_s, a8)
        h8 = a8 * h + b8
        a_buf[pl.ds(r0, SUBLANES), :] = h8
        return h8[SUBLANES - 1:SUBLANES, :]

    h_last = lax.fori_loop(0, tm // SUBLANES, scan_block, h_carry[...], unroll=2)
    h_carry[...] = h_last
    sh_ref[0] = h_last
    hg = a_buf[...] * gelu_buf[...]
    merged = merged_buf[...] + gate_buf[:, 2 * D_MODEL:3 * D_MODEL] * _dot(
        hg.astype(BF16), wbl_ref[...])

    m_out = _dot(merged.astype(BF16), w_out_ref[...])
    y = _layer_norm(ALPHA * x_ref[...] + m_out, ln_g_ref[...], ln_b_ref[...])
    y_ref[...] = y
    if with_router:
        idx_ref[...], gate_ref[...] = _route(y, wr_ref[...], precise=False)


def _mixer_step_kernel(x_ref, st_pool_ref, st_conv_ref, st_lconv_ref, st_h_ref,
                       w_in_ref, b_gate_ref, w_pool_ref, pool_scale_ref,
                       conv_w_ref, conv_b_ref, cln_g_ref, cln_b_ref,
                       lconv_w_ref, lconv_b_ref, wa_ref, ba_ref, wx_ref, bx_ref, lam_ref,
                       wbp_ref, wbc_ref, wbl_ref, w_out_ref, ln_g_ref, ln_b_ref,
                       y_ref, up_ref, glu_ref, ul_ref, sh_ref, u_buf, *, start_pos, n_blocks):
    k = pl.program_id(0)
    u_buf[k] = _mm_hi(x_ref[...], w_in_ref[...])

    @pl.when(k == n_blocks - 1)
    def _():
        x = x_ref[...]

        def proj(lo):
            return u_buf[lo // STEP_COLS]

        def gate(n):
            lo = n * D_MODEL
            u = jnp.concatenate([proj(O_GATE + lo), proj(O_GATE + lo + STEP_COLS)], axis=1)
            return _sigmoid(u + b_gate_ref[:, lo:lo + D_MODEL])

        u_pool = proj(O_POOL)
        parts = []
        for g, w in enumerate(POOL_WINDOWS):
            lo = g * POOL_GROUP
            acc = u_pool[:, lo:lo + POOL_GROUP]
            for j in range(1, w):
                acc = acc + st_pool_ref[POOL_BUF - j, :, lo:lo + POOL_GROUP]
            pooled = acc * (1.0 / min(w, start_pos + 1)) - u_pool[:, lo:lo + POOL_GROUP]
            parts.append(_mm_hi(pooled, w_pool_ref[g]))
        mixed = jnp.concatenate(parts, axis=1) * pool_scale_ref[...]
        merged = gate(0) * _mm_hi(mixed, wbp_ref[...])
        up_ref[...] = u_pool

        glu = proj(O_VAL) * _sigmoid(proj(O_GLU))
        acc = glu * conv_w_ref[CONV_BUF:CONV_BUF + 1, :] + conv_b_ref[...]
        for j in range(CONV_BUF):
            acc = acc + st_conv_ref[j] * conv_w_ref[j:j + 1, :]
        act = _silu(_layer_norm(acc, cln_g_ref[...], cln_b_ref[...]))
        merged = merged + gate(1) * _mm_hi(act, wbc_ref[...])
        glu_ref[...] = glu

        u_lru = proj(O_LRU)
        xc = u_lru * lconv_w_ref[LRU_CONV_BUF:LRU_CONV_BUF + 1, :] + lconv_b_ref[...]
        for j in range(LRU_CONV_BUF):
            xc = xc + st_lconv_ref[j] * lconv_w_ref[j:j + 1, :]
        ul_ref[...] = u_lru
        r = _sigmoid(_mm_hi(xc, wa_ref[...]) + ba_ref[...])
        ig = _sigmoid(_mm_hi(xc, wx_ref[...]) + bx_ref[...])
        reset = jnp.full(xc.shape, True) if start_pos == 0 else None
        a, b = _lru_coeffs(xc, r, ig, lam_ref[...], reset)
        h = a * st_h_ref[...] + b
        sh_ref[...] = h
        hg = h * _gelu_tanh(proj(O_GELU))
        merged = merged + gate(2) * _mm_hi(hg, wbl_ref[...])

        m_out = _mm_hi(merged, w_out_ref[...])
        y_ref[...] = _layer_norm(ALPHA * x + m_out, ln_g_ref[...], ln_b_ref[...])


def _const_spec(shape):
    nd = len(shape)
    return pl.BlockSpec(shape, lambda *_: (0,) * nd, pipeline_mode=pl.Buffered(1))


def _mixer_weight_list(p):
    return [p["w_in"], p["b_gate"], p["w_pool"], p["pool_scale"], p["conv_w"], p["conv_b"],
            p["conv_ln_g"], p["conv_ln_b"], p["lru_conv_w"], p["lru_conv_b"], p["lru_wa"],
            p["lru_ba"], p["lru_wx"], p["lru_bx"], p["lru_lambda"], p["w_br_pool"],
            p["w_br_conv"], p["w_br_lru"], p["w_out"], p["ln1_g"], p["ln1_b"]]


def _mixer_seq(x, p, batch, seq, wr_pad=None):
    tm = min(SEQ_TILE, seq)
    nt = seq // tm
    with_router = wr_pad is not None
    weights = _mixer_weight_list(p) + ([wr_pad] if with_router else [])
    in_specs = [pl.BlockSpec((tm, D_MODEL), lambda b, i: (b * nt + i, 0))]
    in_specs += [_const_spec(w.shape) for w in weights]
    out_shape = [
        jax.ShapeDtypeStruct((batch * seq, D_MODEL), F32),
        jax.ShapeDtypeStruct((batch, POOL_BUF, D_POOL), F32),
        jax.ShapeDtypeStruct((batch, CONV_BUF, D_CONV), F32),
        jax.ShapeDtypeStruct((batch, LRU_CONV_BUF, D_LRU), F32),
        jax.ShapeDtypeStruct((batch, 1, D_LRU), F32),
    ]
    out_specs = [
        pl.BlockSpec((tm, D_MODEL), lambda b, i: (b * nt + i, 0)),
        pl.BlockSpec((1, POOL_BUF, D_POOL), lambda b, i: (b, 0, 0)),
        pl.BlockSpec((1, CONV_BUF, D_CONV), lambda b, i: (b, 0, 0)),
        pl.BlockSpec((1, LRU_CONV_BUF, D_LRU), lambda b, i: (b, 0, 0)),
        pl.BlockSpec((1, 1, D_LRU), lambda b, i: (b, 0, 0)),
    ]
    if with_router:
        out_shape += [jax.ShapeDtypeStruct((batch * seq, LANES), jnp.int32),
                      jax.ShapeDtypeStruct((batch * seq, LANES), F32)]
        out_specs += [pl.BlockSpec((tm, LANES), lambda b, i: (b * nt + i, 0)),
                      pl.BlockSpec((tm, LANES), lambda b, i: (b * nt + i, 0))]
    scratch = [
        pltpu.VMEM((tm + POOL_OFF, D_POOL), F32),
        pltpu.VMEM((tm + CONV_OFF + SUBLANES, D_CONV), F32),
        pltpu.VMEM((tm + LCONV_OFF, D_LRU), F32),
        pltpu.VMEM((tm, D_CONV), F32),
        pltpu.VMEM((tm, D_LRU), F32),
        pltpu.VMEM((tm, D_LRU), F32),
        pltpu.VMEM((1, D_LRU), F32),
        pltpu.VMEM((tm, 3 * D_MODEL), F32),
        pltpu.VMEM((tm, D_LRU), F32),
        pltpu.VMEM((tm, D_MODEL), F32),
        pltpu.VMEM((tm, D_MODEL), BF16),
    ]
    y, sp, sc, slc, sh, *routing = pl.pallas_call(
        functools.partial(_mixer_seq_kernel, tm=tm, with_router=with_router),
        grid=(batch, nt),
        in_specs=in_specs,
        out_specs=out_specs,
        out_shape=out_shape,
        scratch_shapes=scratch,
        compiler_params=pltpu.CompilerParams(
            dimension_semantics=("arbitrary", "arbitrary"),
            vmem_limit_bytes=VMEM_LIMIT_BYTES),
        name="mixer_seq",
    )(x, *weights)
    return (y, sp, sc, slc, sh.reshape(batch, D_LRU), *routing)


def _mixer_step(x, st_pool, st_conv, st_lconv, st_h, p, start_pos):
    batch = x.shape[0]
    weights = _mixer_weight_list(p)
    states = [jnp.transpose(st_pool, (1, 0, 2)), jnp.transpose(st_conv, (1, 0, 2)),
              jnp.transpose(st_lconv, (1, 0, 2)), st_h]

    def spec(a):
        if not isinstance(a, tuple):
            return _const_spec(a.shape)
        arr, layer = a
        zeros = (0,) * (arr.ndim - 1)
        return pl.BlockSpec((None,) + arr.shape[1:], lambda k: (layer,) + zeros,
                            pipeline_mode=pl.Buffered(1))

    w_in_all, layer = weights[0]
    in_cols = w_in_all.shape[2]
    n_blocks = in_cols // STEP_COLS
    assert n_blocks * STEP_COLS == in_cols
    in_specs = [spec(a) for a in [x] + states + weights]
    in_specs[1 + len(states)] = pl.BlockSpec((None, D_MODEL, STEP_COLS), lambda k: (layer, 0, k))
    args = [x] + states + [w[0] if isinstance(w, tuple) else w for w in weights]
    out_shape = (
        jax.ShapeDtypeStruct((batch, D_MODEL), F32),
        jax.ShapeDtypeStruct((batch, D_POOL), F32),
        jax.ShapeDtypeStruct((batch, D_CONV), F32),
        jax.ShapeDtypeStruct((batch, D_LRU), F32),
        jax.ShapeDtypeStruct((batch, D_LRU), F32),
    )
    y, u_pool, glu, u_lru, sh = pl.pallas_call(
        functools.partial(_mixer_step_kernel, start_pos=start_pos, n_blocks=n_blocks),
        grid=(n_blocks,),
        in_specs=in_specs,
        out_specs=tuple(pl.BlockSpec(s.shape, lambda k: (0, 0)) for s in out_shape),
        out_shape=out_shape,
        scratch_shapes=[pltpu.VMEM((n_blocks, batch, STEP_COLS), F32)],
        compiler_params=pltpu.CompilerParams(
            dimension_semantics=("arbitrary",), vmem_limit_bytes=VMEM_LIMIT_BYTES),
        name="mixer_step",
    )(*args)

    def push(state, row):
        return jnp.concatenate([state[:, 1:], row[:, None]], axis=1)

    return y, push(st_pool, u_pool), push(st_conv, glu), push(st_lconv, u_lru), sh


def _ffn_kernel(x_ref, wg_ref, wu_ref, wd_ref, g_ref, b_ref, y_ref, *, d_ff, precise):
    x = x_ref[...]
    mm = _mm_hi if precise else _dot
    xb = x if precise else x.astype(BF16)
    acc = jnp.zeros(x.shape, F32)
    for c0 in range(0, d_ff, FFN_CHUNK):
        h = _silu(mm(xb, wg_ref[:, c0:c0 + FFN_CHUNK])) * mm(xb, wu_ref[:, c0:c0 + FFN_CHUNK])
        acc = acc + mm(h if precise else h.astype(BF16), wd_ref[c0:c0 + FFN_CHUNK, :])
    y_ref[...] = _layer_norm(ALPHA * x + acc, g_ref[...], b_ref[...])


def _ffn(x, wg, wu, wd, g, b, precise):
    n = x.shape[0]
    tm = min(SEQ_TILE, n)
    d_ff = wg.shape[1]
    weights = [wg, wu, wd, g, b]
    return pl.pallas_call(
        functools.partial(_ffn_kernel, d_ff=d_ff, precise=precise),
        grid=(n // tm,),
        in_specs=[pl.BlockSpec((tm, D_MODEL), lambda i: (i, 0))] + [_const_spec(w.shape) for w in weights],
        out_specs=pl.BlockSpec((tm, D_MODEL), lambda i: (i, 0)),
        out_shape=jax.ShapeDtypeStruct((n, D_MODEL), F32),
        compiler_params=pltpu.CompilerParams(
            dimension_semantics=("arbitrary",), vmem_limit_bytes=VMEM_LIMIT_BYTES),
        name="ffn_dense",
    )(x, *weights)


def _route(x, wr, precise):
    logits = _mm_hi(x, wr) if precise else _mm(x, wr)
    lane = lax.broadcasted_iota(jnp.int32, logits.shape, 1)
    neg = jnp.float32(-jnp.inf)
    l1 = jnp.where(lane < N_EXPERTS, logits, neg)
    v1 = jnp.max(l1, axis=-1, keepdims=True)
    i1 = jnp.min(jnp.where(l1 == v1, lane, LANES), axis=-1, keepdims=True)
    l2 = jnp.where(lane == i1, neg, l1)
    v2 = jnp.max(l2, axis=-1, keepdims=True)
    i2 = jnp.min(jnp.where(l2 == v2, lane, LANES), axis=-1, keepdims=True)
    e2 = jnp.exp(v2 - v1)
    den = 1.0 + e2
    idx = jnp.where(lane == 0, i1, jnp.where(lane == 1, i2, 0))
    gates = jnp.where(lane == 0, 1.0 / den, jnp.where(lane == 1, e2 / den, 0.0))
    return idx, gates


def _router_kernel(x_ref, wr_ref, idx_ref, gate_ref):
    idx_ref[...], gate_ref[...] = _route(x_ref[...], wr_ref[...], precise=True)


def _router(x, wr_pad):
    n = x.shape[0]
    tm = min(SEQ_TILE, n)
    return pl.pallas_call(
        _router_kernel,
        grid=(n // tm,),
        in_specs=[pl.BlockSpec((tm, D_MODEL), lambda i: (i, 0)), _const_spec(wr_pad.shape)],
        out_specs=(pl.BlockSpec((tm, LANES), lambda i: (i, 0)),
                   pl.BlockSpec((tm, LANES), lambda i: (i, 0))),
        out_shape=(jax.ShapeDtypeStruct((n, LANES), jnp.int32),
                   jax.ShapeDtypeStruct((n, LANES), F32)),
        compiler_params=pltpu.CompilerParams(dimension_semantics=("arbitrary",)),
        name="moe_router",
    )(x, wr_pad)


def _sc_row_gather(src, idx):
    n_idx = idx.shape[0]
    assert n_idx % SC_WINDOW == 0
    n_win = n_idx // SC_WINDOW
    n_workers = SC_CORES * SC_SUBCORES
    n_sub = SC_WINDOW // SC_SUB_ROWS
    mesh = plsc.VectorSubcoreMesh(core_axis_name="c", subcore_axis_name="s",
                                  num_cores=SC_CORES, num_subcores=SC_SUBCORES)

    def body(src_hbm, idx_hbm, out_hbm, idx_v, buf, g0, g1, w0, w1):
        gsem = (g0, g1)
        wsem = (w0, w1)
        worker = lax.axis_index("c") * SC_SUBCORES + lax.axis_index("s")

        @pl.loop(0, pl.cdiv(n_win, n_workers))
        def _(t):
            win = worker + t * n_workers

            @pl.when(win < n_win)
            def _():
                base = win * SC_WINDOW
                pltpu.sync_copy(idx_hbm.at[:, pl.ds(base, SC_WINDOW)], idx_v)

                def fetch(q):
                    return pltpu.async_copy(
                        src_hbm.at[idx_v.at[0, pl.ds(q * SC_SUB_ROWS, SC_SUB_ROWS)]],
                        buf.at[q % 2], gsem[q % 2])

                def put(q):
                    return pltpu.async_copy(
                        buf.at[q % 2], out_hbm.at[pl.ds(base + q * SC_SUB_ROWS, SC_SUB_ROWS)],
                        wsem[q % 2])

                fetches = {0: fetch(0)}
                puts = {}
                for q in range(n_sub):
                    if q + 1 < n_sub:
                        if q >= 1:
                            puts[q - 1].wait()
                        fetches[q + 1] = fetch(q + 1)
                    fetches[q].wait()
                    puts[q] = put(q)
                for q in range(max(0, n_sub - 2), n_sub):
                    puts[q].wait()

    return pl.kernel(
        body,
        out_type=jax.ShapeDtypeStruct((n_idx, D_MODEL), src.dtype),
        mesh=mesh,
        scratch_types=[pltpu.VMEM((1, SC_WINDOW), jnp.int32),
                       pltpu.VMEM((2, SC_SUB_ROWS, D_MODEL), src.dtype),
                       pltpu.SemaphoreType.DMA, pltpu.SemaphoreType.DMA,
                       pltpu.SemaphoreType.DMA, pltpu.SemaphoreType.DMA],
        cost_estimate=pl.CostEstimate(
            flops=0, transcendentals=0,
            bytes_accessed=n_idx * (2 * D_MODEL * src.dtype.itemsize + idx.dtype.itemsize)),
        name="sc_row_gather",
    )(src, idx.reshape(1, n_idx))


def _expert_kernel(te_ref, nu_ref, xs_ref, wg_ref, wu_ref, wd_ref, ys_ref, xs_bf, acc, *, tf, nf):
    del te_ref
    i = pl.program_id(0)
    j = pl.program_id(1)
    used = i < nu_ref[0]

    @pl.when(jnp.logical_and(used, j == 0))
    def _():
        xs_bf[...] = xs_ref[...].astype(BF16)

    @pl.when(used)
    def _():
        xb = xs_bf[...]
        part = None
        for c0 in range(0, tf, FFN_CHUNK):
            h = _silu(_dot(xb, wg_ref[:, c0:c0 + FFN_CHUNK])) * _dot(xb, wu_ref[:, c0:c0 + FFN_CHUNK])
            d = _dot(h.astype(BF16), wd_ref[c0:c0 + FFN_CHUNK, :])
            part = d if part is None else part + d

        if nf == 1:
            ys_ref[...] = part
        else:
            @pl.when(j == 0)
            def _():
                acc[...] = part

            @pl.when(jnp.logical_and(j > 0, j < nf - 1))
            def _():
                acc[...] += part

            @pl.when(j == nf - 1)
            def _():
                ys_ref[...] = acc[...] + part

    @pl.when(jnp.logical_and(jnp.logical_not(used), j == nf - 1))
    def _():
        ys_ref[...] = jnp.zeros(ys_ref.shape, F32)


def _experts(xs, tile_expert, n_used, wg, wu, wd, tm):
    n_tiles = tile_expert.shape[0]
    d_e = wg.shape[2]
    tf = EXPERT_F_TILE
    nf = d_e // tf
    assert nf * tf == d_e and tf % FFN_CHUNK == 0

    def jeff(i, j, nu):
        return jnp.where(i < nu[0], j, nf - 1)

    grid_spec = pltpu.PrefetchScalarGridSpec(
        num_scalar_prefetch=2,
        grid=(n_tiles, nf),
        in_specs=[
            pl.BlockSpec((tm, D_MODEL), lambda i, j, te, nu: (jnp.minimum(i, nu[0] - 1), 0)),
            pl.BlockSpec((None, D_MODEL, tf), lambda i, j, te, nu: (te[i], 0, jeff(i, j, nu))),
            pl.BlockSpec((None, D_MODEL, tf), lambda i, j, te, nu: (te[i], 0, jeff(i, j, nu))),
            pl.BlockSpec((None, tf, D_MODEL), lambda i, j, te, nu: (te[i], jeff(i, j, nu), 0)),
        ],
        out_specs=pl.BlockSpec((tm, D_MODEL), lambda i, j, te, nu: (i, 0)),
        scratch_shapes=[
            pltpu.VMEM((tm, D_MODEL), BF16),
            pltpu.VMEM((tm if nf > 1 else SUBLANES, D_MODEL), F32),
        ],
    )
    return pl.pallas_call(
        functools.partial(_expert_kernel, tf=tf, nf=nf),
        grid_spec=grid_spec,
        out_shape=jax.ShapeDtypeStruct((n_tiles * tm, D_MODEL), F32),
        compiler_params=pltpu.CompilerParams(
            dimension_semantics=("arbitrary", "arbitrary"), vmem_limit_bytes=VMEM_LIMIT_BYTES),
        name="moe_experts",
    )(tile_expert, n_used, xs, wg, wu, wd)


def _combine_kernel(x_ref, gate_ref, y0_ref, y1_ref, g_ref, b_ref, y_ref):
    gates = gate_ref[...]
    f = gates[:, 0:1] * y0_ref[...] + gates[:, 1:2] * y1_ref[...]
    y_ref[...] = _layer_norm(ALPHA * x_ref[...] + f, g_ref[...], b_ref[...])


def _combine(x, gates, y01, g, b):
    n = x.shape[0]
    tm = min(SEQ_TILE, n)
    nt = n // tm
    return pl.pallas_call(
        _combine_kernel,
        grid=(nt,),
        in_specs=[
            pl.BlockSpec((tm, D_MODEL), lambda i: (i, 0)),
            pl.BlockSpec((tm, LANES), lambda i: (i, 0)),
            pl.BlockSpec((tm, D_MODEL), lambda i: (i, 0)),
            pl.BlockSpec((tm, D_MODEL), lambda i: (i + nt, 0)),
            _const_spec(g.shape),
            _const_spec(b.shape),
        ],
        out_specs=pl.BlockSpec((tm, D_MODEL), lambda i: (i, 0)),
        out_shape=jax.ShapeDtypeStruct((n, D_MODEL), F32),
        compiler_params=pltpu.CompilerParams(
            dimension_semantics=("arbitrary",), vmem_limit_bytes=VMEM_LIMIT_BYTES),
        name="moe_combine",
    )(x, gates, y01, y01, g, b)


def _moe(x, wr_pad, wg, wu, wd, g, b, tm, routing=None, after=None, wait_for=None):
    n = x.shape[0]
    if wait_for is not None:
        x, _ = lax.optimization_barrier((x, wait_for))
    idx, gates = routing if routing is not None else _router(x, wr_pad)
    flat_e = idx[:, :TOP_K].reshape(-1)
    onehot = (flat_e[:, None] == jnp.arange(N_EXPERTS, dtype=jnp.int32)[None, :]).astype(jnp.int32)
    csum = jnp.cumsum(onehot, axis=0)
    rank = jnp.sum((csum - onehot) * onehot, axis=1)
    counts = csum[-1]
    padded = ((counts + tm - 1) // tm) * tm
    ends = jnp.cumsum(padded)
    slot = (ends - padded)[flat_e] + rank
    n_pairs = TOP_K * n
    n_tiles = n_pairs // tm + N_EXPERTS
    n_used = (ends[-1] // tm).astype(jnp.int32).reshape(1)
    tile_start = jnp.arange(n_tiles, dtype=jnp.int32) * tm
    tile_expert = jnp.sum((tile_start[:, None] >= ends[None, :]).astype(jnp.int32), axis=1)
    last_e = jnp.sum((ends[-1] - 1 >= ends).astype(jnp.int32))
    tile_expert = jnp.minimum(tile_expert, last_e).astype(jnp.int32)
    pair_sorted = jnp.sort(flat_e * n_pairs + jnp.arange(n_pairs, dtype=jnp.int32)) % n_pairs
    slot_ids = jnp.arange(n_tiles * tm, dtype=jnp.int32)
    slot_e = jnp.repeat(tile_expert, tm)
    local = slot_ids - (ends - padded)[slot_e]
    q = (jnp.cumsum(counts) - counts)[slot_e] + local
    valid = jnp.logical_and(local >= 0, local < counts[slot_e])
    tok_of_slot = jnp.where(valid, pair_sorted[jnp.clip(q, 0, n_pairs - 1)] // TOP_K, slot_ids % n)
    if after is not None:
        tok_of_slot, after = lax.optimization_barrier((tok_of_slot, after))
    xs = _sc_row_gather(x, tok_of_slot)
    ys = _experts(xs, tile_expert, n_used, wg, wu, wd, tm)
    y01 = _sc_row_gather(ys, jnp.transpose(slot.reshape(n, TOP_K)).reshape(-1))
    return _combine(x, gates, y01, g, b), after, xs


def _block_diag(w):
    n, c, _ = w.shape
    eye = jnp.eye(n, dtype=w.dtype)
    return (eye[:, None, :, None] * w[:, :, None, :]).reshape(n * c, n * c)


def _row(v):
    return v.reshape(1, -1)


def kernel(x_prompt, x_sample, state_pool, state_conv, state_lru_conv, state_lru_h, w_in, b_gate, w_pool, pool_scale, conv_w, conv_b, conv_ln_g, conv_ln_b, lru_conv_w, lru_conv_b, lru_wa, lru_ba, lru_wx, lru_bx, lru_lambda, w_br_pool, w_br_conv, w_br_lru, w_out, ln1_g, ln1_b, ln2_g, ln2_b, ffn_w_gate, ffn_w_up, ffn_w_down, moe_router, moe_w_gate, moe_w_up, moe_w_down):
    batch, seq, _ = x_prompt.shape
    dec_batch = x_sample.shape[0]

    def layer_params(l, mat):
        return {
            "w_in": w_in[l].astype(mat), "b_gate": _row(b_gate[l]),
            "w_pool": w_pool[l].astype(mat), "pool_scale": _row(pool_scale[l]),
            "conv_w": conv_w[l], "conv_b": _row(conv_b[l]),
            "conv_ln_g": _row(conv_ln_g[l]), "conv_ln_b": _row(conv_ln_b[l]),
            "lru_conv_w": lru_conv_w[l], "lru_conv_b": _row(lru_conv_b[l]),
            "lru_wa": _block_diag(lru_wa[l]).astype(mat), "lru_ba": _row(lru_ba[l]),
            "lru_wx": _block_diag(lru_wx[l]).astype(mat), "lru_bx": _row(lru_bx[l]),
            "lru_lambda": _row(lru_lambda[l]),
            "w_br_pool": w_br_pool[l].astype(mat), "w_br_conv": w_br_conv[l].astype(mat),
            "w_br_lru": w_br_lru[l].astype(mat), "w_out": w_out[l].astype(mat),
            "ln1_g": _row(ln1_g[l]), "ln1_b": _row(ln1_b[l]),
        }

    layers = [layer_params(l, BF16) for l in range(DEPTH)]
    layers_f32 = [layer_params(l, F32) for l in range(DEPTH)]
    for l in range(DEPTH):
        layers_f32[l].update({"w_in": (w_in, l), "w_br_pool": (w_br_pool, l), "w_br_conv": (w_br_conv, l),
                              "w_br_lru": (w_br_lru, l), "w_out": (w_out, l)})
    ffn_w = [(ffn_w_gate[m].astype(BF16), ffn_w_up[m].astype(BF16), ffn_w_down[m].astype(BF16))
             for m in range(ffn_w_gate.shape[0])]
    ffn_w_f32 = [(ffn_w_gate[m], ffn_w_up[m], ffn_w_down[m]) for m in range(ffn_w_gate.shape[0])]
    moe_w = [(jnp.pad(moe_router[m], ((0, 0), (0, LANES - N_EXPERTS))),
              moe_w_gate[m].astype(BF16), moe_w_up[m].astype(BF16), moe_w_down[m].astype(BF16))
             for m in range(moe_router.shape[0])]

    x_prompt, moe_w = lax.optimization_barrier((x_prompt, moe_w))

    def channel_mixer(l, x, moe_tile, routing=None, precise=False, after=None, wait_for=None):
        g, b = _row(ln2_g[l]), _row(ln2_b[l])
        if l % 2 == 0:
            return _ffn(x, *(ffn_w_f32 if precise else ffn_w)[l // 2], g, b, precise), after, wait_for
        return _moe(x, *moe_w[l // 2], g, b, moe_tile, routing, after, wait_for)

    x = x_prompt.reshape(batch * seq, D_MODEL)
    xs_in = x_sample.reshape(dec_batch, D_MODEL)
    gathered = None
    p_states = []
    for l in range(DEPTH):
        wr_pad = moe_w[l // 2][0].astype(BF16) if l % 2 == 1 else None
        x, sp, sc, slc, sh, *routing = _mixer_seq(x, layers[l], batch, seq, wr_pad)
        p_states.append((sp, sc, slc, sh))
        x, xs_in, gathered = channel_mixer(l, x, MOE_TILE_SEQ, tuple(routing) or None, after=xs_in,
                                           wait_for=gathered)
    y_prompt = x.reshape(batch, seq, D_MODEL)

    x = xs_in
    s_states = []
    for l in range(DEPTH):
        x, sp, sc, slc, sh = _mixer_step(x, state_pool[l], state_conv[l], state_lru_conv[l],
                                         state_lru_h[l], layers_f32[l], PAST_LEN)
        s_states.append((sp, sc, slc, sh))
        x, _, _ = channel_mixer(l, x, MOE_TILE_STEP, precise=True, wait_for=gathered)
    y_sample = x.reshape(dec_batch, 1, D_MODEL)

    def stack(states, k):
        return jnp.stack([s[k] for s in states])

    return (y_prompt, y_sample,
            stack(p_states, 0), stack(p_states, 1), stack(p_states, 2), stack(p_states, 3),
            stack(s_states, 0), stack(s_states, 1), stack(s_states, 2), stack(s_states, 3))
```

```python
import functools

import jax
import jax.numpy as jnp
from jax import lax
from jax.experimental import pallas as pl
from jax.experimental.pallas import tpu as pltpu
from jax.experimental.pallas import tpu_sc as plsc

D_MODEL = 1024
DEPTH = 2
PAST_LEN = 16384
D_POOL = 512
POOL_GROUP = 128
POOL_WINDOWS = (2, 4, 8, 16)
POOL_BUF = 15
D_CONV = 512
CONV_WIDTH = 31
CONV_BUF = 30
D_LRU = 512
LRU_CONV_WIDTH = 4
LRU_CONV_BUF = 3
LRU_C = 8.0
N_EXPERTS = 8
TOP_K = 2
ALPHA = (2.0 * DEPTH) ** 0.25
LN_EPS = 1e-5

O_POOL = 0
O_VAL = D_POOL
O_GLU = O_VAL + D_CONV
O_LRU = O_GLU + D_CONV
O_GELU = O_LRU + D_LRU
O_GATE = O_GELU + D_LRU

SUBLANES = 8
LANES = 128
VMEM_LIMIT_BYTES = 56 * 1024 * 1024
SC_CORES = 2
SC_SUBCORES = 16
SC_WINDOW = 128
SC_SUB_ROWS = 32

SEQ_TILE = 512
FFN_TILE = 1024
SCAN_UNROLL = 4
CONV_ROWS = 128
POOL_OFF = 16
CONV_OFF = 32
LCONV_OFF = 8
FFN_CHUNK = 256
PIECE = 256
STEP_COLS = 512
EXPERT_F_TILE = 3584
MOE_TILE_SEQ = 512
MOE_TILE_STEP = 128

BF16 = jnp.bfloat16
F32 = jnp.float32


def _sigmoid(x):
    return 0.5 * jnp.tanh(0.5 * x) + 0.5


def _silu(x):
    return x * _sigmoid(x)


def _gelu_tanh(x):
    return x * (0.5 * (1.0 + jnp.tanh(0.7978845608028654 * (x + 0.044715 * (x * x * x)))))


def _softplus(z):
    return jnp.maximum(z, 0.0) + jnp.log1p(jnp.exp(-jnp.abs(z)))


def _layer_norm(x, g, b):
    mu = jnp.mean(x, axis=-1, keepdims=True)
    xc = x - mu
    var = jnp.mean(xc * xc, axis=-1, keepdims=True)
    return xc * lax.rsqrt(var + LN_EPS) * g + b


def _dot(a, b):
    return jnp.dot(a, b, preferred_element_type=F32)


def _mm(a, w):
    return _dot(a.astype(BF16), w)


def _mm_hi(a, w):
    return jnp.dot(a, w, preferred_element_type=F32, precision=lax.Precision.HIGHEST)


def _lru_coeffs(xc, r, ig, lam, reset):
    log_a = (-LRU_C * r) * _softplus(-lam)
    a = jnp.exp(log_a)
    mult = jnp.sqrt(-jnp.tanh(log_a) * (a * a + 1.0))
    if reset is not None:
        a = jnp.where(reset, 0.0, a)
        mult = jnp.where(reset, 1.0, mult)
    return a, mult * ig * xc


def _mixer_seq_kernel(x_ref, w_in_ref, b_gate_ref, w_pool_ref, pool_scale_ref,
                      conv_w_ref, conv_b_ref, cln_g_ref, cln_b_ref,
                      lconv_w_ref, lconv_b_ref, wa_ref, ba_ref, wx_ref, bx_ref, lam_ref,
                      wbp_ref, wbc_ref, wbl_ref, w_out_ref, ln_g_ref, ln_b_ref, *rest, tm, with_router):
    if with_router:
        wr_ref, rest = rest[0], rest[1:]
        idx_ref, gate_ref, rest = rest[5], rest[6], rest[:5] + rest[7:]
    (y_ref, sp_ref, sc_ref, slc_ref, sh_ref,
     pool_ext, conv_ext, lconv_ext, act_buf, a_buf, b_buf, h_carry, gate_buf, gelu_buf,
     merged_buf, xb_buf) = rest
    i = pl.program_id(1)

    @pl.when(i == 0)
    def _():
        pool_ext[0:POOL_OFF, :] = jnp.zeros((POOL_OFF, D_POOL), F32)
        conv_ext[0:CONV_OFF, :] = jnp.zeros((CONV_OFF, D_CONV), F32)
        conv_ext[tm + CONV_OFF:tm + CONV_OFF + SUBLANES, :] = jnp.zeros((SUBLANES, D_CONV), F32)
        lconv_ext[0:LCONV_OFF, :] = jnp.zeros((LCONV_OFF, D_LRU), F32)
        h_carry[...] = jnp.zeros((1, D_LRU), F32)

    xb_buf[...] = x_ref[...].astype(BF16)

    def proj(lo, width):
        return _dot(xb_buf[...], w_in_ref[:, lo:lo + width])

    pos = i * tm + lax.broadcasted_iota(jnp.int32, (tm, 1), 0)

    glu = proj(O_VAL, D_CONV) * _sigmoid(proj(O_GLU, D_CONV))
    conv_ext[CONV_OFF:CONV_OFF + tm, :] = glu

    def pool_piece():
        u_pool = proj(O_POOL, D_POOL)
        pool_ext[POOL_OFF:POOL_OFF + tm, :] = u_pool
        parts = []
        for g, w in enumerate(POOL_WINDOWS):
            lo = g * POOL_GROUP
            acc = u_pool[:, lo:lo + POOL_GROUP]
            for k in range(1, w):
                acc = acc + pool_ext[POOL_OFF - k:POOL_OFF - k + tm, lo:lo + POOL_GROUP]
            inv_cnt = 1.0 / jnp.minimum(w, pos + 1).astype(F32)
            pooled = acc * inv_cnt - u_pool[:, lo:lo + POOL_GROUP]
            parts.append(_dot(pooled.astype(BF16), w_pool_ref[g]))
        mixed = jnp.concatenate(parts, axis=1) * pool_scale_ref[...]
        merged_buf[...] = gate_buf[:, 0:D_MODEL] * _dot(mixed.astype(BF16), wbp_ref[...])
        sp_ref[0] = pool_ext[tm + POOL_OFF - POOL_BUF:tm + POOL_OFF, :]
        pool_ext[0:POOL_OFF, :] = pool_ext[tm:tm + POOL_OFF, :]

    def lru_piece(lo):
        lconv_ext[LCONV_OFF:LCONV_OFF + tm, lo:lo + PIECE] = proj(O_LRU + lo, PIECE)

    def gelu_piece(lo):
        gelu_buf[:, lo:lo + PIECE] = _gelu_tanh(proj(O_GELU + lo, PIECE))

    def gate_piece(lo):
        gate_buf[:, lo:lo + PIECE] = _sigmoid(
            proj(O_GATE + lo, PIECE) + b_gate_ref[:, lo:lo + PIECE])

    pieces = ([functools.partial(gate_piece, lo) for lo in range(0, D_MODEL, PIECE)]
              + [pool_piece]
              + [functools.partial(lru_piece, lo) for lo in range(0, D_LRU, PIECE)]
              + [functools.partial(gate_piece, lo) for lo in range(D_MODEL, 2 * D_MODEL, PIECE)]
              + [functools.partial(gelu_piece, lo) for lo in range(0, D_LRU, PIECE)]
              + [functools.partial(gate_piece, lo) for lo in range(2 * D_MODEL, 3 * D_MODEL, PIECE)])
    n_conv_chunks = tm // CONV_ROWS
    base = CONV_OFF - CONV_BUF
    rows = CONV_ROWS + SUBLANES
    n_units = n_conv_chunks * (D_CONV // LANES)
    unit = 0
    for c in range(n_conv_chunks):
        c0 = c * CONV_ROWS
        for l0 in range(0, D_CONV, LANES):
            acc = jnp.zeros((CONV_ROWS, LANES), F32) + conv_b_ref[:, l0:l0 + LANES]
            for r in range(SUBLANES):
                z = None
                for k in range(CONV_WIDTH):
                    if (base + k) % SUBLANES != r:
                        continue
                    q8 = base + k - r
                    t = (conv_ext[c0 + q8:c0 + q8 + rows, l0:l0 + LANES]
                         * conv_w_ref[k:k + 1, l0:l0 + LANES])
                    z = t if z is None else z + t
                acc = acc + z[r:r + CONV_ROWS, :]
            act_buf[c0:c0 + CONV_ROWS, l0:l0 + LANES] = acc
            for piece in pieces[unit * len(pieces) // n_units:(unit + 1) * len(pieces) // n_units]:
                piece()
            unit += 1
        act_buf[c0:c0 + CONV_ROWS, :] = _silu(_layer_norm(
            act_buf[c0:c0 + CONV_ROWS, :], cln_g_ref[...], cln_b_ref[...]))
    merged_buf[...] += gate_buf[:, D_MODEL:2 * D_MODEL] * _dot(
        act_buf[...].astype(BF16), wbc_ref[...])
    sc_ref[0] = conv_ext[tm + CONV_OFF - CONV_BUF:tm + CONV_OFF, :]
    conv_ext[0:CONV_OFF, :] = conv_ext[tm:tm + CONV_OFF, :]

    base = LCONV_OFF - LRU_CONV_BUF
    xc = jnp.zeros((tm, D_LRU), F32) + lconv_b_ref[...]
    for k in range(LRU_CONV_WIDTH):
        xc = xc + lconv_ext[base + k:base + k + tm, :] * lconv_w_ref[k:k + 1, :]
    slc_ref[0] = lconv_ext[tm + LCONV_OFF - LRU_CONV_BUF:tm + LCONV_OFF, :]
    lconv_ext[0:LCONV_OFF, :] = lconv_ext[tm:tm + LCONV_OFF, :]
    xcb = xc.astype(BF16)
    r = _sigmoid(_dot(xcb, wa_ref[...]) + ba_ref[...])
    ig = _sigmoid(_dot(xcb, wx_ref[...]) + bx_ref[...])
    a, b = _lru_coeffs(xc, r, ig, lam_ref[...], pos == 0)
    a_buf[...] = a
    b_buf[...] = b

    row8 = lax.broadcasted_iota(jnp.int32, (SUBLANES, D_LRU), 0)

    def scan_block(j, h):
        r0 = pl.multiple_of(j * SUBLANES, SUBLANES)
        a8 = a_buf[pl.ds(r0, SUBLANES), :]
        b8 = b_buf[pl.ds(r0, SUBLANES), :]
        for k in (1, 2, 4):
            a_s = pltpu.roll(a8, k, 0)
            b_s = pltpu.roll(b8, k, 0)
            m = row8 >= k
            b8 = jnp.where(m, a8 * b_s + b8, b8)
            a8 = jnp.where(m, a8 * a_s, a8)
        h8 = a8 * h + b8
        a_buf[pl.ds(r0, SUBLANES), :] = h8
        return h8[SUBLANES - 1:SUBLANES, :]

    h_last = lax.fori_loop(0, tm // SUBLANES, scan_block, h_carry[...], unroll=SCAN_UNROLL)
    h_carry[...] = h_last
    sh_ref[0] = h_last
    hg = a_buf[...] * gelu_buf[...]
    merged = merged_buf[...] + gate_buf[:, 2 * D_MODEL:3 * D_MODEL] * _dot(
        hg.astype(BF16), wbl_ref[...])

    m_out = _dot(merged.astype(BF16), w_out_ref[...])
    y = _layer_norm(ALPHA * x_ref[...] + m_out, ln_g_ref[...], ln_b_ref[...])
    y_ref[...] = y
    if with_router:
        idx_ref[...], gate_ref[...] = _route(y, wr_ref[...], precise=False)


def _mixer_step_kernel(x_ref, st_pool_ref, st_conv_ref, st_lconv_ref, st_h_ref,
                       w_in_ref, b_gate_ref, w_pool_ref, pool_scale_ref,
                       conv_w_ref, conv_b_ref, cln_g_ref, cln_b_ref,
                       lconv_w_ref, lconv_b_ref, wa_ref, ba_ref, wx_ref, bx_ref, lam_ref,
                       wbp_ref, wbc_ref, wbl_ref, w_out_ref, ln_g_ref, ln_b_ref,
                       y_ref, up_ref, glu_ref, ul_ref, sh_ref, u_buf, *, start_pos, n_blocks):
    k = pl.program_id(0)
    u_buf[k] = _mm_hi(x_ref[...], w_in_ref[...])

    @pl.when(k == n_blocks - 1)
    def _():
        x = x_ref[...]

        def proj(lo):
            return u_buf[lo // STEP_COLS]

        def gate(n):
            lo = n * D_MODEL
            u = jnp.concatenate([proj(O_GATE + lo), proj(O_GATE + lo + STEP_COLS)], axis=1)
            return _sigmoid(u + b_gate_ref[:, lo:lo + D_MODEL])

        u_pool = proj(O_POOL)
        parts = []
        for g, w in enumerate(POOL_WINDOWS):
            lo = g * POOL_GROUP
            acc = u_pool[:, lo:lo + POOL_GROUP]
            for j in range(1, w):
                acc = acc + st_pool_ref[POOL_BUF - j, :, lo:lo + POOL_GROUP]
            pooled = acc * (1.0 / min(w, start_pos + 1)) - u_pool[:, lo:lo + POOL_GROUP]
            parts.append(_mm_hi(pooled, w_pool_ref[g]))
        mixed = jnp.concatenate(parts, axis=1) * pool_scale_ref[...]
        merged = gate(0) * _mm_hi(mixed, wbp_ref[...])
        up_ref[...] = u_pool

        glu = proj(O_VAL) * _sigmoid(proj(O_GLU))
        acc = glu * conv_w_ref[CONV_BUF:CONV_BUF + 1, :] + conv_b_ref[...]
        for j in range(CONV_BUF):
            acc = acc + st_conv_ref[j] * conv_w_ref[j:j + 1, :]
        act = _silu(_layer_norm(acc, cln_g_ref[...], cln_b_ref[...]))
        merged = merged + gate(1) * _mm_hi(act, wbc_ref[...])
        glu_ref[...] = glu

        u_lru = proj(O_LRU)
        xc = u_lru * lconv_w_ref[LRU_CONV_BUF:LRU_CONV_BUF + 1, :] + lconv_b_ref[...]
        for j in range(LRU_CONV_BUF):
            xc = xc + st_lconv_ref[j] * lconv_w_ref[j:j + 1, :]
        ul_ref[...] = u_lru
        r = _sigmoid(_mm_hi(xc, wa_ref[...]) + ba_ref[...])
        ig = _sigmoid(_mm_hi(xc, wx_ref[...]) + bx_ref[...])
        reset = jnp.full(xc.shape, True) if start_pos == 0 else None
        a, b = _lru_coeffs(xc, r, ig, lam_ref[...], reset)
        h = a * st_h_ref[...] + b
        sh_ref[...] = h
        hg = h * _gelu_tanh(proj(O_GELU))
        merged = merged + gate(2) * _mm_hi(hg, wbl_ref[...])

        m_out = _mm_hi(merged, w_out_ref[...])
        y_ref[...] = _layer_norm(ALPHA * x + m_out, ln_g_ref[...], ln_b_ref[...])


def _const_spec(shape):
    nd = len(shape)
    return pl.BlockSpec(shape, lambda *_: (0,) * nd, pipeline_mode=pl.Buffered(1))


def _mixer_weight_list(p):
    return [p["w_in"], p["b_gate"], p["w_pool"], p["pool_scale"], p["conv_w"], p["conv_b"],
            p["conv_ln_g"], p["conv_ln_b"], p["lru_conv_w"], p["lru_conv_b"], p["lru_wa"],
            p["lru_ba"], p["lru_wx"], p["lru_bx"], p["lru_lambda"], p["w_br_pool"],
            p["w_br_conv"], p["w_br_lru"], p["w_out"], p["ln1_g"], p["ln1_b"]]


def _mixer_seq(x, p, batch, seq, wr_pad=None):
    tm = min(SEQ_TILE, seq)
    nt = seq // tm
    with_router = wr_pad is not None
    weights = _mixer_weight_list(p) + ([wr_pad] if with_router else [])
    in_specs = [pl.BlockSpec((tm, D_MODEL), lambda b, i: (b * nt + i, 0))]
    in_specs += [_const_spec(w.shape) for w in weights]
    out_shape = [
        jax.ShapeDtypeStruct((batch * seq, D_MODEL), F32),
        jax.ShapeDtypeStruct((batch, POOL_BUF, D_POOL), F32),
        jax.ShapeDtypeStruct((batch, CONV_BUF, D_CONV), F32),
        jax.ShapeDtypeStruct((batch, LRU_CONV_BUF, D_LRU), F32),
        jax.ShapeDtypeStruct((batch, 1, D_LRU), F32),
    ]
    out_specs = [
        pl.BlockSpec((tm, D_MODEL), lambda b, i: (b * nt + i, 0)),
        pl.BlockSpec((1, POOL_BUF, D_POOL), lambda b, i: (b, 0, 0)),
        pl.BlockSpec((1, CONV_BUF, D_CONV), lambda b, i: (b, 0, 0)),
        pl.BlockSpec((1, LRU_CONV_BUF, D_LRU), lambda b, i: (b, 0, 0)),
        pl.BlockSpec((1, 1, D_LRU), lambda b, i: (b, 0, 0)),
    ]
    if with_router:
        out_shape += [jax.ShapeDtypeStruct((batch * seq, LANES), jnp.int32),
                      jax.ShapeDtypeStruct((batch * seq, LANES), F32)]
        out_specs += [pl.BlockSpec((tm, LANES), lambda b, i: (b * nt + i, 0)),
                      pl.BlockSpec((tm, LANES), lambda b, i: (b * nt + i, 0))]
    scratch = [
        pltpu.VMEM((tm + POOL_OFF, D_POOL), F32),
        pltpu.VMEM((tm + CONV_OFF + SUBLANES, D_CONV), F32),
        pltpu.VMEM((tm + LCONV_OFF, D_LRU), F32),
        pltpu.VMEM((tm, D_CONV), F32),
        pltpu.VMEM((tm, D_LRU), F32),
        pltpu.VMEM((tm, D_LRU), F32),
        pltpu.VMEM((1, D_LRU), F32),
        pltpu.VMEM((tm, 3 * D_MODEL), F32),
        pltpu.VMEM((tm, D_LRU), F32),
        pltpu.VMEM((tm, D_MODEL), F32),
        pltpu.VMEM((tm, D_MODEL), BF16),
    ]
    y, sp, sc, slc, sh, *routing = pl.pallas_call(
        functools.partial(_mixer_seq_kernel, tm=tm, with_router=with_router),
        grid=(batch, nt),
        in_specs=in_specs,
        out_specs=out_specs,
        out_shape=out_shape,
        scratch_shapes=scratch,
        compiler_params=pltpu.CompilerParams(
            dimension_semantics=("arbitrary", "arbitrary"),
            vmem_limit_bytes=VMEM_LIMIT_BYTES),
        name="mixer_seq",
    )(x, *weights)
    return (y, sp, sc, slc, sh.reshape(batch, D_LRU), *routing)


def _mixer_step(x, st_pool, st_conv, st_lconv, st_h, p, start_pos):
    batch = x.shape[0]
    weights = _mixer_weight_list(p)
    states = [jnp.transpose(st_pool, (1, 0, 2)), jnp.transpose(st_conv, (1, 0, 2)),
              jnp.transpose(st_lconv, (1, 0, 2)), st_h]

    def spec(a):
        if not isinstance(a, tuple):
            return _const_spec(a.shape)
        arr, layer = a
        zeros = (0,) * (arr.ndim - 1)
        return pl.BlockSpec((None,) + arr.shape[1:], lambda k: (layer,) + zeros,
                            pipeline_mode=pl.Buffered(1))

    w_in_all, layer = weights[0]
    in_cols = w_in_all.shape[2]
    n_blocks = in_cols // STEP_COLS
    assert n_blocks * STEP_COLS == in_cols
    in_specs = [spec(a) for a in [x] + states + weights]
    in_specs[1 + len(states)] = pl.BlockSpec((None, D_MODEL, STEP_COLS), lambda k: (layer, 0, k))
    args = [x] + states + [w[0] if isinstance(w, tuple) else w for w in weights]
    out_shape = (
        jax.ShapeDtypeStruct((batch, D_MODEL), F32),
        jax.ShapeDtypeStruct((batch, D_POOL), F32),
        jax.ShapeDtypeStruct((batch, D_CONV), F32),
        jax.ShapeDtypeStruct((batch, D_LRU), F32),
        jax.ShapeDtypeStruct((batch, D_LRU), F32),
    )
    y, u_pool, glu, u_lru, sh = pl.pallas_call(
        functools.partial(_mixer_step_kernel, start_pos=start_pos, n_blocks=n_blocks),
        grid=(n_blocks,),
        in_specs=in_specs,
        out_specs=tuple(pl.BlockSpec(s.shape, lambda k: (0, 0)) for s in out_shape),
        out_shape=out_shape,
        scratch_shapes=[pltpu.VMEM((n_blocks, batch, STEP_COLS), F32)],
        compiler_params=pltpu.CompilerParams(
            dimension_semantics=("arbitrary",), vmem_limit_bytes=VMEM_LIMIT_BYTES),
        name="mixer_step",
    )(*args)

    def push(state, row):
        return jnp.concatenate([state[:, 1:], row[:, None]], axis=1)

    return y, push(st_pool, u_pool), push(st_conv, glu), push(st_lconv, u_lru), sh


def _ffn_kernel(x_ref, wg_ref, wu_ref, wd_ref, g_ref, b_ref, y_ref, *, d_ff, precise):
    x = x_ref[...]
    mm = _mm_hi if precise else _dot
    xb = x if precise else x.astype(BF16)
    acc = jnp.zeros(x.shape, F32)
    for c0 in range(0, d_ff, FFN_CHUNK):
        h = _silu(mm(xb, wg_ref[:, c0:c0 + FFN_CHUNK])) * mm(xb, wu_ref[:, c0:c0 + FFN_CHUNK])
        acc = acc + mm(h if precise else h.astype(BF16), wd_ref[c0:c0 + FFN_CHUNK, :])
    y_ref[...] = _layer_norm(ALPHA * x + acc, g_ref[...], b_ref[...])


def _ffn(x, wg, wu, wd, g, b, precise):
    n = x.shape[0]
    tm = min(FFN_TILE, n)
    d_ff = wg.shape[1]
    weights = [wg, wu, wd, g, b]
    return pl.pallas_call(
        functools.partial(_ffn_kernel, d_ff=d_ff, precise=precise),
        grid=(n // tm,),
        in_specs=[pl.BlockSpec((tm, D_MODEL), lambda i: (i, 0))] + [_const_spec(w.shape) for w in weights],
        out_specs=pl.BlockSpec((tm, D_MODEL), lambda i: (i, 0)),
        out_shape=jax.ShapeDtypeStruct((n, D_MODEL), F32),
        compiler_params=pltpu.CompilerParams(
            dimension_semantics=("arbitrary",), vmem_limit_bytes=VMEM_LIMIT_BYTES),
        name="ffn_dense",
    )(x, *weights)


def _route(x, wr, precise):
    logits = _mm_hi(x, wr) if precise else _mm(x, wr)
    lane = lax.broadcasted_iota(jnp.int32, logits.shape, 1)
    neg = jnp.float32(-jnp.inf)
    l1 = jnp.where(lane < N_EXPERTS, logits, neg)
    v1 = jnp.max(l1, axis=-1, keepdims=True)
    i1 = jnp.min(jnp.where(l1 == v1, lane, LANES), axis=-1, keepdims=True)
    l2 = jnp.where(lane == i1, neg, l1)
    v2 = jnp.max(l2, axis=-1, keepdims=True)
    i2 = jnp.min(jnp.where(l2 == v2, lane, LANES), axis=-1, keepdims=True)
    e2 = jnp.exp(v2 - v1)
    den = 1.0 + e2
    idx = jnp.where(lane == 0, i1, jnp.where(lane == 1, i2, 0))
    gates = jnp.where(lane == 0, 1.0 / den, jnp.where(lane == 1, e2 / den, 0.0))
    return idx, gates


def _router_kernel(x_ref, wr_ref, idx_ref, gate_ref):
    idx_ref[...], gate_ref[...] = _route(x_ref[...], wr_ref[...], precise=True)


def _router(x, wr_pad):
    n = x.shape[0]
    tm = min(SEQ_TILE, n)
    return pl.pallas_call(
        _router_kernel,
        grid=(n // tm,),
        in_specs=[pl.BlockSpec((tm, D_MODEL), lambda i: (i, 0)), _const_spec(wr_pad.shape)],
        out_specs=(pl.BlockSpec((tm, LANES), lambda i: (i, 0)),
                   pl.BlockSpec((tm, LANES), lambda i: (i, 0))),
        out_shape=(jax.ShapeDtypeStruct((n, LANES), jnp.int32),
                   jax.ShapeDtypeStruct((n, LANES), F32)),
        compiler_params=pltpu.CompilerParams(dimension_semantics=("arbitrary",)),
        name="moe_router",
    )(x, wr_pad)


def _sc_row_gather(src, idx):
    n_idx = idx.shape[0]
    assert n_idx % SC_WINDOW == 0
    n_win = n_idx // SC_WINDOW
    n_workers = SC_CORES * SC_SUBCORES
    n_sub = SC_WINDOW // SC_SUB_ROWS
    mesh = plsc.VectorSubcoreMesh(core_axis_name="c", subcore_axis_name="s",
                                  num_cores=SC_CORES, num_subcores=SC_SUBCORES)

    def body(src_hbm, idx_hbm, out_hbm, idx_v, buf, g0, g1, w0, w1):
        gsem = (g0, g1)
        wsem = (w0, w1)
        worker = lax.axis_index("c") * SC_SUBCORES + lax.axis_index("s")

        @pl.loop(0, pl.cdiv(n_win, n_workers))
        def _(t):
            win = worker + t * n_workers

            @pl.when(win < n_win)
            def _():
                base = win * SC_WINDOW
                pltpu.sync_copy(idx_hbm.at[:, pl.ds(base, SC_WINDOW)], idx_v)

                def fetch(q):
                    return pltpu.async_copy(
                        src_hbm.at[idx_v.at[0, pl.ds(q * SC_SUB_ROWS, SC_SUB_ROWS)]],
                        buf.at[q % 2], gsem[q % 2])

                def put(q):
                    return pltpu.async_copy(
                        buf.at[q % 2], out_hbm.at[pl.ds(base + q * SC_SUB_ROWS, SC_SUB_ROWS)],
                        wsem[q % 2])

                fetches = {0: fetch(0)}
                puts = {}
                for q in range(n_sub):
                    if q + 1 < n_sub:
                        if q >= 1:
                            puts[q - 1].wait()
                        fetches[q + 1] = fetch(q + 1)
                    fetches[q].wait()
                    puts[q] = put(q)
                for q in range(max(0, n_sub - 2), n_sub):
                    puts[q].wait()

    return pl.kernel(
        body,
        out_type=jax.ShapeDtypeStruct((n_idx, D_MODEL), src.dtype),
        mesh=mesh,
        scratch_types=[pltpu.VMEM((1, SC_WINDOW), jnp.int32),
                       pltpu.VMEM((2, SC_SUB_ROWS, D_MODEL), src.dtype),
                       pltpu.SemaphoreType.DMA, pltpu.SemaphoreType.DMA,
                       pltpu.SemaphoreType.DMA, pltpu.SemaphoreType.DMA],
        cost_estimate=pl.CostEstimate(
            flops=0, transcendentals=0,
            bytes_accessed=n_idx * (2 * D_MODEL * src.dtype.itemsize + idx.dtype.itemsize)),
        name="sc_row_gather",
    )(src, idx.reshape(1, n_idx))


def _expert_kernel(te_ref, nu_ref, xs_ref, wg_ref, wu_ref, wd_ref, ys_ref, xs_bf, acc, *, tf, nf):
    del te_ref
    i = pl.program_id(0)
    j = pl.program_id(1)
    used = i < nu_ref[0]

    @pl.when(jnp.logical_and(used, j == 0))
    def _():
        xs_bf[...] = xs_ref[...].astype(BF16)

    @pl.when(used)
    def _():
        xb = xs_bf[...]
        part = None
        for c0 in range(0, tf, FFN_CHUNK):
            h = _silu(_dot(xb, wg_ref[:, c0:c0 + FFN_CHUNK])) * _dot(xb, wu_ref[:, c0:c0 + FFN_CHUNK])
            d = _dot(h.astype(BF16), wd_ref[c0:c0 + FFN_CHUNK, :])
            part = d if part is None else part + d

        if nf == 1:
            ys_ref[...] = part
        else:
            @pl.when(j == 0)
            def _():
                acc[...] = part

            @pl.when(jnp.logical_and(j > 0, j < nf - 1))
            def _():
                acc[...] += part

            @pl.when(j == nf - 1)
            def _():
                ys_ref[...] = acc[...] + part

    @pl.when(jnp.logical_and(jnp.logical_not(used), j == nf - 1))
    def _():
        ys_ref[...] = jnp.zeros(ys_ref.shape, F32)


def _experts(xs, tile_expert, n_used, wg, wu, wd, tm):
    n_tiles = tile_expert.shape[0]
    d_e = wg.shape[2]
    tf = EXPERT_F_TILE
    nf = d_e // tf
    assert nf * tf == d_e and tf % FFN_CHUNK == 0

    def jeff(i, j, nu):
        return jnp.where(i < nu[0], j, nf - 1)

    grid_spec = pltpu.PrefetchScalarGridSpec(
        num_scalar_prefetch=2,
        grid=(n_tiles, nf),
        in_specs=[
            pl.BlockSpec((tm, D_MODEL), lambda i, j, te, nu: (jnp.minimum(i, nu[0] - 1), 0)),
            pl.BlockSpec((None, D_MODEL, tf), lambda i, j, te, nu: (te[i], 0, jeff(i, j, nu))),
            pl.BlockSpec((None, D_MODEL, tf), lambda i, j, te, nu: (te[i], 0, jeff(i, j, nu))),
            pl.BlockSpec((None, tf, D_MODEL), lambda i, j, te, nu: (te[i], jeff(i, j, nu), 0)),
        ],
        out_specs=pl.BlockSpec((tm, D_MODEL), lambda i, j, te, nu: (i, 0)),
        scratch_shapes=[
            pltpu.VMEM((tm, D_MODEL), BF16),
            pltpu.VMEM((tm if nf > 1 else SUBLANES, D_MODEL), F32),
        ],
    )
    return pl.pallas_call(
        functools.partial(_expert_kernel, tf=tf, nf=nf),
        grid_spec=grid_spec,
        out_shape=jax.ShapeDtypeStruct((n_tiles * tm, D_MODEL), F32),
        compiler_params=pltpu.CompilerParams(
            dimension_semantics=("arbitrary", "arbitrary"), vmem_limit_bytes=VMEM_LIMIT_BYTES),
        name="moe_experts",
    )(tile_expert, n_used, xs, wg, wu, wd)


def _combine_kernel(x_ref, gate_ref, y0_ref, y1_ref, g_ref, b_ref, y_ref):
    gates = gate_ref[...]
    f = gates[:, 0:1] * y0_ref[...] + gates[:, 1:2] * y1_ref[...]
    y_ref[...] = _layer_norm(ALPHA * x_ref[...] + f, g_ref[...], b_ref[...])


def _combine(x, gates, y01, g, b):
    n = x.shape[0]
    tm = min(SEQ_TILE, n)
    nt = n // tm
    return pl.pallas_call(
        _combine_kernel,
        grid=(nt,),
        in_specs=[
            pl.BlockSpec((tm, D_MODEL), lambda i: (i, 0)),
            pl.BlockSpec((tm, LANES), lambda i: (i, 0)),
            pl.BlockSpec((tm, D_MODEL), lambda i: (i, 0)),
            pl.BlockSpec((tm, D_MODEL), lambda i: (i + nt, 0)),
            _const_spec(g.shape),
            _const_spec(b.shape),
        ],
        out_specs=pl.BlockSpec((tm, D_MODEL), lambda i: (i, 0)),
        out_shape=jax.ShapeDtypeStruct((n, D_MODEL), F32),
        compiler_params=pltpu.CompilerParams(
            dimension_semantics=("arbitrary",), vmem_limit_bytes=VMEM_LIMIT_BYTES),
        name="moe_combine",
    )(x, gates, y01, y01, g, b)


def _moe(x, wr_pad, wg, wu, wd, g, b, tm, routing=None, after=None, wait_for=None):
    n = x.shape[0]
    if wait_for is not None:
        x, _ = lax.optimization_barrier((x, wait_for))
    idx, gates = routing if routing is not None else _router(x, wr_pad)
    flat_e = idx[:, :TOP_K].reshape(-1)
    onehot = (flat_e[:, None] == jnp.arange(N_EXPERTS, dtype=jnp.int32)[None, :]).astype(jnp.int32)
    csum = jnp.cumsum(onehot, axis=0)
    rank = jnp.sum((csum - onehot) * onehot, axis=1)
    counts = csum[-1]
    padded = ((counts + tm - 1) // tm) * tm
    ends = jnp.cumsum(padded)
    slot = (ends - padded)[flat_e] + rank
    n_pairs = TOP_K * n
    n_tiles = n_pairs // tm + N_EXPERTS
    n_used = (ends[-1] // tm).astype(jnp.int32).reshape(1)
    tile_start = jnp.arange(n_tiles, dtype=jnp.int32) * tm
    tile_expert = jnp.sum((tile_start[:, None] >= ends[None, :]).astype(jnp.int32), axis=1)
    last_e = jnp.sum((ends[-1] - 1 >= ends).astype(jnp.int32))
    tile_expert = jnp.minimum(tile_expert, last_e).astype(jnp.int32)
    pair_sorted = jnp.sort(flat_e * n_pairs + jnp.arange(n_pairs, dtype=jnp.int32)) % n_pairs
    slot_ids = jnp.arange(n_tiles * tm, dtype=jnp.int32)
    slot_e = jnp.repeat(tile_expert, tm)
    local = slot_ids - (ends - padded)[slot_e]
    q = (jnp.cumsum(counts) - counts)[slot_e] + local
    valid = jnp.logical_and(local >= 0, local < counts[slot_e])
    tok_of_slot = jnp.where(valid, pair_sorted[jnp.clip(q, 0, n_pairs - 1)] // TOP_K, slot_ids % n)
    if after is not None:
        tok_of_slot, after = lax.optimization_barrier((tok_of_slot, after))
    xs = _sc_row_gather(x, tok_of_slot)
    ys = _experts(xs, tile_expert, n_used, wg, wu, wd, tm)
    y01 = _sc_row_gather(ys, jnp.transpose(slot.reshape(n, TOP_K)).reshape(-1))
    return _combine(x, gates, y01, g, b), after, xs


def _block_diag(w):
    n, c, _ = w.shape
    eye = jnp.eye(n, dtype=w.dtype)
    return (eye[:, None, :, None] * w[:, :, None, :]).reshape(n * c, n * c)


def _row(v):
    return v.reshape(1, -1)


def kernel(x_prompt, x_sample, state_pool, state_conv, state_lru_conv, state_lru_h, w_in, b_gate, w_pool, pool_scale, conv_w, conv_b, conv_ln_g, conv_ln_b, lru_conv_w, lru_conv_b, lru_wa, lru_ba, lru_wx, lru_bx, lru_lambda, w_br_pool, w_br_conv, w_br_lru, w_out, ln1_g, ln1_b, ln2_g, ln2_b, ffn_w_gate, ffn_w_up, ffn_w_down, moe_router, moe_w_gate, moe_w_up, moe_w_down):
    batch, seq, _ = x_prompt.shape
    dec_batch = x_sample.shape[0]

    def layer_params(l, mat):
        return {
            "w_in": w_in[l].astype(mat), "b_gate": _row(b_gate[l]),
            "w_pool": w_pool[l].astype(mat), "pool_scale": _row(pool_scale[l]),
            "conv_w": conv_w[l], "conv_b": _row(conv_b[l]),
            "conv_ln_g": _row(conv_ln_g[l]), "conv_ln_b": _row(conv_ln_b[l]),
            "lru_conv_w": lru_conv_w[l], "lru_conv_b": _row(lru_conv_b[l]),
            "lru_wa": _block_diag(lru_wa[l]).astype(mat), "lru_ba": _row(lru_ba[l]),
            "lru_wx": _block_diag(lru_wx[l]).astype(mat), "lru_bx": _row(lru_bx[l]),
            "lru_lambda": _row(lru_lambda[l]),
            "w_br_pool": w_br_pool[l].astype(mat), "w_br_conv": w_br_conv[l].astype(mat),
            "w_br_lru": w_br_lru[l].astype(mat), "w_out": w_out[l].astype(mat),
            "ln1_g": _row(ln1_g[l]), "ln1_b": _row(ln1_b[l]),
        }

    layers = [layer_params(l, BF16) for l in range(DEPTH)]
    layers_f32 = [layer_params(l, F32) for l in range(DEPTH)]
    for l in range(DEPTH):
        layers_f32[l].update({"w_in": (w_in, l), "w_br_pool": (w_br_pool, l), "w_br_conv": (w_br_conv, l),
                              "w_br_lru": (w_br_lru, l), "w_out": (w_out, l)})
    ffn_w = [(ffn_w_gate[m].astype(BF16), ffn_w_up[m].astype(BF16), ffn_w_down[m].astype(BF16))
             for m in range(ffn_w_gate.shape[0])]
    ffn_w_f32 = [(ffn_w_gate[m], ffn_w_up[m], ffn_w_down[m]) for m in range(ffn_w_gate.shape[0])]
    moe_w = [(jnp.pad(moe_router[m], ((0, 0), (0, LANES - N_EXPERTS))),
              moe_w_gate[m].astype(BF16), moe_w_up[m].astype(BF16), moe_w_down[m].astype(BF16))
             for m in range(moe_router.shape[0])]

    x_prompt, moe_w = lax.optimization_barrier((x_prompt, moe_w))

    def channel_mixer(l, x, moe_tile, routing=None, precise=False, after=None, wait_for=None):
        g, b = _row(ln2_g[l]), _row(ln2_b[l])
        if l % 2 == 0:
            return _ffn(x, *(ffn_w_f32 if precise else ffn_w)[l // 2], g, b, precise), after, wait_for
        return _moe(x, *moe_w[l // 2], g, b, moe_tile, routing, after, wait_for)

    x = x_prompt.reshape(batch * seq, D_MODEL)
    xs_in = x_sample.reshape(dec_batch, D_MODEL)
    gathered = None
    p_states = []
    for l in range(DEPTH):
        wr_pad = moe_w[l // 2][0].astype(BF16) if l % 2 == 1 else None
        x, sp, sc, slc, sh, *routing = _mixer_seq(x, layers[l], batch, seq, wr_pad)
        p_states.append((sp, sc, slc, sh))
        x, xs_in, gathered = channel_mixer(l, x, MOE_TILE_SEQ, tuple(routing) or None, after=xs_in,
                                           wait_for=gathered)
    y_prompt = x.reshape(batch, seq, D_MODEL)

    x = xs_in
    s_states = []
    for l in range(DEPTH):
        x, sp, sc, slc, sh = _mixer_step(x, state_pool[l], state_conv[l], state_lru_conv[l],
                                         state_lru_h[l], layers_f32[l], PAST_LEN)
        s_states.append((sp, sc, slc, sh))
        x, _, _ = channel_mixer(l, x, MOE_TILE_STEP, precise=True, wait_for=gathered)
    y_sample = x.reshape(dec_batch, 1, D_MODEL)

    def stack(states, k):
        return jnp.stack([s[k] for s in states])

    return (y_prompt, y_sample,
            stack(p_states, 0), stack(p_states, 1), stack(p_states, 2), stack(p_states, 3),
            stack(s_states, 0), stack(s_states, 1), stack(s_states, 2), stack(s_states, 3))
```

```python
import functools

import jax
import jax.numpy as jnp
from jax import lax
from jax.experimental import pallas as pl
from jax.experimental.pallas import tpu as pltpu
from jax.experimental.pallas import tpu_sc as plsc

D_MODEL = 1024
DEPTH = 2
PAST_LEN = 16384
D_POOL = 512
POOL_GROUP = 128
POOL_WINDOWS = (2, 4, 8, 16)
POOL_BUF = 15
D_CONV = 512
CONV_WIDTH = 31
CONV_BUF = 30
D_LRU = 512
LRU_CONV_WIDTH = 4
LRU_CONV_BUF = 3
LRU_C = 8.0
N_EXPERTS = 8
TOP_K = 2
ALPHA = (2.0 * DEPTH) ** 0.25
LN_EPS = 1e-5

O_POOL = 0
O_VAL = D_POOL
O_GLU = O_VAL + D_CONV
O_LRU = O_GLU + D_CONV
O_GELU = O_LRU + D_LRU
O_GATE = O_GELU + D_LRU

SUBLANES = 8
LANES = 128
VMEM_LIMIT_BYTES = 56 * 1024 * 1024
SC_CORES = 2
SC_SUBCORES = 16
SC_WINDOW = 128
SC_SUB_ROWS = 32

SEQ_TILE = 512
SCAN_UNROLL = 2
CONV_ROWS = 128
POOL_OFF = 16
CONV_OFF = 32
LCONV_OFF = 8
FFN_CHUNK = 256
PIECE = 256
STEP_COLS = 512
EXPERT_F_TILE = 3584
MOE_TILE_SEQ = 512
MOE_TILE_STEP = 128

BF16 = jnp.bfloat16
F32 = jnp.float32


def _sigmoid(x):
    return 0.5 * jnp.tanh(0.5 * x) + 0.5


def _silu(x):
    return x * _sigmoid(x)


def _gelu_tanh(x):
    return x * (0.5 * (1.0 + jnp.tanh(0.7978845608028654 * (x + 0.044715 * (x * x * x)))))


def _softplus(z):
    return jnp.maximum(z, 0.0) + jnp.log1p(jnp.exp(-jnp.abs(z)))


def _layer_norm(x, g, b):
    mu = jnp.mean(x, axis=-1, keepdims=True)
    xc = x - mu
    var = jnp.mean(xc * xc, axis=-1, keepdims=True)
    return xc * lax.rsqrt(var + LN_EPS) * g + b


def _dot(a, b):
    return jnp.dot(a, b, preferred_element_type=F32)


def _mm(a, w):
    return _dot(a.astype(BF16), w)


def _mm_hi(a, w):
    return jnp.dot(a, w, preferred_element_type=F32, precision=lax.Precision.HIGHEST)


def _lru_coeffs(xc, r, ig, lam, reset):
    log_a = (-LRU_C * r) * _softplus(-lam)
    a = jnp.exp(log_a)
    mult = jnp.sqrt(-jnp.tanh(log_a) * (a * a + 1.0))
    if reset is not None:
        a = jnp.where(reset, 0.0, a)
        mult = jnp.where(reset, 1.0, mult)
    return a, mult * ig * xc


def _mixer_seq_kernel(x_ref, w_in_ref, b_gate_ref, w_pool_ref, pool_scale_ref,
                      conv_w_ref, conv_b_ref, cln_g_ref, cln_b_ref,
                      lconv_w_ref, lconv_b_ref, wa_ref, ba_ref, wx_ref, bx_ref, lam_ref,
                      wbp_ref, wbc_ref, wbl_ref, w_out_ref, ln_g_ref, ln_b_ref, *rest, tm, with_router):
    if with_router:
        wr_ref, rest = rest[0], rest[1:]
        idx_ref, gate_ref, rest = rest[5], rest[6], rest[:5] + rest[7:]
    (y_ref, sp_ref, sc_ref, slc_ref, sh_ref,
     pool_ext, conv_ext, lconv_ext, act_buf, a_buf, b_buf, h_carry, gate_buf, gelu_buf,
     merged_buf, xb_buf) = rest
    i = pl.program_id(1)

    @pl.when(i == 0)
    def _():
        pool_ext[0:POOL_OFF, :] = jnp.zeros((POOL_OFF, D_POOL), F32)
        conv_ext[0:CONV_OFF, :] = jnp.zeros((CONV_OFF, D_CONV), F32)
        conv_ext[tm + CONV_OFF:tm + CONV_OFF + SUBLANES, :] = jnp.zeros((SUBLANES, D_CONV), F32)
        lconv_ext[0:LCONV_OFF, :] = jnp.zeros((LCONV_OFF, D_LRU), F32)
        h_carry[...] = jnp.zeros((1, D_LRU), F32)

    xb_buf[...] = x_ref[...].astype(BF16)

    def proj(lo, width):
        return _dot(xb_buf[...], w_in_ref[:, lo:lo + width])

    pos = i * tm + lax.broadcasted_iota(jnp.int32, (tm, 1), 0)

    glu = proj(O_VAL, D_CONV) * _sigmoid(proj(O_GLU, D_CONV))
    conv_ext[CONV_OFF:CONV_OFF + tm, :] = glu

    def pool_piece():
        u_pool = proj(O_POOL, D_POOL)
        pool_ext[POOL_OFF:POOL_OFF + tm, :] = u_pool
        parts = []
        for g, w in enumerate(POOL_WINDOWS):
            lo = g * POOL_GROUP
            acc = u_pool[:, lo:lo + POOL_GROUP]
            for k in range(1, w):
                acc = acc + pool_ext[POOL_OFF - k:POOL_OFF - k + tm, lo:lo + POOL_GROUP]
            inv_cnt = 1.0 / jnp.minimum(w, pos + 1).astype(F32)
            pooled = acc * inv_cnt - u_pool[:, lo:lo + POOL_GROUP]
            parts.append(_dot(pooled.astype(BF16), w_pool_ref[g]))
        mixed = jnp.concatenate(parts, axis=1) * pool_scale_ref[...]
        merged_buf[...] = gate_buf[:, 0:D_MODEL] * _dot(mixed.astype(BF16), wbp_ref[...])
        sp_ref[0] = pool_ext[tm + POOL_OFF - POOL_BUF:tm + POOL_OFF, :]
        pool_ext[0:POOL_OFF, :] = pool_ext[tm:tm + POOL_OFF, :]

    def lru_piece(lo):
        lconv_ext[LCONV_OFF:LCONV_OFF + tm, lo:lo + PIECE] = proj(O_LRU + lo, PIECE)

    def gelu_piece(lo):
        gelu_buf[:, lo:lo + PIECE] = _gelu_tanh(proj(O_GELU + lo, PIECE))

    def gate_piece(lo):
        gate_buf[:, lo:lo + PIECE] = _sigmoid(
            proj(O_GATE + lo, PIECE) + b_gate_ref[:, lo:lo + PIECE])

    pieces = ([functools.partial(gate_piece, lo) for lo in range(0, D_MODEL, PIECE)]
              + [pool_piece]
              + [functools.partial(lru_piece, lo) for lo in range(0, D_LRU, PIECE)]
              + [functools.partial(gate_piece, lo) for lo in range(D_MODEL, 2 * D_MODEL, PIECE)]
              + [functools.partial(gelu_piece, lo) for lo in range(0, D_LRU, PIECE)]
              + [functools.partial(gate_piece, lo) for lo in range(2 * D_MODEL, 3 * D_MODEL, PIECE)])
    n_conv_chunks = tm // CONV_ROWS
    base = CONV_OFF - CONV_BUF
    rows = CONV_ROWS + SUBLANES
    n_units = n_conv_chunks * (D_CONV // LANES)
    unit = 0
    for c in range(n_conv_chunks):
        c0 = c * CONV_ROWS
        for l0 in range(0, D_CONV, LANES):
            acc = jnp.zeros((CONV_ROWS, LANES), F32) + conv_b_ref[:, l0:l0 + LANES]
            for r in range(SUBLANES):
                z = None
                for k in range(CONV_WIDTH):
                    if (base + k) % SUBLANES != r:
                        continue
                    q8 = base + k - r
                    t = (conv_ext[c0 + q8:c0 + q8 + rows, l0:l0 + LANES]
                         * conv_w_ref[k:k + 1, l0:l0 + LANES])
                    z = t if z is None else z + t
                acc = acc + z[r:r + CONV_ROWS, :]
            act_buf[c0:c0 + CONV_ROWS, l0:l0 + LANES] = acc
            for piece in pieces[unit * len(pieces) // n_units:(unit + 1) * len(pieces) // n_units]:
                piece()
            unit += 1
        act_buf[c0:c0 + CONV_ROWS, :] = _silu(_layer_norm(
            act_buf[c0:c0 + CONV_ROWS, :], cln_g_ref[...], cln_b_ref[...]))
    merged_buf[...] += gate_buf[:, D_MODEL:2 * D_MODEL] * _dot(
        act_buf[...].astype(BF16), wbc_ref[...])
    sc_ref[0] = conv_ext[tm + CONV_OFF - CONV_BUF:tm + CONV_OFF, :]
    conv_ext[0:CONV_OFF, :] = conv_ext[tm:tm + CONV_OFF, :]

    base = LCONV_OFF - LRU_CONV_BUF
    xc = jnp.zeros((tm, D_LRU), F32) + lconv_b_ref[...]
    for k in range(LRU_CONV_WIDTH):
        xc = xc + lconv_ext[base + k:base + k + tm, :] * lconv_w_ref[k:k + 1, :]
    slc_ref[0] = lconv_ext[tm + LCONV_OFF - LRU_CONV_BUF:tm + LCONV_OFF, :]
    lconv_ext[0:LCONV_OFF, :] = lconv_ext[tm:tm + LCONV_OFF, :]
    xcb = xc.astype(BF16)
    r = _sigmoid(_dot(xcb, wa_ref[...]) + ba_ref[...])
    ig = _sigmoid(_dot(xcb, wx_ref[...]) + bx_ref[...])
    a, b = _lru_coeffs(xc, r, ig, lam_ref[...], pos == 0)
    a_buf[...] = a
    b_buf[...] = b

    row8 = lax.broadcasted_iota(jnp.int32, (SUBLANES, D_LRU), 0)

    def scan_block(j, h):
        r0 = pl.multiple_of(j * SUBLANES, SUBLANES)
        a8 = a_buf[pl.ds(r0, SUBLANES), :]
        b8 = b_buf[pl.ds(r0, SUBLANES), :]
        for k in (1, 2, 4):
            a_s = pltpu.roll(a8, k, 0)
            b_s = pltpu.roll(b8, k, 0)
            m = row8 >= k
            b8 = jnp.where(m, a8 * b_s + b8, b8)
            a8 = jnp.where(m, a8 * a_s, a8)
        h8 = a8 * h + b8
        a_buf[pl.ds(r0, SUBLANES), :] = h8
        return h8[SUBLANES - 1:SUBLANES, :]

    h_last = lax.fori_loop(0, tm // SUBLANES, scan_block, h_carry[...], unroll=SCAN_UNROLL)
    h_carry[...] = h_last
    sh_ref[0] = h_last
    hg = a_buf[...] * gelu_buf[...]
    merged = merged_buf[...] + gate_buf[:, 2 * D_MODEL:3 * D_MODEL] * _dot(
        hg.astype(BF16), wbl_ref[...])

    m_out = _dot(merged.astype(BF16), w_out_ref[...])
    y = _layer_norm(ALPHA * x_ref[...] + m_out, ln_g_ref[...], ln_b_ref[...])
    y_ref[...] = y
    if with_router:
        idx_ref[...], gate_ref[...] = _route(y, wr_ref[...], precise=False)


def _mixer_step_kernel(x_ref, st_pool_ref, st_conv_ref, st_lconv_ref, st_h_ref,
                       w_in_ref, b_gate_ref, w_pool_ref, pool_scale_ref,
                       conv_w_ref, conv_b_ref, cln_g_ref, cln_b_ref,
                       lconv_w_ref, lconv_b_ref, wa_ref, ba_ref, wx_ref, bx_ref, lam_ref,
                       wbp_ref, wbc_ref, wbl_ref, w_out_ref, ln_g_ref, ln_b_ref,
                       y_ref, up_ref, glu_ref, ul_ref, sh_ref, u_buf, *, start_pos, n_blocks):
    k = pl.program_id(0)
    u_buf[k] = _mm_hi(x_ref[...], w_in_ref[...])

    @pl.when(k == n_blocks - 1)
    def _():
        x = x_ref[...]

        def proj(lo):
            return u_buf[lo // STEP_COLS]

        def gate(n):
            lo = n * D_MODEL
            u = jnp.concatenate([proj(O_GATE + lo), proj(O_GATE + lo + STEP_COLS)], axis=1)
            return _sigmoid(u + b_gate_ref[:, lo:lo + D_MODEL])

        u_pool = proj(O_POOL)
        parts = []
        for g, w in enumerate(POOL_WINDOWS):
            lo = g * POOL_GROUP
            acc = u_pool[:, lo:lo + POOL_GROUP]
            for j in range(1, w):
                acc = acc + st_pool_ref[POOL_BUF - j, :, lo:lo + POOL_GROUP]
            pooled = acc * (1.0 / min(w, start_pos + 1)) - u_pool[:, lo:lo + POOL_GROUP]
            parts.append(_mm_hi(pooled, w_pool_ref[g]))
        mixed = jnp.concatenate(parts, axis=1) * pool_scale_ref[...]
        merged = gate(0) * _mm_hi(mixed, wbp_ref[...])
        up_ref[...] = u_pool

        glu = proj(O_VAL) * _sigmoid(proj(O_GLU))
        acc = glu * conv_w_ref[CONV_BUF:CONV_BUF + 1, :] + conv_b_ref[...]
        for j in range(CONV_BUF):
            acc = acc + st_conv_ref[j] * conv_w_ref[j:j + 1, :]
        act = _silu(_layer_norm(acc, cln_g_ref[...], cln_b_ref[...]))
        merged = merged + gate(1) * _mm_hi(act, wbc_ref[...])
        glu_ref[...] = glu

        u_lru = proj(O_LRU)
        xc = u_lru * lconv_w_ref[LRU_CONV_BUF:LRU_CONV_BUF + 1, :] + lconv_b_ref[...]
        for j in range(LRU_CONV_BUF):
            xc = xc + st_lconv_ref[j] * lconv_w_ref[j:j + 1, :]
        ul_ref[...] = u_lru
        r = _sigmoid(_mm_hi(xc, wa_ref[...]) + ba_ref[...])
        ig = _sigmoid(_mm_hi(xc, wx_ref[...]) + bx_ref[...])
        reset = jnp.full(xc.shape, True) if start_pos == 0 else None
        a, b = _lru_coeffs(xc, r, ig, lam_ref[...], reset)
        h = a * st_h_ref[...] + b
        sh_ref[...] = h
        hg = h * _gelu_tanh(proj(O_GELU))
        merged = merged + gate(2) * _mm_hi(hg, wbl_ref[...])

        m_out = _mm_hi(merged, w_out_ref[...])
        y_ref[...] = _layer_norm(ALPHA * x + m_out, ln_g_ref[...], ln_b_ref[...])


def _const_spec(shape):
    nd = len(shape)
    return pl.BlockSpec(shape, lambda *_: (0,) * nd, pipeline_mode=pl.Buffered(1))


def _mixer_weight_list(p):
    return [p["w_in"], p["b_gate"], p["w_pool"], p["pool_scale"], p["conv_w"], p["conv_b"],
            p["conv_ln_g"], p["conv_ln_b"], p["lru_conv_w"], p["lru_conv_b"], p["lru_wa"],
            p["lru_ba"], p["lru_wx"], p["lru_bx"], p["lru_lambda"], p["w_br_pool"],
            p["w_br_conv"], p["w_br_lru"], p["w_out"], p["ln1_g"], p["ln1_b"]]


def _mixer_seq(x, p, batch, seq, wr_pad=None):
    tm = min(SEQ_TILE, seq)
    nt = seq // tm
    with_router = wr_pad is not None
    weights = _mixer_weight_list(p) + ([wr_pad] if with_router else [])
    in_specs = [pl.BlockSpec((tm, D_MODEL), lambda b, i: (b * nt + i, 0))]
    in_specs += [_const_spec(w.shape) for w in weights]
    out_shape = [
        jax.ShapeDtypeStruct((batch * seq, D_MODEL), F32),
        jax.ShapeDtypeStruct((batch, POOL_BUF, D_POOL), F32),
        jax.ShapeDtypeStruct((batch, CONV_BUF, D_CONV), F32),
        jax.ShapeDtypeStruct((batch, LRU_CONV_BUF, D_LRU), F32),
        jax.ShapeDtypeStruct((batch, 1, D_LRU), F32),
    ]
    out_specs = [
        pl.BlockSpec((tm, D_MODEL), lambda b, i: (b * nt + i, 0)),
        pl.BlockSpec((1, POOL_BUF, D_POOL), lambda b, i: (b, 0, 0)),
        pl.BlockSpec((1, CONV_BUF, D_CONV), lambda b, i: (b, 0, 0)),
        pl.BlockSpec((1, LRU_CONV_BUF, D_LRU), lambda b, i: (b, 0, 0)),
        pl.BlockSpec((1, 1, D_LRU), lambda b, i: (b, 0, 0)),
    ]
    if with_router:
        out_shape += [jax.ShapeDtypeStruct((batch * seq, LANES), jnp.int32),
                      jax.ShapeDtypeStruct((batch * seq, LANES), F32)]
        out_specs += [pl.BlockSpec((tm, LANES), lambda b, i: (b * nt + i, 0)),
                      pl.BlockSpec((tm, LANES), lambda b, i: (b * nt + i, 0))]
    scratch = [
        pltpu.VMEM((tm + POOL_OFF, D_POOL), F32),
        pltpu.VMEM((tm + CONV_OFF + SUBLANES, D_CONV), F32),
        pltpu.VMEM((tm + LCONV_OFF, D_LRU), F32),
        pltpu.VMEM((tm, D_CONV), F32),
        pltpu.VMEM((tm, D_LRU), F32),
        pltpu.VMEM((tm, D_LRU), F32),
        pltpu.VMEM((1, D_LRU), F32),
        pltpu.VMEM((tm, 3 * D_MODEL), F32),
        pltpu.VMEM((tm, D_LRU), F32),
        pltpu.VMEM((tm, D_MODEL), F32),
        pltpu.VMEM((tm, D_MODEL), BF16),
    ]
    y, sp, sc, slc, sh, *routing = pl.pallas_call(
        functools.partial(_mixer_seq_kernel, tm=tm, with_router=with_router),
        grid=(batch, nt),
        in_specs=in_specs,
        out_specs=out_specs,
        out_shape=out_shape,
        scratch_shapes=scratch,
        compiler_params=pltpu.CompilerParams(
            dimension_semantics=("arbitrary", "arbitrary"),
            vmem_limit_bytes=VMEM_LIMIT_BYTES),
        name="mixer_seq",
    )(x, *weights)
    return (y, sp, sc, slc, sh.reshape(batch, D_LRU), *routing)


def _mixer_step(x, st_pool, st_conv, st_lconv, st_h, p, start_pos):
    batch = x.shape[0]
    weights = _mixer_weight_list(p)
    states = [jnp.transpose(st_pool, (1, 0, 2)), jnp.transpose(st_conv, (1, 0, 2)),
              jnp.transpose(st_lconv, (1, 0, 2)), st_h]

    def spec(a):
        if not isinstance(a, tuple):
            return _const_spec(a.shape)
        arr, layer = a
        zeros = (0,) * (arr.ndim - 1)
        return pl.BlockSpec((None,) + arr.shape[1:], lambda k: (layer,) + zeros,
                            pipeline_mode=pl.Buffered(1))

    w_in_all, layer = weights[0]
    in_cols = w_in_all.shape[2]
    n_blocks = in_cols // STEP_COLS
    assert n_blocks * STEP_COLS == in_cols
    in_specs = [spec(a) for a in [x] + states + weights]
    in_specs[1 + len(states)] = pl.BlockSpec((None, D_MODEL, STEP_COLS), lambda k: (layer, 0, k))
    args = [x] + states + [w[0] if isinstance(w, tuple) else w for w in weights]
    out_shape = (
        jax.ShapeDtypeStruct((batch, D_MODEL), F32),
        jax.ShapeDtypeStruct((batch, D_POOL), F32),
        jax.ShapeDtypeStruct((batch, D_CONV), F32),
        jax.ShapeDtypeStruct((batch, D_LRU), F32),
        jax.ShapeDtypeStruct((batch, D_LRU), F32),
    )
    y, u_pool, glu, u_lru, sh = pl.pallas_call(
        functools.partial(_mixer_step_kernel, start_pos=start_pos, n_blocks=n_blocks),
        grid=(n_blocks,),
        in_specs=in_specs,
        out_specs=tuple(pl.BlockSpec(s.shape, lambda k: (0, 0)) for s in out_shape),
        out_shape=out_shape,
        scratch_shapes=[pltpu.VMEM((n_blocks, batch, STEP_COLS), F32)],
        compiler_params=pltpu.CompilerParams(
            dimension_semantics=("arbitrary",), vmem_limit_bytes=VMEM_LIMIT_BYTES),
        name="mixer_step",
    )(*args)

    def push(state, row):
        return jnp.concatenate([state[:, 1:], row[:, None]], axis=1)

    return y, push(st_pool, u_pool), push(st_conv, glu), push(st_lconv, u_lru), sh


def _ffn_kernel(x_ref, wg_ref, wu_ref, wd_ref, g_ref, b_ref, y_ref, *, d_ff, precise):
    x = x_ref[...]
    mm = _mm_hi if precise else _dot
    xb = x if precise else x.astype(BF16)
    acc = jnp.zeros(x.shape, F32)
    for c0 in range(0, d_ff, FFN_CHUNK):
        h = _silu(mm(xb, wg_ref[:, c0:c0 + FFN_CHUNK])) * mm(xb, wu_ref[:, c0:c0 + FFN_CHUNK])
        acc = acc + mm(h if precise else h.astype(BF16), wd_ref[c0:c0 + FFN_CHUNK, :])
    y_ref[...] = _layer_norm(ALPHA * x + acc, g_ref[...], b_ref[...])


def _ffn(x, wg, wu, wd, g, b, precise):
    n = x.shape[0]
    tm = min(SEQ_TILE, n)
    d_ff = wg.shape[1]
    weights = [wg, wu, wd, g, b]
    return pl.pallas_call(
        functools.partial(_ffn_kernel, d_ff=d_ff, precise=precise),
        grid=(n // tm,),
        in_specs=[pl.BlockSpec((tm, D_MODEL), lambda i: (i, 0))] + [_const_spec(w.shape) for w in weights],
        out_specs=pl.BlockSpec((tm, D_MODEL), lambda i: (i, 0)),
        out_shape=jax.ShapeDtypeStruct((n, D_MODEL), F32),
        compiler_params=pltpu.CompilerParams(
            dimension_semantics=("arbitrary",), vmem_limit_bytes=VMEM_LIMIT_BYTES),
        name="ffn_dense",
    )(x, *weights)


def _route(x, wr, precise):
    logits = _mm_hi(x, wr) if precise else _mm(x, wr)
    lane = lax.broadcasted_iota(jnp.int32, logits.shape, 1)
    neg = jnp.float32(-jnp.inf)
    l1 = jnp.where(lane < N_EXPERTS, logits, neg)
    v1 = jnp.max(l1, axis=-1, keepdims=True)
    i1 = jnp.min(jnp.where(l1 == v1, lane, LANES), axis=-1, keepdims=True)
    l2 = jnp.where(lane == i1, neg, l1)
    v2 = jnp.max(l2, axis=-1, keepdims=True)
    i2 = jnp.min(jnp.where(l2 == v2, lane, LANES), axis=-1, keepdims=True)
    e2 = jnp.exp(v2 - v1)
    den = 1.0 + e2
    idx = jnp.where(lane == 0, i1, jnp.where(lane == 1, i2, 0))
    gates = jnp.where(lane == 0, 1.0 / den, jnp.where(lane == 1, e2 / den, 0.0))
    return idx, gates


def _router_kernel(x_ref, wr_ref, idx_ref, gate_ref):
    idx_ref[...], gate_ref[...] = _route(x_ref[...], wr_ref[...], precise=True)


def _router(x, wr_pad):
    n = x.shape[0]
    tm = min(SEQ_TILE, n)
    return pl.pallas_call(
        _router_kernel,
        grid=(n // tm,),
        in_specs=[pl.BlockSpec((tm, D_MODEL), lambda i: (i, 0)), _const_spec(wr_pad.shape)],
        out_specs=(pl.BlockSpec((tm, LANES), lambda i: (i, 0)),
                   pl.BlockSpec((tm, LANES), lambda i: (i, 0))),
        out_shape=(jax.ShapeDtypeStruct((n, LANES), jnp.int32),
                   jax.ShapeDtypeStruct((n, LANES), F32)),
        compiler_params=pltpu.CompilerParams(dimension_semantics=("arbitrary",)),
        name="moe_router",
    )(x, wr_pad)


def _sc_row_gather(src, idx):
    n_idx = idx.shape[0]
    assert n_idx % SC_WINDOW == 0
    n_win = n_idx // SC_WINDOW
    n_workers = SC_CORES * SC_SUBCORES
    n_sub = SC_WINDOW // SC_SUB_ROWS
    mesh = plsc.VectorSubcoreMesh(core_axis_name="c", subcore_axis_name="s",
                                  num_cores=SC_CORES, num_subcores=SC_SUBCORES)

    def body(src_hbm, idx_hbm, out_hbm, idx_v, buf, g0, g1, w0, w1):
        gsem = (g0, g1)
        wsem = (w0, w1)
        worker = lax.axis_index("c") * SC_SUBCORES + lax.axis_index("s")

        @pl.loop(0, pl.cdiv(n_win, n_workers))
        def _(t):
            win = worker + t * n_workers

            @pl.when(win < n_win)
            def _():
                base = win * SC_WINDOW
                pltpu.sync_copy(idx_hbm.at[:, pl.ds(base, SC_WINDOW)], idx_v)

                def fetch(q):
                    return pltpu.async_copy(
                        src_hbm.at[idx_v.at[0, pl.ds(q * SC_SUB_ROWS, SC_SUB_ROWS)]],
                        buf.at[q % 2], gsem[q % 2])

                def put(q):
                    return pltpu.async_copy(
                        buf.at[q % 2], out_hbm.at[pl.ds(base + q * SC_SUB_ROWS, SC_SUB_ROWS)],
                        wsem[q % 2])

                fetches = {0: fetch(0)}
                puts = {}
                for q in range(n_sub):
                    if q + 1 < n_sub:
                        if q >= 1:
                            puts[q - 1].wait()
                        fetches[q + 1] = fetch(q + 1)
                    fetches[q].wait()
                    puts[q] = put(q)
                for q in range(max(0, n_sub - 2), n_sub):
                    puts[q].wait()

    return pl.kernel(
        body,
        out_type=jax.ShapeDtypeStruct((n_idx, D_MODEL), src.dtype),
        mesh=mesh,
        scratch_types=[pltpu.VMEM((1, SC_WINDOW), jnp.int32),
                       pltpu.VMEM((2, SC_SUB_ROWS, D_MODEL), src.dtype),
                       pltpu.SemaphoreType.DMA, pltpu.SemaphoreType.DMA,
                       pltpu.SemaphoreType.DMA, pltpu.SemaphoreType.DMA],
        cost_estimate=pl.CostEstimate(
            flops=0, transcendentals=0,
            bytes_accessed=n_idx * (2 * D_MODEL * src.dtype.itemsize + idx.dtype.itemsize)),
        name="sc_row_gather",
    )(src, idx.reshape(1, n_idx))


def _expert_kernel(te_ref, nu_ref, xs_ref, wg_ref, wu_ref, wd_ref, ys_ref, xs_bf, acc, *, tf, nf):
    del te_ref
    i = pl.program_id(0)
    j = pl.program_id(1)
    used = i < nu_ref[0]

    @pl.when(jnp.logical_and(used, j == 0))
    def _():
        xs_bf[...] = xs_ref[...].astype(BF16)

    @pl.when(used)
    def _():
        xb = xs_bf[...]
        part = None
        for c0 in range(0, tf, FFN_CHUNK):
            h = _silu(_dot(xb, wg_ref[:, c0:c0 + FFN_CHUNK])) * _dot(xb, wu_ref[:, c0:c0 + FFN_CHUNK])
            d = _dot(h.astype(BF16), wd_ref[c0:c0 + FFN_CHUNK, :])
            part = d if part is None else part + d

        if nf == 1:
            ys_ref[...] = part
        else:
            @pl.when(j == 0)
            def _():
                acc[...] = part

            @pl.when(jnp.logical_and(j > 0, j < nf - 1))
            def _():
                acc[...] += part

            @pl.when(j == nf - 1)
            def _():
                ys_ref[...] = acc[...] + part

    @pl.when(jnp.logical_and(jnp.logical_not(used), j == nf - 1))
    def _():
        ys_ref[...] = jnp.zeros(ys_ref.shape, F32)


def _experts(xs, tile_expert, n_used, wg, wu, wd, tm):
    n_tiles = tile_expert.shape[0]
    d_e = wg.shape[2]
    tf = EXPERT_F_TILE
    nf = d_e // tf
    assert nf * tf == d_e and tf % FFN_CHUNK == 0

    def jeff(i, j, nu):
        return jnp.where(i < nu[0], j, nf - 1)

    grid_spec = pltpu.PrefetchScalarGridSpec(
        num_scalar_prefetch=2,
        grid=(n_tiles, nf),
        in_specs=[
            pl.BlockSpec((tm, D_MODEL), lambda i, j, te, nu: (jnp.minimum(i, nu[0] - 1), 0)),
            pl.BlockSpec((None, D_MODEL, tf), lambda i, j, te, nu: (te[i], 0, jeff(i, j, nu))),
            pl.BlockSpec((None, D_MODEL, tf), lambda i, j, te, nu: (te[i], 0, jeff(i, j, nu))),
            pl.BlockSpec((None, tf, D_MODEL), lambda i, j, te, nu: (te[i], jeff(i, j, nu), 0)),
        ],
        out_specs=pl.BlockSpec((tm, D_MODEL), lambda i, j, te, nu: (i, 0)),
        scratch_shapes=[
            pltpu.VMEM((tm, D_MODEL), BF16),
            pltpu.VMEM((tm if nf > 1 else SUBLANES, D_MODEL), F32),
        ],
    )
    return pl.pallas_call(
        functools.partial(_expert_kernel, tf=tf, nf=nf),
        grid_spec=grid_spec,
        out_shape=jax.ShapeDtypeStruct((n_tiles * tm, D_MODEL), F32),
        compiler_params=pltpu.CompilerParams(
            dimension_semantics=("arbitrary", "arbitrary"), vmem_limit_bytes=VMEM_LIMIT_BYTES),
        name="moe_experts",
    )(tile_expert, n_used, xs, wg, wu, wd)


def _combine_kernel(x_ref, gate_ref, y0_ref, y1_ref, g_ref, b_ref, y_ref):
    gates = gate_ref[...]
    f = gates[:, 0:1] * y0_ref[...] + gates[:, 1:2] * y1_ref[...]
    y_ref[...] = _layer_norm(ALPHA * x_ref[...] + f, g_ref[...], b_ref[...])


def _combine(x, gates, y01, g, b):
    n = x.shape[0]
    tm = min(SEQ_TILE, n)
    nt = n // tm
    return pl.pallas_call(
        _combine_kernel,
        grid=(nt,),
        in_specs=[
            pl.BlockSpec((tm, D_MODEL), lambda i: (i, 0)),
            pl.BlockSpec((tm, LANES), lambda i: (i, 0)),
            pl.BlockSpec((tm, D_MODEL), lambda i: (i, 0)),
            pl.BlockSpec((tm, D_MODEL), lambda i: (i + nt, 0)),
            _const_spec(g.shape),
            _const_spec(b.shape),
        ],
        out_specs=pl.BlockSpec((tm, D_MODEL), lambda i: (i, 0)),
        out_shape=jax.ShapeDtypeStruct((n, D_MODEL), F32),
        compiler_params=pltpu.CompilerParams(
            dimension_semantics=("arbitrary",), vmem_limit_bytes=VMEM_LIMIT_BYTES),
        name="moe_combine",
    )(x, gates, y01, y01, g, b)


def _moe(x, wr_pad, wg, wu, wd, g, b, tm, routing=None, after=None, wait_for=None):
    n = x.shape[0]
    if wait_for is not None:
        x, _ = lax.optimization_barrier((x, wait_for))
    idx, gates = routing if routing is not None else _router(x, wr_pad)
    flat_e = idx[:, :TOP_K].reshape(-1)
    onehot = (flat_e[:, None] == jnp.arange(N_EXPERTS, dtype=jnp.int32)[None, :]).astype(jnp.int32)
    csum = jnp.cumsum(onehot, axis=0)
    rank = jnp.sum((csum - onehot) * onehot, axis=1)
    counts = csum[-1]
    padded = ((counts + tm - 1) // tm) * tm
    ends = jnp.cumsum(padded)
    slot = (ends - padded)[flat_e] + rank
    n_pairs = TOP_K * n
    n_tiles = n_pairs // tm + N_EXPERTS
    n_used = (ends[-1] // tm).astype(jnp.int32).reshape(1)
    tile_start = jnp.arange(n_tiles, dtype=jnp.int32) * tm
    tile_expert = jnp.sum((tile_start[:, None] >= ends[None, :]).astype(jnp.int32), axis=1)
    last_e = jnp.sum((ends[-1] - 1 >= ends).astype(jnp.int32))
    tile_expert = jnp.minimum(tile_expert, last_e).astype(jnp.int32)
    pair_sorted = jnp.sort(flat_e * n_pairs + jnp.arange(n_pairs, dtype=jnp.int32)) % n_pairs
    slot_ids = jnp.arange(n_tiles * tm, dtype=jnp.int32)
    slot_e = jnp.repeat(tile_expert, tm)
    local = slot_ids - (ends - padded)[slot_e]
    q = (jnp.cumsum(counts) - counts)[slot_e] + local
    valid = jnp.logical_and(local >= 0, local < counts[slot_e])
    tok_of_slot = jnp.where(valid, pair_sorted[jnp.clip(q, 0, n_pairs - 1)] // TOP_K, slot_ids % n)
    if after is not None:
        tok_of_slot, after = lax.optimization_barrier((tok_of_slot, after))
    xs = _sc_row_gather(x, tok_of_slot)
    ys = _experts(xs, tile_expert, n_used, wg, wu, wd, tm)
    y01 = _sc_row_gather(ys, jnp.transpose(slot.reshape(n, TOP_K)).reshape(-1))
    return _combine(x, gates, y01, g, b), after, xs


def _block_diag(w):
    n, c, _ = w.shape
    eye = jnp.eye(n, dtype=w.dtype)
    return (eye[:, None, :, None] * w[:, :, None, :]).reshape(n * c, n * c)


def _row(v):
    return v.reshape(1, -1)


def kernel(x_prompt, x_sample, state_pool, state_conv, state_lru_conv, state_lru_h, w_in, b_gate, w_pool, pool_scale, conv_w, conv_b, conv_ln_g, conv_ln_b, lru_conv_w, lru_conv_b, lru_wa, lru_ba, lru_wx, lru_bx, lru_lambda, w_br_pool, w_br_conv, w_br_lru, w_out, ln1_g, ln1_b, ln2_g, ln2_b, ffn_w_gate, ffn_w_up, ffn_w_down, moe_router, moe_w_gate, moe_w_up, moe_w_down):
    batch, seq, _ = x_prompt.shape
    dec_batch = x_sample.shape[0]

    def layer_params(l, mat):
        return {
            "w_in": w_in[l].astype(mat), "b_gate": _row(b_gate[l]),
            "w_pool": w_pool[l].astype(mat), "pool_scale": _row(pool_scale[l]),
            "conv_w": conv_w[l], "conv_b": _row(conv_b[l]),
            "conv_ln_g": _row(conv_ln_g[l]), "conv_ln_b": _row(conv_ln_b[l]),
            "lru_conv_w": lru_conv_w[l], "lru_conv_b": _row(lru_conv_b[l]),
            "lru_wa": _block_diag(lru_wa[l]).astype(mat), "lru_ba": _row(lru_ba[l]),
            "lru_wx": _block_diag(lru_wx[l]).astype(mat), "lru_bx": _row(lru_bx[l]),
            "lru_lambda": _row(lru_lambda[l]),
            "w_br_pool": w_br_pool[l].astype(mat), "w_br_conv": w_br_conv[l].astype(mat),
            "w_br_lru": w_br_lru[l].astype(mat), "w_out": w_out[l].astype(mat),
            "ln1_g": _row(ln1_g[l]), "ln1_b": _row(ln1_b[l]),
        }

    layers = [layer_params(l, BF16) for l in range(DEPTH)]
    layers_f32 = [layer_params(l, F32) for l in range(DEPTH)]
    for l in range(DEPTH):
        layers_f32[l].update({"w_in": (w_in, l), "w_br_pool": (w_br_pool, l), "w_br_conv": (w_br_conv, l),
                              "w_br_lru": (w_br_lru, l), "w_out": (w_out, l)})
    ffn_w = [(ffn_w_gate[m].astype(BF16), ffn_w_up[m].astype(BF16), ffn_w_down[m].astype(BF16))
             for m in range(ffn_w_gate.shape[0])]
    ffn_w_f32 = [(ffn_w_gate[m], ffn_w_up[m], ffn_w_down[m]) for m in range(ffn_w_gate.shape[0])]
    moe_w = [(jnp.pad(moe_router[m], ((0, 0), (0, LANES - N_EXPERTS))),
              moe_w_gate[m].astype(BF16), moe_w_up[m].astype(BF16), moe_w_down[m].astype(BF16))
             for m in range(moe_router.shape[0])]

    x_prompt, moe_w = lax.optimization_barrier((x_prompt, moe_w))

    def channel_mixer(l, x, moe_tile, routing=None, precise=False, after=None, wait_for=None):
        g, b = _row(ln2_g[l]), _row(ln2_b[l])
        if l % 2 == 0:
            return _ffn(x, *(ffn_w_f32 if precise else ffn_w)[l // 2], g, b, precise), after, wait_for
        return _moe(x, *moe_w[l // 2], g, b, moe_tile, routing, after, wait_for)

    x = x_prompt.reshape(batch * seq, D_MODEL)
    xs_in = x_sample.reshape(dec_batch, D_MODEL)
    gathered = None
    p_states = []
    for l in range(DEPTH):
        wr_pad = moe_w[l // 2][0].astype(BF16) if l % 2 == 1 else None
        x, sp, sc, slc, sh, *routing = _mixer_seq(x, layers[l], batch, seq, wr_pad)
        p_states.append((sp, sc, slc, sh))
        x, xs_in, gathered = channel_mixer(l, x, MOE_TILE_SEQ, tuple(routing) or None, after=xs_in,
                                           wait_for=gathered)
    y_prompt = x.reshape(batch, seq, D_MODEL)

    x = xs_in
    s_states = []
    for l in range(DEPTH):
        x, sp, sc, slc, sh = _mixer_step(x, state_pool[l], state_conv[l], state_lru_conv[l],
                                         state_lru_h[l], layers_f32[l], PAST_LEN)
        s_states.append((sp, sc, slc, sh))
        x, _, _ = channel_mixer(l, x, MOE_TILE_STEP, precise=True, wait_for=gathered)
    y_sample = x.reshape(dec_batch, 1, D_MODEL)

    def stack(states, k):
        return jnp.stack([s[k] for s in states])

    return (y_prompt, y_sample,
            stack(p_states, 0), stack(p_states, 1), stack(p_states, 2), stack(p_states, 3),
            stack(s_states, 0), stack(s_states, 1), stack(s_states, 2), stack(s_states, 3))
```

```python
import functools

import jax
import jax.numpy as jnp
from jax import lax
from jax.experimental import pallas as pl
from jax.experimental.pallas import tpu as pltpu
from jax.experimental.pallas import tpu_sc as plsc

D_MODEL = 1024
DEPTH = 2
PAST_LEN = 16384
D_POOL = 512
POOL_GROUP = 128
POOL_WINDOWS = (2, 4, 8, 16)
POOL_BUF = 15
D_CONV = 512
CONV_WIDTH = 31
CONV_BUF = 30
D_LRU = 512
LRU_CONV_WIDTH = 4
LRU_CONV_BUF = 3
LRU_C = 8.0
N_EXPERTS = 8
TOP_K = 2
ALPHA = (2.0 * DEPTH) ** 0.25
LN_EPS = 1e-5

O_POOL = 0
O_VAL = D_POOL
O_GLU = O_VAL + D_CONV
O_LRU = O_GLU + D_CONV
O_GELU = O_LRU + D_LRU
O_GATE = O_GELU + D_LRU

SUBLANES = 8
LANES = 128
VMEM_LIMIT_BYTES = 56 * 1024 * 1024
SC_CORES = 2
SC_SUBCORES = 16
SC_WINDOW = 128
SC_SUB_ROWS = 32

SEQ_TILE = 512
SCAN_UNROLL = 2
CONV_ROWS = 128
POOL_OFF = 16
CONV_OFF = 32
LCONV_OFF = 8
FFN_CHUNK = 256
PIECE = 256
STEP_COLS = 512
EXPERT_F_TILE = 3584
MOE_TILE_SEQ = 512

BF16 = jnp.bfloat16
F32 = jnp.float32


def _sigmoid(x):
    return 0.5 * jnp.tanh(0.5 * x) + 0.5


def _silu(x):
    return x * _sigmoid(x)


def _gelu_tanh(x):
    return x * (0.5 * (1.0 + jnp.tanh(0.7978845608028654 * (x + 0.044715 * (x * x * x)))))


def _softplus(z):
    return jnp.maximum(z, 0.0) + jnp.log1p(jnp.exp(-jnp.abs(z)))


def _layer_norm(x, g, b):
    mu = jnp.mean(x, axis=-1, keepdims=True)
    xc = x - mu
    var = jnp.mean(xc * xc, axis=-1, keepdims=True)
    return xc * lax.rsqrt(var + LN_EPS) * g + b


def _dot(a, b):
    return jnp.dot(a, b, preferred_element_type=F32)


def _mm(a, w):
    return _dot(a.astype(BF16), w)


def _mm_hi(a, w):
    return jnp.dot(a, w, preferred_element_type=F32, precision=lax.Precision.HIGHEST)


def _lru_coeffs(xc, r, ig, lam, reset):
    log_a = (-LRU_C * r) * _softplus(-lam)
    a = jnp.exp(log_a)
    mult = jnp.sqrt(-jnp.tanh(log_a) * (a * a + 1.0))
    if reset is not None:
        a = jnp.where(reset, 0.0, a)
        mult = jnp.where(reset, 1.0, mult)
    return a, mult * ig * xc


def _mixer_seq_kernel(x_ref, w_in_ref, b_gate_ref, w_pool_ref, pool_scale_ref,
                      conv_w_ref, conv_b_ref, cln_g_ref, cln_b_ref,
                      lconv_w_ref, lconv_b_ref, wa_ref, ba_ref, wx_ref, bx_ref, lam_ref,
                      wbp_ref, wbc_ref, wbl_ref, w_out_ref, ln_g_ref, ln_b_ref, *rest, tm, with_router):
    if with_router:
        wr_ref, rest = rest[0], rest[1:]
        idx_ref, gate_ref, rest = rest[5], rest[6], rest[:5] + rest[7:]
    (y_ref, sp_ref, sc_ref, slc_ref, sh_ref,
     pool_ext, conv_ext, lconv_ext, act_buf, a_buf, b_buf, h_carry, gate_buf, gelu_buf,
     merged_buf, xb_buf) = rest
    i = pl.program_id(1)

    @pl.when(i == 0)
    def _():
        pool_ext[0:POOL_OFF, :] = jnp.zeros((POOL_OFF, D_POOL), F32)
        conv_ext[0:CONV_OFF, :] = jnp.zeros((CONV_OFF, D_CONV), F32)
        conv_ext[tm + CONV_OFF:tm + CONV_OFF + SUBLANES, :] = jnp.zeros((SUBLANES, D_CONV), F32)
        lconv_ext[0:LCONV_OFF, :] = jnp.zeros((LCONV_OFF, D_LRU), F32)
        h_carry[...] = jnp.zeros((1, D_LRU), F32)

    xb_buf[...] = x_ref[...].astype(BF16)

    def proj(lo, width):
        return _dot(xb_buf[...], w_in_ref[:, lo:lo + width])

    pos = i * tm + lax.broadcasted_iota(jnp.int32, (tm, 1), 0)

    glu = proj(O_VAL, D_CONV) * _sigmoid(proj(O_GLU, D_CONV))
    conv_ext[CONV_OFF:CONV_OFF + tm, :] = glu

    def pool_piece():
        u_pool = proj(O_POOL, D_POOL)
        pool_ext[POOL_OFF:POOL_OFF + tm, :] = u_pool
        parts = []
        for g, w in enumerate(POOL_WINDOWS):
            lo = g * POOL_GROUP
            acc = u_pool[:, lo:lo + POOL_GROUP]
            for k in range(1, w):
                acc = acc + pool_ext[POOL_OFF - k:POOL_OFF - k + tm, lo:lo + POOL_GROUP]
            inv_cnt = 1.0 / jnp.minimum(w, pos + 1).astype(F32)
            pooled = acc * inv_cnt - u_pool[:, lo:lo + POOL_GROUP]
            parts.append(_dot(pooled.astype(BF16), w_pool_ref[g]))
        mixed = jnp.concatenate(parts, axis=1) * pool_scale_ref[...]
        merged_buf[...] = gate_buf[:, 0:D_MODEL] * _dot(mixed.astype(BF16), wbp_ref[...])
        sp_ref[0] = pool_ext[tm + POOL_OFF - POOL_BUF:tm + POOL_OFF, :]
        pool_ext[0:POOL_OFF, :] = pool_ext[tm:tm + POOL_OFF, :]

    def lru_piece(lo):
        lconv_ext[LCONV_OFF:LCONV_OFF + tm, lo:lo + PIECE] = proj(O_LRU + lo, PIECE)

    def gelu_piece(lo):
        gelu_buf[:, lo:lo + PIECE] = _gelu_tanh(proj(O_GELU + lo, PIECE))

    def gate_piece(lo):
        gate_buf[:, lo:lo + PIECE] = _sigmoid(
            proj(O_GATE + lo, PIECE) + b_gate_ref[:, lo:lo + PIECE])

    pieces = ([functools.partial(gate_piece, lo) for lo in range(0, D_MODEL, PIECE)]
              + [pool_piece]
              + [functools.partial(lru_piece, lo) for lo in range(0, D_LRU, PIECE)]
              + [functools.partial(gate_piece, lo) for lo in range(D_MODEL, 2 * D_MODEL, PIECE)]
              + [functools.partial(gelu_piece, lo) for lo in range(0, D_LRU, PIECE)]
              + [functools.partial(gate_piece, lo) for lo in range(2 * D_MODEL, 3 * D_MODEL, PIECE)])
    n_conv_chunks = tm // CONV_ROWS
    base = CONV_OFF - CONV_BUF
    rows = CONV_ROWS + SUBLANES
    n_units = n_conv_chunks * (D_CONV // LANES)
    unit = 0
    for c in range(n_conv_chunks):
        c0 = c * CONV_ROWS
        for l0 in range(0, D_CONV, LANES):
            acc = jnp.zeros((CONV_ROWS, LANES), F32) + conv_b_ref[:, l0:l0 + LANES]
            for r in range(SUBLANES):
                z = None
                for k in range(CONV_WIDTH):
                    if (base + k) % SUBLANES != r:
                        continue
                    q8 = base + k - r
                    t = (conv_ext[c0 + q8:c0 + q8 + rows, l0:l0 + LANES]
                         * conv_w_ref[k:k + 1, l0:l0 + LANES])
                    z = t if z is None else z + t
                acc = acc + z[r:r + CONV_ROWS, :]
            act_buf[c0:c0 + CONV_ROWS, l0:l0 + LANES] = acc
            for piece in pieces[unit * len(pieces) // n_units:(unit + 1) * len(pieces) // n_units]:
                piece()
            unit += 1
        act_buf[c0:c0 + CONV_ROWS, :] = _silu(_layer_norm(
            act_buf[c0:c0 + CONV_ROWS, :], cln_g_ref[...], cln_b_ref[...]))
    merged_buf[...] += gate_buf[:, D_MODEL:2 * D_MODEL] * _dot(
        act_buf[...].astype(BF16), wbc_ref[...])
    sc_ref[0] = conv_ext[tm + CONV_OFF - CONV_BUF:tm + CONV_OFF, :]
    conv_ext[0:CONV_OFF, :] = conv_ext[tm:tm + CONV_OFF, :]

    base = LCONV_OFF - LRU_CONV_BUF
    xc = jnp.zeros((tm, D_LRU), F32) + lconv_b_ref[...]
    for k in range(LRU_CONV_WIDTH):
        xc = xc + lconv_ext[base + k:base + k + tm, :] * lconv_w_ref[k:k + 1, :]
    slc_ref[0] = lconv_ext[tm + LCONV_OFF - LRU_CONV_BUF:tm + LCONV_OFF, :]
    lconv_ext[0:LCONV_OFF, :] = lconv_ext[tm:tm + LCONV_OFF, :]
    xcb = xc.astype(BF16)
    r = _sigmoid(_dot(xcb, wa_ref[...]) + ba_ref[...])
    ig = _sigmoid(_dot(xcb, wx_ref[...]) + bx_ref[...])
    a, b = _lru_coeffs(xc, r, ig, lam_ref[...], pos == 0)
    a_buf[...] = a
    b_buf[...] = b

    row8 = lax.broadcasted_iota(jnp.int32, (SUBLANES, D_LRU), 0)

    def scan_block(j, h):
        r0 = pl.multiple_of(j * SUBLANES, SUBLANES)
        a8 = a_buf[pl.ds(r0, SUBLANES), :]
        b8 = b_buf[pl.ds(r0, SUBLANES), :]
        for k in (1, 2, 4):
            a_s = pltpu.roll(a8, k, 0)
            b_s = pltpu.roll(b8, k, 0)
            m = row8 >= k
            b8 = jnp.where(m, a8 * b_s + b8, b8)
            a8 = jnp.where(m, a8 * a_s, a8)
        h8 = a8 * h + b8
        a_buf[pl.ds(r0, SUBLANES), :] = h8
        return h8[SUBLANES - 1:SUBLANES, :]

    h_last = lax.fori_loop(0, tm // SUBLANES, scan_block, h_carry[...], unroll=SCAN_UNROLL)
    h_carry[...] = h_last
    sh_ref[0] = h_last
    hg = a_buf[...] * gelu_buf[...]
    merged = merged_buf[...] + gate_buf[:, 2 * D_MODEL:3 * D_MODEL] * _dot(
        hg.astype(BF16), wbl_ref[...])

    m_out = _dot(merged.astype(BF16), w_out_ref[...])
    y = _layer_norm(ALPHA * x_ref[...] + m_out, ln_g_ref[...], ln_b_ref[...])
    y_ref[...] = y
    if with_router:
        idx_ref[...], gate_ref[...] = _route(y, wr_ref[...], precise=False)


def _mixer_step_kernel(x_ref, st_pool_ref, st_conv_ref, st_lconv_ref, st_h_ref,
                       w_in_ref, b_gate_ref, w_pool_ref, pool_scale_ref,
                       conv_w_ref, conv_b_ref, cln_g_ref, cln_b_ref,
                       lconv_w_ref, lconv_b_ref, wa_ref, ba_ref, wx_ref, bx_ref, lam_ref,
                       wbp_ref, wbc_ref, wbl_ref, w_out_ref, ln_g_ref, ln_b_ref,
                       y_ref, up_ref, glu_ref, ul_ref, sh_ref, u_buf, *, start_pos, n_blocks):
    k = pl.program_id(0)
    u_buf[k] = _mm_hi(x_ref[...], w_in_ref[...])

    @pl.when(k == n_blocks - 1)
    def _():
        x = x_ref[...]

        def proj(lo):
            return u_buf[lo // STEP_COLS]

        def gate(n):
            lo = n * D_MODEL
            u = jnp.concatenate([proj(O_GATE + lo), proj(O_GATE + lo + STEP_COLS)], axis=1)
            return _sigmoid(u + b_gate_ref[:, lo:lo + D_MODEL])

        u_pool = proj(O_POOL)
        parts = []
        for g, w in enumerate(POOL_WINDOWS):
            lo = g * POOL_GROUP
            acc = u_pool[:, lo:lo + POOL_GROUP]
            for j in range(1, w):
                acc = acc + st_pool_ref[POOL_BUF - j, :, lo:lo + POOL_GROUP]
            pooled = acc * (1.0 / min(w, start_pos + 1)) - u_pool[:, lo:lo + POOL_GROUP]
            parts.append(_mm_hi(pooled, w_pool_ref[g]))
        mixed = jnp.concatenate(parts, axis=1) * pool_scale_ref[...]
        merged = gate(0) * _mm_hi(mixed, wbp_ref[...])
        up_ref[...] = u_pool

        glu = proj(O_VAL) * _sigmoid(proj(O_GLU))
        acc = glu * conv_w_ref[CONV_BUF:CONV_BUF + 1, :] + conv_b_ref[...]
        for j in range(CONV_BUF):
            acc = acc + st_conv_ref[j] * conv_w_ref[j:j + 1, :]
        act = _silu(_layer_norm(acc, cln_g_ref[...], cln_b_ref[...]))
        merged = merged + gate(1) * _mm_hi(act, wbc_ref[...])
        glu_ref[...] = glu

        u_lru = proj(O_LRU)
        xc = u_lru * lconv_w_ref[LRU_CONV_BUF:LRU_CONV_BUF + 1, :] + lconv_b_ref[...]
        for j in range(LRU_CONV_BUF):
            xc = xc + st_lconv_ref[j] * lconv_w_ref[j:j + 1, :]
        ul_ref[...] = u_lru
        r = _sigmoid(_mm_hi(xc, wa_ref[...]) + ba_ref[...])
        ig = _sigmoid(_mm_hi(xc, wx_ref[...]) + bx_ref[...])
        reset = jnp.full(xc.shape, True) if start_pos == 0 else None
        a, b = _lru_coeffs(xc, r, ig, lam_ref[...], reset)
        h = a * st_h_ref[...] + b
        sh_ref[...] = h
        hg = h * _gelu_tanh(proj(O_GELU))
        merged = merged + gate(2) * _mm_hi(hg, wbl_ref[...])

        m_out = _mm_hi(merged, w_out_ref[...])
        y_ref[...] = _layer_norm(ALPHA * x + m_out, ln_g_ref[...], ln_b_ref[...])


def _const_spec(shape):
    nd = len(shape)
    return pl.BlockSpec(shape, lambda *_: (0,) * nd, pipeline_mode=pl.Buffered(1))


def _mixer_weight_list(p):
    return [p["w_in"], p["b_gate"], p["w_pool"], p["pool_scale"], p["conv_w"], p["conv_b"],
            p["conv_ln_g"], p["conv_ln_b"], p["lru_conv_w"], p["lru_conv_b"], p["lru_wa"],
            p["lru_ba"], p["lru_wx"], p["lru_bx"], p["lru_lambda"], p["w_br_pool"],
            p["w_br_conv"], p["w_br_lru"], p["w_out"], p["ln1_g"], p["ln1_b"]]


def _mixer_seq(x, p, batch, seq, wr_pad=None):
    tm = min(SEQ_TILE, seq)
    nt = seq // tm
    with_router = wr_pad is not None
    weights = _mixer_weight_list(p) + ([wr_pad] if with_router else [])
    in_specs = [pl.BlockSpec((tm, D_MODEL), lambda b, i: (b * nt + i, 0))]
    in_specs += [_const_spec(w.shape) for w in weights]
    out_shape = [
        jax.ShapeDtypeStruct((batch * seq, D_MODEL), F32),
        jax.ShapeDtypeStruct((batch, POOL_BUF, D_POOL), F32),
        jax.ShapeDtypeStruct((batch, CONV_BUF, D_CONV), F32),
        jax.ShapeDtypeStruct((batch, LRU_CONV_BUF, D_LRU), F32),
        jax.ShapeDtypeStruct((batch, 1, D_LRU), F32),
    ]
    out_specs = [
        pl.BlockSpec((tm, D_MODEL), lambda b, i: (b * nt + i, 0)),
        pl.BlockSpec((1, POOL_BUF, D_POOL), lambda b, i: (b, 0, 0)),
        pl.BlockSpec((1, CONV_BUF, D_CONV), lambda b, i: (b, 0, 0)),
        pl.BlockSpec((1, LRU_CONV_BUF, D_LRU), lambda b, i: (b, 0, 0)),
        pl.BlockSpec((1, 1, D_LRU), lambda b, i: (b, 0, 0)),
    ]
    if with_router:
        out_shape += [jax.ShapeDtypeStruct((batch * seq, LANES), jnp.int32),
                      jax.ShapeDtypeStruct((batch * seq, LANES), F32)]
        out_specs += [pl.BlockSpec((tm, LANES), lambda b, i: (b * nt + i, 0)),
                      pl.BlockSpec((tm, LANES), lambda b, i: (b * nt + i, 0))]
    scratch = [
        pltpu.VMEM((tm + POOL_OFF, D_POOL), F32),
        pltpu.VMEM((tm + CONV_OFF + SUBLANES, D_CONV), F32),
        pltpu.VMEM((tm + LCONV_OFF, D_LRU), F32),
        pltpu.VMEM((tm, D_CONV), F32),
        pltpu.VMEM((tm, D_LRU), F32),
        pltpu.VMEM((tm, D_LRU), F32),
        pltpu.VMEM((1, D_LRU), F32),
        pltpu.VMEM((tm, 3 * D_MODEL), F32),
        pltpu.VMEM((tm, D_LRU), F32),
        pltpu.VMEM((tm, D_MODEL), F32),
        pltpu.VMEM((tm, D_MODEL), BF16),
    ]
    y, sp, sc, slc, sh, *routing = pl.pallas_call(
        functools.partial(_mixer_seq_kernel, tm=tm, with_router=with_router),
        grid=(batch, nt),
        in_specs=in_specs,
        out_specs=out_specs,
        out_shape=out_shape,
        scratch_shapes=scratch,
        compiler_params=pltpu.CompilerParams(
            dimension_semantics=("arbitrary", "arbitrary"),
            vmem_limit_bytes=VMEM_LIMIT_BYTES),
        name="mixer_seq",
    )(x, *weights)
    return (y, sp, sc, slc, sh.reshape(batch, D_LRU), *routing)


def _mixer_step(x, st_pool, st_conv, st_lconv, st_h, p, start_pos):
    batch = x.shape[0]
    weights = _mixer_weight_list(p)
    states = [jnp.transpose(st_pool, (1, 0, 2)), jnp.transpose(st_conv, (1, 0, 2)),
              jnp.transpose(st_lconv, (1, 0, 2)), st_h]

    def spec(a):
        if not isinstance(a, tuple):
            return _const_spec(a.shape)
        arr, layer = a
        zeros = (0,) * (arr.ndim - 1)
        return pl.BlockSpec((None,) + arr.shape[1:], lambda k: (layer,) + zeros,
                            pipeline_mode=pl.Buffered(1))

    w_in_all, layer = weights[0]
    in_cols = w_in_all.shape[2]
    n_blocks = in_cols // STEP_COLS
    assert n_blocks * STEP_COLS == in_cols
    in_specs = [spec(a) for a in [x] + states + weights]
    in_specs[1 + len(states)] = pl.BlockSpec((None, D_MODEL, STEP_COLS), lambda k: (layer, 0, k))
    args = [x] + states + [w[0] if isinstance(w, tuple) else w for w in weights]
    out_shape = (
        jax.ShapeDtypeStruct((batch, D_MODEL), F32),
        jax.ShapeDtypeStruct((batch, D_POOL), F32),
        jax.ShapeDtypeStruct((batch, D_CONV), F32),
        jax.ShapeDtypeStruct((batch, D_LRU), F32),
        jax.ShapeDtypeStruct((batch, D_LRU), F32),
    )
    y, u_pool, glu, u_lru, sh = pl.pallas_call(
        functools.partial(_mixer_step_kernel, start_pos=start_pos, n_blocks=n_blocks),
        grid=(n_blocks,),
        in_specs=in_specs,
        out_specs=tuple(pl.BlockSpec(s.shape, lambda k: (0, 0)) for s in out_shape),
        out_shape=out_shape,
        scratch_shapes=[pltpu.VMEM((n_blocks, batch, STEP_COLS), F32)],
        compiler_params=pltpu.CompilerParams(
            dimension_semantics=("arbitrary",), vmem_limit_bytes=VMEM_LIMIT_BYTES),
        name="mixer_step",
    )(*args)

    def push(state, row):
        return jnp.concatenate([state[:, 1:], row[:, None]], axis=1)

    return y, push(st_pool, u_pool), push(st_conv, glu), push(st_lconv, u_lru), sh


def _ffn_kernel(x_ref, wg_ref, wu_ref, wd_ref, g_ref, b_ref, y_ref, *, d_ff, precise):
    x = x_ref[...]
    mm = _mm_hi if precise else _dot
    xb = x if precise else x.astype(BF16)
    acc = jnp.zeros(x.shape, F32)
    for c0 in range(0, d_ff, FFN_CHUNK):
        h = _silu(mm(xb, wg_ref[:, c0:c0 + FFN_CHUNK])) * mm(xb, wu_ref[:, c0:c0 + FFN_CHUNK])
        acc = acc + mm(h if precise else h.astype(BF16), wd_ref[c0:c0 + FFN_CHUNK, :])
    y_ref[...] = _layer_norm(ALPHA * x + acc, g_ref[...], b_ref[...])


def _ffn(x, wg, wu, wd, g, b, precise):
    n = x.shape[0]
    tm = min(SEQ_TILE, n)
    d_ff = wg.shape[1]
    weights = [wg, wu, wd, g, b]
    return pl.pallas_call(
        functools.partial(_ffn_kernel, d_ff=d_ff, precise=precise),
        grid=(n // tm,),
        in_specs=[pl.BlockSpec((tm, D_MODEL), lambda i: (i, 0))] + [_const_spec(w.shape) for w in weights],
        out_specs=pl.BlockSpec((tm, D_MODEL), lambda i: (i, 0)),
        out_shape=jax.ShapeDtypeStruct((n, D_MODEL), F32),
        compiler_params=pltpu.CompilerParams(
            dimension_semantics=("arbitrary",), vmem_limit_bytes=VMEM_LIMIT_BYTES),
        name="ffn_dense",
    )(x, *weights)


def _route(x, wr, precise):
    logits = _mm_hi(x, wr) if precise else _mm(x, wr)
    lane = lax.broadcasted_iota(jnp.int32, logits.shape, 1)
    neg = jnp.float32(-jnp.inf)
    l1 = jnp.where(lane < N_EXPERTS, logits, neg)
    v1 = jnp.max(l1, axis=-1, keepdims=True)
    i1 = jnp.min(jnp.where(l1 == v1, lane, LANES), axis=-1, keepdims=True)
    l2 = jnp.where(lane == i1, neg, l1)
    v2 = jnp.max(l2, axis=-1, keepdims=True)
    i2 = jnp.min(jnp.where(l2 == v2, lane, LANES), axis=-1, keepdims=True)
    e2 = jnp.exp(v2 - v1)
    den = 1.0 + e2
    idx = jnp.where(lane == 0, i1, jnp.where(lane == 1, i2, 0))
    gates = jnp.where(lane == 0, 1.0 / den, jnp.where(lane == 1, e2 / den, 0.0))
    return idx, gates


def _router_kernel(x_ref, wr_ref, idx_ref, gate_ref):
    idx_ref[...], gate_ref[...] = _route(x_ref[...], wr_ref[...], precise=True)


def _router(x, wr_pad):
    n = x.shape[0]
    tm = min(SEQ_TILE, n)
    return pl.pallas_call(
        _router_kernel,
        grid=(n // tm,),
        in_specs=[pl.BlockSpec((tm, D_MODEL), lambda i: (i, 0)), _const_spec(wr_pad.shape)],
        out_specs=(pl.BlockSpec((tm, LANES), lambda i: (i, 0)),
                   pl.BlockSpec((tm, LANES), lambda i: (i, 0))),
        out_shape=(jax.ShapeDtypeStruct((n, LANES), jnp.int32),
                   jax.ShapeDtypeStruct((n, LANES), F32)),
        compiler_params=pltpu.CompilerParams(dimension_semantics=("arbitrary",)),
        name="moe_router",
    )(x, wr_pad)


def _sc_row_gather(src, idx):
    n_idx = idx.shape[0]
    assert n_idx % SC_WINDOW == 0
    n_win = n_idx // SC_WINDOW
    n_workers = SC_CORES * SC_SUBCORES
    n_sub = SC_WINDOW // SC_SUB_ROWS
    mesh = plsc.VectorSubcoreMesh(core_axis_name="c", subcore_axis_name="s",
                                  num_cores=SC_CORES, num_subcores=SC_SUBCORES)

    def body(src_hbm, idx_hbm, out_hbm, idx_v, buf, g0, g1, w0, w1):
        gsem = (g0, g1)
        wsem = (w0, w1)
        worker = lax.axis_index("c") * SC_SUBCORES + lax.axis_index("s")

        @pl.loop(0, pl.cdiv(n_win, n_workers))
        def _(t):
            win = worker + t * n_workers

            @pl.when(win < n_win)
            def _():
                base = win * SC_WINDOW
                pltpu.sync_copy(idx_hbm.at[:, pl.ds(base, SC_WINDOW)], idx_v)

                def fetch(q):
                    return pltpu.async_copy(
                        src_hbm.at[idx_v.at[0, pl.ds(q * SC_SUB_ROWS, SC_SUB_ROWS)]],
                        buf.at[q % 2], gsem[q % 2])

                def put(q):
                    return pltpu.async_copy(
                        buf.at[q % 2], out_hbm.at[pl.ds(base + q * SC_SUB_ROWS, SC_SUB_ROWS)],
                        wsem[q % 2])

                fetches = {0: fetch(0)}
                puts = {}
                for q in range(n_sub):
                    if q + 1 < n_sub:
                        if q >= 1:
                            puts[q - 1].wait()
                        fetches[q + 1] = fetch(q + 1)
                    fetches[q].wait()
                    puts[q] = put(q)
                for q in range(max(0, n_sub - 2), n_sub):
                    puts[q].wait()

    return pl.kernel(
        body,
        out_type=jax.ShapeDtypeStruct((n_idx, D_MODEL), src.dtype),
        mesh=mesh,
        scratch_types=[pltpu.VMEM((1, SC_WINDOW), jnp.int32),
                       pltpu.VMEM((2, SC_SUB_ROWS, D_MODEL), src.dtype),
                       pltpu.SemaphoreType.DMA, pltpu.SemaphoreType.DMA,
                       pltpu.SemaphoreType.DMA, pltpu.SemaphoreType.DMA],
        cost_estimate=pl.CostEstimate(
            flops=0, transcendentals=0,
            bytes_accessed=n_idx * (2 * D_MODEL * src.dtype.itemsize + idx.dtype.itemsize)),
        name="sc_row_gather",
    )(src, idx.reshape(1, n_idx))


def _expert_kernel(te_ref, nu_ref, xs_ref, wg_ref, wu_ref, wd_ref, ys_ref, xs_bf, acc, *, tf, nf):
    del te_ref
    i = pl.program_id(0)
    j = pl.program_id(1)
    used = i < nu_ref[0]

    @pl.when(jnp.logical_and(used, j == 0))
    def _():
        xs_bf[...] = xs_ref[...].astype(BF16)

    @pl.when(used)
    def _():
        xb = xs_bf[...]
        part = None
        for c0 in range(0, tf, FFN_CHUNK):
            h = _silu(_dot(xb, wg_ref[:, c0:c0 + FFN_CHUNK])) * _dot(xb, wu_ref[:, c0:c0 + FFN_CHUNK])
            d = _dot(h.astype(BF16), wd_ref[c0:c0 + FFN_CHUNK, :])
            part = d if part is None else part + d

        if nf == 1:
            ys_ref[...] = part
        else:
            @pl.when(j == 0)
            def _():
                acc[...] = part

            @pl.when(jnp.logical_and(j > 0, j < nf - 1))
            def _():
                acc[...] += part

            @pl.when(j == nf - 1)
            def _():
                ys_ref[...] = acc[...] + part

    @pl.when(jnp.logical_and(jnp.logical_not(used), j == nf - 1))
    def _():
        ys_ref[...] = jnp.zeros(ys_ref.shape, F32)


def _experts(xs, tile_expert, n_used, wg, wu, wd, tm):
    n_tiles = tile_expert.shape[0]
    d_e = wg.shape[2]
    tf = EXPERT_F_TILE
    nf = d_e // tf
    assert nf * tf == d_e and tf % FFN_CHUNK == 0

    def jeff(i, j, nu):
        return jnp.where(i < nu[0], j, nf - 1)

    grid_spec = pltpu.PrefetchScalarGridSpec(
        num_scalar_prefetch=2,
        grid=(n_tiles, nf),
        in_specs=[
            pl.BlockSpec((tm, D_MODEL), lambda i, j, te, nu: (jnp.minimum(i, nu[0] - 1), 0)),
            pl.BlockSpec((None, D_MODEL, tf), lambda i, j, te, nu: (te[i], 0, jeff(i, j, nu))),
            pl.BlockSpec((None, D_MODEL, tf), lambda i, j, te, nu: (te[i], 0, jeff(i, j, nu))),
            pl.BlockSpec((None, tf, D_MODEL), lambda i, j, te, nu: (te[i], jeff(i, j, nu), 0)),
        ],
        out_specs=pl.BlockSpec((tm, D_MODEL), lambda i, j, te, nu: (i, 0)),
        scratch_shapes=[
            pltpu.VMEM((tm, D_MODEL), BF16),
            pltpu.VMEM((tm if nf > 1 else SUBLANES, D_MODEL), F32),
        ],
    )
    return pl.pallas_call(
        functools.partial(_expert_kernel, tf=tf, nf=nf),
        grid_spec=grid_spec,
        out_shape=jax.ShapeDtypeStruct((n_tiles * tm, D_MODEL), F32),
        compiler_params=pltpu.CompilerParams(
            dimension_semantics=("arbitrary", "arbitrary"), vmem_limit_bytes=VMEM_LIMIT_BYTES),
        name="moe_experts",
    )(tile_expert, n_used, xs, wg, wu, wd)


def _combine_kernel(x_ref, gate_ref, y0_ref, y1_ref, g_ref, b_ref, y_ref):
    gates = gate_ref[...]
    f = gates[:, 0:1] * y0_ref[...] + gates[:, 1:2] * y1_ref[...]
    y_ref[...] = _layer_norm(ALPHA * x_ref[...] + f, g_ref[...], b_ref[...])


def _combine(x, gates, y01, g, b, n, x_row0, y0_row0, y1_row0):
    tm = min(SEQ_TILE, n)
    nt = n // tm
    assert nt * tm == n and x_row0 % tm == 0 and y0_row0 % tm == 0 and y1_row0 % tm == 0
    xb0, y0b, y1b = x_row0 // tm, y0_row0 // tm, y1_row0 // tm
    return pl.pallas_call(
        _combine_kernel,
        grid=(nt,),
        in_specs=[
            pl.BlockSpec((tm, D_MODEL), lambda i: (i + xb0, 0)),
            pl.BlockSpec((tm, LANES), lambda i: (i, 0)),
            pl.BlockSpec((tm, D_MODEL), lambda i: (i + y0b, 0)),
            pl.BlockSpec((tm, D_MODEL), lambda i: (i + y1b, 0)),
            _const_spec(g.shape),
            _const_spec(b.shape),
        ],
        out_specs=pl.BlockSpec((tm, D_MODEL), lambda i: (i, 0)),
        out_shape=jax.ShapeDtypeStruct((n, D_MODEL), F32),
        compiler_params=pltpu.CompilerParams(
            dimension_semantics=("arbitrary",), vmem_limit_bytes=VMEM_LIMIT_BYTES),
        name="moe_combine",
    )(x, gates, y01, y01, g, b)


def _moe(x, groups, wg, wu, wd, g, b, tm):
    n = x.shape[0]
    assert sum(rows for rows, _, _ in groups) == n
    idx = jnp.concatenate([ids[:, :TOP_K] for _, ids, _ in groups], axis=0)
    flat_e = idx.reshape(-1)
    onehot = (flat_e[:, None] == jnp.arange(N_EXPERTS, dtype=jnp.int32)[None, :]).astype(jnp.int32)
    csum = jnp.cumsum(onehot, axis=0)
    rank = jnp.sum((csum - onehot) * onehot, axis=1)
    counts = csum[-1]
    padded = ((counts + tm - 1) // tm) * tm
    ends = jnp.cumsum(padded)
    slot = (ends - padded)[flat_e] + rank
    n_pairs = TOP_K * n
    n_tiles = pl.cdiv(n_pairs, tm) + N_EXPERTS
    n_used = (ends[-1] // tm).astype(jnp.int32).reshape(1)
    tile_start = jnp.arange(n_tiles, dtype=jnp.int32) * tm
    tile_expert = jnp.sum((tile_start[:, None] >= ends[None, :]).astype(jnp.int32), axis=1)
    last_e = jnp.sum((ends[-1] - 1 >= ends).astype(jnp.int32))
    tile_expert = jnp.minimum(tile_expert, last_e).astype(jnp.int32)
    pair_sorted = jnp.sort(flat_e * n_pairs + jnp.arange(n_pairs, dtype=jnp.int32)) % n_pairs
    slot_ids = jnp.arange(n_tiles * tm, dtype=jnp.int32)
    slot_e = jnp.repeat(tile_expert, tm)
    local = slot_ids - (ends - padded)[slot_e]
    q = (jnp.cumsum(counts) - counts)[slot_e] + local
    valid = jnp.logical_and(local >= 0, local < counts[slot_e])
    tok_of_slot = jnp.where(valid, pair_sorted[jnp.clip(q, 0, n_pairs - 1)] // TOP_K, slot_ids % n)
    xs = _sc_row_gather(x, tok_of_slot)
    ys = _experts(xs, tile_expert, n_used, wg, wu, wd, tm)
    slot2 = slot.reshape(n, TOP_K)
    order, row0 = [], 0
    for rows, _, _ in groups:
        order += [slot2[row0:row0 + rows, 0], slot2[row0:row0 + rows, 1]]
        row0 += rows
    y01 = _sc_row_gather(ys, jnp.concatenate(order))
    outs, row0 = [], 0
    for rows, _, gates in groups:
        outs.append(_combine(x, gates, y01, g, b, rows, row0, TOP_K * row0, TOP_K * row0 + rows))
        row0 += rows
    return outs


def _block_diag(w):
    n, c, _ = w.shape
    eye = jnp.eye(n, dtype=w.dtype)
    return (eye[:, None, :, None] * w[:, :, None, :]).reshape(n * c, n * c)


def _row(v):
    return v.reshape(1, -1)


def kernel(x_prompt, x_sample, state_pool, state_conv, state_lru_conv, state_lru_h, w_in, b_gate, w_pool, pool_scale, conv_w, conv_b, conv_ln_g, conv_ln_b, lru_conv_w, lru_conv_b, lru_wa, lru_ba, lru_wx, lru_bx, lru_lambda, w_br_pool, w_br_conv, w_br_lru, w_out, ln1_g, ln1_b, ln2_g, ln2_b, ffn_w_gate, ffn_w_up, ffn_w_down, moe_router, moe_w_gate, moe_w_up, moe_w_down):
    batch, seq, _ = x_prompt.shape
    dec_batch = x_sample.shape[0]

    def layer_params(l, mat):
        return {
            "w_in": w_in[l].astype(mat), "b_gate": _row(b_gate[l]),
            "w_pool": w_pool[l].astype(mat), "pool_scale": _row(pool_scale[l]),
            "conv_w": conv_w[l], "conv_b": _row(conv_b[l]),
            "conv_ln_g": _row(conv_ln_g[l]), "conv_ln_b": _row(conv_ln_b[l]),
            "lru_conv_w": lru_conv_w[l], "lru_conv_b": _row(lru_conv_b[l]),
            "lru_wa": _block_diag(lru_wa[l]).astype(mat), "lru_ba": _row(lru_ba[l]),
            "lru_wx": _block_diag(lru_wx[l]).astype(mat), "lru_bx": _row(lru_bx[l]),
            "lru_lambda": _row(lru_lambda[l]),
            "w_br_pool": w_br_pool[l].astype(mat), "w_br_conv": w_br_conv[l].astype(mat),
            "w_br_lru": w_br_lru[l].astype(mat), "w_out": w_out[l].astype(mat),
            "ln1_g": _row(ln1_g[l]), "ln1_b": _row(ln1_b[l]),
        }

    layers = [layer_params(l, BF16) for l in range(DEPTH)]
    layers_f32 = [layer_params(l, F32) for l in range(DEPTH)]
    for l in range(DEPTH):
        layers_f32[l].update({"w_in": (w_in, l), "w_br_pool": (w_br_pool, l), "w_br_conv": (w_br_conv, l),
                              "w_br_lru": (w_br_lru, l), "w_out": (w_out, l)})
    ffn_w = [(ffn_w_gate[m].astype(BF16), ffn_w_up[m].astype(BF16), ffn_w_down[m].astype(BF16))
             for m in range(ffn_w_gate.shape[0])]
    ffn_w_f32 = [(ffn_w_gate[m], ffn_w_up[m], ffn_w_down[m]) for m in range(ffn_w_gate.shape[0])]
    moe_w = [(jnp.pad(moe_router[m], ((0, 0), (0, LANES - N_EXPERTS))),
              moe_w_gate[m].astype(BF16), moe_w_up[m].astype(BF16), moe_w_down[m].astype(BF16))
             for m in range(moe_router.shape[0])]

    x_prompt, moe_w = lax.optimization_barrier((x_prompt, moe_w))

    n_p = batch * seq
    xp = x_prompt.reshape(n_p, D_MODEL)
    xd = x_sample.reshape(dec_batch, D_MODEL)
    p_states, s_states = [], []
    for l in range(DEPTH):
        g, b = _row(ln2_g[l]), _row(ln2_b[l])
        dense = l % 2 == 0
        wr, wg, wu, wd = (None,) * 4 if dense else moe_w[l // 2]
        xp, sp, sc, slc, sh, *routing = _mixer_seq(
            xp, layers[l], batch, seq, None if dense else wr.astype(BF16))
        p_states.append((sp, sc, slc, sh))
        xd, sp, sc, slc, sh = _mixer_step(xd, state_pool[l], state_conv[l], state_lru_conv[l],
                                          state_lru_h[l], layers_f32[l], PAST_LEN)
        s_states.append((sp, sc, slc, sh))
        if dense:
            xp = _ffn(xp, *ffn_w[l // 2], g, b, precise=False)
            xd = _ffn(xd, *ffn_w_f32[l // 2], g, b, precise=True)
        else:
            rows = jnp.concatenate([xp, xd], axis=0)
            xp, xd = _moe(rows, [(n_p, *routing), (dec_batch, *_router(xd, wr))],
                          wg, wu, wd, g, b, MOE_TILE_SEQ)
    y_prompt = xp.reshape(batch, seq, D_MODEL)
    y_sample = xd.reshape(dec_batch, 1, D_MODEL)

    def stack(states, k):
        return jnp.stack([s[k] for s in states])

    return (y_prompt, y_sample,
            stack(p_states, 0), stack(p_states, 1), stack(p_states, 2), stack(p_states, 3),
            stack(s_states, 0), stack(s_states, 1), stack(s_states, 2), stack(s_states, 3))
```

```python
import functools

import jax
import jax.numpy as jnp
from jax import lax
from jax.experimental import pallas as pl
from jax.experimental.pallas import tpu as pltpu
from jax.experimental.pallas import tpu_sc as plsc

D_MODEL = 1024
DEPTH = 2
PAST_LEN = 16384
D_POOL = 512
POOL_GROUP = 128
POOL_WINDOWS = (2, 4, 8, 16)
POOL_BUF = 15
D_CONV = 512
CONV_WIDTH = 31
CONV_BUF = 30
D_LRU = 512
LRU_CONV_WIDTH = 4
LRU_CONV_BUF = 3
LRU_C = 8.0
N_EXPERTS = 8
TOP_K = 2
ALPHA = (2.0 * DEPTH) ** 0.25
LN_EPS = 1e-5

O_POOL = 0
O_VAL = D_POOL
O_GLU = O_VAL + D_CONV
O_LRU = O_GLU + D_CONV
O_GELU = O_LRU + D_LRU
O_GATE = O_GELU + D_LRU

SUBLANES = 8
LANES = 128
VMEM_LIMIT_BYTES = 56 * 1024 * 1024
SC_CORES = 2
SC_SUBCORES = 16
SC_WINDOW = 128
SC_SUB_ROWS = 32

SEQ_TILE = 512
SCAN_UNROLL = 2
CONV_ROWS = 128
POOL_OFF = 16
CONV_OFF = 32
LCONV_OFF = 8
FFN_CHUNK = 256
PIECE = 256
STEP_COLS = 512
EXPERT_F_TILE = 3584
MOE_TILE_SEQ = 512

BF16 = jnp.bfloat16
F32 = jnp.float32


def _sigmoid(x):
    return 0.5 * jnp.tanh(0.5 * x) + 0.5


def _silu(x):
    return x * _sigmoid(x)


def _gelu_tanh(x):
    return x * (0.5 * (1.0 + jnp.tanh(0.7978845608028654 * (x + 0.044715 * (x * x * x)))))


def _softplus(z):
    return jnp.maximum(z, 0.0) + jnp.log1p(jnp.exp(-jnp.abs(z)))


def _layer_norm(x, g, b):
    mu = jnp.mean(x, axis=-1, keepdims=True)
    xc = x - mu
    var = jnp.mean(xc * xc, axis=-1, keepdims=True)
    return xc * lax.rsqrt(var + LN_EPS) * g + b


def _dot(a, b):
    return jnp.dot(a, b, preferred_element_type=F32)


def _mm(a, w):
    return _dot(a.astype(BF16), w)


def _mm_hi(a, w):
    return jnp.dot(a, w, preferred_element_type=F32, precision=lax.Precision.HIGHEST)


def _lru_coeffs(xc, r, ig, lam, reset):
    log_a = (-LRU_C * r) * _softplus(-lam)
    a = jnp.exp(log_a)
    mult = jnp.sqrt(-jnp.tanh(log_a) * (a * a + 1.0))
    if reset is not None:
        a = jnp.where(reset, 0.0, a)
        mult = jnp.where(reset, 1.0, mult)
    return a, mult * ig * xc


def _mixer_seq_kernel(*refs, tm, with_router, batch, n_in):
    seq_row = pl.program_id(0) < batch

    @pl.when(jnp.logical_not(seq_row))
    def _():
        y_ref = refs[n_in]
        y_ref[...] = jnp.zeros(y_ref.shape, F32)

    @pl.when(seq_row)
    def _():
        _mixer_seq_body(*refs, tm=tm, with_router=with_router)


def _mixer_seq_body(x_ref, w_in_ref, b_gate_ref, w_pool_ref, pool_scale_ref,
                    conv_w_ref, conv_b_ref, cln_g_ref, cln_b_ref,
                    lconv_w_ref, lconv_b_ref, wa_ref, ba_ref, wx_ref, bx_ref, lam_ref,
                    wbp_ref, wbc_ref, wbl_ref, w_out_ref, ln_g_ref, ln_b_ref, *rest, tm, with_router):
    if with_router:
        wr_ref, rest = rest[0], rest[1:]
        idx_ref, gate_ref, rest = rest[5], rest[6], rest[:5] + rest[7:]
    (y_ref, sp_ref, sc_ref, slc_ref, sh_ref,
     pool_ext, conv_ext, lconv_ext, act_buf, a_buf, b_buf, h_carry, gate_buf, gelu_buf,
     merged_buf, xb_buf) = rest
    i = pl.program_id(1)

    @pl.when(i == 0)
    def _():
        pool_ext[0:POOL_OFF, :] = jnp.zeros((POOL_OFF, D_POOL), F32)
        conv_ext[0:CONV_OFF, :] = jnp.zeros((CONV_OFF, D_CONV), F32)
        conv_ext[tm + CONV_OFF:tm + CONV_OFF + SUBLANES, :] = jnp.zeros((SUBLANES, D_CONV), F32)
        lconv_ext[0:LCONV_OFF, :] = jnp.zeros((LCONV_OFF, D_LRU), F32)
        h_carry[...] = jnp.zeros((1, D_LRU), F32)

    xb_buf[...] = x_ref[...].astype(BF16)

    def proj(lo, width):
        return _dot(xb_buf[...], w_in_ref[:, lo:lo + width])

    pos = i * tm + lax.broadcasted_iota(jnp.int32, (tm, 1), 0)

    glu = proj(O_VAL, D_CONV) * _sigmoid(proj(O_GLU, D_CONV))
    conv_ext[CONV_OFF:CONV_OFF + tm, :] = glu

    def pool_piece():
        u_pool = proj(O_POOL, D_POOL)
        pool_ext[POOL_OFF:POOL_OFF + tm, :] = u_pool
        parts = []
        for g, w in enumerate(POOL_WINDOWS):
            lo = g * POOL_GROUP
            acc = u_pool[:, lo:lo + POOL_GROUP]
            for k in range(1, w):
                acc = acc + pool_ext[POOL_OFF - k:POOL_OFF - k + tm, lo:lo + POOL_GROUP]
            inv_cnt = 1.0 / jnp.minimum(w, pos + 1).astype(F32)
            pooled = acc * inv_cnt - u_pool[:, lo:lo + POOL_GROUP]
            parts.append(_dot(pooled.astype(BF16), w_pool_ref[g]))
        mixed = jnp.concatenate(parts, axis=1) * pool_scale_ref[...]
        merged_buf[...] = gate_buf[:, 0:D_MODEL] * _dot(mixed.astype(BF16), wbp_ref[...])
        sp_ref[0] = pool_ext[tm + POOL_OFF - POOL_BUF:tm + POOL_OFF, :]
        pool_ext[0:POOL_OFF, :] = pool_ext[tm:tm + POOL_OFF, :]

    def lru_piece(lo):
        lconv_ext[LCONV_OFF:LCONV_OFF + tm, lo:lo + PIECE] = proj(O_LRU + lo, PIECE)

    def gelu_piece(lo):
        gelu_buf[:, lo:lo + PIECE] = _gelu_tanh(proj(O_GELU + lo, PIECE))

    def gate_piece(lo):
        gate_buf[:, lo:lo + PIECE] = _sigmoid(
            proj(O_GATE + lo, PIECE) + b_gate_ref[:, lo:lo + PIECE])

    pieces = ([functools.partial(gate_piece, lo) for lo in range(0, D_MODEL, PIECE)]
              + [pool_piece]
              + [functools.partial(lru_piece, lo) for lo in range(0, D_LRU, PIECE)]
              + [functools.partial(gate_piece, lo) for lo in range(D_MODEL, 2 * D_MODEL, PIECE)]
              + [functools.partial(gelu_piece, lo) for lo in range(0, D_LRU, PIECE)]
              + [functools.partial(gate_piece, lo) for lo in range(2 * D_MODEL, 3 * D_MODEL, PIECE)])
    n_conv_chunks = tm // CONV_ROWS
    base = CONV_OFF - CONV_BUF
    rows = CONV_ROWS + SUBLANES
    n_units = n_conv_chunks * (D_CONV // LANES)
    unit = 0
    for c in range(n_conv_chunks):
        c0 = c * CONV_ROWS
        for l0 in range(0, D_CONV, LANES):
            acc = jnp.zeros((CONV_ROWS, LANES), F32) + conv_b_ref[:, l0:l0 + LANES]
            for r in range(SUBLANES):
                z = None
                for k in range(CONV_WIDTH):
                    if (base + k) % SUBLANES != r:
                        continue
                    q8 = base + k - r
                    t = (conv_ext[c0 + q8:c0 + q8 + rows, l0:l0 + LANES]
                         * conv_w_ref[k:k + 1, l0:l0 + LANES])
                    z = t if z is None else z + t
                acc = acc + z[r:r + CONV_ROWS, :]
            act_buf[c0:c0 + CONV_ROWS, l0:l0 + LANES] = acc
            for piece in pieces[unit * len(pieces) // n_units:(unit + 1) * len(pieces) // n_units]:
                piece()
            unit += 1
        act_buf[c0:c0 + CONV_ROWS, :] = _silu(_layer_norm(
            act_buf[c0:c0 + CONV_ROWS, :], cln_g_ref[...], cln_b_ref[...]))
    merged_buf[...] += gate_buf[:, D_MODEL:2 * D_MODEL] * _dot(
        act_buf[...].astype(BF16), wbc_ref[...])
    sc_ref[0] = conv_ext[tm + CONV_OFF - CONV_BUF:tm + CONV_OFF, :]
    conv_ext[0:CONV_OFF, :] = conv_ext[tm:tm + CONV_OFF, :]

    base = LCONV_OFF - LRU_CONV_BUF
    xc = jnp.zeros((tm, D_LRU), F32) + lconv_b_ref[...]
    for k in range(LRU_CONV_WIDTH):
        xc = xc + lconv_ext[base + k:base + k + tm, :] * lconv_w_ref[k:k + 1, :]
    slc_ref[0] = lconv_ext[tm + LCONV_OFF - LRU_CONV_BUF:tm + LCONV_OFF, :]
    lconv_ext[0:LCONV_OFF, :] = lconv_ext[tm:tm + LCONV_OFF, :]
    xcb = xc.astype(BF16)
    r = _sigmoid(_dot(xcb, wa_ref[...]) + ba_ref[...])
    ig = _sigmoid(_dot(xcb, wx_ref[...]) + bx_ref[...])
    a, b = _lru_coeffs(xc, r, ig, lam_ref[...], pos == 0)
    a_buf[...] = a
    b_buf[...] = b

    row8 = lax.broadcasted_iota(jnp.int32, (SUBLANES, D_LRU), 0)

    def scan_block(j, h):
        r0 = pl.multiple_of(j * SUBLANES, SUBLANES)
        a8 = a_buf[pl.ds(r0, SUBLANES), :]
        b8 = b_buf[pl.ds(r0, SUBLANES), :]
        for k in (1, 2, 4):
            a_s = pltpu.roll(a8, k, 0)
            b_s = pltpu.roll(b8, k, 0)
            m = row8 >= k
            b8 = jnp.where(m, a8 * b_s + b8, b8)
            a8 = jnp.where(m, a8 * a_s, a8)
        h8 = a8 * h + b8
        a_buf[pl.ds(r0, SUBLANES), :] = h8
        return h8[SUBLANES - 1:SUBLANES, :]

    h_last = lax.fori_loop(0, tm // SUBLANES, scan_block, h_carry[...], unroll=SCAN_UNROLL)
    h_carry[...] = h_last
    sh_ref[0] = h_last
    hg = a_buf[...] * gelu_buf[...]
    merged = merged_buf[...] + gate_buf[:, 2 * D_MODEL:3 * D_MODEL] * _dot(
        hg.astype(BF16), wbl_ref[...])

    m_out = _dot(merged.astype(BF16), w_out_ref[...])
    y = _layer_norm(ALPHA * x_ref[...] + m_out, ln_g_ref[...], ln_b_ref[...])
    y_ref[...] = y
    if with_router:
        idx_ref[...], gate_ref[...] = _route(y, wr_ref[...], precise=False)


def _mixer_step_kernel(x_ref, st_pool_ref, st_conv_ref, st_lconv_ref, st_h_ref,
                       w_in_ref, b_gate_ref, w_pool_ref, pool_scale_ref,
                       conv_w_ref, conv_b_ref, cln_g_ref, cln_b_ref,
                       lconv_w_ref, lconv_b_ref, wa_ref, ba_ref, wx_ref, bx_ref, lam_ref,
                       wbp_ref, wbc_ref, wbl_ref, w_out_ref, ln_g_ref, ln_b_ref,
                       y_ref, up_ref, glu_ref, ul_ref, sh_ref, u_buf, *, start_pos, n_blocks):
    k = pl.program_id(0)
    u_buf[k] = _mm_hi(x_ref[...], w_in_ref[...])

    @pl.when(k == n_blocks - 1)
    def _():
        x = x_ref[...]

        def proj(lo):
            return u_buf[lo // STEP_COLS]

        def gate(n):
            lo = n * D_MODEL
            u = jnp.concatenate([proj(O_GATE + lo), proj(O_GATE + lo + STEP_COLS)], axis=1)
            return _sigmoid(u + b_gate_ref[:, lo:lo + D_MODEL])

        u_pool = proj(O_POOL)
        parts = []
        for g, w in enumerate(POOL_WINDOWS):
            lo = g * POOL_GROUP
            acc = u_pool[:, lo:lo + POOL_GROUP]
            for j in range(1, w):
                acc = acc + st_pool_ref[POOL_BUF - j, :, lo:lo + POOL_GROUP]
            pooled = acc * (1.0 / min(w, start_pos + 1)) - u_pool[:, lo:lo + POOL_GROUP]
            parts.append(_mm_hi(pooled, w_pool_ref[g]))
        mixed = jnp.concatenate(parts, axis=1) * pool_scale_ref[...]
        merged = gate(0) * _mm_hi(mixed, wbp_ref[...])
        up_ref[...] = u_pool

        glu = proj(O_VAL) * _sigmoid(proj(O_GLU))
        acc = glu * conv_w_ref[CONV_BUF:CONV_BUF + 1, :] + conv_b_ref[...]
        for j in range(CONV_BUF):
            acc = acc + st_conv_ref[j] * conv_w_ref[j:j + 1, :]
        act = _silu(_layer_norm(acc, cln_g_ref[...], cln_b_ref[...]))
        merged = merged + gate(1) * _mm_hi(act, wbc_ref[...])
        glu_ref[...] = glu

        u_lru = proj(O_LRU)
        xc = u_lru * lconv_w_ref[LRU_CONV_BUF:LRU_CONV_BUF + 1, :] + lconv_b_ref[...]
        for j in range(LRU_CONV_BUF):
            xc = xc + st_lconv_ref[j] * lconv_w_ref[j:j + 1, :]
        ul_ref[...] = u_lru
        r = _sigmoid(_mm_hi(xc, wa_ref[...]) + ba_ref[...])
        ig = _sigmoid(_mm_hi(xc, wx_ref[...]) + bx_ref[...])
        reset = jnp.full(xc.shape, True) if start_pos == 0 else None
        a, b = _lru_coeffs(xc, r, ig, lam_ref[...], reset)
        h = a * st_h_ref[...] + b
        sh_ref[...] = h
        hg = h * _gelu_tanh(proj(O_GELU))
        merged = merged + gate(2) * _mm_hi(hg, wbl_ref[...])

        m_out = _mm_hi(merged, w_out_ref[...])
        y_ref[...] = _layer_norm(ALPHA * x + m_out, ln_g_ref[...], ln_b_ref[...])


def _const_spec(shape):
    nd = len(shape)
    return pl.BlockSpec(shape, lambda *_: (0,) * nd, pipeline_mode=pl.Buffered(1))


def _mixer_weight_list(p):
    return [p["w_in"], p["b_gate"], p["w_pool"], p["pool_scale"], p["conv_w"], p["conv_b"],
            p["conv_ln_g"], p["conv_ln_b"], p["lru_conv_w"], p["lru_conv_b"], p["lru_wa"],
            p["lru_ba"], p["lru_wx"], p["lru_bx"], p["lru_lambda"], p["w_br_pool"],
            p["w_br_conv"], p["w_br_lru"], p["w_out"], p["ln1_g"], p["ln1_b"]]


def _mixer_seq(x, p, batch, seq, wr_pad=None, extra_rows=0):
    tm = min(SEQ_TILE, seq)
    nt = seq // tm
    assert extra_rows <= tm
    with_router = wr_pad is not None
    weights = _mixer_weight_list(p) + ([wr_pad] if with_router else [])
    last = batch * nt - 1

    def row_block(b, i):
        return (jnp.minimum(b * nt + i, last), 0)

    def y_block(b, i):
        return (jnp.where(b < batch, b * nt + i, batch * nt), 0)

    def state_block(b, i):
        return (jnp.minimum(b, batch - 1), 0, 0)

    in_specs = [pl.BlockSpec((tm, D_MODEL), row_block)]
    in_specs += [_const_spec(w.shape) for w in weights]
    out_shape = [
        jax.ShapeDtypeStruct((batch * seq + extra_rows, D_MODEL), F32),
        jax.ShapeDtypeStruct((batch, POOL_BUF, D_POOL), F32),
        jax.ShapeDtypeStruct((batch, CONV_BUF, D_CONV), F32),
        jax.ShapeDtypeStruct((batch, LRU_CONV_BUF, D_LRU), F32),
        jax.ShapeDtypeStruct((batch, 1, D_LRU), F32),
    ]
    out_specs = [
        pl.BlockSpec((tm, D_MODEL), y_block),
        pl.BlockSpec((1, POOL_BUF, D_POOL), state_block),
        pl.BlockSpec((1, CONV_BUF, D_CONV), state_block),
        pl.BlockSpec((1, LRU_CONV_BUF, D_LRU), state_block),
        pl.BlockSpec((1, 1, D_LRU), state_block),
    ]
    if with_router:
        out_shape += [jax.ShapeDtypeStruct((batch * seq, LANES), jnp.int32),
                      jax.ShapeDtypeStruct((batch * seq, LANES), F32)]
        out_specs += [pl.BlockSpec((tm, LANES), row_block),
                      pl.BlockSpec((tm, LANES), row_block)]
    scratch = [
        pltpu.VMEM((tm + POOL_OFF, D_POOL), F32),
        pltpu.VMEM((tm + CONV_OFF + SUBLANES, D_CONV), F32),
        pltpu.VMEM((tm + LCONV_OFF, D_LRU), F32),
        pltpu.VMEM((tm, D_CONV), F32),
        pltpu.VMEM((tm, D_LRU), F32),
        pltpu.VMEM((tm, D_LRU), F32),
        pltpu.VMEM((1, D_LRU), F32),
        pltpu.VMEM((tm, 3 * D_MODEL), F32),
        pltpu.VMEM((tm, D_LRU), F32),
        pltpu.VMEM((tm, D_MODEL), F32),
        pltpu.VMEM((tm, D_MODEL), BF16),
    ]
    y, sp, sc, slc, sh, *routing = pl.pallas_call(
        functools.partial(_mixer_seq_kernel, tm=tm, with_router=with_router, batch=batch,
                          n_in=1 + len(weights)),
        grid=(batch + (1 if extra_rows else 0), nt),
        in_specs=in_specs,
        out_specs=out_specs,
        out_shape=out_shape,
        scratch_shapes=scratch,
        compiler_params=pltpu.CompilerParams(
            dimension_semantics=("arbitrary", "arbitrary"),
            vmem_limit_bytes=VMEM_LIMIT_BYTES),
        name="mixer_seq",
    )(x, *weights)
    return (y, sp, sc, slc, sh.reshape(batch, D_LRU), *routing)


def _mixer_step(x, st_pool, st_conv, st_lconv, st_h, p, start_pos):
    batch = x.shape[0]
    weights = _mixer_weight_list(p)
    states = [jnp.transpose(st_pool, (1, 0, 2)), jnp.transpose(st_conv, (1, 0, 2)),
              jnp.transpose(st_lconv, (1, 0, 2)), st_h]

    def spec(a):
        if not isinstance(a, tuple):
            return _const_spec(a.shape)
        arr, layer = a
        zeros = (0,) * (arr.ndim - 1)
        return pl.BlockSpec((None,) + arr.shape[1:], lambda k: (layer,) + zeros,
                            pipeline_mode=pl.Buffered(1))

    w_in_all, layer = weights[0]
    in_cols = w_in_all.shape[2]
    n_blocks = in_cols // STEP_COLS
    assert n_blocks * STEP_COLS == in_cols
    in_specs = [spec(a) for a in [x] + states + weights]
    in_specs[1 + len(states)] = pl.BlockSpec((None, D_MODEL, STEP_COLS), lambda k: (layer, 0, k))
    args = [x] + states + [w[0] if isinstance(w, tuple) else w for w in weights]
    out_shape = (
        jax.ShapeDtypeStruct((batch, D_MODEL), F32),
        jax.ShapeDtypeStruct((batch, D_POOL), F32),
        jax.ShapeDtypeStruct((batch, D_CONV), F32),
        jax.ShapeDtypeStruct((batch, D_LRU), F32),
        jax.ShapeDtypeStruct((batch, D_LRU), F32),
    )
    y, u_pool, glu, u_lru, sh = pl.pallas_call(
        functools.partial(_mixer_step_kernel, start_pos=start_pos, n_blocks=n_blocks),
        grid=(n_blocks,),
        in_specs=in_specs,
        out_specs=tuple(pl.BlockSpec(s.shape, lambda k: (0, 0)) for s in out_shape),
        out_shape=out_shape,
        scratch_shapes=[pltpu.VMEM((n_blocks, batch, STEP_COLS), F32)],
        compiler_params=pltpu.CompilerParams(
            dimension_semantics=("arbitrary",), vmem_limit_bytes=VMEM_LIMIT_BYTES),
        name="mixer_step",
    )(*args)

    def push(state, row):
        return jnp.concatenate([state[:, 1:], row[:, None]], axis=1)

    return y, push(st_pool, u_pool), push(st_conv, glu), push(st_lconv, u_lru), sh


def _ffn_kernel(x_ref, wg_ref, wu_ref, wd_ref, g_ref, b_ref, y_ref, *, d_ff, precise):
    x = x_ref[...]
    mm = _mm_hi if precise else _dot
    xb = x if precise else x.astype(BF16)
    acc = jnp.zeros(x.shape, F32)
    for c0 in range(0, d_ff, FFN_CHUNK):
        h = _silu(mm(xb, wg_ref[:, c0:c0 + FFN_CHUNK])) * mm(xb, wu_ref[:, c0:c0 + FFN_CHUNK])
        acc = acc + mm(h if precise else h.astype(BF16), wd_ref[c0:c0 + FFN_CHUNK, :])
    y_ref[...] = _layer_norm(ALPHA * x + acc, g_ref[...], b_ref[...])


def _ffn(x, wg, wu, wd, g, b, precise):
    n = x.shape[0]
    tm = min(SEQ_TILE, n)
    d_ff = wg.shape[1]
    weights = [wg, wu, wd, g, b]
    return pl.pallas_call(
        functools.partial(_ffn_kernel, d_ff=d_ff, precise=precise),
        grid=(n // tm,),
        in_specs=[pl.BlockSpec((tm, D_MODEL), lambda i: (i, 0))] + [_const_spec(w.shape) for w in weights],
        out_specs=pl.BlockSpec((tm, D_MODEL), lambda i: (i, 0)),
        out_shape=jax.ShapeDtypeStruct((n, D_MODEL), F32),
        compiler_params=pltpu.CompilerParams(
            dimension_semantics=("arbitrary",), vmem_limit_bytes=VMEM_LIMIT_BYTES),
        name="ffn_dense",
    )(x, *weights)


def _route(x, wr, precise):
    logits = _mm_hi(x, wr) if precise else _mm(x, wr)
    lane = lax.broadcasted_iota(jnp.int32, logits.shape, 1)
    neg = jnp.float32(-jnp.inf)
    l1 = jnp.where(lane < N_EXPERTS, logits, neg)
    v1 = jnp.max(l1, axis=-1, keepdims=True)
    i1 = jnp.min(jnp.where(l1 == v1, lane, LANES), axis=-1, keepdims=True)
    l2 = jnp.where(lane == i1, neg, l1)
    v2 = jnp.max(l2, axis=-1, keepdims=True)
    i2 = jnp.min(jnp.where(l2 == v2, lane, LANES), axis=-1, keepdims=True)
    e2 = jnp.exp(v2 - v1)
    den = 1.0 + e2
    idx = jnp.where(lane == 0, i1, jnp.where(lane == 1, i2, 0))
    gates = jnp.where(lane == 0, 1.0 / den, jnp.where(lane == 1, e2 / den, 0.0))
    return idx, gates


def _router_kernel(x_ref, wr_ref, idx_ref, gate_ref):
    idx_ref[...], gate_ref[...] = _route(x_ref[...], wr_ref[...], precise=True)


def _router(x, wr_pad):
    n = x.shape[0]
    tm = min(SEQ_TILE, n)
    return pl.pallas_call(
        _router_kernel,
        grid=(n // tm,),
        in_specs=[pl.BlockSpec((tm, D_MODEL), lambda i: (i, 0)), _const_spec(wr_pad.shape)],
        out_specs=(pl.BlockSpec((tm, LANES), lambda i: (i, 0)),
                   pl.BlockSpec((tm, LANES), lambda i: (i, 0))),
        out_shape=(jax.ShapeDtypeStruct((n, LANES), jnp.int32),
                   jax.ShapeDtypeStruct((n, LANES), F32)),
        compiler_params=pltpu.CompilerParams(dimension_semantics=("arbitrary",)),
        name="moe_router",
    )(x, wr_pad)


def _sc_row_gather(src, idx):
    n_idx = idx.shape[0]
    assert n_idx % SC_WINDOW == 0
    n_win = n_idx // SC_WINDOW
    n_workers = SC_CORES * SC_SUBCORES
    n_sub = SC_WINDOW // SC_SUB_ROWS
    mesh = plsc.VectorSubcoreMesh(core_axis_name="c", subcore_axis_name="s",
                                  num_cores=SC_CORES, num_subcores=SC_SUBCORES)

    def body(src_hbm, idx_hbm, out_hbm, idx_v, buf, g0, g1, w0, w1):
        gsem = (g0, g1)
        wsem = (w0, w1)
        worker = lax.axis_index("c") * SC_SUBCORES + lax.axis_index("s")

        @pl.loop(0, pl.cdiv(n_win, n_workers))
        def _(t):
            win = worker + t * n_workers

            @pl.when(win < n_win)
            def _():
                base = win * SC_WINDOW
                pltpu.sync_copy(idx_hbm.at[:, pl.ds(base, SC_WINDOW)], idx_v)

                def fetch(q):
                    return pltpu.async_copy(
                        src_hbm.at[idx_v.at[0, pl.ds(q * SC_SUB_ROWS, SC_SUB_ROWS)]],
                        buf.at[q % 2], gsem[q % 2])

                def put(q):
                    return pltpu.async_copy(
                        buf.at[q % 2], out_hbm.at[pl.ds(base + q * SC_SUB_ROWS, SC_SUB_ROWS)],
                        wsem[q % 2])

                fetches = {0: fetch(0)}
                puts = {}
                for q in range(n_sub):
                    if q + 1 < n_sub:
                        if q >= 1:
                            puts[q - 1].wait()
                        fetches[q + 1] = fetch(q + 1)
                    fetches[q].wait()
                    puts[q] = put(q)
                for q in range(max(0, n_sub - 2), n_sub):
                    puts[q].wait()

    return pl.kernel(
        body,
        out_type=jax.ShapeDtypeStruct((n_idx, D_MODEL), src.dtype),
        mesh=mesh,
        scratch_types=[pltpu.VMEM((1, SC_WINDOW), jnp.int32),
                       pltpu.VMEM((2, SC_SUB_ROWS, D_MODEL), src.dtype),
                       pltpu.SemaphoreType.DMA, pltpu.SemaphoreType.DMA,
                       pltpu.SemaphoreType.DMA, pltpu.SemaphoreType.DMA],
        cost_estimate=pl.CostEstimate(
            flops=0, transcendentals=0,
            bytes_accessed=n_idx * (2 * D_MODEL * src.dtype.itemsize + idx.dtype.itemsize)),
        name="sc_row_gather",
    )(src, idx.reshape(1, n_idx))


def _expert_kernel(te_ref, nu_ref, xs_ref, wg_ref, wu_ref, wd_ref, ys_ref, xs_bf, acc, *, tf, nf):
    del te_ref
    i = pl.program_id(0)
    j = pl.program_id(1)
    used = i < nu_ref[0]

    @pl.when(jnp.logical_and(used, j == 0))
    def _():
        xs_bf[...] = xs_ref[...].astype(BF16)

    @pl.when(used)
    def _():
        xb = xs_bf[...]
        part = None
        for c0 in range(0, tf, FFN_CHUNK):
            h = _silu(_dot(xb, wg_ref[:, c0:c0 + FFN_CHUNK])) * _dot(xb, wu_ref[:, c0:c0 + FFN_CHUNK])
            d = _dot(h.astype(BF16), wd_ref[c0:c0 + FFN_CHUNK, :])
            part = d if part is None else part + d

        if nf == 1:
            ys_ref[...] = part
        else:
            @pl.when(j == 0)
            def _():
                acc[...] = part

            @pl.when(jnp.logical_and(j > 0, j < nf - 1))
            def _():
                acc[...] += part

            @pl.when(j == nf - 1)
            def _():
                ys_ref[...] = acc[...] + part

    @pl.when(jnp.logical_and(jnp.logical_not(used), j == nf - 1))
    def _():
        ys_ref[...] = jnp.zeros(ys_ref.shape, F32)


def _experts(xs, tile_expert, n_used, wg, wu, wd, tm):
    n_tiles = tile_expert.shape[0]
    d_e = wg.shape[2]
    tf = EXPERT_F_TILE
    nf = d_e // tf
    assert nf * tf == d_e and tf % FFN_CHUNK == 0

    def jeff(i, j, nu):
        return jnp.where(i < nu[0], j, nf - 1)

    grid_spec = pltpu.PrefetchScalarGridSpec(
        num_scalar_prefetch=2,
        grid=(n_tiles, nf),
        in_specs=[
            pl.BlockSpec((tm, D_MODEL), lambda i, j, te, nu: (jnp.minimum(i, nu[0] - 1), 0)),
            pl.BlockSpec((None, D_MODEL, tf), lambda i, j, te, nu: (te[i], 0, jeff(i, j, nu))),
            pl.BlockSpec((None, D_MODEL, tf), lambda i, j, te, nu: (te[i], 0, jeff(i, j, nu))),
            pl.BlockSpec((None, tf, D_MODEL), lambda i, j, te, nu: (te[i], jeff(i, j, nu), 0)),
        ],
        out_specs=pl.BlockSpec((tm, D_MODEL), lambda i, j, te, nu: (i, 0)),
        scratch_shapes=[
            pltpu.VMEM((tm, D_MODEL), BF16),
            pltpu.VMEM((tm if nf > 1 else SUBLANES, D_MODEL), F32),
        ],
    )
    return pl.pallas_call(
        functools.partial(_expert_kernel, tf=tf, nf=nf),
        grid_spec=grid_spec,
        out_shape=jax.ShapeDtypeStruct((n_tiles * tm, D_MODEL), F32),
        compiler_params=pltpu.CompilerParams(
            dimension_semantics=("arbitrary", "arbitrary"), vmem_limit_bytes=VMEM_LIMIT_BYTES),
        name="moe_experts",
    )(tile_expert, n_used, xs, wg, wu, wd)


def _combine_kernel(x_ref, gate_ref, y0_ref, y1_ref, g_ref, b_ref, y_ref):
    gates = gate_ref[...]
    f = gates[:, 0:1] * y0_ref[...] + gates[:, 1:2] * y1_ref[...]
    y_ref[...] = _layer_norm(ALPHA * x_ref[...] + f, g_ref[...], b_ref[...])


def _combine(x, gates, y01, g, b, n, x_row0, y0_row0, y1_row0):
    tm = min(SEQ_TILE, n)
    nt = n // tm
    assert nt * tm == n and x_row0 % tm == 0 and y0_row0 % tm == 0 and y1_row0 % tm == 0
    xb0, y0b, y1b = x_row0 // tm, y0_row0 // tm, y1_row0 // tm
    return pl.pallas_call(
        _combine_kernel,
        grid=(nt,),
        in_specs=[
            pl.BlockSpec((tm, D_MODEL), lambda i: (i + xb0, 0)),
            pl.BlockSpec((tm, LANES), lambda i: (i, 0)),
            pl.BlockSpec((tm, D_MODEL), lambda i: (i + y0b, 0)),
            pl.BlockSpec((tm, D_MODEL), lambda i: (i + y1b, 0)),
            _const_spec(g.shape),
            _const_spec(b.shape),
        ],
        out_specs=pl.BlockSpec((tm, D_MODEL), lambda i: (i, 0)),
        out_shape=jax.ShapeDtypeStruct((n, D_MODEL), F32),
        compiler_params=pltpu.CompilerParams(
            dimension_semantics=("arbitrary",), vmem_limit_bytes=VMEM_LIMIT_BYTES),
        name="moe_combine",
    )(x, gates, y01, y01, g, b)


def _moe(x, groups, wg, wu, wd, g, b, tm):
    n = x.shape[0]
    assert sum(rows for rows, _, _ in groups) == n
    idx = jnp.concatenate([ids[:, :TOP_K] for _, ids, _ in groups], axis=0)
    flat_e = idx.reshape(-1)
    onehot = (flat_e[:, None] == jnp.arange(N_EXPERTS, dtype=jnp.int32)[None, :]).astype(jnp.int32)
    csum = jnp.cumsum(onehot, axis=0)
    rank = jnp.sum((csum - onehot) * onehot, axis=1)
    counts = csum[-1]
    padded = ((counts + tm - 1) // tm) * tm
    ends = jnp.cumsum(padded)
    slot = (ends - padded)[flat_e] + rank
    n_pairs = TOP_K * n
    n_tiles = pl.cdiv(n_pairs, tm) + N_EXPERTS
    n_used = (ends[-1] // tm).astype(jnp.int32).reshape(1)
    tile_start = jnp.arange(n_tiles, dtype=jnp.int32) * tm
    tile_expert = jnp.sum((tile_start[:, None] >= ends[None, :]).astype(jnp.int32), axis=1)
    last_e = jnp.sum((ends[-1] - 1 >= ends).astype(jnp.int32))
    tile_expert = jnp.minimum(tile_expert, last_e).astype(jnp.int32)
    pair_sorted = jnp.sort(flat_e * n_pairs + jnp.arange(n_pairs, dtype=jnp.int32)) % n_pairs
    slot_ids = jnp.arange(n_tiles * tm, dtype=jnp.int32)
    slot_e = jnp.repeat(tile_expert, tm)
    local = slot_ids - (ends - padded)[slot_e]
    q = (jnp.cumsum(counts) - counts)[slot_e] + local
    valid = jnp.logical_and(local >= 0, local < counts[slot_e])
    tok_of_slot = jnp.where(valid, pair_sorted[jnp.clip(q, 0, n_pairs - 1)] // TOP_K, slot_ids % n)
    xs = _sc_row_gather(x, tok_of_slot)
    ys = _experts(xs, tile_expert, n_used, wg, wu, wd, tm)
    slot2 = slot.reshape(n, TOP_K)
    order, row0 = [], 0
    for rows, _, _ in groups:
        order += [slot2[row0:row0 + rows, 0], slot2[row0:row0 + rows, 1]]
        row0 += rows
    y01 = _sc_row_gather(ys, jnp.concatenate(order))
    outs, row0 = [], 0
    for rows, _, gates in groups:
        outs.append(_combine(x, gates, y01, g, b, rows, row0, TOP_K * row0, TOP_K * row0 + rows))
        row0 += rows
    return outs


def _block_diag(w):
    n, c, _ = w.shape
    eye = jnp.eye(n, dtype=w.dtype)
    return (eye[:, None, :, None] * w[:, :, None, :]).reshape(n * c, n * c)


def _row(v):
    return v.reshape(1, -1)


def kernel(x_prompt, x_sample, state_pool, state_conv, state_lru_conv, state_lru_h, w_in, b_gate, w_pool, pool_scale, conv_w, conv_b, conv_ln_g, conv_ln_b, lru_conv_w, lru_conv_b, lru_wa, lru_ba, lru_wx, lru_bx, lru_lambda, w_br_pool, w_br_conv, w_br_lru, w_out, ln1_g, ln1_b, ln2_g, ln2_b, ffn_w_gate, ffn_w_up, ffn_w_down, moe_router, moe_w_gate, moe_w_up, moe_w_down):
    batch, seq, _ = x_prompt.shape
    dec_batch = x_sample.shape[0]

    def layer_params(l, mat):
        return {
            "w_in": w_in[l].astype(mat), "b_gate": _row(b_gate[l]),
            "w_pool": w_pool[l].astype(mat), "pool_scale": _row(pool_scale[l]),
            "conv_w": conv_w[l], "conv_b": _row(conv_b[l]),
            "conv_ln_g": _row(conv_ln_g[l]), "conv_ln_b": _row(conv_ln_b[l]),
            "lru_conv_w": lru_conv_w[l], "lru_conv_b": _row(lru_conv_b[l]),
            "lru_wa": _block_diag(lru_wa[l]).astype(mat), "lru_ba": _row(lru_ba[l]),
            "lru_wx": _block_diag(lru_wx[l]).astype(mat), "lru_bx": _row(lru_bx[l]),
            "lru_lambda": _row(lru_lambda[l]),
            "w_br_pool": w_br_pool[l].astype(mat), "w_br_conv": w_br_conv[l].astype(mat),
            "w_br_lru": w_br_lru[l].astype(mat), "w_out": w_out[l].astype(mat),
            "ln1_g": _row(ln1_g[l]), "ln1_b": _row(ln1_b[l]),
        }

    layers = [layer_params(l, BF16) for l in range(DEPTH)]
    layers_f32 = [layer_params(l, F32) for l in range(DEPTH)]
    for l in range(DEPTH):
        layers_f32[l].update({"w_in": (w_in, l), "w_br_pool": (w_br_pool, l), "w_br_conv": (w_br_conv, l),
                              "w_br_lru": (w_br_lru, l), "w_out": (w_out, l)})
    ffn_w = [(ffn_w_gate[m].astype(BF16), ffn_w_up[m].astype(BF16), ffn_w_down[m].astype(BF16))
             for m in range(ffn_w_gate.shape[0])]
    ffn_w_f32 = [(ffn_w_gate[m], ffn_w_up[m], ffn_w_down[m]) for m in range(ffn_w_gate.shape[0])]
    moe_w = [(jnp.pad(moe_router[m], ((0, 0), (0, LANES - N_EXPERTS))),
              moe_w_gate[m].astype(BF16), moe_w_up[m].astype(BF16), moe_w_down[m].astype(BF16))
             for m in range(moe_router.shape[0])]

    x_prompt, moe_w = lax.optimization_barrier((x_prompt, moe_w))

    n_p = batch * seq
    xp = x_prompt.reshape(n_p, D_MODEL)
    xd = x_sample.reshape(dec_batch, D_MODEL)
    p_states, s_states = [], []
    for l in range(DEPTH):
        g, b = _row(ln2_g[l]), _row(ln2_b[l])
        dense = l % 2 == 0
        wr, wg, wu, wd = (None,) * 4 if dense else moe_w[l // 2]
        xp, sp, sc, slc, sh, *routing = _mixer_seq(
            xp, layers[l], batch, seq, None if dense else wr.astype(BF16),
            extra_rows=0 if dense else dec_batch)
        p_states.append((sp, sc, slc, sh))
        xd, sp, sc, slc, sh = _mixer_step(xd, state_pool[l], state_conv[l], state_lru_conv[l],
                                          state_lru_h[l], layers_f32[l], PAST_LEN)
        s_states.append((sp, sc, slc, sh))
        if dense:
            xp = _ffn(xp, *ffn_w[l // 2], g, b, precise=False)
            xd = _ffn(xd, *ffn_w_f32[l // 2], g, b, precise=True)
        else:
            rows = lax.dynamic_update_slice(xp, xd, (n_p, 0))
            xp, xd = _moe(rows, [(n_p, *routing), (dec_batch, *_router(xd, wr))],
                          wg, wu, wd, g, b, MOE_TILE_SEQ)
    y_prompt = xp.reshape(batch, seq, D_MODEL)
    y_sample = xd.reshape(dec_batch, 1, D_MODEL)

    def stack(states, k):
        return jnp.stack([s[k] for s in states])

    return (y_prompt, y_sample,
            stack(p_states, 0), stack(p_states, 1), stack(p_states, 2), stack(p_states, 3),
            stack(s_states, 0), stack(s_states, 1), stack(s_states, 2), stack(s_states, 3))
```

```python
import functools

import jax
import jax.numpy as jnp
from jax import lax
from jax.experimental import pallas as pl
from jax.experimental.pallas import tpu as pltpu
from jax.experimental.pallas import tpu_sc as plsc

D_MODEL = 1024
DEPTH = 2
PAST_LEN = 16384
D_POOL = 512
POOL_GROUP = 128
POOL_WINDOWS = (2, 4, 8, 16)
POOL_BUF = 15
D_CONV = 512
CONV_WIDTH = 31
CONV_BUF = 30
D_LRU = 512
LRU_CONV_WIDTH = 4
LRU_CONV_BUF = 3
LRU_C = 8.0
N_EXPERTS = 8
TOP_K = 2
ALPHA = (2.0 * DEPTH) ** 0.25
LN_EPS = 1e-5

O_POOL = 0
O_VAL = D_POOL
O_GLU = O_VAL + D_CONV
O_LRU = O_GLU + D_CONV
O_GELU = O_LRU + D_LRU
O_GATE = O_GELU + D_LRU

SUBLANES = 8
LANES = 128
VMEM_LIMIT_BYTES = 56 * 1024 * 1024
SC_CORES = 2
SC_SUBCORES = 16
SC_WINDOW = 128
SC_SUB_ROWS = 32

SEQ_TILE = 512
SCAN_UNROLL = 2
CONV_ROWS = 128
POOL_OFF = 16
CONV_OFF = 32
LCONV_OFF = 8
FFN_CHUNK = 256
PIECE = 256
STEP_COLS = 512
EXPERT_F_TILE = 3584
MOE_TILE_SEQ = 512

BF16 = jnp.bfloat16
F32 = jnp.float32


def _sigmoid(x):
    return 0.5 * jnp.tanh(0.5 * x) + 0.5


def _silu(x):
    return x * _sigmoid(x)


def _gelu_tanh(x):
    return x * (0.5 * (1.0 + jnp.tanh(0.7978845608028654 * (x + 0.044715 * (x * x * x)))))


def _softplus(z):
    return jnp.maximum(z, 0.0) + jnp.log1p(jnp.exp(-jnp.abs(z)))


def _layer_norm(x, g, b):
    mu = jnp.mean(x, axis=-1, keepdims=True)
    xc = x - mu
    var = jnp.mean(xc * xc, axis=-1, keepdims=True)
    return xc * lax.rsqrt(var + LN_EPS) * g + b


def _dot(a, b):
    return jnp.dot(a, b, preferred_element_type=F32)


def _mm(a, w):
    return _dot(a.astype(BF16), w)


def _mm_hi(a, w):
    return jnp.dot(a, w, preferred_element_type=F32, precision=lax.Precision.HIGHEST)


def _lru_coeffs(xc, r, ig, lam, reset):
    log_a = (-LRU_C * r) * _softplus(-lam)
    a = jnp.exp(log_a)
    mult = jnp.sqrt(-jnp.tanh(log_a) * (a * a + 1.0))
    if reset is not None:
        a = jnp.where(reset, 0.0, a)
        mult = jnp.where(reset, 1.0, mult)
    return a, mult * ig * xc


def _mixer_seq_kernel(*refs, tm, with_router, batch, n_in):
    seq_row = pl.program_id(0) < batch

    @pl.when(jnp.logical_not(seq_row))
    def _():
        y_ref = refs[n_in]
        y_ref[...] = jnp.zeros(y_ref.shape, F32)

    @pl.when(seq_row)
    def _():
        _mixer_seq_body(*refs, tm=tm, with_router=with_router)


def _mixer_seq_body(x_ref, w_in_ref, b_gate_ref, w_pool_ref, pool_scale_ref,
                    conv_w_ref, conv_b_ref, cln_g_ref, cln_b_ref,
                    lconv_w_ref, lconv_b_ref, wa_ref, ba_ref, wx_ref, bx_ref, lam_ref,
                    wbp_ref, wbc_ref, wbl_ref, w_out_ref, ln_g_ref, ln_b_ref, *rest, tm, with_router):
    if with_router:
        wr_ref, rest = rest[0], rest[1:]
        idx_ref, gate_ref, rest = rest[5], rest[6], rest[:5] + rest[7:]
    (y_ref, sp_ref, sc_ref, slc_ref, sh_ref,
     pool_ext, conv_ext, lconv_ext, act_buf, a_buf, b_buf, h_carry, gate_buf, gelu_buf,
     merged_buf, xb_buf) = rest
    i = pl.program_id(1)

    @pl.when(i == 0)
    def _():
        pool_ext[0:POOL_OFF, :] = jnp.zeros((POOL_OFF, D_POOL), F32)
        conv_ext[0:CONV_OFF, :] = jnp.zeros((CONV_OFF, D_CONV), F32)
        conv_ext[tm + CONV_OFF:tm + CONV_OFF + SUBLANES, :] = jnp.zeros((SUBLANES, D_CONV), F32)
        lconv_ext[0:LCONV_OFF, :] = jnp.zeros((LCONV_OFF, D_LRU), F32)
        h_carry[...] = jnp.zeros((1, D_LRU), F32)

    xb_buf[...] = x_ref[...].astype(BF16)

    def proj(lo, width):
        return _dot(xb_buf[...], w_in_ref[:, lo:lo + width])

    pos = i * tm + lax.broadcasted_iota(jnp.int32, (tm, 1), 0)

    glu = proj(O_VAL, D_CONV) * _sigmoid(proj(O_GLU, D_CONV))
    conv_ext[CONV_OFF:CONV_OFF + tm, :] = glu

    def pool_piece():
        u_pool = proj(O_POOL, D_POOL)
        pool_ext[POOL_OFF:POOL_OFF + tm, :] = u_pool
        parts = []
        for g, w in enumerate(POOL_WINDOWS):
            lo = g * POOL_GROUP
            acc = u_pool[:, lo:lo + POOL_GROUP]
            for k in range(1, w):
                acc = acc + pool_ext[POOL_OFF - k:POOL_OFF - k + tm, lo:lo + POOL_GROUP]
            inv_cnt = 1.0 / jnp.minimum(w, pos + 1).astype(F32)
            pooled = acc * inv_cnt - u_pool[:, lo:lo + POOL_GROUP]
            parts.append(_dot(pooled.astype(BF16), w_pool_ref[g]))
        mixed = jnp.concatenate(parts, axis=1) * pool_scale_ref[...]
        merged_buf[...] = gate_buf[:, 0:D_MODEL] * _dot(mixed.astype(BF16), wbp_ref[...])
        sp_ref[0] = pool_ext[tm + POOL_OFF - POOL_BUF:tm + POOL_OFF, :]
        pool_ext[0:POOL_OFF, :] = pool_ext[tm:tm + POOL_OFF, :]

    def lru_piece(lo):
        lconv_ext[LCONV_OFF:LCONV_OFF + tm, lo:lo + PIECE] = proj(O_LRU + lo, PIECE)

    def gelu_piece(lo):
        gelu_buf[:, lo:lo + PIECE] = _gelu_tanh(proj(O_GELU + lo, PIECE))

    def gate_piece(lo):
        gate_buf[:, lo:lo + PIECE] = _sigmoid(
            proj(O_GATE + lo, PIECE) + b_gate_ref[:, lo:lo + PIECE])

    pieces = ([functools.partial(gate_piece, lo) for lo in range(0, D_MODEL, PIECE)]
              + [pool_piece]
              + [functools.partial(lru_piece, lo) for lo in range(0, D_LRU, PIECE)]
              + [functools.partial(gate_piece, lo) for lo in range(D_MODEL, 2 * D_MODEL, PIECE)]
              + [functools.partial(gelu_piece, lo) for lo in range(0, D_LRU, PIECE)]
              + [functools.partial(gate_piece, lo) for lo in range(2 * D_MODEL, 3 * D_MODEL, PIECE)])
    n_conv_chunks = tm // CONV_ROWS
    base = CONV_OFF - CONV_BUF
    rows = CONV_ROWS + SUBLANES
    n_units = n_conv_chunks * (D_CONV // LANES)
    unit = 0
    for c in range(n_conv_chunks):
        c0 = c * CONV_ROWS
        for l0 in range(0, D_CONV, LANES):
            acc = jnp.zeros((CONV_ROWS, LANES), F32) + conv_b_ref[:, l0:l0 + LANES]
            for r in range(SUBLANES):
                z = None
                for k in range(CONV_WIDTH):
                    if (base + k) % SUBLANES != r:
                        continue
                    q8 = base + k - r
                    t = (conv_ext[c0 + q8:c0 + q8 + rows, l0:l0 + LANES]
                         * conv_w_ref[k:k + 1, l0:l0 + LANES])
                    z = t if z is None else z + t
                acc = acc + z[r:r + CONV_ROWS, :]
            act_buf[c0:c0 + CONV_ROWS, l0:l0 + LANES] = acc
            for piece in pieces[unit * len(pieces) // n_units:(unit + 1) * len(pieces) // n_units]:
                piece()
            unit += 1
        act_buf[c0:c0 + CONV_ROWS, :] = _silu(_layer_norm(
            act_buf[c0:c0 + CONV_ROWS, :], cln_g_ref[...], cln_b_ref[...]))
    merged_buf[...] += gate_buf[:, D_MODEL:2 * D_MODEL] * _dot(
        act_buf[...].astype(BF16), wbc_ref[...])
    sc_ref[0] = conv_ext[tm + CONV_OFF - CONV_BUF:tm + CONV_OFF, :]
    conv_ext[0:CONV_OFF, :] = conv_ext[tm:tm + CONV_OFF, :]

    base = LCONV_OFF - LRU_CONV_BUF
    xc = jnp.zeros((tm, D_LRU), F32) + lconv_b_ref[...]
    for k in range(LRU_CONV_WIDTH):
        xc = xc + lconv_ext[base + k:base + k + tm, :] * lconv_w_ref[k:k + 1, :]
    slc_ref[0] = lconv_ext[tm + LCONV_OFF - LRU_CONV_BUF:tm + LCONV_OFF, :]
    lconv_ext[0:LCONV_OFF, :] = lconv_ext[tm:tm + LCONV_OFF, :]
    xcb = xc.astype(BF16)
    r = _sigmoid(_dot(xcb, wa_ref[...]) + ba_ref[...])
    ig = _sigmoid(_dot(xcb, wx_ref[...]) + bx_ref[...])
    a, b = _lru_coeffs(xc, r, ig, lam_ref[...], pos == 0)
    a_buf[...] = a
    b_buf[...] = b

    row8 = lax.broadcasted_iota(jnp.int32, (SUBLANES, D_LRU), 0)

    def scan_block(j, h):
        r0 = pl.multiple_of(j * SUBLANES, SUBLANES)
        a8 = a_buf[pl.ds(r0, SUBLANES), :]
        b8 = b_buf[pl.ds(r0, SUBLANES), :]
        for k in (1, 2, 4):
            a_s = pltpu.roll(a8, k, 0)
            b_s = pltpu.roll(b8, k, 0)
            m = row8 >= k
            b8 = jnp.where(m, a8 * b_s + b8, b8)
            a8 = jnp.where(m, a8 * a_s, a8)
        h8 = a8 * h + b8
        a_buf[pl.ds(r0, SUBLANES), :] = h8
        return h8[SUBLANES - 1:SUBLANES, :]

    h_last = lax.fori_loop(0, tm // SUBLANES, scan_block, h_carry[...], unroll=SCAN_UNROLL)
    h_carry[...] = h_last
    sh_ref[0] = h_last
    hg = a_buf[...] * gelu_buf[...]
    merged = merged_buf[...] + gate_buf[:, 2 * D_MODEL:3 * D_MODEL] * _dot(
        hg.astype(BF16), wbl_ref[...])

    m_out = _dot(merged.astype(BF16), w_out_ref[...])
    y = _layer_norm(ALPHA * x_ref[...] + m_out, ln_g_ref[...], ln_b_ref[...])
    y_ref[...] = y
    if with_router:
        idx_ref[...], gate_ref[...] = _route(y, wr_ref[...], precise=False)


def _mixer_step_kernel(x_ref, st_pool_ref, st_conv_ref, st_lconv_ref, st_h_ref,
                       w_in_ref, b_gate_ref, w_pool_ref, pool_scale_ref,
                       conv_w_ref, conv_b_ref, cln_g_ref, cln_b_ref,
                       lconv_w_ref, lconv_b_ref, wa_ref, ba_ref, wx_ref, bx_ref, lam_ref,
                       wbp_ref, wbc_ref, wbl_ref, w_out_ref, ln_g_ref, ln_b_ref,
                       y_ref, up_ref, glu_ref, ul_ref, sh_ref, u_buf, *, start_pos, n_blocks):
    k = pl.program_id(0)
    u_buf[k] = _mm_hi(x_ref[...], w_in_ref[...])

    @pl.when(k == n_blocks - 1)
    def _():
        x = x_ref[...]

        def proj(lo):
            return u_buf[lo // STEP_COLS]

        def gate(n):
            lo = n * D_MODEL
            u = jnp.concatenate([proj(O_GATE + lo), proj(O_GATE + lo + STEP_COLS)], axis=1)
            return _sigmoid(u + b_gate_ref[:, lo:lo + D_MODEL])

        u_pool = proj(O_POOL)
        parts = []
        for g, w in enumerate(POOL_WINDOWS):
            lo = g * POOL_GROUP
            acc = u_pool[:, lo:lo + POOL_GROUP]
            for j in range(1, w):
                acc = acc + st_pool_ref[POOL_BUF - j, :, lo:lo + POOL_GROUP]
            pooled = acc * (1.0 / min(w, start_pos + 1)) - u_pool[:, lo:lo + POOL_GROUP]
            parts.append(_mm_hi(pooled, w_pool_ref[g]))
        mixed = jnp.concatenate(parts, axis=1) * pool_scale_ref[...]
        merged = gate(0) * _mm_hi(mixed, wbp_ref[...])
        up_ref[...] = u_pool

        glu = proj(O_VAL) * _sigmoid(proj(O_GLU))
        acc = glu * conv_w_ref[CONV_BUF:CONV_BUF + 1, :] + conv_b_ref[...]
        for j in range(CONV_BUF):
            acc = acc + st_conv_ref[j] * conv_w_ref[j:j + 1, :]
        act = _silu(_layer_norm(acc, cln_g_ref[...], cln_b_ref[...]))
        merged = merged + gate(1) * _mm_hi(act, wbc_ref[...])
        glu_ref[...] = glu

        u_lru = proj(O_LRU)
        xc = u_lru * lconv_w_ref[LRU_CONV_BUF:LRU_CONV_BUF + 1, :] + lconv_b_ref[...]
        for j in range(LRU_CONV_BUF):
            xc = xc + st_lconv_ref[j] * lconv_w_ref[j:j + 1, :]
        ul_ref[...] = u_lru
        r = _sigmoid(_mm_hi(xc, wa_ref[...]) + ba_ref[...])
        ig = _sigmoid(_mm_hi(xc, wx_ref[...]) + bx_ref[...])
        reset = jnp.full(xc.shape, True) if start_pos == 0 else None
        a, b = _lru_coeffs(xc, r, ig, lam_ref[...], reset)
        h = a * st_h_ref[...] + b
        sh_ref[...] = h
        hg = h * _gelu_tanh(proj(O_GELU))
        merged = merged + gate(2) * _mm_hi(hg, wbl_ref[...])

        m_out = _mm_hi(merged, w_out_ref[...])
        y_ref[...] = _layer_norm(ALPHA * x + m_out, ln_g_ref[...], ln_b_ref[...])


def _const_spec(shape):
    nd = len(shape)
    return pl.BlockSpec(shape, lambda *_: (0,) * nd, pipeline_mode=pl.Buffered(1))


def _mixer_weight_list(p):
    return [p["w_in"], p["b_gate"], p["w_pool"], p["pool_scale"], p["conv_w"], p["conv_b"],
            p["conv_ln_g"], p["conv_ln_b"], p["lru_conv_w"], p["lru_conv_b"], p["lru_wa"],
            p["lru_ba"], p["lru_wx"], p["lru_bx"], p["lru_lambda"], p["w_br_pool"],
            p["w_br_conv"], p["w_br_lru"], p["w_out"], p["ln1_g"], p["ln1_b"]]


def _mixer_seq(x, p, batch, seq, wr_pad=None, extra_rows=0):
    tm = min(SEQ_TILE, seq)
    nt = seq // tm
    assert extra_rows <= tm
    with_router = wr_pad is not None
    weights = _mixer_weight_list(p) + ([wr_pad] if with_router else [])
    last = batch * nt - 1

    def row_block(b, i):
        return (jnp.minimum(b * nt + i, last), 0)

    def y_block(b, i):
        return (jnp.where(b < batch, b * nt + i, batch * nt), 0)

    def state_block(b, i):
        return (jnp.minimum(b, batch - 1), 0, 0)

    in_specs = [pl.BlockSpec((tm, D_MODEL), row_block)]
    in_specs += [_const_spec(w.shape) for w in weights]
    out_shape = [
        jax.ShapeDtypeStruct((batch * seq + extra_rows, D_MODEL), F32),
        jax.ShapeDtypeStruct((batch, POOL_BUF, D_POOL), F32),
        jax.ShapeDtypeStruct((batch, CONV_BUF, D_CONV), F32),
        jax.ShapeDtypeStruct((batch, LRU_CONV_BUF, D_LRU), F32),
        jax.ShapeDtypeStruct((batch, 1, D_LRU), F32),
    ]
    out_specs = [
        pl.BlockSpec((tm, D_MODEL), y_block),
        pl.BlockSpec((1, POOL_BUF, D_POOL), state_block),
        pl.BlockSpec((1, CONV_BUF, D_CONV), state_block),
        pl.BlockSpec((1, LRU_CONV_BUF, D_LRU), state_block),
        pl.BlockSpec((1, 1, D_LRU), state_block),
    ]
    if with_router:
        out_shape += [jax.ShapeDtypeStruct((batch * seq, LANES), jnp.int32),
                      jax.ShapeDtypeStruct((batch * seq, LANES), F32)]
        out_specs += [pl.BlockSpec((tm, LANES), row_block),
                      pl.BlockSpec((tm, LANES), row_block)]
    scratch = [
        pltpu.VMEM((tm + POOL_OFF, D_POOL), F32),
        pltpu.VMEM((tm + CONV_OFF + SUBLANES, D_CONV), F32),
        pltpu.VMEM((tm + LCONV_OFF, D_LRU), F32),
        pltpu.VMEM((tm, D_CONV), F32),
        pltpu.VMEM((tm, D_LRU), F32),
        pltpu.VMEM((tm, D_LRU), F32),
        pltpu.VMEM((1, D_LRU), F32),
        pltpu.VMEM((tm, 3 * D_MODEL), F32),
        pltpu.VMEM((tm, D_LRU), F32),
        pltpu.VMEM((tm, D_MODEL), F32),
        pltpu.VMEM((tm, D_MODEL), BF16),
    ]
    y, sp, sc, slc, sh, *routing = pl.pallas_call(
        functools.partial(_mixer_seq_kernel, tm=tm, with_router=with_router, batch=batch,
                          n_in=1 + len(weights)),
        grid=(batch + (1 if extra_rows else 0), nt),
        in_specs=in_specs,
        out_specs=out_specs,
        out_shape=out_shape,
        scratch_shapes=scratch,
        compiler_params=pltpu.CompilerParams(
            dimension_semantics=("arbitrary", "arbitrary"),
            vmem_limit_bytes=VMEM_LIMIT_BYTES),
        name="mixer_seq",
    )(x, *weights)
    return (y, sp, sc, slc, sh.reshape(batch, D_LRU), *routing)


def _mixer_step(x, st_pool, st_conv, st_lconv, st_h, p, start_pos):
    batch = x.shape[0]
    weights = _mixer_weight_list(p)
    states = [jnp.transpose(st_pool, (1, 0, 2)), jnp.transpose(st_conv, (1, 0, 2)),
              jnp.transpose(st_lconv, (1, 0, 2)), st_h]

    def spec(a):
        if not isinstance(a, tuple):
            return _const_spec(a.shape)
        arr, layer = a
        zeros = (0,) * (arr.ndim - 1)
        return pl.BlockSpec((None,) + arr.shape[1:], lambda k: (layer,) + zeros,
                            pipeline_mode=pl.Buffered(1))

    w_in_all, layer = weights[0]
    in_cols = w_in_all.shape[2]
    n_blocks = in_cols // STEP_COLS
    assert n_blocks * STEP_COLS == in_cols
    in_specs = [spec(a) for a in [x] + states + weights]
    in_specs[1 + len(states)] = pl.BlockSpec((None, D_MODEL, STEP_COLS), lambda k: (layer, 0, k))
    args = [x] + states + [w[0] if isinstance(w, tuple) else w for w in weights]
    out_shape = (
        jax.ShapeDtypeStruct((batch, D_MODEL), F32),
        jax.ShapeDtypeStruct((batch, D_POOL), F32),
        jax.ShapeDtypeStruct((batch, D_CONV), F32),
        jax.ShapeDtypeStruct((batch, D_LRU), F32),
        jax.ShapeDtypeStruct((batch, D_LRU), F32),
    )
    y, u_pool, glu, u_lru, sh = pl.pallas_call(
        functools.partial(_mixer_step_kernel, start_pos=start_pos, n_blocks=n_blocks),
        grid=(n_blocks,),
        in_specs=in_specs,
        out_specs=tuple(pl.BlockSpec(s.shape, lambda k: (0, 0)) for s in out_shape),
        out_shape=out_shape,
        scratch_shapes=[pltpu.VMEM((n_blocks, batch, STEP_COLS), F32)],
        compiler_params=pltpu.CompilerParams(
            dimension_semantics=("arbitrary",), vmem_limit_bytes=VMEM_LIMIT_BYTES),
        name="mixer_step",
    )(*args)

    def push(state, row):
        return jnp.concatenate([state[:, 1:], row[:, None]], axis=1)

    return y, push(st_pool, u_pool), push(st_conv, glu), push(st_lconv, u_lru), sh


def _ffn_kernel(x_ref, wg_ref, wu_ref, wd_ref, g_ref, b_ref, y_ref, *, d_ff, precise):
    x = x_ref[...]
    mm = _mm_hi if precise else _dot
    xb = x if precise else x.astype(BF16)
    acc = jnp.zeros(x.shape, F32)
    for c0 in range(0, d_ff, FFN_CHUNK):
        h = _silu(mm(xb, wg_ref[:, c0:c0 + FFN_CHUNK])) * mm(xb, wu_ref[:, c0:c0 + FFN_CHUNK])
        acc = acc + mm(h if precise else h.astype(BF16), wd_ref[c0:c0 + FFN_CHUNK, :])
    y_ref[...] = _layer_norm(ALPHA * x + acc, g_ref[...], b_ref[...])


def _ffn(x, wg, wu, wd, g, b, precise):
    n = x.shape[0]
    tm = min(SEQ_TILE, n)
    d_ff = wg.shape[1]
    weights = [wg, wu, wd, g, b]
    return pl.pallas_call(
        functools.partial(_ffn_kernel, d_ff=d_ff, precise=precise),
        grid=(n // tm,),
        in_specs=[pl.BlockSpec((tm, D_MODEL), lambda i: (i, 0))] + [_const_spec(w.shape) for w in weights],
        out_specs=pl.BlockSpec((tm, D_MODEL), lambda i: (i, 0)),
        out_shape=jax.ShapeDtypeStruct((n, D_MODEL), F32),
        compiler_params=pltpu.CompilerParams(
            dimension_semantics=("arbitrary",), vmem_limit_bytes=VMEM_LIMIT_BYTES),
        name="ffn_dense",
    )(x, *weights)


def _route(x, wr, precise):
    logits = _mm_hi(x, wr) if precise else _mm(x, wr)
    lane = lax.broadcasted_iota(jnp.int32, logits.shape, 1)
    neg = jnp.float32(-jnp.inf)
    l1 = jnp.where(lane < N_EXPERTS, logits, neg)
    v1 = jnp.max(l1, axis=-1, keepdims=True)
    i1 = jnp.min(jnp.where(l1 == v1, lane, LANES), axis=-1, keepdims=True)
    l2 = jnp.where(lane == i1, neg, l1)
    v2 = jnp.max(l2, axis=-1, keepdims=True)
    i2 = jnp.min(jnp.where(l2 == v2, lane, LANES), axis=-1, keepdims=True)
    e2 = jnp.exp(v2 - v1)
    den = 1.0 + e2
    idx = jnp.where(lane == 0, i1, jnp.where(lane == 1, i2, 0))
    gates = jnp.where(lane == 0, 1.0 / den, jnp.where(lane == 1, e2 / den, 0.0))
    return idx, gates


def _router_kernel(x_ref, wr_ref, idx_ref, gate_ref):
    idx_ref[...], gate_ref[...] = _route(x_ref[...], wr_ref[...], precise=True)


def _router(x, wr_pad):
    n = x.shape[0]
    tm = min(SEQ_TILE, n)
    return pl.pallas_call(
        _router_kernel,
        grid=(n // tm,),
        in_specs=[pl.BlockSpec((tm, D_MODEL), lambda i: (i, 0)), _const_spec(wr_pad.shape)],
        out_specs=(pl.BlockSpec((tm, LANES), lambda i: (i, 0)),
                   pl.BlockSpec((tm, LANES), lambda i: (i, 0))),
        out_shape=(jax.ShapeDtypeStruct((n, LANES), jnp.int32),
                   jax.ShapeDtypeStruct((n, LANES), F32)),
        compiler_params=pltpu.CompilerParams(dimension_semantics=("arbitrary",)),
        name="moe_router",
    )(x, wr_pad)


def _sc_row_gather(src, idx):
    n_idx = idx.shape[0]
    assert n_idx % SC_WINDOW == 0
    n_win = n_idx // SC_WINDOW
    n_workers = SC_CORES * SC_SUBCORES
    per = pl.cdiv(n_win, n_workers)
    n_sub = SC_WINDOW // SC_SUB_ROWS
    assert n_sub >= 2 and n_sub % 2 == 0
    mesh = plsc.VectorSubcoreMesh(core_axis_name="c", subcore_axis_name="s",
                                  num_cores=SC_CORES, num_subcores=SC_SUBCORES)

    def body(src_hbm, idx_hbm, out_hbm, idx_v, buf, g0, g1, w0, w1):
        gsem = (g0, g1)
        wsem = (w0, w1)
        worker = lax.axis_index("c") * SC_SUBCORES + lax.axis_index("s")
        pltpu.sync_copy(idx_hbm.at[:, pl.ds(worker * (per * SC_WINDOW), per * SC_WINDOW)], idx_v)
        count = jnp.clip(n_win - worker * per, 0, per)

        def out_rows(t, q):
            return out_hbm.at[pl.ds((worker * per + t) * SC_WINDOW + q * SC_SUB_ROWS, SC_SUB_ROWS)]

        def fetch(t, q):
            return pltpu.async_copy(
                src_hbm.at[idx_v.at[0, pl.ds(t * SC_WINDOW + q * SC_SUB_ROWS, SC_SUB_ROWS)]],
                buf.at[q % 2], gsem[q % 2])

        def put(t, q):
            return pltpu.async_copy(buf.at[q % 2], out_rows(t, q), wsem[q % 2])

        def wait_put(t, q):
            pltpu.make_async_copy(buf.at[q % 2], out_rows(t, q), wsem[q % 2]).wait()

        @pl.loop(0, count)
        def _(t):
            @pl.when(t > 0)
            def _():
                wait_put(t - 1, n_sub - 2)

            fetches = {0: fetch(t, 0)}
            puts = {}
            for q in range(n_sub):
                if q + 1 < n_sub:
                    if q >= 1:
                        puts[q - 1].wait()
                    else:
                        @pl.when(t > 0)
                        def _():
                            wait_put(t - 1, n_sub - 1)
                    fetches[q + 1] = fetch(t, q + 1)
                fetches[q].wait()
                puts[q] = put(t, q)

        @pl.when(count > 0)
        def _():
            wait_put(count - 1, n_sub - 2)
            wait_put(count - 1, n_sub - 1)

    idx_rows = jnp.pad(idx, (0, n_workers * per * SC_WINDOW - n_idx)).reshape(1, -1)
    return pl.kernel(
        body,
        out_type=jax.ShapeDtypeStruct((n_idx, D_MODEL), src.dtype),
        mesh=mesh,
        scratch_types=[pltpu.VMEM((1, per * SC_WINDOW), jnp.int32),
                       pltpu.VMEM((2, SC_SUB_ROWS, D_MODEL), src.dtype),
                       pltpu.SemaphoreType.DMA, pltpu.SemaphoreType.DMA,
                       pltpu.SemaphoreType.DMA, pltpu.SemaphoreType.DMA],
        cost_estimate=pl.CostEstimate(
            flops=0, transcendentals=0,
            bytes_accessed=n_idx * (2 * D_MODEL * src.dtype.itemsize + idx.dtype.itemsize)),
        name="sc_row_gather",
    )(src, idx_rows)


def _expert_kernel(te_ref, nu_ref, xs_ref, wg_ref, wu_ref, wd_ref, ys_ref, xs_bf, acc, *, tf, nf):
    del te_ref
    i = pl.program_id(0)
    j = pl.program_id(1)
    used = i < nu_ref[0]

    @pl.when(jnp.logical_and(used, j == 0))
    def _():
        xs_bf[...] = xs_ref[...].astype(BF16)

    @pl.when(used)
    def _():
        xb = xs_bf[...]
        part = None
        for c0 in range(0, tf, FFN_CHUNK):
            h = _silu(_dot(xb, wg_ref[:, c0:c0 + FFN_CHUNK])) * _dot(xb, wu_ref[:, c0:c0 + FFN_CHUNK])
            d = _dot(h.astype(BF16), wd_ref[c0:c0 + FFN_CHUNK, :])
            part = d if part is None else part + d

        if nf == 1:
            ys_ref[...] = part
        else:
            @pl.when(j == 0)
            def _():
                acc[...] = part

            @pl.when(jnp.logical_and(j > 0, j < nf - 1))
            def _():
                acc[...] += part

            @pl.when(j == nf - 1)
            def _():
                ys_ref[...] = acc[...] + part

    @pl.when(jnp.logical_and(jnp.logical_not(used), j == nf - 1))
    def _():
        ys_ref[...] = jnp.zeros(ys_ref.shape, F32)


def _experts(xs, tile_expert, n_used, wg, wu, wd, tm):
    n_tiles = tile_expert.shape[0]
    d_e = wg.shape[2]
    tf = EXPERT_F_TILE
    nf = d_e // tf
    assert nf * tf == d_e and tf % FFN_CHUNK == 0

    def jeff(i, j, nu):
        return jnp.where(i < nu[0], j, nf - 1)

    grid_spec = pltpu.PrefetchScalarGridSpec(
        num_scalar_prefetch=2,
        grid=(n_tiles, nf),
        in_specs=[
            pl.BlockSpec((tm, D_MODEL), lambda i, j, te, nu: (jnp.minimum(i, nu[0] - 1), 0)),
            pl.BlockSpec((None, D_MODEL, tf), lambda i, j, te, nu: (te[i], 0, jeff(i, j, nu))),
            pl.BlockSpec((None, D_MODEL, tf), lambda i, j, te, nu: (te[i], 0, jeff(i, j, nu))),
            pl.BlockSpec((None, tf, D_MODEL), lambda i, j, te, nu: (te[i], jeff(i, j, nu), 0)),
        ],
        out_specs=pl.BlockSpec((tm, D_MODEL), lambda i, j, te, nu: (i, 0)),
        scratch_shapes=[
            pltpu.VMEM((tm, D_MODEL), BF16),
            pltpu.VMEM((tm if nf > 1 else SUBLANES, D_MODEL), F32),
        ],
    )
    return pl.pallas_call(
        functools.partial(_expert_kernel, tf=tf, nf=nf),
        grid_spec=grid_spec,
        out_shape=jax.ShapeDtypeStruct((n_tiles * tm, D_MODEL), F32),
        compiler_params=pltpu.CompilerParams(
            dimension_semantics=("arbitrary", "arbitrary"), vmem_limit_bytes=VMEM_LIMIT_BYTES),
        name="moe_experts",
    )(tile_expert, n_used, xs, wg, wu, wd)


def _combine_kernel(x_ref, gate_ref, y0_ref, y1_ref, g_ref, b_ref, y_ref):
    gates = gate_ref[...]
    f = gates[:, 0:1] * y0_ref[...] + gates[:, 1:2] * y1_ref[...]
    y_ref[...] = _layer_norm(ALPHA * x_ref[...] + f, g_ref[...], b_ref[...])


def _combine(x, gates, y01, g, b, n, x_row0, y0_row0, y1_row0):
    tm = min(SEQ_TILE, n)
    nt = n // tm
    assert nt * tm == n and x_row0 % tm == 0 and y0_row0 % tm == 0 and y1_row0 % tm == 0
    xb0, y0b, y1b = x_row0 // tm, y0_row0 // tm, y1_row0 // tm
    return pl.pallas_call(
        _combine_kernel,
        grid=(nt,),
        in_specs=[
            pl.BlockSpec((tm, D_MODEL), lambda i: (i + xb0, 0)),
            pl.BlockSpec((tm, LANES), lambda i: (i, 0)),
            pl.BlockSpec((tm, D_MODEL), lambda i: (i + y0b, 0)),
            pl.BlockSpec((tm, D_MODEL), lambda i: (i + y1b, 0)),
            _const_spec(g.shape),
            _const_spec(b.shape),
        ],
        out_specs=pl.BlockSpec((tm, D_MODEL), lambda i: (i, 0)),
        out_shape=jax.ShapeDtypeStruct((n, D_MODEL), F32),
        compiler_params=pltpu.CompilerParams(
            dimension_semantics=("arbitrary",), vmem_limit_bytes=VMEM_LIMIT_BYTES),
        name="moe_combine",
    )(x, gates, y01, y01, g, b)


def _moe(x, groups, wg, wu, wd, g, b, tm):
    n = x.shape[0]
    assert sum(rows for rows, _, _ in groups) == n
    idx = jnp.concatenate([ids[:, :TOP_K] for _, ids, _ in groups], axis=0)
    flat_e = idx.reshape(-1)
    onehot = (flat_e[:, None] == jnp.arange(N_EXPERTS, dtype=jnp.int32)[None, :]).astype(jnp.int32)
    csum = jnp.cumsum(onehot, axis=0)
    rank = jnp.sum((csum - onehot) * onehot, axis=1)
    counts = csum[-1]
    padded = ((counts + tm - 1) // tm) * tm
    ends = jnp.cumsum(padded)
    slot = (ends - padded)[flat_e] + rank
    n_pairs = TOP_K * n
    n_tiles = pl.cdiv(n_pairs, tm) + N_EXPERTS
    n_used = (ends[-1] // tm).astype(jnp.int32).reshape(1)
    tile_start = jnp.arange(n_tiles, dtype=jnp.int32) * tm
    tile_expert = jnp.sum((tile_start[:, None] >= ends[None, :]).astype(jnp.int32), axis=1)
    last_e = jnp.sum((ends[-1] - 1 >= ends).astype(jnp.int32))
    tile_expert = jnp.minimum(tile_expert, last_e).astype(jnp.int32)
    pair_sorted = jnp.sort(flat_e * n_pairs + jnp.arange(n_pairs, dtype=jnp.int32)) % n_pairs
    slot_ids = jnp.arange(n_tiles * tm, dtype=jnp.int32)
    slot_e = jnp.repeat(tile_expert, tm)
    local = slot_ids - (ends - padded)[slot_e]
    q = (jnp.cumsum(counts) - counts)[slot_e] + local
    valid = jnp.logical_and(local >= 0, local < counts[slot_e])
    tok_of_slot = jnp.where(valid, pair_sorted[jnp.clip(q, 0, n_pairs - 1)] // TOP_K, slot_ids % n)
    xs = _sc_row_gather(x, tok_of_slot)
    ys = _experts(xs, tile_expert, n_used, wg, wu, wd, tm)
    slot2 = slot.reshape(n, TOP_K)
    order, row0 = [], 0
    for rows, _, _ in groups:
        order += [slot2[row0:row0 + rows, 0], slot2[row0:row0 + rows, 1]]
        row0 += rows
    y01 = _sc_row_gather(ys, jnp.concatenate(order))
    outs, row0 = [], 0
    for rows, _, gates in groups:
        outs.append(_combine(x, gates, y01, g, b, rows, row0, TOP_K * row0, TOP_K * row0 + rows))
        row0 += rows
    return outs


def _block_diag(w):
    n, c, _ = w.shape
    eye = jnp.eye(n, dtype=w.dtype)
    return (eye[:, None, :, None] * w[:, :, None, :]).reshape(n * c, n * c)


def _row(v):
    return v.reshape(1, -1)


def kernel(x_prompt, x_sample, state_pool, state_conv, state_lru_conv, state_lru_h, w_in, b_gate, w_pool, pool_scale, conv_w, conv_b, conv_ln_g, conv_ln_b, lru_conv_w, lru_conv_b, lru_wa, lru_ba, lru_wx, lru_bx, lru_lambda, w_br_pool, w_br_conv, w_br_lru, w_out, ln1_g, ln1_b, ln2_g, ln2_b, ffn_w_gate, ffn_w_up, ffn_w_down, moe_router, moe_w_gate, moe_w_up, moe_w_down):
    batch, seq, _ = x_prompt.shape
    dec_batch = x_sample.shape[0]

    def layer_params(l, mat):
        return {
            "w_in": w_in[l].astype(mat), "b_gate": _row(b_gate[l]),
            "w_pool": w_pool[l].astype(mat), "pool_scale": _row(pool_scale[l]),
            "conv_w": conv_w[l], "conv_b": _row(conv_b[l]),
            "conv_ln_g": _row(conv_ln_g[l]), "conv_ln_b": _row(conv_ln_b[l]),
            "lru_conv_w": lru_conv_w[l], "lru_conv_b": _row(lru_conv_b[l]),
            "lru_wa": _block_diag(lru_wa[l]).astype(mat), "lru_ba": _row(lru_ba[l]),
            "lru_wx": _block_diag(lru_wx[l]).astype(mat), "lru_bx": _row(lru_bx[l]),
            "lru_lambda": _row(lru_lambda[l]),
            "w_br_pool": w_br_pool[l].astype(mat), "w_br_conv": w_br_conv[l].astype(mat),
            "w_br_lru": w_br_lru[l].astype(mat), "w_out": w_out[l].astype(mat),
            "ln1_g": _row(ln1_g[l]), "ln1_b": _row(ln1_b[l]),
        }

    layers = [layer_params(l, BF16) for l in range(DEPTH)]
    layers_f32 = [layer_params(l, F32) for l in range(DEPTH)]
    for l in range(DEPTH):
        layers_f32[l].update({"w_in": (w_in, l), "w_br_pool": (w_br_pool, l), "w_br_conv": (w_br_conv, l),
                              "w_br_lru": (w_br_lru, l), "w_out": (w_out, l)})
    ffn_w = [(ffn_w_gate[m].astype(BF16), ffn_w_up[m].astype(BF16), ffn_w_down[m].astype(BF16))
             for m in range(ffn_w_gate.shape[0])]
    ffn_w_f32 = [(ffn_w_gate[m], ffn_w_up[m], ffn_w_down[m]) for m in range(ffn_w_gate.shape[0])]
    moe_w = [(jnp.pad(moe_router[m], ((0, 0), (0, LANES - N_EXPERTS))),
              moe_w_gate[m].astype(BF16), moe_w_up[m].astype(BF16), moe_w_down[m].astype(BF16))
             for m in range(moe_router.shape[0])]

    x_prompt, moe_w = lax.optimization_barrier((x_prompt, moe_w))

    n_p = batch * seq
    xp = x_prompt.reshape(n_p, D_MODEL)
    xd = x_sample.reshape(dec_batch, D_MODEL)
    p_states, s_states = [], []
    for l in range(DEPTH):
        g, b = _row(ln2_g[l]), _row(ln2_b[l])
        dense = l % 2 == 0
        wr, wg, wu, wd = (None,) * 4 if dense else moe_w[l // 2]
        xp, sp, sc, slc, sh, *routing = _mixer_seq(
            xp, layers[l], batch, seq, None if dense else wr.astype(BF16),
            extra_rows=0 if dense else dec_batch)
        p_states.append((sp, sc, slc, sh))
        xd, sp, sc, slc, sh = _mixer_step(xd, state_pool[l], state_conv[l], state_lru_conv[l],
                                          state_lru_h[l], layers_f32[l], PAST_LEN)
        s_states.append((sp, sc, slc, sh))
        if dense:
            xp = _ffn(xp, *ffn_w[l // 2], g, b, precise=False)
            xd = _ffn(xd, *ffn_w_f32[l // 2], g, b, precise=True)
        else:
            rows = lax.dynamic_update_slice(xp, xd, (n_p, 0))
            xp, xd = _moe(rows, [(n_p, *routing), (dec_batch, *_router(xd, wr))],
                          wg, wu, wd, g, b, MOE_TILE_SEQ)
    y_prompt = xp.reshape(batch, seq, D_MODEL)
    y_sample = xd.reshape(dec_batch, 1, D_MODEL)

    def stack(states, k):
        return jnp.stack([s[k] for s in states])

    return (y_prompt, y_sample,
            stack(p_states, 0), stack(p_states, 1), stack(p_states, 2), stack(p_states, 3),
            stack(s_states, 0), stack(s_states, 1), stack(s_states, 2), stack(s_states, 3))
```

```python
import functools

import jax
import jax.numpy as jnp
from jax import lax
from jax.experimental import pallas as pl
from jax.experimental.pallas import tpu as pltpu
from jax.experimental.pallas import tpu_sc as plsc

D_MODEL = 1024
DEPTH = 2
PAST_LEN = 16384
D_POOL = 512
POOL_GROUP = 128
POOL_WINDOWS = (2, 4, 8, 16)
POOL_BUF = 15
D_CONV = 512
CONV_WIDTH = 31
CONV_BUF = 30
D_LRU = 512
LRU_CONV_WIDTH = 4
LRU_CONV_BUF = 3
LRU_C = 8.0
N_EXPERTS = 8
TOP_K = 2
ALPHA = (2.0 * DEPTH) ** 0.25
LN_EPS = 1e-5

O_POOL = 0
O_VAL = D_POOL
O_GLU = O_VAL + D_CONV
O_LRU = O_GLU + D_CONV
O_GELU = O_LRU + D_LRU
O_GATE = O_GELU + D_LRU

SUBLANES = 8
LANES = 128
VMEM_LIMIT_BYTES = 56 * 1024 * 1024
SC_CORES = 2
SC_SUBCORES = 16
SC_WINDOW = 128
SC_SUB_ROWS = 32

SEQ_TILE = 512
SCAN_UNROLL = 2
CONV_ROWS = 128
POOL_OFF = 16
CONV_OFF = 32
LCONV_OFF = 8
FFN_CHUNK = 256
PIECE = 256
STEP_COLS = 512
EXPERT_F_TILE = 3584
MOE_TILE_SEQ = 512

BF16 = jnp.bfloat16
F32 = jnp.float32


def _sigmoid(x):
    return 0.5 * jnp.tanh(0.5 * x) + 0.5


def _silu(x):
    return x * _sigmoid(x)


def _gelu_tanh(x):
    return x * (0.5 * (1.0 + jnp.tanh(0.7978845608028654 * (x + 0.044715 * (x * x * x)))))


def _softplus(z):
    return jnp.maximum(z, 0.0) + jnp.log1p(jnp.exp(-jnp.abs(z)))


def _layer_norm(x, g, b):
    mu = jnp.mean(x, axis=-1, keepdims=True)
    xc = x - mu
    var = jnp.mean(xc * xc, axis=-1, keepdims=True)
    return xc * lax.rsqrt(var + LN_EPS) * g + b


def _dot(a, b):
    return jnp.dot(a, b, preferred_element_type=F32)


def _mm(a, w):
    return _dot(a.astype(BF16), w)


def _mm_hi(a, w):
    return jnp.dot(a, w, preferred_element_type=F32, precision=lax.Precision.HIGHEST)


def _lru_coeffs(xc, r, ig, lam, reset):
    log_a = (-LRU_C * r) * _softplus(-lam)
    a = jnp.exp(log_a)
    mult = jnp.sqrt(-jnp.tanh(log_a) * (a * a + 1.0))
    if reset is not None:
        a = jnp.where(reset, 0.0, a)
        mult = jnp.where(reset, 1.0, mult)
    return a, mult * ig * xc


def _mixer_seq_kernel(*refs, tm, with_router, batch, n_in):
    seq_row = pl.program_id(0) < batch

    @pl.when(jnp.logical_not(seq_row))
    def _():
        y_ref = refs[n_in]
        y_ref[...] = jnp.zeros(y_ref.shape, F32)

    @pl.when(seq_row)
    def _():
        _mixer_seq_body(*refs, tm=tm, with_router=with_router)


def _mixer_seq_body(x_ref, w_in_ref, b_gate_ref, w_pool_ref, pool_scale_ref,
                    conv_w_ref, conv_b_ref, cln_g_ref, cln_b_ref,
                    lconv_w_ref, lconv_b_ref, wa_ref, ba_ref, wx_ref, bx_ref, lam_ref,
                    wbp_ref, wbc_ref, wbl_ref, w_out_ref, ln_g_ref, ln_b_ref, *rest, tm, with_router):
    if with_router:
        wr_ref, rest = rest[0], rest[1:]
        idx_ref, gate_ref, rest = rest[5], rest[6], rest[:5] + rest[7:]
    (y_ref, sp_ref, sc_ref, slc_ref, sh_ref,
     pool_ext, conv_ext, lconv_ext, act_buf, a_buf, b_buf, h_carry, gate_buf, gelu_buf,
     merged_buf, xb_buf) = rest
    i = pl.program_id(1)

    @pl.when(i == 0)
    def _():
        pool_ext[0:POOL_OFF, :] = jnp.zeros((POOL_OFF, D_POOL), F32)
        conv_ext[0:CONV_OFF, :] = jnp.zeros((CONV_OFF, D_CONV), F32)
        conv_ext[tm + CONV_OFF:tm + CONV_OFF + SUBLANES, :] = jnp.zeros((SUBLANES, D_CONV), F32)
        lconv_ext[0:LCONV_OFF, :] = jnp.zeros((LCONV_OFF, D_LRU), F32)
        h_carry[...] = jnp.zeros((1, D_LRU), F32)

    xb_buf[...] = x_ref[...].astype(BF16)

    def proj(lo, width):
        return _dot(xb_buf[...], w_in_ref[:, lo:lo + width])

    pos = i * tm + lax.broadcasted_iota(jnp.int32, (tm, 1), 0)

    glu = proj(O_VAL, D_CONV) * _sigmoid(proj(O_GLU, D_CONV))
    conv_ext[CONV_OFF:CONV_OFF + tm, :] = glu

    def pool_piece():
        u_pool = proj(O_POOL, D_POOL)
        pool_ext[POOL_OFF:POOL_OFF + tm, :] = u_pool
        parts = []
        for g, w in enumerate(POOL_WINDOWS):
            lo = g * POOL_GROUP
            acc = u_pool[:, lo:lo + POOL_GROUP]
            for k in range(1, w):
                acc = acc + pool_ext[POOL_OFF - k:POOL_OFF - k + tm, lo:lo + POOL_GROUP]
            inv_cnt = 1.0 / jnp.minimum(w, pos + 1).astype(F32)
            pooled = acc * inv_cnt - u_pool[:, lo:lo + POOL_GROUP]
            parts.append(_dot(pooled.astype(BF16), w_pool_ref[g]))
        mixed = jnp.concatenate(parts, axis=1) * pool_scale_ref[...]
        merged_buf[...] = gate_buf[:, 0:D_MODEL] * _dot(mixed.astype(BF16), wbp_ref[...])
        sp_ref[0] = pool_ext[tm + POOL_OFF - POOL_BUF:tm + POOL_OFF, :]
        pool_ext[0:POOL_OFF, :] = pool_ext[tm:tm + POOL_OFF, :]

    def lru_piece(lo):
        lconv_ext[LCONV_OFF:LCONV_OFF + tm, lo:lo + PIECE] = proj(O_LRU + lo, PIECE)

    def gelu_piece(lo):
        gelu_buf[:, lo:lo + PIECE] = _gelu_tanh(proj(O_GELU + lo, PIECE))

    def gate_piece(lo):
        gate_buf[:, lo:lo + PIECE] = _sigmoid(
            proj(O_GATE + lo, PIECE) + b_gate_ref[:, lo:lo + PIECE])

    pieces = ([functools.partial(gate_piece, lo) for lo in range(0, D_MODEL, PIECE)]
              + [pool_piece]
              + [functools.partial(lru_piece, lo) for lo in range(0, D_LRU, PIECE)]
              + [functools.partial(gate_piece, lo) for lo in range(D_MODEL, 2 * D_MODEL, PIECE)]
              + [functools.partial(gelu_piece, lo) for lo in range(0, D_LRU, PIECE)]
              + [functools.partial(gate_piece, lo) for lo in range(2 * D_MODEL, 3 * D_MODEL, PIECE)])
    n_conv_chunks = tm // CONV_ROWS
    base = CONV_OFF - CONV_BUF
    rows = CONV_ROWS + SUBLANES
    n_units = n_conv_chunks * (D_CONV // LANES)
    unit = 0
    for c in range(n_conv_chunks):
        c0 = c * CONV_ROWS
        for l0 in range(0, D_CONV, LANES):
            acc = jnp.zeros((CONV_ROWS, LANES), F32) + conv_b_ref[:, l0:l0 + LANES]
            for r in range(SUBLANES):
                z = None
                for k in range(CONV_WIDTH):
                    if (base + k) % SUBLANES != r:
                        continue
                    q8 = base + k - r
                    t = (conv_ext[c0 + q8:c0 + q8 + rows, l0:l0 + LANES]
                         * conv_w_ref[k:k + 1, l0:l0 + LANES])
                    z = t if z is None else z + t
                acc = acc + z[r:r + CONV_ROWS, :]
            act_buf[c0:c0 + CONV_ROWS, l0:l0 + LANES] = acc
            for piece in pieces[unit * len(pieces) // n_units:(unit + 1) * len(pieces) // n_units]:
                piece()
            unit += 1
        act_buf[c0:c0 + CONV_ROWS, :] = _silu(_layer_norm(
            act_buf[c0:c0 + CONV_ROWS, :], cln_g_ref[...], cln_b_ref[...]))
    merged_buf[...] += gate_buf[:, D_MODEL:2 * D_MODEL] * _dot(
        act_buf[...].astype(BF16), wbc_ref[...])
    sc_ref[0] = conv_ext[tm + CONV_OFF - CONV_BUF:tm + CONV_OFF, :]
    conv_ext[0:CONV_OFF, :] = conv_ext[tm:tm + CONV_OFF, :]

    base = LCONV_OFF - LRU_CONV_BUF
    xc = jnp.zeros((tm, D_LRU), F32) + lconv_b_ref[...]
    for k in range(LRU_CONV_WIDTH):
        xc = xc + lconv_ext[base + k:base + k + tm, :] * lconv_w_ref[k:k + 1, :]
    slc_ref[0] = lconv_ext[tm + LCONV_OFF - LRU_CONV_BUF:tm + LCONV_OFF, :]
    lconv_ext[0:LCONV_OFF, :] = lconv_ext[tm:tm + LCONV_OFF, :]
    xcb = xc.astype(BF16)
    r = _sigmoid(_dot(xcb, wa_ref[...]) + ba_ref[...])
    ig = _sigmoid(_dot(xcb, wx_ref[...]) + bx_ref[...])
    a, b = _lru_coeffs(xc, r, ig, lam_ref[...], pos == 0)
    a_buf[...] = a
    b_buf[...] = b

    row8 = lax.broadcasted_iota(jnp.int32, (SUBLANES, D_LRU), 0)

    def scan_block(j, h):
        r0 = pl.multiple_of(j * SUBLANES, SUBLANES)
        a8 = a_buf[pl.ds(r0, SUBLANES), :]
        b8 = b_buf[pl.ds(r0, SUBLANES), :]
        for k in (1, 2, 4):
            a_s = pltpu.roll(a8, k, 0)
            b_s = pltpu.roll(b8, k, 0)
            m = row8 >= k
            b8 = jnp.where(m, a8 * b_s + b8, b8)
            a8 = jnp.where(m, a8 * a_s, a8)
        h8 = a8 * h + b8
        a_buf[pl.ds(r0, SUBLANES), :] = h8
        return h8[SUBLANES - 1:SUBLANES, :]

    h_last = lax.fori_loop(0, tm // SUBLANES, scan_block, h_carry[...], unroll=SCAN_UNROLL)
    h_carry[...] = h_last
    sh_ref[0] = h_last
    hg = a_buf[...] * gelu_buf[...]
    merged = merged_buf[...] + gate_buf[:, 2 * D_MODEL:3 * D_MODEL] * _dot(
        hg.astype(BF16), wbl_ref[...])

    m_out = _dot(merged.astype(BF16), w_out_ref[...])
    y = _layer_norm(ALPHA * x_ref[...] + m_out, ln_g_ref[...], ln_b_ref[...])
    y_ref[...] = y
    if with_router:
        idx_ref[...], gate_ref[...] = _route(y, wr_ref[...], precise=False)


def _mixer_step_kernel(x_ref, st_pool_ref, st_conv_ref, st_lconv_ref, st_h_ref,
                       w_in_ref, b_gate_ref, w_pool_ref, pool_scale_ref,
                       conv_w_ref, conv_b_ref, cln_g_ref, cln_b_ref,
                       lconv_w_ref, lconv_b_ref, wa_ref, ba_ref, wx_ref, bx_ref, lam_ref,
                       wbp_ref, wbc_ref, wbl_ref, w_out_ref, ln_g_ref, ln_b_ref,
                       y_ref, up_ref, glu_ref, ul_ref, sh_ref, u_buf, *, start_pos, n_blocks):
    k = pl.program_id(0)
    u_buf[k] = _mm_hi(x_ref[...], w_in_ref[...])

    @pl.when(k == n_blocks - 1)
    def _():
        x = x_ref[...]

        def proj(lo):
            return u_buf[lo // STEP_COLS]

        def gate(n):
            lo = n * D_MODEL
            u = jnp.concatenate([proj(O_GATE + lo), proj(O_GATE + lo + STEP_COLS)], axis=1)
            return _sigmoid(u + b_gate_ref[:, lo:lo + D_MODEL])

        u_pool = proj(O_POOL)
        parts = []
        for g, w in enumerate(POOL_WINDOWS):
            lo = g * POOL_GROUP
            acc = u_pool[:, lo:lo + POOL_GROUP]
            for j in range(1, w):
                acc = acc + st_pool_ref[POOL_BUF - j, :, lo:lo + POOL_GROUP]
            pooled = acc * (1.0 / min(w, start_pos + 1)) - u_pool[:, lo:lo + POOL_GROUP]
            parts.append(_mm_hi(pooled, w_pool_ref[g]))
        mixed = jnp.concatenate(parts, axis=1) * pool_scale_ref[...]
        merged = gate(0) * _mm_hi(mixed, wbp_ref[...])
        up_ref[...] = u_pool

        glu = proj(O_VAL) * _sigmoid(proj(O_GLU))
        acc = glu * conv_w_ref[CONV_BUF:CONV_BUF + 1, :] + conv_b_ref[...]
        for j in range(CONV_BUF):
            acc = acc + st_conv_ref[j] * conv_w_ref[j:j + 1, :]
        act = _silu(_layer_norm(acc, cln_g_ref[...], cln_b_ref[...]))
        merged = merged + gate(1) * _mm_hi(act, wbc_ref[...])
        glu_ref[...] = glu

        u_lru = proj(O_LRU)
        xc = u_lru * lconv_w_ref[LRU_CONV_BUF:LRU_CONV_BUF + 1, :] + lconv_b_ref[...]
        for j in range(LRU_CONV_BUF):
            xc = xc + st_lconv_ref[j] * lconv_w_ref[j:j + 1, :]
        ul_ref[...] = u_lru
        r = _sigmoid(_mm_hi(xc, wa_ref[...]) + ba_ref[...])
        ig = _sigmoid(_mm_hi(xc, wx_ref[...]) + bx_ref[...])
        reset = jnp.full(xc.shape, True) if start_pos == 0 else None
        a, b = _lru_coeffs(xc, r, ig, lam_ref[...], reset)
        h = a * st_h_ref[...] + b
        sh_ref[...] = h
        hg = h * _gelu_tanh(proj(O_GELU))
        merged = merged + gate(2) * _mm_hi(hg, wbl_ref[...])

        m_out = _mm_hi(merged, w_out_ref[...])
        y_ref[...] = _layer_norm(ALPHA * x + m_out, ln_g_ref[...], ln_b_ref[...])


def _const_spec(shape):
    nd = len(shape)
    return pl.BlockSpec(shape, lambda *_: (0,) * nd, pipeline_mode=pl.Buffered(1))


def _mixer_weight_list(p):
    return [p["w_in"], p["b_gate"], p["w_pool"], p["pool_scale"], p["conv_w"], p["conv_b"],
            p["conv_ln_g"], p["conv_ln_b"], p["lru_conv_w"], p["lru_conv_b"], p["lru_wa"],
            p["lru_ba"], p["lru_wx"], p["lru_bx"], p["lru_lambda"], p["w_br_pool"],
            p["w_br_conv"], p["w_br_lru"], p["w_out"], p["ln1_g"], p["ln1_b"]]


def _mixer_seq(x, p, batch, seq, wr_pad=None, extra_rows=0):
    tm = min(SEQ_TILE, seq)
    nt = seq // tm
    assert extra_rows <= tm
    with_router = wr_pad is not None
    weights = _mixer_weight_list(p) + ([wr_pad] if with_router else [])
    last = batch * nt - 1

    def row_block(b, i):
        return (jnp.minimum(b * nt + i, last), 0)

    def y_block(b, i):
        return (jnp.where(b < batch, b * nt + i, batch * nt), 0)

    def state_block(b, i):
        return (jnp.minimum(b, batch - 1), 0, 0)

    in_specs = [pl.BlockSpec((tm, D_MODEL), row_block)]
    in_specs += [_const_spec(w.shape) for w in weights]
    out_shape = [
        jax.ShapeDtypeStruct((batch * seq + extra_rows, D_MODEL), F32),
        jax.ShapeDtypeStruct((batch, POOL_BUF, D_POOL), F32),
        jax.ShapeDtypeStruct((batch, CONV_BUF, D_CONV), F32),
        jax.ShapeDtypeStruct((batch, LRU_CONV_BUF, D_LRU), F32),
        jax.ShapeDtypeStruct((batch, 1, D_LRU), F32),
    ]
    out_specs = [
        pl.BlockSpec((tm, D_MODEL), y_block),
        pl.BlockSpec((1, POOL_BUF, D_POOL), state_block),
        pl.BlockSpec((1, CONV_BUF, D_CONV), state_block),
        pl.BlockSpec((1, LRU_CONV_BUF, D_LRU), state_block),
        pl.BlockSpec((1, 1, D_LRU), state_block),
    ]
    if with_router:
        out_shape += [jax.ShapeDtypeStruct((batch * seq, LANES), jnp.int32),
                      jax.ShapeDtypeStruct((batch * seq, LANES), F32)]
        out_specs += [pl.BlockSpec((tm, LANES), row_block),
                      pl.BlockSpec((tm, LANES), row_block)]
    scratch = [
        pltpu.VMEM((tm + POOL_OFF, D_POOL), F32),
        pltpu.VMEM((tm + CONV_OFF + SUBLANES, D_CONV), F32),
        pltpu.VMEM((tm + LCONV_OFF, D_LRU), F32),
        pltpu.VMEM((tm, D_CONV), F32),
        pltpu.VMEM((tm, D_LRU), F32),
        pltpu.VMEM((tm, D_LRU), F32),
        pltpu.VMEM((1, D_LRU), F32),
        pltpu.VMEM((tm, 3 * D_MODEL), F32),
        pltpu.VMEM((tm, D_LRU), F32),
        pltpu.VMEM((tm, D_MODEL), F32),
        pltpu.VMEM((tm, D_MODEL), BF16),
    ]
    y, sp, sc, slc, sh, *routing = pl.pallas_call(
        functools.partial(_mixer_seq_kernel, tm=tm, with_router=with_router, batch=batch,
                          n_in=1 + len(weights)),
        grid=(batch + (1 if extra_rows else 0), nt),
        in_specs=in_specs,
        out_specs=out_specs,
        out_shape=out_shape,
        scratch_shapes=scratch,
        compiler_params=pltpu.CompilerParams(
            dimension_semantics=("arbitrary", "arbitrary"),
            vmem_limit_bytes=VMEM_LIMIT_BYTES),
        name="mixer_seq",
    )(x, *weights)
    return (y, sp, sc, slc, sh.reshape(batch, D_LRU), *routing)


def _mixer_step(x, st_pool, st_conv, st_lconv, st_h, p, start_pos):
    batch = x.shape[0]
    weights = _mixer_weight_list(p)
    states = [jnp.transpose(st_pool, (1, 0, 2)), jnp.transpose(st_conv, (1, 0, 2)),
              jnp.transpose(st_lconv, (1, 0, 2)), st_h]

    def spec(a):
        if not isinstance(a, tuple):
            return _const_spec(a.shape)
        arr, layer = a
        zeros = (0,) * (arr.ndim - 1)
        return pl.BlockSpec((None,) + arr.shape[1:], lambda k: (layer,) + zeros,
                            pipeline_mode=pl.Buffered(1))

    w_in_all, layer = weights[0]
    in_cols = w_in_all.shape[2]
    n_blocks = in_cols // STEP_COLS
    assert n_blocks * STEP_COLS == in_cols
    in_specs = [spec(a) for a in [x] + states + weights]
    in_specs[1 + len(states)] = pl.BlockSpec((None, D_MODEL, STEP_COLS), lambda k: (layer, 0, k))
    args = [x] + states + [w[0] if isinstance(w, tuple) else w for w in weights]
    out_shape = (
        jax.ShapeDtypeStruct((batch, D_MODEL), F32),
        jax.ShapeDtypeStruct((batch, D_POOL), F32),
        jax.ShapeDtypeStruct((batch, D_CONV), F32),
        jax.ShapeDtypeStruct((batch, D_LRU), F32),
        jax.ShapeDtypeStruct((batch, D_LRU), F32),
    )
    y, u_pool, glu, u_lru, sh = pl.pallas_call(
        functools.partial(_mixer_step_kernel, start_pos=start_pos, n_blocks=n_blocks),
        grid=(n_blocks,),
        in_specs=in_specs,
        out_specs=tuple(pl.BlockSpec(s.shape, lambda k: (0, 0)) for s in out_shape),
        out_shape=out_shape,
        scratch_shapes=[pltpu.VMEM((n_blocks, batch, STEP_COLS), F32)],
        compiler_params=pltpu.CompilerParams(
            dimension_semantics=("arbitrary",), vmem_limit_bytes=VMEM_LIMIT_BYTES),
        name="mixer_step",
    )(*args)

    def push(state, row):
        return jnp.concatenate([state[:, 1:], row[:, None]], axis=1)

    return y, push(st_pool, u_pool), push(st_conv, glu), push(st_lconv, u_lru), sh


def _ffn_kernel(x_ref, wg_ref, wu_ref, wd_ref, g_ref, b_ref, y_ref, *, d_ff, precise):
    x = x_ref[...]
    mm = _mm_hi if precise else _dot
    xb = x if precise else x.astype(BF16)
    acc = jnp.zeros(x.shape, F32)
    for c0 in range(0, d_ff, FFN_CHUNK):
        h = _silu(mm(xb, wg_ref[:, c0:c0 + FFN_CHUNK])) * mm(xb, wu_ref[:, c0:c0 + FFN_CHUNK])
        acc = acc + mm(h if precise else h.astype(BF16), wd_ref[c0:c0 + FFN_CHUNK, :])
    y_ref[...] = _layer_norm(ALPHA * x + acc, g_ref[...], b_ref[...])


def _ffn(x, wg, wu, wd, g, b, precise):
    n = x.shape[0]
    tm = min(SEQ_TILE, n)
    d_ff = wg.shape[1]
    weights = [wg, wu, wd, g, b]
    return pl.pallas_call(
        functools.partial(_ffn_kernel, d_ff=d_ff, precise=precise),
        grid=(n // tm,),
        in_specs=[pl.BlockSpec((tm, D_MODEL), lambda i: (i, 0))] + [_const_spec(w.shape) for w in weights],
        out_specs=pl.BlockSpec((tm, D_MODEL), lambda i: (i, 0)),
        out_shape=jax.ShapeDtypeStruct((n, D_MODEL), F32),
        compiler_params=pltpu.CompilerParams(
            dimension_semantics=("arbitrary",), vmem_limit_bytes=VMEM_LIMIT_BYTES),
        name="ffn_dense",
    )(x, *weights)


def _route(x, wr, precise):
    logits = _mm_hi(x, wr) if precise else _mm(x, wr)
    lane = lax.broadcasted_iota(jnp.int32, logits.shape, 1)
    neg = jnp.float32(-jnp.inf)
    l1 = jnp.where(lane < N_EXPERTS, logits, neg)
    v1 = jnp.max(l1, axis=-1, keepdims=True)
    i1 = jnp.min(jnp.where(l1 == v1, lane, LANES), axis=-1, keepdims=True)
    l2 = jnp.where(lane == i1, neg, l1)
    v2 = jnp.max(l2, axis=-1, keepdims=True)
    i2 = jnp.min(jnp.where(l2 == v2, lane, LANES), axis=-1, keepdims=True)
    e2 = jnp.exp(v2 - v1)
    den = 1.0 + e2
    idx = jnp.where(lane == 0, i1, jnp.where(lane == 1, i2, 0))
    gates = jnp.where(lane == 0, 1.0 / den, jnp.where(lane == 1, e2 / den, 0.0))
    return idx, gates


def _router_kernel(x_ref, wr_ref, idx_ref, gate_ref):
    idx_ref[...], gate_ref[...] = _route(x_ref[...], wr_ref[...], precise=True)


def _router(x, wr_pad):
    n = x.shape[0]
    tm = min(SEQ_TILE, n)
    return pl.pallas_call(
        _router_kernel,
        grid=(n // tm,),
        in_specs=[pl.BlockSpec((tm, D_MODEL), lambda i: (i, 0)), _const_spec(wr_pad.shape)],
        out_specs=(pl.BlockSpec((tm, LANES), lambda i: (i, 0)),
                   pl.BlockSpec((tm, LANES), lambda i: (i, 0))),
        out_shape=(jax.ShapeDtypeStruct((n, LANES), jnp.int32),
                   jax.ShapeDtypeStruct((n, LANES), F32)),
        compiler_params=pltpu.CompilerParams(dimension_semantics=("arbitrary",)),
        name="moe_router",
    )(x, wr_pad)


def _sc_row_gather(src, idx):
    n_idx = idx.shape[0]
    assert n_idx % SC_WINDOW == 0
    n_win = n_idx // SC_WINDOW
    n_workers = SC_CORES * SC_SUBCORES
    per = pl.cdiv(n_win, n_workers)
    n_sub = SC_WINDOW // SC_SUB_ROWS
    assert n_sub >= 2 and n_sub % 2 == 0
    mesh = plsc.VectorSubcoreMesh(core_axis_name="c", subcore_axis_name="s",
                                  num_cores=SC_CORES, num_subcores=SC_SUBCORES)

    def body(src_hbm, idx_hbm, out_hbm, idx_v, buf, g0, g1, w0, w1):
        gsem = (g0, g1)
        wsem = (w0, w1)
        worker = lax.axis_index("c") * SC_SUBCORES + lax.axis_index("s")
        pltpu.sync_copy(idx_hbm.at[:, pl.ds(worker * (per * SC_WINDOW), per * SC_WINDOW)], idx_v)
        count = jnp.clip(n_win - worker * per, 0, per)

        def out_rows(t, q):
            return out_hbm.at[pl.ds((worker * per + t) * SC_WINDOW + q * SC_SUB_ROWS, SC_SUB_ROWS)]

        def fetch(t, q):
            return pltpu.async_copy(
                src_hbm.at[idx_v.at[0, pl.ds(t * SC_WINDOW + q * SC_SUB_ROWS, SC_SUB_ROWS)]],
                buf.at[q % 2], gsem[q % 2])

        def put(t, q):
            return pltpu.async_copy(buf.at[q % 2], out_rows(t, q), wsem[q % 2])

        def wait_put(t, q):
            pltpu.make_async_copy(buf.at[q % 2], out_rows(t, q), wsem[q % 2]).wait()

        @pl.loop(0, count)
        def _(t):
            @pl.when(t > 0)
            def _():
                wait_put(t - 1, n_sub - 2)

            fetches = {0: fetch(t, 0)}
            puts = {}
            for q in range(n_sub):
                if q + 1 < n_sub:
                    if q >= 1:
                        puts[q - 1].wait()
                    else:
                        @pl.when(t > 0)
                        def _():
                            wait_put(t - 1, n_sub - 1)
                    fetches[q + 1] = fetch(t, q + 1)
                fetches[q].wait()
                puts[q] = put(t, q)

        @pl.when(count > 0)
        def _():
            wait_put(count - 1, n_sub - 2)
            wait_put(count - 1, n_sub - 1)

    idx_rows = jnp.pad(idx, (0, n_workers * per * SC_WINDOW - n_idx)).reshape(1, -1)
    return pl.kernel(
        body,
        out_type=jax.ShapeDtypeStruct((n_idx, D_MODEL), src.dtype),
        mesh=mesh,
        scratch_types=[pltpu.VMEM((1, per * SC_WINDOW), jnp.int32),
                       pltpu.VMEM((2, SC_SUB_ROWS, D_MODEL), src.dtype),
                       pltpu.SemaphoreType.DMA, pltpu.SemaphoreType.DMA,
                       pltpu.SemaphoreType.DMA, pltpu.SemaphoreType.DMA],
        cost_estimate=pl.CostEstimate(
            flops=0, transcendentals=0,
            bytes_accessed=n_idx * (2 * D_MODEL * src.dtype.itemsize + idx.dtype.itemsize)),
        name="sc_row_gather",
    )(src, idx_rows)


def _expert_kernel(te_ref, nu_ref, xs_ref, wg_ref, wu_ref, wd_ref, ys_ref, xs_bf, acc, *, tf, nf):
    del te_ref
    i = pl.program_id(0)
    j = pl.program_id(1)
    used = i < nu_ref[0]

    @pl.when(jnp.logical_and(used, j == 0))
    def _():
        xs_bf[...] = xs_ref[...].astype(BF16)

    @pl.when(used)
    def _():
        xb = xs_bf[...]
        part = None
        for c0 in range(0, tf, FFN_CHUNK):
            h = _silu(_dot(xb, wg_ref[:, c0:c0 + FFN_CHUNK])) * _dot(xb, wu_ref[:, c0:c0 + FFN_CHUNK])
            d = _dot(h.astype(BF16), wd_ref[c0:c0 + FFN_CHUNK, :])
            part = d if part is None else part + d

        if nf == 1:
            ys_ref[...] = part
        else:
            @pl.when(j == 0)
            def _():
                acc[...] = part

            @pl.when(jnp.logical_and(j > 0, j < nf - 1))
            def _():
                acc[...] += part

            @pl.when(j == nf - 1)
            def _():
                ys_ref[...] = acc[...] + part

    @pl.when(jnp.logical_and(jnp.logical_not(used), j == nf - 1))
    def _():
        ys_ref[...] = jnp.zeros(ys_ref.shape, F32)


def _experts(xs, tile_expert, n_used, wg, wu, wd, tm):
    n_tiles = tile_expert.shape[0]
    d_e = wg.shape[2]
    tf = EXPERT_F_TILE
    nf = d_e // tf
    assert nf * tf == d_e and tf % FFN_CHUNK == 0

    def jeff(i, j, nu):
        return jnp.where(i < nu[0], j, nf - 1)

    grid_spec = pltpu.PrefetchScalarGridSpec(
        num_scalar_prefetch=2,
        grid=(n_tiles, nf),
        in_specs=[
            pl.BlockSpec((tm, D_MODEL), lambda i, j, te, nu: (jnp.minimum(i, nu[0] - 1), 0)),
            pl.BlockSpec((None, D_MODEL, tf), lambda i, j, te, nu: (te[i], 0, jeff(i, j, nu))),
            pl.BlockSpec((None, D_MODEL, tf), lambda i, j, te, nu: (te[i], 0, jeff(i, j, nu))),
            pl.BlockSpec((None, tf, D_MODEL), lambda i, j, te, nu: (te[i], jeff(i, j, nu), 0)),
        ],
        out_specs=pl.BlockSpec((tm, D_MODEL), lambda i, j, te, nu: (i, 0)),
        scratch_shapes=[
            pltpu.VMEM((tm, D_MODEL), BF16),
            pltpu.VMEM((tm if nf > 1 else SUBLANES, D_MODEL), F32),
        ],
    )
    return pl.pallas_call(
        functools.partial(_expert_kernel, tf=tf, nf=nf),
        grid_spec=grid_spec,
        out_shape=jax.ShapeDtypeStruct((n_tiles * tm, D_MODEL), F32),
        compiler_params=pltpu.CompilerParams(
            dimension_semantics=("arbitrary", "arbitrary"), vmem_limit_bytes=VMEM_LIMIT_BYTES),
        name="moe_experts",
    )(tile_expert, n_used, xs, wg, wu, wd)


def _combine_kernel(x_ref, gate_ref, y0_ref, y1_ref, g_ref, b_ref, y_ref):
    gates = gate_ref[...]
    f = gates[:, 0:1] * y0_ref[...] + gates[:, 1:2] * y1_ref[...]
    y_ref[...] = _layer_norm(ALPHA * x_ref[...] + f, g_ref[...], b_ref[...])


def _combine(x, gates, y01, g, b, n, x_row0, y0_row0, y1_row0):
    tm = min(SEQ_TILE, n)
    nt = n // tm
    assert nt * tm == n and x_row0 % tm == 0 and y0_row0 % tm == 0 and y1_row0 % tm == 0
    xb0, y0b, y1b = x_row0 // tm, y0_row0 // tm, y1_row0 // tm
    return pl.pallas_call(
        _combine_kernel,
        grid=(nt,),
        in_specs=[
            pl.BlockSpec((tm, D_MODEL), lambda i: (i + xb0, 0)),
            pl.BlockSpec((tm, LANES), lambda i: (i, 0)),
            pl.BlockSpec((tm, D_MODEL), lambda i: (i + y0b, 0)),
            pl.BlockSpec((tm, D_MODEL), lambda i: (i + y1b, 0)),
            _const_spec(g.shape),
            _const_spec(b.shape),
        ],
        out_specs=pl.BlockSpec((tm, D_MODEL), lambda i: (i, 0)),
        out_shape=jax.ShapeDtypeStruct((n, D_MODEL), F32),
        compiler_params=pltpu.CompilerParams(
            dimension_semantics=("arbitrary",), vmem_limit_bytes=VMEM_LIMIT_BYTES),
        name="moe_combine",
    )(x, gates, y01, y01, g, b)


def _moe(x, groups, wg, wu, wd, g, b, tm):
    n = x.shape[0]
    assert sum(rows for rows, _, _ in groups) == n
    flat_e = jnp.concatenate([ids[:, k] for k in range(TOP_K) for _, ids, _ in groups])
    onehot = (flat_e[:, None] == jnp.arange(N_EXPERTS, dtype=jnp.int32)[None, :]).astype(jnp.int32)
    csum = jnp.cumsum(onehot, axis=0)
    rank = jnp.sum((csum - onehot) * onehot, axis=1)
    counts = csum[-1]
    padded = ((counts + tm - 1) // tm) * tm
    ends = jnp.cumsum(padded)
    slot = (ends - padded)[flat_e] + rank
    n_pairs = TOP_K * n
    n_tiles = pl.cdiv(n_pairs, tm) + N_EXPERTS
    n_used = (ends[-1] // tm).astype(jnp.int32).reshape(1)
    tile_start = jnp.arange(n_tiles, dtype=jnp.int32) * tm
    tile_expert = jnp.sum((tile_start[:, None] >= ends[None, :]).astype(jnp.int32), axis=1)
    last_e = jnp.sum((ends[-1] - 1 >= ends).astype(jnp.int32))
    tile_expert = jnp.minimum(tile_expert, last_e).astype(jnp.int32)
    pair_sorted = jnp.sort(flat_e * n_pairs + jnp.arange(n_pairs, dtype=jnp.int32)) % n_pairs
    slot_ids = jnp.arange(n_tiles * tm, dtype=jnp.int32)
    slot_e = jnp.repeat(tile_expert, tm)
    local = slot_ids - (ends - padded)[slot_e]
    q = (jnp.cumsum(counts) - counts)[slot_e] + local
    valid = jnp.logical_and(local >= 0, local < counts[slot_e])
    tok_of_slot = jnp.where(valid, pair_sorted[jnp.clip(q, 0, n_pairs - 1)] % n, slot_ids % n)
    xs = _sc_row_gather(x, tok_of_slot)
    ys = _experts(xs, tile_expert, n_used, wg, wu, wd, tm)
    order, row0 = [], 0
    for rows, _, _ in groups:
        order += [slot[k * n + row0:k * n + row0 + rows] for k in range(TOP_K)]
        row0 += rows
    y01 = _sc_row_gather(ys, jnp.concatenate(order))
    outs, row0 = [], 0
    for rows, _, gates in groups:
        outs.append(_combine(x, gates, y01, g, b, rows, row0, TOP_K * row0, TOP_K * row0 + rows))
        row0 += rows
    return outs


def _block_diag(w):
    n, c, _ = w.shape
    eye = jnp.eye(n, dtype=w.dtype)
    return (eye[:, None, :, None] * w[:, :, None, :]).reshape(n * c, n * c)


def _row(v):
    return v.reshape(1, -1)


def kernel(x_prompt, x_sample, state_pool, state_conv, state_lru_conv, state_lru_h, w_in, b_gate, w_pool, pool_scale, conv_w, conv_b, conv_ln_g, conv_ln_b, lru_conv_w, lru_conv_b, lru_wa, lru_ba, lru_wx, lru_bx, lru_lambda, w_br_pool, w_br_conv, w_br_lru, w_out, ln1_g, ln1_b, ln2_g, ln2_b, ffn_w_gate, ffn_w_up, ffn_w_down, moe_router, moe_w_gate, moe_w_up, moe_w_down):
    batch, seq, _ = x_prompt.shape
    dec_batch = x_sample.shape[0]

    def layer_params(l, mat):
        return {
            "w_in": w_in[l].astype(mat), "b_gate": _row(b_gate[l]),
            "w_pool": w_pool[l].astype(mat), "pool_scale": _row(pool_scale[l]),
            "conv_w": conv_w[l], "conv_b": _row(conv_b[l]),
            "conv_ln_g": _row(conv_ln_g[l]), "conv_ln_b": _row(conv_ln_b[l]),
            "lru_conv_w": lru_conv_w[l], "lru_conv_b": _row(lru_conv_b[l]),
            "lru_wa": _block_diag(lru_wa[l]).astype(mat), "lru_ba": _row(lru_ba[l]),
            "lru_wx": _block_diag(lru_wx[l]).astype(mat), "lru_bx": _row(lru_bx[l]),
            "lru_lambda": _row(lru_lambda[l]),
            "w_br_pool": w_br_pool[l].astype(mat), "w_br_conv": w_br_conv[l].astype(mat),
            "w_br_lru": w_br_lru[l].astype(mat), "w_out": w_out[l].astype(mat),
            "ln1_g": _row(ln1_g[l]), "ln1_b": _row(ln1_b[l]),
        }

    layers = [layer_params(l, BF16) for l in range(DEPTH)]
    layers_f32 = [layer_params(l, F32) for l in range(DEPTH)]
    for l in range(DEPTH):
        layers_f32[l].update({"w_in": (w_in, l), "w_br_pool": (w_br_pool, l), "w_br_conv": (w_br_conv, l),
                              "w_br_lru": (w_br_lru, l), "w_out": (w_out, l)})
    ffn_w = [(ffn_w_gate[m].astype(BF16), ffn_w_up[m].astype(BF16), ffn_w_down[m].astype(BF16))
             for m in range(ffn_w_gate.shape[0])]
    ffn_w_f32 = [(ffn_w_gate[m], ffn_w_up[m], ffn_w_down[m]) for m in range(ffn_w_gate.shape[0])]
    moe_w = [(jnp.pad(moe_router[m], ((0, 0), (0, LANES - N_EXPERTS))),
              moe_w_gate[m].astype(BF16), moe_w_up[m].astype(BF16), moe_w_down[m].astype(BF16))
             for m in range(moe_router.shape[0])]

    x_prompt, moe_w = lax.optimization_barrier((x_prompt, moe_w))

    n_p = batch * seq
    xp = x_prompt.reshape(n_p, D_MODEL)
    xd = x_sample.reshape(dec_batch, D_MODEL)
    p_states, s_states = [], []
    for l in range(DEPTH):
        g, b = _row(ln2_g[l]), _row(ln2_b[l])
        dense = l % 2 == 0
        wr, wg, wu, wd = (None,) * 4 if dense else moe_w[l // 2]
        xp, sp, sc, slc, sh, *routing = _mixer_seq(
            xp, layers[l], batch, seq, None if dense else wr.astype(BF16),
            extra_rows=0 if dense else dec_batch)
        p_states.append((sp, sc, slc, sh))
        xd, sp, sc, slc, sh = _mixer_step(xd, state_pool[l], state_conv[l], state_lru_conv[l],
                                          state_lru_h[l], layers_f32[l], PAST_LEN)
        s_states.append((sp, sc, slc, sh))
        if dense:
            xp = _ffn(xp, *ffn_w[l // 2], g, b, precise=False)
            xd = _ffn(xd, *ffn_w_f32[l // 2], g, b, precise=True)
        else:
            rows = lax.dynamic_update_slice(xp, xd, (n_p, 0))
            xp, xd = _moe(rows, [(n_p, *routing), (dec_batch, *_router(xd, wr))],
                          wg, wu, wd, g, b, MOE_TILE_SEQ)
    y_prompt = xp.reshape(batch, seq, D_MODEL)
    y_sample = xd.reshape(dec_batch, 1, D_MODEL)

    def stack(states, k):
        return jnp.stack([s[k] for s in states])

    return (y_prompt, y_sample,
            stack(p_states, 0), stack(p_states, 1), stack(p_states, 2), stack(p_states, 3),
            stack(s_states, 0), stack(s_states, 1), stack(s_states, 2), stack(s_states, 3))
```

```python
import functools

import jax
import jax.numpy as jnp
from jax import lax
from jax.experimental import pallas as pl
from jax.experimental.pallas import tpu as pltpu
from jax.experimental.pallas import tpu_sc as plsc

D_MODEL = 1024
DEPTH = 2
PAST_LEN = 16384
D_POOL = 512
POOL_GROUP = 128
POOL_WINDOWS = (2, 4, 8, 16)
POOL_BUF = 15
D_CONV = 512
CONV_WIDTH = 31
CONV_BUF = 30
D_LRU = 512
LRU_CONV_WIDTH = 4
LRU_CONV_BUF = 3
LRU_C = 8.0
N_EXPERTS = 8
TOP_K = 2
ALPHA = (2.0 * DEPTH) ** 0.25
LN_EPS = 1e-5

O_POOL = 0
O_VAL = D_POOL
O_GLU = O_VAL + D_CONV
O_LRU = O_GLU + D_CONV
O_GELU = O_LRU + D_LRU
O_GATE = O_GELU + D_LRU

SUBLANES = 8
LANES = 128
VMEM_LIMIT_BYTES = 56 * 1024 * 1024
SC_CORES = 2
SC_SUBCORES = 16
SC_WINDOW = 128
SC_SUB_ROWS = 32

SEQ_TILE = 512
SCAN_UNROLL = 2
CONV_ROWS = 128
POOL_OFF = 16
CONV_OFF = 32
LCONV_OFF = 8
FFN_CHUNK = 256
PIECE = 256
STEP_COLS = 512
EXPERT_F_TILE = 3584
MOE_TILE_SEQ = 512

BF16 = jnp.bfloat16
F32 = jnp.float32


def _sigmoid(x):
    return 0.5 * jnp.tanh(0.5 * x) + 0.5


def _silu(x):
    return x * _sigmoid(x)


def _gelu_tanh(x):
    return x * (0.5 * (1.0 + jnp.tanh(0.7978845608028654 * (x + 0.044715 * (x * x * x)))))


def _softplus(z):
    return jnp.maximum(z, 0.0) + jnp.log1p(jnp.exp(-jnp.abs(z)))


def _layer_norm(x, g, b):
    mu = jnp.mean(x, axis=-1, keepdims=True)
    xc = x - mu
    var = jnp.mean(xc * xc, axis=-1, keepdims=True)
    return xc * lax.rsqrt(var + LN_EPS) * g + b


def _dot(a, b):
    return jnp.dot(a, b, preferred_element_type=F32)


def _mm(a, w):
    return _dot(a.astype(BF16), w)


def _mm_hi(a, w):
    return jnp.dot(a, w, preferred_element_type=F32, precision=lax.Precision.HIGHEST)


def _lru_coeffs(xc, r, ig, lam, reset):
    log_a = (-LRU_C * r) * _softplus(-lam)
    a = jnp.exp(log_a)
    mult = jnp.sqrt(-jnp.tanh(log_a) * (a * a + 1.0))
    if reset is not None:
        a = jnp.where(reset, 0.0, a)
        mult = jnp.where(reset, 1.0, mult)
    return a, mult * ig * xc


def _mixer_seq_kernel(*refs, tm, with_router, batch, n_in):
    seq_row = pl.program_id(0) < batch

    @pl.when(jnp.logical_not(seq_row))
    def _():
        y_ref = refs[n_in]
        y_ref[...] = jnp.zeros(y_ref.shape, F32)

    @pl.when(seq_row)
    def _():
        _mixer_seq_body(*refs, tm=tm, with_router=with_router)


def _mixer_seq_body(x_ref, w_in_ref, b_gate_ref, w_pool_ref, pool_scale_ref,
                    conv_w_ref, conv_b_ref, cln_g_ref, cln_b_ref,
                    lconv_w_ref, lconv_b_ref, wa_ref, ba_ref, wx_ref, bx_ref, lam_ref,
                    wbp_ref, wbc_ref, wbl_ref, w_out_ref, ln_g_ref, ln_b_ref, *rest, tm, with_router):
    if with_router:
        wr_ref, rest = rest[0], rest[1:]
        idx_ref, gate_ref, rest = rest[5], rest[6], rest[:5] + rest[7:]
    (y_ref, sp_ref, sc_ref, slc_ref, sh_ref,
     pool_ext, conv_ext, lconv_ext, act_buf, a_buf, b_buf, h_carry, gate_buf, gelu_buf,
     merged_buf, xb_buf) = rest
    i = pl.program_id(1)

    @pl.when(i == 0)
    def _():
        pool_ext[0:POOL_OFF, :] = jnp.zeros((POOL_OFF, D_POOL), F32)
        conv_ext[0:CONV_OFF, :] = jnp.zeros((CONV_OFF, D_CONV), F32)
        conv_ext[tm + CONV_OFF:tm + CONV_OFF + SUBLANES, :] = jnp.zeros((SUBLANES, D_CONV), F32)
        lconv_ext[0:LCONV_OFF, :] = jnp.zeros((LCONV_OFF, D_LRU), F32)
        h_carry[...] = jnp.zeros((1, D_LRU), F32)

    xb_buf[...] = x_ref[...].astype(BF16)

    def proj(lo, width):
        return _dot(xb_buf[...], w_in_ref[:, lo:lo + width])

    pos = i * tm + lax.broadcasted_iota(jnp.int32, (tm, 1), 0)

    glu = proj(O_VAL, D_CONV) * _sigmoid(proj(O_GLU, D_CONV))
    conv_ext[CONV_OFF:CONV_OFF + tm, :] = glu

    def pool_piece():
        u_pool = proj(O_POOL, D_POOL)
        pool_ext[POOL_OFF:POOL_OFF + tm, :] = u_pool
        parts = []
        for g, w in enumerate(POOL_WINDOWS):
            lo = g * POOL_GROUP
            acc = u_pool[:, lo:lo + POOL_GROUP]
            for k in range(1, w):
                acc = acc + pool_ext[POOL_OFF - k:POOL_OFF - k + tm, lo:lo + POOL_GROUP]
            inv_cnt = 1.0 / jnp.minimum(w, pos + 1).astype(F32)
            pooled = acc * inv_cnt - u_pool[:, lo:lo + POOL_GROUP]
            parts.append(_dot(pooled.astype(BF16), w_pool_ref[g]))
        mixed = jnp.concatenate(parts, axis=1) * pool_scale_ref[...]
        merged_buf[...] = gate_buf[:, 0:D_MODEL] * _dot(mixed.astype(BF16), wbp_ref[...])
        sp_ref[0] = pool_ext[tm + POOL_OFF - POOL_BUF:tm + POOL_OFF, :]
        pool_ext[0:POOL_OFF, :] = pool_ext[tm:tm + POOL_OFF, :]

    def lru_piece(lo):
        lconv_ext[LCONV_OFF:LCONV_OFF + tm, lo:lo + PIECE] = proj(O_LRU + lo, PIECE)

    def gelu_piece(lo):
        gelu_buf[:, lo:lo + PIECE] = _gelu_tanh(proj(O_GELU + lo, PIECE))

    def gate_piece(lo):
        gate_buf[:, lo:lo + PIECE] = _sigmoid(
            proj(O_GATE + lo, PIECE) + b_gate_ref[:, lo:lo + PIECE])

    pieces = ([functools.partial(gate_piece, lo) for lo in range(0, D_MODEL, PIECE)]
              + [pool_piece]
              + [functools.partial(lru_piece, lo) for lo in range(0, D_LRU, PIECE)]
              + [functools.partial(gate_piece, lo) for lo in range(D_MODEL, 2 * D_MODEL, PIECE)]
              + [functools.partial(gelu_piece, lo) for lo in range(0, D_LRU, PIECE)]
              + [functools.partial(gate_piece, lo) for lo in range(2 * D_MODEL, 3 * D_MODEL, PIECE)])
    n_conv_chunks = tm // CONV_ROWS
    base = CONV_OFF - CONV_BUF
    rows = CONV_ROWS + SUBLANES
    n_units = n_conv_chunks * (D_CONV // LANES)
    unit = 0
    for c in range(n_conv_chunks):
        c0 = c * CONV_ROWS
        for l0 in range(0, D_CONV, LANES):
            acc = jnp.zeros((CONV_ROWS, LANES), F32) + conv_b_ref[:, l0:l0 + LANES]
            for r in range(SUBLANES):
                z = None
                for k in range(CONV_WIDTH):
                    if (base + k) % SUBLANES != r:
                        continue
                    q8 = base + k - r
                    t = (conv_ext[c0 + q8:c0 + q8 + rows, l0:l0 + LANES]
                         * conv_w_ref[k:k + 1, l0:l0 + LANES])
                    z = t if z is None else z + t
                acc = acc + z[r:r + CONV_ROWS, :]
            act_buf[c0:c0 + CONV_ROWS, l0:l0 + LANES] = acc
            for piece in pieces[unit * len(pieces) // n_units:(unit + 1) * len(pieces) // n_units]:
                piece()
            unit += 1
        act_buf[c0:c0 + CONV_ROWS, :] = _silu(_layer_norm(
            act_buf[c0:c0 + CONV_ROWS, :], cln_g_ref[...], cln_b_ref[...]))
    merged_buf[...] += gate_buf[:, D_MODEL:2 * D_MODEL] * _dot(
        act_buf[...].astype(BF16), wbc_ref[...])
    sc_ref[0] = conv_ext[tm + CONV_OFF - CONV_BUF:tm + CONV_OFF, :]
    conv_ext[0:CONV_OFF, :] = conv_ext[tm:tm + CONV_OFF, :]

    base = LCONV_OFF - LRU_CONV_BUF
    xc = jnp.zeros((tm, D_LRU), F32) + lconv_b_ref[...]
    for k in range(LRU_CONV_WIDTH):
        xc = xc + lconv_ext[base + k:base + k + tm, :] * lconv_w_ref[k:k + 1, :]
    slc_ref[0] = lconv_ext[tm + LCONV_OFF - LRU_CONV_BUF:tm + LCONV_OFF, :]
    lconv_ext[0:LCONV_OFF, :] = lconv_ext[tm:tm + LCONV_OFF, :]
    xcb = xc.astype(BF16)
    r = _sigmoid(_dot(xcb, wa_ref[...]) + ba_ref[...])
    ig = _sigmoid(_dot(xcb, wx_ref[...]) + bx_ref[...])
    a, b = _lru_coeffs(xc, r, ig, lam_ref[...], pos == 0)
    a_buf[...] = a
    b_buf[...] = b

    row8 = lax.broadcasted_iota(jnp.int32, (SUBLANES, D_LRU), 0)

    def scan_block(j, h):
        r0 = pl.multiple_of(j * SUBLANES, SUBLANES)
        a8 = a_buf[pl.ds(r0, SUBLANES), :]
        b8 = b_buf[pl.ds(r0, SUBLANES), :]
        for k in (1, 2, 4):
            a_s = pltpu.roll(a8, k, 0)
            b_s = pltpu.roll(b8, k, 0)
            m = row8 >= k
            b8 = jnp.where(m, a8 * b_s + b8, b8)
            a8 = jnp.where(m, a8 * a_s, a8)
        h8 = a8 * h + b8
        a_buf[pl.ds(r0, SUBLANES), :] = h8
        return h8[SUBLANES - 1:SUBLANES, :]

    h_last = lax.fori_loop(0, tm // SUBLANES, scan_block, h_carry[...], unroll=SCAN_UNROLL)
    h_carry[...] = h_last
    sh_ref[0] = h_last
    hg = a_buf[...] * gelu_buf[...]
    merged = merged_buf[...] + gate_buf[:, 2 * D_MODEL:3 * D_MODEL] * _dot(
        hg.astype(BF16), wbl_ref[...])

    m_out = _dot(merged.astype(BF16), w_out_ref[...])
    y = _layer_norm(ALPHA * x_ref[...] + m_out, ln_g_ref[...], ln_b_ref[...])
    y_ref[...] = y
    if with_router:
        idx_ref[...], gate_ref[...] = _route(y, wr_ref[...], precise=False)


def _mixer_step_kernel(x_ref, st_pool_ref, st_conv_ref, st_lconv_ref, st_h_ref,
                       w_in_ref, b_gate_ref, w_pool_ref, pool_scale_ref,
                       conv_w_ref, conv_b_ref, cln_g_ref, cln_b_ref,
                       lconv_w_ref, lconv_b_ref, wa_ref, ba_ref, wx_ref, bx_ref, lam_ref,
                       wbp_ref, wbc_ref, wbl_ref, w_out_ref, ln_g_ref, ln_b_ref,
                       y_ref, up_ref, glu_ref, ul_ref, sh_ref, u_buf, *, start_pos, n_blocks):
    k = pl.program_id(0)
    u_buf[k] = _mm_hi(x_ref[...], w_in_ref[...])

    @pl.when(k == n_blocks - 1)
    def _():
        x = x_ref[...]

        def proj(lo):
            return u_buf[lo // STEP_COLS]

        def gate(n):
            lo = n * D_MODEL
            u = jnp.concatenate([proj(O_GATE + lo), proj(O_GATE + lo + STEP_COLS)], axis=1)
            return _sigmoid(u + b_gate_ref[:, lo:lo + D_MODEL])

        u_pool = proj(O_POOL)
        parts = []
        for g, w in enumerate(POOL_WINDOWS):
            lo = g * POOL_GROUP
            acc = u_pool[:, lo:lo + POOL_GROUP]
            for j in range(1, w):
                acc = acc + st_pool_ref[POOL_BUF - j, :, lo:lo + POOL_GROUP]
            pooled = acc * (1.0 / min(w, start_pos + 1)) - u_pool[:, lo:lo + POOL_GROUP]
            parts.append(_mm_hi(pooled, w_pool_ref[g]))
        mixed = jnp.concatenate(parts, axis=1) * pool_scale_ref[...]
        merged = gate(0) * _mm_hi(mixed, wbp_ref[...])
        up_ref[...] = u_pool

        glu = proj(O_VAL) * _sigmoid(proj(O_GLU))
        acc = glu * conv_w_ref[CONV_BUF:CONV_BUF + 1, :] + conv_b_ref[...]
        for j in range(CONV_BUF):
            acc = acc + st_conv_ref[j] * conv_w_ref[j:j + 1, :]
        act = _silu(_layer_norm(acc, cln_g_ref[...], cln_b_ref[...]))
        merged = merged + gate(1) * _mm_hi(act, wbc_ref[...])
        glu_ref[...] = glu

        u_lru = proj(O_LRU)
        xc = u_lru * lconv_w_ref[LRU_CONV_BUF:LRU_CONV_BUF + 1, :] + lconv_b_ref[...]
        for j in range(LRU_CONV_BUF):
            xc = xc + st_lconv_ref[j] * lconv_w_ref[j:j + 1, :]
        ul_ref[...] = u_lru
        r = _sigmoid(_mm_hi(xc, wa_ref[...]) + ba_ref[...])
        ig = _sigmoid(_mm_hi(xc, wx_ref[...]) + bx_ref[...])
        reset = jnp.full(xc.shape, True) if start_pos == 0 else None
        a, b = _lru_coeffs(xc, r, ig, lam_ref[...], reset)
        h = a * st_h_ref[...] + b
        sh_ref[...] = h
        hg = h * _gelu_tanh(proj(O_GELU))
        merged = merged + gate(2) * _mm_hi(hg, wbl_ref[...])

        m_out = _mm_hi(merged, w_out_ref[...])
        y_ref[...] = _layer_norm(ALPHA * x + m_out, ln_g_ref[...], ln_b_ref[...])


def _const_spec(shape):
    nd = len(shape)
    return pl.BlockSpec(shape, lambda *_: (0,) * nd, pipeline_mode=pl.Buffered(1))


def _weight_spec(a):
    if not isinstance(a, tuple):
        return _const_spec(a.shape)
    arr, layer = a
    zeros = (0,) * (arr.ndim - 1)
    return pl.BlockSpec((None,) + arr.shape[1:], lambda *_: (layer,) + zeros,
                        pipeline_mode=pl.Buffered(1))


def _weight_arg(a):
    return a[0] if isinstance(a, tuple) else a


def _mixer_weight_list(p):
    return [p["w_in"], p["b_gate"], p["w_pool"], p["pool_scale"], p["conv_w"], p["conv_b"],
            p["conv_ln_g"], p["conv_ln_b"], p["lru_conv_w"], p["lru_conv_b"], p["lru_wa"],
            p["lru_ba"], p["lru_wx"], p["lru_bx"], p["lru_lambda"], p["w_br_pool"],
            p["w_br_conv"], p["w_br_lru"], p["w_out"], p["ln1_g"], p["ln1_b"]]


def _mixer_seq(x, p, batch, seq, wr_pad=None, extra_rows=0):
    tm = min(SEQ_TILE, seq)
    nt = seq // tm
    assert extra_rows <= tm
    with_router = wr_pad is not None
    weights = _mixer_weight_list(p) + ([wr_pad] if with_router else [])
    last = batch * nt - 1

    def row_block(b, i):
        return (jnp.minimum(b * nt + i, last), 0)

    def y_block(b, i):
        return (jnp.where(b < batch, b * nt + i, batch * nt), 0)

    def state_block(b, i):
        return (jnp.minimum(b, batch - 1), 0, 0)

    in_specs = [pl.BlockSpec((tm, D_MODEL), row_block)]
    in_specs += [_weight_spec(w) for w in weights]
    out_shape = [
        jax.ShapeDtypeStruct((batch * seq + extra_rows, D_MODEL), F32),
        jax.ShapeDtypeStruct((batch, POOL_BUF, D_POOL), F32),
        jax.ShapeDtypeStruct((batch, CONV_BUF, D_CONV), F32),
        jax.ShapeDtypeStruct((batch, LRU_CONV_BUF, D_LRU), F32),
        jax.ShapeDtypeStruct((batch, 1, D_LRU), F32),
    ]
    out_specs = [
        pl.BlockSpec((tm, D_MODEL), y_block),
        pl.BlockSpec((1, POOL_BUF, D_POOL), state_block),
        pl.BlockSpec((1, CONV_BUF, D_CONV), state_block),
        pl.BlockSpec((1, LRU_CONV_BUF, D_LRU), state_block),
        pl.BlockSpec((1, 1, D_LRU), state_block),
    ]
    if with_router:
        out_shape += [jax.ShapeDtypeStruct((batch * seq, LANES), jnp.int32),
                      jax.ShapeDtypeStruct((batch * seq, LANES), F32)]
        out_specs += [pl.BlockSpec((tm, LANES), row_block),
                      pl.BlockSpec((tm, LANES), row_block)]
    scratch = [
        pltpu.VMEM((tm + POOL_OFF, D_POOL), F32),
        pltpu.VMEM((tm + CONV_OFF + SUBLANES, D_CONV), F32),
        pltpu.VMEM((tm + LCONV_OFF, D_LRU), F32),
        pltpu.VMEM((tm, D_CONV), F32),
        pltpu.VMEM((tm, D_LRU), F32),
        pltpu.VMEM((tm, D_LRU), F32),
        pltpu.VMEM((1, D_LRU), F32),
        pltpu.VMEM((tm, 3 * D_MODEL), F32),
        pltpu.VMEM((tm, D_LRU), F32),
        pltpu.VMEM((tm, D_MODEL), F32),
        pltpu.VMEM((tm, D_MODEL), BF16),
    ]
    y, sp, sc, slc, sh, *routing = pl.pallas_call(
        functools.partial(_mixer_seq_kernel, tm=tm, with_router=with_router, batch=batch,
                          n_in=1 + len(weights)),
        grid=(batch + (1 if extra_rows else 0), nt),
        in_specs=in_specs,
        out_specs=out_specs,
        out_shape=out_shape,
        scratch_shapes=scratch,
        compiler_params=pltpu.CompilerParams(
            dimension_semantics=("arbitrary", "arbitrary"),
            vmem_limit_bytes=VMEM_LIMIT_BYTES),
        name="mixer_seq",
    )(x, *[_weight_arg(w) for w in weights])
    return (y, sp, sc, slc, sh.reshape(batch, D_LRU), *routing)


def _mixer_step(x, st_pool, st_conv, st_lconv, st_h, p, start_pos):
    batch = x.shape[0]
    weights = _mixer_weight_list(p)
    states = [jnp.transpose(st_pool, (1, 0, 2)), jnp.transpose(st_conv, (1, 0, 2)),
              jnp.transpose(st_lconv, (1, 0, 2)), st_h]

    def spec(a):
        if not isinstance(a, tuple):
            return _const_spec(a.shape)
        arr, layer = a
        zeros = (0,) * (arr.ndim - 1)
        return pl.BlockSpec((None,) + arr.shape[1:], lambda k: (layer,) + zeros,
                            pipeline_mode=pl.Buffered(1))

    w_in_all, layer = weights[0]
    in_cols = w_in_all.shape[2]
    n_blocks = in_cols // STEP_COLS
    assert n_blocks * STEP_COLS == in_cols
    in_specs = [spec(a) for a in [x] + states + weights]
    in_specs[1 + len(states)] = pl.BlockSpec((None, D_MODEL, STEP_COLS), lambda k: (layer, 0, k))
    args = [x] + states + [w[0] if isinstance(w, tuple) else w for w in weights]
    out_shape = (
        jax.ShapeDtypeStruct((batch, D_MODEL), F32),
        jax.ShapeDtypeStruct((batch, D_POOL), F32),
        jax.ShapeDtypeStruct((batch, D_CONV), F32),
        jax.ShapeDtypeStruct((batch, D_LRU), F32),
        jax.ShapeDtypeStruct((batch, D_LRU), F32),
    )
    y, u_pool, glu, u_lru, sh = pl.pallas_call(
        functools.partial(_mixer_step_kernel, start_pos=start_pos, n_blocks=n_blocks),
        grid=(n_blocks,),
        in_specs=in_specs,
        out_specs=tuple(pl.BlockSpec(s.shape, lambda k: (0, 0)) for s in out_shape),
        out_shape=out_shape,
        scratch_shapes=[pltpu.VMEM((n_blocks, batch, STEP_COLS), F32)],
        compiler_params=pltpu.CompilerParams(
            dimension_semantics=("arbitrary",), vmem_limit_bytes=VMEM_LIMIT_BYTES),
        name="mixer_step",
    )(*args)

    def push(state, row):
        return jnp.concatenate([state[:, 1:], row[:, None]], axis=1)

    return y, push(st_pool, u_pool), push(st_conv, glu), push(st_lconv, u_lru), sh


def _ffn_kernel(x_ref, wg_ref, wu_ref, wd_ref, g_ref, b_ref, y_ref, *, d_ff, precise):
    x = x_ref[...]
    mm = _mm_hi if precise else _dot
    xb = x if precise else x.astype(BF16)
    acc = jnp.zeros(x.shape, F32)
    for c0 in range(0, d_ff, FFN_CHUNK):
        h = _silu(mm(xb, wg_ref[:, c0:c0 + FFN_CHUNK])) * mm(xb, wu_ref[:, c0:c0 + FFN_CHUNK])
        acc = acc + mm(h if precise else h.astype(BF16), wd_ref[c0:c0 + FFN_CHUNK, :])
    y_ref[...] = _layer_norm(ALPHA * x + acc, g_ref[...], b_ref[...])


def _ffn(x, wg, wu, wd, g, b, precise):
    n = x.shape[0]
    tm = min(SEQ_TILE, n)
    d_ff = wg.shape[1]
    weights = [wg, wu, wd, g, b]
    return pl.pallas_call(
        functools.partial(_ffn_kernel, d_ff=d_ff, precise=precise),
        grid=(n // tm,),
        in_specs=[pl.BlockSpec((tm, D_MODEL), lambda i: (i, 0))] + [_const_spec(w.shape) for w in weights],
        out_specs=pl.BlockSpec((tm, D_MODEL), lambda i: (i, 0)),
        out_shape=jax.ShapeDtypeStruct((n, D_MODEL), F32),
        compiler_params=pltpu.CompilerParams(
            dimension_semantics=("arbitrary",), vmem_limit_bytes=VMEM_LIMIT_BYTES),
        name="ffn_dense",
    )(x, *weights)


def _route(x, wr, precise):
    logits = _mm_hi(x, wr) if precise else _mm(x, wr)
    lane = lax.broadcasted_iota(jnp.int32, logits.shape, 1)
    neg = jnp.float32(-jnp.inf)
    l1 = jnp.where(lane < N_EXPERTS, logits, neg)
    v1 = jnp.max(l1, axis=-1, keepdims=True)
    i1 = jnp.min(jnp.where(l1 == v1, lane, LANES), axis=-1, keepdims=True)
    l2 = jnp.where(lane == i1, neg, l1)
    v2 = jnp.max(l2, axis=-1, keepdims=True)
    i2 = jnp.min(jnp.where(l2 == v2, lane, LANES), axis=-1, keepdims=True)
    e2 = jnp.exp(v2 - v1)
    den = 1.0 + e2
    idx = jnp.where(lane == 0, i1, jnp.where(lane == 1, i2, 0))
    gates = jnp.where(lane == 0, 1.0 / den, jnp.where(lane == 1, e2 / den, 0.0))
    return idx, gates


def _router_kernel(x_ref, wr_ref, idx_ref, gate_ref):
    idx_ref[...], gate_ref[...] = _route(x_ref[...], wr_ref[...], precise=True)


def _router(x, wr_pad):
    n = x.shape[0]
    tm = min(SEQ_TILE, n)
    return pl.pallas_call(
        _router_kernel,
        grid=(n // tm,),
        in_specs=[pl.BlockSpec((tm, D_MODEL), lambda i: (i, 0)), _const_spec(wr_pad.shape)],
        out_specs=(pl.BlockSpec((tm, LANES), lambda i: (i, 0)),
                   pl.BlockSpec((tm, LANES), lambda i: (i, 0))),
        out_shape=(jax.ShapeDtypeStruct((n, LANES), jnp.int32),
                   jax.ShapeDtypeStruct((n, LANES), F32)),
        compiler_params=pltpu.CompilerParams(dimension_semantics=("arbitrary",)),
        name="moe_router",
    )(x, wr_pad)


def _sc_row_gather(src, idx):
    n_idx = idx.shape[0]
    assert n_idx % SC_WINDOW == 0
    n_win = n_idx // SC_WINDOW
    n_workers = SC_CORES * SC_SUBCORES
    per = pl.cdiv(n_win, n_workers)
    n_sub = SC_WINDOW // SC_SUB_ROWS
    assert n_sub >= 2 and n_sub % 2 == 0
    mesh = plsc.VectorSubcoreMesh(core_axis_name="c", subcore_axis_name="s",
                                  num_cores=SC_CORES, num_subcores=SC_SUBCORES)

    def body(src_hbm, idx_hbm, out_hbm, idx_v, buf, g0, g1, w0, w1):
        gsem = (g0, g1)
        wsem = (w0, w1)
        worker = lax.axis_index("c") * SC_SUBCORES + lax.axis_index("s")
        pltpu.sync_copy(idx_hbm.at[:, pl.ds(worker * (per * SC_WINDOW), per * SC_WINDOW)], idx_v)
        count = jnp.clip(n_win - worker * per, 0, per)

        def out_rows(t, q):
            return out_hbm.at[pl.ds((worker * per + t) * SC_WINDOW + q * SC_SUB_ROWS, SC_SUB_ROWS)]

        def fetch(t, q):
            return pltpu.async_copy(
                src_hbm.at[idx_v.at[0, pl.ds(t * SC_WINDOW + q * SC_SUB_ROWS, SC_SUB_ROWS)]],
                buf.at[q % 2], gsem[q % 2])

        def put(t, q):
            return pltpu.async_copy(buf.at[q % 2], out_rows(t, q), wsem[q % 2])

        def wait_put(t, q):
            pltpu.make_async_copy(buf.at[q % 2], out_rows(t, q), wsem[q % 2]).wait()

        @pl.loop(0, count)
        def _(t):
            @pl.when(t > 0)
            def _():
                wait_put(t - 1, n_sub - 2)

            fetches = {0: fetch(t, 0)}
            puts = {}
            for q in range(n_sub):
                if q + 1 < n_sub:
                    if q >= 1:
                        puts[q - 1].wait()
                    else:
                        @pl.when(t > 0)
                        def _():
                            wait_put(t - 1, n_sub - 1)
                    fetches[q + 1] = fetch(t, q + 1)
                fetches[q].wait()
                puts[q] = put(t, q)

        @pl.when(count > 0)
        def _():
            wait_put(count - 1, n_sub - 2)
            wait_put(count - 1, n_sub - 1)

    idx_rows = jnp.pad(idx, (0, n_workers * per * SC_WINDOW - n_idx)).reshape(1, -1)
    return pl.kernel(
        body,
        out_type=jax.ShapeDtypeStruct((n_idx, D_MODEL), src.dtype),
        mesh=mesh,
        scratch_types=[pltpu.VMEM((1, per * SC_WINDOW), jnp.int32),
                       pltpu.VMEM((2, SC_SUB_ROWS, D_MODEL), src.dtype),
                       pltpu.SemaphoreType.DMA, pltpu.SemaphoreType.DMA,
                       pltpu.SemaphoreType.DMA, pltpu.SemaphoreType.DMA],
        cost_estimate=pl.CostEstimate(
            flops=0, transcendentals=0,
            bytes_accessed=n_idx * (2 * D_MODEL * src.dtype.itemsize + idx.dtype.itemsize)),
        name="sc_row_gather",
    )(src, idx_rows)


def _expert_kernel(te_ref, nu_ref, xs_ref, wg_ref, wu_ref, wd_ref, ys_ref, xs_bf, acc, *, tf, nf):
    del te_ref
    i = pl.program_id(0)
    j = pl.program_id(1)
    used = i < nu_ref[0]

    @pl.when(jnp.logical_and(used, j == 0))
    def _():
        xs_bf[...] = xs_ref[...].astype(BF16)

    @pl.when(used)
    def _():
        xb = xs_bf[...]
        part = None
        for c0 in range(0, tf, FFN_CHUNK):
            h = _silu(_dot(xb, wg_ref[:, c0:c0 + FFN_CHUNK])) * _dot(xb, wu_ref[:, c0:c0 + FFN_CHUNK])
            d = _dot(h.astype(BF16), wd_ref[c0:c0 + FFN_CHUNK, :])
            part = d if part is None else part + d

        if nf == 1:
            ys_ref[...] = part
        else:
            @pl.when(j == 0)
            def _():
                acc[...] = part

            @pl.when(jnp.logical_and(j > 0, j < nf - 1))
            def _():
                acc[...] += part

            @pl.when(j == nf - 1)
            def _():
                ys_ref[...] = acc[...] + part

    @pl.when(jnp.logical_and(jnp.logical_not(used), j == nf - 1))
    def _():
        ys_ref[...] = jnp.zeros(ys_ref.shape, F32)


def _experts(xs, tile_expert, n_used, wg, wu, wd, tm):
    n_tiles = tile_expert.shape[0]
    d_e = wg.shape[2]
    tf = EXPERT_F_TILE
    nf = d_e // tf
    assert nf * tf == d_e and tf % FFN_CHUNK == 0

    def jeff(i, j, nu):
        return jnp.where(i < nu[0], j, nf - 1)

    grid_spec = pltpu.PrefetchScalarGridSpec(
        num_scalar_prefetch=2,
        grid=(n_tiles, nf),
        in_specs=[
            pl.BlockSpec((tm, D_MODEL), lambda i, j, te, nu: (jnp.minimum(i, nu[0] - 1), 0)),
            pl.BlockSpec((None, D_MODEL, tf), lambda i, j, te, nu: (te[i], 0, jeff(i, j, nu))),
            pl.BlockSpec((None, D_MODEL, tf), lambda i, j, te, nu: (te[i], 0, jeff(i, j, nu))),
            pl.BlockSpec((None, tf, D_MODEL), lambda i, j, te, nu: (te[i], jeff(i, j, nu), 0)),
        ],
        out_specs=pl.BlockSpec((tm, D_MODEL), lambda i, j, te, nu: (i, 0)),
        scratch_shapes=[
            pltpu.VMEM((tm, D_MODEL), BF16),
            pltpu.VMEM((tm if nf > 1 else SUBLANES, D_MODEL), F32),
        ],
    )
    return pl.pallas_call(
        functools.partial(_expert_kernel, tf=tf, nf=nf),
        grid_spec=grid_spec,
        out_shape=jax.ShapeDtypeStruct((n_tiles * tm, D_MODEL), F32),
        compiler_params=pltpu.CompilerParams(
            dimension_semantics=("arbitrary", "arbitrary"), vmem_limit_bytes=VMEM_LIMIT_BYTES),
        name="moe_experts",
    )(tile_expert, n_used, xs, wg, wu, wd)


def _combine_kernel(x_ref, gate_ref, y0_ref, y1_ref, g_ref, b_ref, y_ref):
    gates = gate_ref[...]
    f = gates[:, 0:1] * y0_ref[...] + gates[:, 1:2] * y1_ref[...]
    y_ref[...] = _layer_norm(ALPHA * x_ref[...] + f, g_ref[...], b_ref[...])


def _combine(x, gates, y01, g, b, n, x_row0, y0_row0, y1_row0):
    tm = min(SEQ_TILE, n)
    nt = n // tm
    assert nt * tm == n and x_row0 % tm == 0 and y0_row0 % tm == 0 and y1_row0 % tm == 0
    xb0, y0b, y1b = x_row0 // tm, y0_row0 // tm, y1_row0 // tm
    return pl.pallas_call(
        _combine_kernel,
        grid=(nt,),
        in_specs=[
            pl.BlockSpec((tm, D_MODEL), lambda i: (i + xb0, 0)),
            pl.BlockSpec((tm, LANES), lambda i: (i, 0)),
            pl.BlockSpec((tm, D_MODEL), lambda i: (i + y0b, 0)),
            pl.BlockSpec((tm, D_MODEL), lambda i: (i + y1b, 0)),
            _const_spec(g.shape),
            _const_spec(b.shape),
        ],
        out_specs=pl.BlockSpec((tm, D_MODEL), lambda i: (i, 0)),
        out_shape=jax.ShapeDtypeStruct((n, D_MODEL), F32),
        compiler_params=pltpu.CompilerParams(
            dimension_semantics=("arbitrary",), vmem_limit_bytes=VMEM_LIMIT_BYTES),
        name="moe_combine",
    )(x, gates, y01, y01, g, b)


def _moe(x, groups, wg, wu, wd, g, b, tm):
    n = x.shape[0]
    assert sum(rows for rows, _, _ in groups) == n
    flat_e = jnp.concatenate([ids[:, k] for k in range(TOP_K) for _, ids, _ in groups])
    onehot = (flat_e[:, None] == jnp.arange(N_EXPERTS, dtype=jnp.int32)[None, :]).astype(jnp.int32)
    csum = jnp.cumsum(onehot, axis=0)
    rank = jnp.sum((csum - onehot) * onehot, axis=1)
    counts = csum[-1]
    padded = ((counts + tm - 1) // tm) * tm
    ends = jnp.cumsum(padded)
    slot = (ends - padded)[flat_e] + rank
    n_pairs = TOP_K * n
    n_tiles = pl.cdiv(n_pairs, tm) + N_EXPERTS
    n_used = (ends[-1] // tm).astype(jnp.int32).reshape(1)
    tile_start = jnp.arange(n_tiles, dtype=jnp.int32) * tm
    tile_expert = jnp.sum((tile_start[:, None] >= ends[None, :]).astype(jnp.int32), axis=1)
    last_e = jnp.sum((ends[-1] - 1 >= ends).astype(jnp.int32))
    tile_expert = jnp.minimum(tile_expert, last_e).astype(jnp.int32)
    pair_sorted = jnp.sort(flat_e * n_pairs + jnp.arange(n_pairs, dtype=jnp.int32)) % n_pairs
    slot_ids = jnp.arange(n_tiles * tm, dtype=jnp.int32)
    slot_e = jnp.repeat(tile_expert, tm)
    local = slot_ids - (ends - padded)[slot_e]
    q = (jnp.cumsum(counts) - counts)[slot_e] + local
    valid = jnp.logical_and(local >= 0, local < counts[slot_e])
    tok_of_slot = jnp.where(valid, pair_sorted[jnp.clip(q, 0, n_pairs - 1)] % n, slot_ids % n)
    xs = _sc_row_gather(x, tok_of_slot)
    ys = _experts(xs, tile_expert, n_used, wg, wu, wd, tm)
    order, row0 = [], 0
    for rows, _, _ in groups:
        order += [slot[k * n + row0:k * n + row0 + rows] for k in range(TOP_K)]
        row0 += rows
    y01 = _sc_row_gather(ys, jnp.concatenate(order))
    outs, row0 = [], 0
    for rows, _, gates in groups:
        outs.append(_combine(x, gates, y01, g, b, rows, row0, TOP_K * row0, TOP_K * row0 + rows))
        row0 += rows
    return outs


def _block_diag(w):
    n, c, _ = w.shape
    eye = jnp.eye(n, dtype=w.dtype)
    return (eye[:, None, :, None] * w[:, :, None, :]).reshape(n * c, n * c)


def _row(v):
    return v.reshape(1, -1)


def kernel(x_prompt, x_sample, state_pool, state_conv, state_lru_conv, state_lru_h, w_in, b_gate, w_pool, pool_scale, conv_w, conv_b, conv_ln_g, conv_ln_b, lru_conv_w, lru_conv_b, lru_wa, lru_ba, lru_wx, lru_bx, lru_lambda, w_br_pool, w_br_conv, w_br_lru, w_out, ln1_g, ln1_b, ln2_g, ln2_b, ffn_w_gate, ffn_w_up, ffn_w_down, moe_router, moe_w_gate, moe_w_up, moe_w_down):
    batch, seq, _ = x_prompt.shape
    dec_batch = x_sample.shape[0]

    def layer_params(l, mat):
        return {
            "w_in": w_in[l].astype(mat), "b_gate": _row(b_gate[l]),
            "w_pool": w_pool[l].astype(mat), "pool_scale": _row(pool_scale[l]),
            "conv_w": conv_w[l], "conv_b": _row(conv_b[l]),
            "conv_ln_g": _row(conv_ln_g[l]), "conv_ln_b": _row(conv_ln_b[l]),
            "lru_conv_w": lru_conv_w[l], "lru_conv_b": _row(lru_conv_b[l]),
            "lru_wa": _block_diag(lru_wa[l]).astype(mat), "lru_ba": _row(lru_ba[l]),
            "lru_wx": _block_diag(lru_wx[l]).astype(mat), "lru_bx": _row(lru_bx[l]),
            "lru_lambda": _row(lru_lambda[l]),
            "w_br_pool": w_br_pool[l].astype(mat), "w_br_conv": w_br_conv[l].astype(mat),
            "w_br_lru": w_br_lru[l].astype(mat), "w_out": w_out[l].astype(mat),
            "ln1_g": _row(ln1_g[l]), "ln1_b": _row(ln1_b[l]),
        }

    layers = [layer_params(l, BF16) for l in range(DEPTH)]
    layers_f32 = [layer_params(l, F32) for l in range(DEPTH)]
    stacked = {"w_in": w_in, "w_br_pool": w_br_pool, "w_br_conv": w_br_conv,
               "w_br_lru": w_br_lru, "w_out": w_out}
    stacked_bf = {k: v.astype(BF16) for k, v in stacked.items()}
    for l in range(DEPTH):
        layers_f32[l].update({k: (v, l) for k, v in stacked.items()})
        layers[l].update({k: (v, l) for k, v in stacked_bf.items()})
    ffn_w = [(ffn_w_gate[m].astype(BF16), ffn_w_up[m].astype(BF16), ffn_w_down[m].astype(BF16))
             for m in range(ffn_w_gate.shape[0])]
    ffn_w_f32 = [(ffn_w_gate[m], ffn_w_up[m], ffn_w_down[m]) for m in range(ffn_w_gate.shape[0])]
    moe_w = [(jnp.pad(moe_router[m], ((0, 0), (0, LANES - N_EXPERTS))),
              moe_w_gate[m].astype(BF16), moe_w_up[m].astype(BF16), moe_w_down[m].astype(BF16))
             for m in range(moe_router.shape[0])]

    x_prompt, moe_w = lax.optimization_barrier((x_prompt, moe_w))

    n_p = batch * seq
    xp = x_prompt.reshape(n_p, D_MODEL)
    xd = x_sample.reshape(dec_batch, D_MODEL)
    p_states, s_states = [], []
    for l in range(DEPTH):
        g, b = _row(ln2_g[l]), _row(ln2_b[l])
        dense = l % 2 == 0
        wr, wg, wu, wd = (None,) * 4 if dense else moe_w[l // 2]
        xp, sp, sc, slc, sh, *routing = _mixer_seq(
            xp, layers[l], batch, seq, None if dense else wr.astype(BF16),
            extra_rows=0 if dense else dec_batch)
        p_states.append((sp, sc, slc, sh))
        xd, sp, sc, slc, sh = _mixer_step(xd, state_pool[l], state_conv[l], state_lru_conv[l],
                                          state_lru_h[l], layers_f32[l], PAST_LEN)
        s_states.append((sp, sc, slc, sh))
        if dense:
            xp = _ffn(xp, *ffn_w[l // 2], g, b, precise=False)
            xd = _ffn(xd, *ffn_w_f32[l // 2], g, b, precise=True)
        else:
            rows = lax.dynamic_update_slice(xp, xd, (n_p, 0))
            xp, xd = _moe(rows, [(n_p, *routing), (dec_batch, *_router(xd, wr))],
                          wg, wu, wd, g, b, MOE_TILE_SEQ)
    y_prompt = xp.reshape(batch, seq, D_MODEL)
    y_sample = xd.reshape(dec_batch, 1, D_MODEL)

    def stack(states, k):
        return jnp.stack([s[k] for s in states])

    return (y_prompt, y_sample,
            stack(p_states, 0), stack(p_states, 1), stack(p_states, 2), stack(p_states, 3),
            stack(s_states, 0), stack(s_states, 1), stack(s_states, 2), stack(s_states, 3))
```
